```python
import jax
import jax.numpy as jnp
from jax import lax
import numpy as np

D_MODEL = 4096
BATCH = 4
SEQ = 2048
DEPTH = 2
DEC_BATCH = 8
DEC_SEQ = 1
PAST_LEN = 16384
PAGE_SIZE = 128

HEAD_DIM = 128
ROPE_DIM = HEAD_DIM // 4
ROPE_THETA = 500000.0
NORM_EPS = 1e-5
NEG_INF = -1e30

GLA_WIDTH = D_MODEL // 4
GLA_HEADS = 4
GLA_DV = GLA_WIDTH // GLA_HEADS
GLA_DK = GLA_DV // 2
GLA_GATE_RANK = 16
GLA_TAU = 16.0
GLA_CHUNK = 64

NSA_HEADS = D_MODEL // 256
NSA_KV_HEADS = 4
NSA_GROUP = NSA_HEADS // NSA_KV_HEADS
NSA_WIDTH = NSA_HEADS * HEAD_DIM
NSA_KV_WIDTH = NSA_KV_HEADS * HEAD_DIM
CMP_STRIDE = 16
CMP_BLOCK = 2 * CMP_STRIDE
CMP_HIDDEN = HEAD_DIM
SEL_BLOCK = 64
SEL_TOP = 16
N_INIT_BLOCKS = 1
N_LOCAL_BLOCKS = 2
WINDOW = 512
SEL_Q_BLOCK = 32
WIN_Q_BLOCK = 128
FORCE_SCORE = 1e4

RWKV_WIDTH = D_MODEL // 4
RWKV_N = 64
RWKV_HEADS = RWKV_WIDTH // RWKV_N
RWKV_DECAY_RANK = max(32, int(round(1.8 * RWKV_WIDTH ** 0.5 / 32)) * 32)
RWKV_AAA_RANK = max(32, int(round(1.8 * RWKV_WIDTH ** 0.5 / 32)) * 32)
RWKV_GATE_RANK = max(32, int(round(0.6 * RWKV_WIDTH ** 0.8 / 32)) * 32)
RWKV_SIZES = (RWKV_WIDTH, RWKV_WIDTH, RWKV_WIDTH, RWKV_DECAY_RANK, RWKV_AAA_RANK, RWKV_GATE_RANK)
RWKV_COLS = sum(RWKV_SIZES)
RWKV_LN_EPS = 64e-5

N_BRANCH = 3
D_FF = 256 * ((8 * D_MODEL // 3 + 255) // 256)
CONV_W = 3

IN_SIZES = (GLA_HEADS * GLA_DK, GLA_HEADS * GLA_DK, GLA_WIDTH, GLA_WIDTH, GLA_GATE_RANK,
            NSA_WIDTH, NSA_KV_WIDTH, NSA_KV_WIDTH, NSA_KV_WIDTH, NSA_KV_WIDTH, NSA_KV_WIDTH, NSA_KV_WIDTH,
            NSA_HEADS * 3,
            RWKV_COLS,
            N_BRANCH * D_MODEL)
N_IN = sum(IN_SIZES)

kernel_name = 'hybrid_gla_nsa_rwkv7_convffn_step'


def rmsnorm(x, g):
    xf = x.astype(jnp.float32)
    y = xf * lax.rsqrt(jnp.mean(xf * xf, axis=-1, keepdims=True) + NORM_EPS)
    return (y * g.astype(jnp.float32)).astype(x.dtype)


def split_sizes(h, sizes):
    return jnp.split(h, [int(s) for s in np.cumsum(sizes)[:-1]], axis=-1)


def rope(x, pos):
    half = ROPE_DIM // 2
    inv = ROPE_THETA ** (-jnp.arange(half, dtype=jnp.float32) / half)
    ang = pos.astype(jnp.float32)[:, None] * inv[None, :]
    cos = jnp.cos(ang)[None, :, None, :]
    sin = jnp.sin(ang)[None, :, None, :]
    xr = x[..., :ROPE_DIM].astype(jnp.float32)
    x1, x2 = xr[..., :half], xr[..., half:]
    rot = jnp.concatenate([x1 * cos - x2 * sin, x2 * cos + x1 * sin], axis=-1).astype(x.dtype)
    return jnp.concatenate([rot, x[..., ROPE_DIM:]], axis=-1)


def gla_chunked(q, k, v, log_a, s0):
    B, T, H, DK = q.shape
    DV = v.shape[-1]
    C = GLA_CHUNK if T % GLA_CHUNK == 0 else T
    n = T // C

    def chunks(t):
        return t.astype(jnp.float32).reshape(B, n, C, H, t.shape[-1]).transpose(1, 0, 3, 2, 4)

    qc, kc, vc = chunks(q), chunks(k), chunks(v)
    bc = jnp.cumsum(chunks(log_a), axis=3)
    b_mid = bc[:, :, :, C // 2:C // 2 + 1]
    b_last = bc[:, :, :, C - 1:C]
    causal = jnp.tril(jnp.ones((C, C), dtype=bool))
    att = jnp.einsum('nbhid,nbhjd->nbhij', qc * jnp.exp(bc - b_mid), kc * jnp.exp(b_mid - bc))
    att = jnp.where(causal, att, 0.0)
    o_intra = jnp.einsum('nbhij,nbhjv->nbhiv', att, vc)
    q_dec = qc * jnp.exp(bc)
    k_dec = kc * jnp.exp(b_last - bc)
    a_last = jnp.exp(b_last[:, :, :, 0])

    def step(s, inp):
        qd, kd, vv, al = inp
        o = jnp.einsum('bhcd,bhdv->bhcv', qd, s)
        s = al[..., None] * s + jnp.einsum('bhcd,bhcv->bhdv', kd, vv)
        return s, o

    s_fin, o_inter = lax.scan(step, s0.astype(jnp.float32), (q_dec, k_dec, vc, a_last))
    o = (o_intra + o_inter).transpose(1, 0, 3, 2, 4).reshape(B, T, H, DV)
    return o, s_fin


def rwkv7_scan(r, w_log, k, v, kk, a, s0):
    def step(s, inp):
        rt, wt, kt, vt, kkt, at = inp
        sa = jnp.einsum('bhij,bhj->bhi', s, -kkt)
        s = (s * jnp.exp(wt)[:, :, None, :] + sa[..., None] * (kkt * at)[:, :, None, :]
             + vt[..., None] * kt[:, :, None, :])
        return s, jnp.einsum('bhij,bhj->bhi', s, rt)

    tf = lambda t: jnp.swapaxes(t, 0, 1)
    s_fin, y = lax.scan(step, s0, (tf(r), tf(w_log), tf(k), tf(v), tf(kk), tf(a)))
    return tf(y), s_fin


def compress_blocks(x, w1, w2, pe):
    B, L, G, HD = x.shape
    ns = L // CMP_STRIDE
    seg = x[:, :ns * CMP_STRIDE].reshape(B, ns, CMP_STRIDE, G, HD)
    first = jnp.einsum('bnpgd,pde->bnge', seg, w1[:CMP_STRIDE])
    second = jnp.einsum('bnpgd,pde->bnge', seg, w1[CMP_STRIDE:])
    pos_term = jnp.einsum('pd,pde->e', pe, w1)
    hid = jax.nn.gelu(first[:, :-1] + second[:, 1:] + pos_term)
    return jnp.einsum('bnge,ed->bngd', hid, w2)


def nsa_compressed_selected(q, k_c, v_c, k_s, v_s, gates, pos, w1k, w2k, pek, w1v, w2v, pev):
    B, T, H, HD = q.shape
    L = k_c.shape[1]
    G, R = NSA_KV_HEADS, NSA_GROUP
    scale = HEAD_DIM ** -0.5
    kcmp = compress_blocks(k_c, w1k, w2k, pek)
    vcmp = compress_blocks(v_c, w1v, w2v, pev)
    NC = kcmp.shape[1]
    cmp_end = jnp.arange(NC) * CMP_STRIDE + CMP_BLOCK - 1
    NS = -(-L // SEL_BLOCK)
    pad = NS * SEL_BLOCK - L
    to_blocks = lambda t: jnp.pad(t, ((0, 0), (0, pad), (0, 0), (0, 0))).reshape(
        B, NS, SEL_BLOCK, G, HD).transpose(0, 3, 1, 2, 4)
    kb, vb = to_blocks(k_s), to_blocks(v_s)
    ci = jnp.arange(NC)[:, None] * CMP_STRIDE
    sj = jnp.arange(NS)[None, :] * SEL_BLOCK
    cover = jnp.clip(jnp.minimum(ci + CMP_BLOCK, sj + SEL_BLOCK) - jnp.maximum(ci, sj), 0, None)
    cover = cover.astype(jnp.float32) / CMP_BLOCK
    n_top = min(SEL_TOP, NS)
    blk = jnp.arange(NS)
    bi = jnp.arange(B)[:, None, None, None]
    gi = jnp.arange(G)[None, None, :, None]
    off = jnp.arange(SEL_BLOCK)
    QC = SEL_Q_BLOCK if T % SEL_Q_BLOCK == 0 else T
    nq = T // QC

    def chunk_fn(args):
        qc, gc, pc = args
        qg = qc.reshape(B, QC, G, R, HD)
        s = jnp.einsum('bqgrd,bngd->bqgrn', qg, kcmp).astype(jnp.float32) * scale
        valid = (cmp_end[None, :] <= pc[:, None])[None, :, None, None, :]
        p = jnp.where(valid, jax.nn.softmax(jnp.where(valid, s, NEG_INF), axis=-1), 0.0)
        o_cmp = jnp.einsum('bqgrn,bngd->bqgrd', p.astype(vcmp.dtype), vcmp)
        imp = jnp.einsum('bqgrn,nj->bqgj', p, cover)
        qblk = (pc // SEL_BLOCK)[:, None]
        forced = (blk[None, :] < N_INIT_BLOCKS) | ((blk[None, :] <= qblk) & (blk[None, :] > qblk - N_LOCAL_BLOCKS))
        causal_blk = blk[None, :] <= qblk
        score = jnp.where(forced[None, :, None, :], FORCE_SCORE,
                          jnp.where(causal_blk[None, :, None, :], imp, NEG_INF))
        _, idx = lax.top_k(score, n_top)
        kg = kb[bi, gi, idx].reshape(B, QC, G, n_top * SEL_BLOCK, HD)
        vg = vb[bi, gi, idx].reshape(B, QC, G, n_top * SEL_BLOCK, HD)
        tok = (idx[..., None] * SEL_BLOCK + off).reshape(B, QC, G, n_top * SEL_BLOCK)
        ok = (tok <= pc[None, :, None, None])[:, :, :, None, :]
        s2 = jnp.einsum('bqgrd,bqgkd->bqgrk', qg, kg).astype(jnp.float32) * scale
        p2 = jax.nn.softmax(jnp.where(ok, s2, NEG_INF), axis=-1)
        o_slc = jnp.einsum('bqgrk,bqgkd->bqgrd', p2.astype(vg.dtype), vg)
        return (gc[..., 0:1] * o_cmp.reshape(B, QC, H, HD)
                + gc[..., 1:2] * o_slc.reshape(B, QC, H, HD))

    xs = (q.reshape(B, nq, QC, H, HD).swapaxes(0, 1),
          gates.reshape(B, nq, QC, H, 3).swapaxes(0, 1),
          pos.reshape(nq, QC))
    o = lax.map(chunk_fn, xs)
    return o.swapaxes(0, 1).reshape(B, T, H, HD)


def window_attn(q, k_pad, v_pad, q_pos0):
    B, T, H, HD = q.shape
    G, R = NSA_KV_HEADS, NSA_GROUP
    QB = WIN_Q_BLOCK if T % WIN_Q_BLOCK == 0 else T
    nb = T // QB
    kidx = jnp.arange(nb)[:, None] * QB + jnp.arange(WINDOW + QB)[None, :]
    kb, vb = k_pad[:, kidx], v_pad[:, kidx]
    qb = q.reshape(B, nb, QB, G, R, HD)
    qpos = q_pos0 + jnp.arange(T).reshape(nb, QB)
    kpos = q_pos0 - WINDOW + kidx
    rel = qpos[:, :, None] - kpos[:, None, :]
    ok = (rel >= 0) & (rel < WINDOW) & (kpos[:, None, :] >= 0)
    s = jnp.einsum('bcqgrd,bckgd->bcgrqk', qb, kb).astype(jnp.float32) * HEAD_DIM ** -0.5
    p = jax.nn.softmax(jnp.where(ok[None, :, None, None], s, NEG_INF), axis=-1)
    o = jnp.einsum('bcgrqk,bckgd->bcqgrd', p.astype(vb.dtype), vb)
    return o.reshape(B, T, H, HD)


def trunk_layer(x, pos0, past_cmp, past_slc, win_buf, gla_s0, rwkv_s0, shift0, conv0, lw):
    B, T, _ = x.shape
    dt = x.dtype
    f32 = jnp.float32
    pos = pos0 + jnp.arange(T, dtype=jnp.int32)
    heads = lambda t, n: t.reshape(B, T, n, t.shape[-1] // n)

    xn = rmsnorm(x, lw['norm1'])
    (gq, gk, gv, gog, glo, nq, nkc, nvc, nks, nvs, nkw, nvw, ng, rw, mg) = split_sizes(xn @ lw['w_in'], IN_SIZES)

    log_a = jax.nn.log_sigmoid((glo @ lw['gla_wa2'] + lw['gla_ba']).astype(f32)) / GLA_TAU
    o_gla, gla_s = gla_chunked(heads(gq, GLA_HEADS) * GLA_DK ** -0.5, heads(gk, GLA_HEADS),
                               heads(gv, GLA_HEADS), heads(log_a, GLA_HEADS), gla_s0)
    o_gla = rmsnorm(o_gla, lw['gla_norm']) * jax.nn.silu(heads(gog, GLA_HEADS).astype(f32))
    y_gla = o_gla.reshape(B, T, GLA_WIDTH).astype(dt) @ lw['w_o_gla']

    kvh = lambda t: heads(t, NSA_KV_HEADS)
    qn = rope(heads(nq, NSA_HEADS), pos)
    new_cmp = jnp.stack([rope(kvh(nkc), pos), kvh(nvc)], axis=2)
    new_slc = jnp.stack([rope(kvh(nks), pos), kvh(nvs)], axis=2)
    new_win = jnp.stack([rope(kvh(nkw), pos), kvh(nvw)], axis=2)
    cmp_all = jnp.concatenate([past_cmp.astype(dt), new_cmp], axis=1)
    slc_all = jnp.concatenate([past_slc.astype(dt), new_slc], axis=1)
    win_all = jnp.concatenate([win_buf.astype(dt), new_win], axis=1)
    n_win = win_all.shape[1]
    win_pad = jnp.pad(win_all, ((0, 0), (WINDOW + T - n_win, 0), (0, 0), (0, 0), (0, 0)))
    g_nsa = jax.nn.sigmoid(ng.astype(f32)).reshape(B, T, NSA_HEADS, 3)
    o_nsa = nsa_compressed_selected(qn, cmp_all[:, :, 0], cmp_all[:, :, 1], slc_all[:, :, 0], slc_all[:, :, 1],
                                    g_nsa, pos, lw['cmp_w1k'], lw['cmp_w2k'], lw['cmp_pek'],
                                    lw['cmp_w1v'], lw['cmp_w2v'], lw['cmp_pev'])
    o_nsa = o_nsa + g_nsa[..., 2:3] * window_attn(qn, win_pad[:, :, 0], win_pad[:, :, 1], pos0)
    y_nsa = o_nsa.reshape(B, T, NSA_WIDTH).astype(dt) @ lw['w_o_nsa']
    win_new = win_all[:, n_win - min(WINDOW, n_win):]

    prev = jnp.concatenate([shift0[:, None].astype(dt), rw[:, :-1]], axis=1)
    rm = (rw + (prev - rw) * lw['rwkv_mu']).astype(f32)
    r_, k_, v_, w_lo, a_lo, g_lo = split_sizes(rm, RWKV_SIZES)
    w_raw = lw['rwkv_w0'] + jnp.tanh(w_lo) @ lw['rwkv_w2']
    w_log = -jnp.exp(-jax.nn.softplus(-w_raw) - 0.5)
    a = jax.nn.sigmoid(lw['rwkv_a0'] + a_lo @ lw['rwkv_a2'])
    gate = jax.nn.sigmoid(g_lo) @ lw['rwkv_g2']
    rh = lambda t: heads(t, RWKV_HEADS)
    kk = rh(k_ * lw['rwkv_kk'])
    kk = kk * lax.rsqrt(jnp.maximum(jnp.sum(kk * kk, axis=-1, keepdims=True), 1e-24))
    k2 = k_ * (1.0 + (a - 1.0) * lw['rwkv_ka'])
    y, rwkv_s = rwkv7_scan(rh(r_), rh(w_log), rh(k2), rh(v_), kk, rh(a), rwkv_s0.astype(f32))
    y_mean = jnp.mean(y, axis=-1, keepdims=True)
    y_var = jnp.mean(jnp.square(y - y_mean), axis=-1, keepdims=True)
    yn = ((y - y_mean) * lax.rsqrt(y_var + RWKV_LN_EPS)).reshape(B, T, RWKV_WIDTH) * lw['rwkv_ln_w'] + lw['rwkv_ln_b']
    bonus = jnp.sum(rh(r_) * rh(k2) * lw['rwkv_rk'], axis=-1, keepdims=True) * rh(v_)
    o_rwkv = (yn + bonus.reshape(B, T, RWKV_WIDTH)) * gate
    y_rwkv = o_rwkv.astype(dt) @ lw['w_o_rwkv']
    shift_new = rw[:, -1]

    gm = jax.nn.sigmoid(mg.astype(f32)).reshape(B, T, N_BRANCH, D_MODEL)
    merged = gm[:, :, 0] * y_gla + gm[:, :, 1] * y_nsa + gm[:, :, 2] * y_rwkv
    x = x + merged.astype(dt) @ lw['w_out']

    xn2 = rmsnorm(x, lw['norm2'])
    hp = jnp.concatenate([conv0.astype(dt), xn2 @ lw['ffn_gate']], axis=1)
    cw = lw['ffn_conv']
    hc = lw['ffn_conv_b'] + sum(hp[:, i:i + T] * cw[i] for i in range(CONV_W))
    x = x + (jax.nn.silu(hc) * (xn2 @ lw['ffn_up'])) @ lw['ffn_down']
    conv_new = hp[:, T:]
    return x, (new_cmp, new_slc, win_new, gla_s, rwkv_s, shift_new, conv_new)


def setup_inputs(seed: int = 0) -> dict:
    key = jax.random.key(seed)
    keys = jax.random.split(key, 64)
    counter = [0]

    def nk():
        counter[0] += 1
        return keys[counter[0] - 1]

    def nrm(shape, scale):
        return scale * jax.random.normal(nk(), shape, jnp.float32)

    f32 = jnp.float32
    n_pages = PAST_LEN // PAGE_SIZE
    n_used = DEC_BATCH * n_pages
    n_phys = n_used + max(1, n_used // 4)
    win_buf = min(WINDOW, PAST_LEN)
    G, HD = NSA_KV_HEADS, HEAD_DIM
    ones_n = lambda shape: 1.0 + nrm(shape, 0.01)

    inp = {}
    inp['x_prompt'] = nrm((BATCH, SEQ, D_MODEL), 1.0)
    inp['x_sample'] = nrm((DEC_BATCH, DEC_SEQ, D_MODEL), 1.0)
    inp['cache_cmp_kv'] = nrm((DEPTH, n_phys, PAGE_SIZE, 2, G, HD), 1.0)
    inp['cache_slc_kv'] = nrm((DEPTH, n_phys, PAGE_SIZE, 2, G, HD), 1.0)
    inp['cache_win_kv'] = nrm((DEPTH, DEC_BATCH, win_buf, 2, G, HD), 1.0)
    inp['state_gla'] = nrm((DEPTH, DEC_BATCH, GLA_HEADS, GLA_DK, GLA_DV), 0.5)
    inp['state_rwkv'] = nrm((DEPTH, DEC_BATCH, RWKV_HEADS, RWKV_N, RWKV_N), 0.3)
    inp['state_rwkv_shift'] = nrm((DEPTH, DEC_BATCH, RWKV_COLS), 1.0)
    inp['state_ffn_conv'] = nrm((DEPTH, DEC_BATCH, CONV_W - 1, D_FF), 1.0)
    inp['page_table'] = jax.random.permutation(nk(), n_phys)[:n_used].reshape(DEC_BATCH, n_pages).astype(jnp.int32)

    inp['norm1'] = ones_n((DEPTH, D_MODEL))
    inp['w_in'] = nrm((DEPTH, D_MODEL, N_IN), D_MODEL ** -0.5)
    inp['gla_wa2'] = nrm((DEPTH, GLA_GATE_RANK, GLA_HEADS * GLA_DK), GLA_GATE_RANK ** -0.5)
    inp['gla_ba'] = nrm((DEPTH, GLA_HEADS * GLA_DK), 0.1)
    inp['gla_norm'] = ones_n((DEPTH, GLA_DV))
    inp['w_o_gla'] = nrm((DEPTH, GLA_WIDTH, D_MODEL), GLA_WIDTH ** -0.5)
    inp['cmp_w1k'] = nrm((DEPTH, CMP_BLOCK, HD, CMP_HIDDEN), (CMP_BLOCK * HD) ** -0.5)
    inp['cmp_w2k'] = nrm((DEPTH, CMP_HIDDEN, HD), CMP_HIDDEN ** -0.5)
    inp['cmp_pek'] = nrm((DEPTH, CMP_BLOCK, HD), 0.5)
    inp['cmp_w1v'] = nrm((DEPTH, CMP_BLOCK, HD, CMP_HIDDEN), (CMP_BLOCK * HD) ** -0.5)
    inp['cmp_w2v'] = nrm((DEPTH, CMP_HIDDEN, HD), CMP_HIDDEN ** -0.5)
    inp['cmp_pev'] = nrm((DEPTH, CMP_BLOCK, HD), 0.5)
    inp['w_o_nsa'] = nrm((DEPTH, NSA_WIDTH, D_MODEL), NSA_WIDTH ** -0.5)
    inp['rwkv_mu'] = jax.random.uniform(nk(), (DEPTH, RWKV_COLS), f32, 0.05, 0.95)
    ramp = jnp.arange(RWKV_WIDTH, dtype=f32) / (RWKV_WIDTH - 1)
    inp['rwkv_w0'] = (-6.0 + 5.0 * ramp ** 0.9 + 0.5)[None, :] + nrm((DEPTH, RWKV_WIDTH), 0.05)
    inp['rwkv_w2'] = nrm((DEPTH, RWKV_DECAY_RANK, RWKV_WIDTH), 0.1 * RWKV_DECAY_RANK ** -0.5)
    inp['rwkv_a0'] = nrm((DEPTH, RWKV_WIDTH), 0.1)
    inp['rwkv_a2'] = nrm((DEPTH, RWKV_AAA_RANK, RWKV_WIDTH), 0.5 * RWKV_AAA_RANK ** -0.5)
    inp['rwkv_g2'] = nrm((DEPTH, RWKV_GATE_RANK, RWKV_WIDTH), RWKV_GATE_RANK ** -0.5)
    inp['rwkv_kk'] = 0.85 + nrm((DEPTH, RWKV_WIDTH), 0.02)
    inp['rwkv_ka'] = 1.0 + nrm((DEPTH, RWKV_WIDTH), 0.02)
    inp['rwkv_rk'] = nrm((DEPTH, RWKV_HEADS, RWKV_N), 0.1)
    inp['rwkv_ln_w'] = ones_n((DEPTH, RWKV_WIDTH))
    inp['rwkv_ln_b'] = nrm((DEPTH, RWKV_WIDTH), 0.01)
    inp['w_o_rwkv'] = nrm((DEPTH, RWKV_WIDTH, D_MODEL), RWKV_WIDTH ** -0.5)
    inp['w_out'] = nrm((DEPTH, D_MODEL, D_MODEL), D_MODEL ** -0.5)
    inp['norm2'] = ones_n((DEPTH, D_MODEL))
    inp['ffn_gate'] = nrm((DEPTH, D_MODEL, D_FF), D_MODEL ** -0.5)
    inp['ffn_conv'] = nrm((DEPTH, CONV_W, D_FF), 0.5)
    inp['ffn_conv_b'] = nrm((DEPTH, D_FF), 0.01)
    inp['ffn_up'] = nrm((DEPTH, D_MODEL, D_FF), D_MODEL ** -0.5)
    inp['ffn_down'] = nrm((DEPTH, D_FF, D_MODEL), D_FF ** -0.5)
    inp['norm_f'] = ones_n((D_MODEL,))
    return inp


def reference(x_prompt, x_sample, cache_cmp_kv, cache_slc_kv, cache_win_kv, state_gla, state_rwkv,
              state_rwkv_shift, state_ffn_conv, page_table, norm1, w_in, gla_wa2, gla_ba, gla_norm, w_o_gla,
              cmp_w1k, cmp_w2k, cmp_pek, cmp_w1v, cmp_w2v, cmp_pev, w_o_nsa, rwkv_mu, rwkv_w0, rwkv_w2,
              rwkv_a0, rwkv_a2, rwkv_g2, rwkv_kk, rwkv_ka, rwkv_rk, rwkv_ln_w, rwkv_ln_b, w_o_rwkv, w_out,
              norm2, ffn_gate, ffn_conv, ffn_conv_b, ffn_up, ffn_down, norm_f):
    G, HD = NSA_KV_HEADS, HEAD_DIM
    n_db, n_pages = page_table.shape
    past_len = n_pages * PAGE_SIZE
    bp = x_prompt.shape[0]
    dt = x_prompt.dtype
    xp, xs = x_prompt, x_sample
    st_p, st_s = [], []
    for l in range(DEPTH):
        lw = {'norm1': norm1[l], 'w_in': w_in[l], 'gla_wa2': gla_wa2[l], 'gla_ba': gla_ba[l],
              'gla_norm': gla_norm[l], 'w_o_gla': w_o_gla[l], 'cmp_w1k': cmp_w1k[l], 'cmp_w2k': cmp_w2k[l],
              'cmp_pek': cmp_pek[l], 'cmp_w1v': cmp_w1v[l], 'cmp_w2v': cmp_w2v[l], 'cmp_pev': cmp_pev[l],
              'w_o_nsa': w_o_nsa[l], 'rwkv_mu': rwkv_mu[l], 'rwkv_w0': rwkv_w0[l], 'rwkv_w2': rwkv_w2[l],
              'rwkv_a0': rwkv_a0[l], 'rwkv_a2': rwkv_a2[l], 'rwkv_g2': rwkv_g2[l], 'rwkv_kk': rwkv_kk[l],
              'rwkv_ka': rwkv_ka[l], 'rwkv_rk': rwkv_rk[l], 'rwkv_ln_w': rwkv_ln_w[l], 'rwkv_ln_b': rwkv_ln_b[l],
              'w_o_rwkv': w_o_rwkv[l], 'w_out': w_out[l], 'norm2': norm2[l], 'ffn_gate': ffn_gate[l],
              'ffn_conv': ffn_conv[l], 'ffn_conv_b': ffn_conv_b[l], 'ffn_up': ffn_up[l], 'ffn_down': ffn_down[l]}
        empty = jnp.zeros((bp, 0, 2, G, HD), dt)
        xp, sp = trunk_layer(xp, 0, empty, empty, empty,
                             jnp.zeros((bp, GLA_HEADS, GLA_DK, GLA_DV), jnp.float32),
                             jnp.zeros((bp, RWKV_HEADS, RWKV_N, RWKV_N), jnp.float32),
                             jnp.zeros((bp, RWKV_COLS), dt),
                             jnp.zeros((bp, CONV_W - 1, D_FF), dt), lw)
        past_c = cache_cmp_kv[l, page_table].reshape(n_db, past_len, 2, G, HD)
        past_s = cache_slc_kv[l, page_table].reshape(n_db, past_len, 2, G, HD)
        xs, ss = trunk_layer(xs, past_len, past_c, past_s, cache_win_kv[l], state_gla[l], state_rwkv[l],
                             state_rwkv_shift[l], state_ffn_conv[l], lw)
        st_p.append(sp)
        st_s.append(ss)
    y_prompt = rmsnorm(xp, norm_f)
    y_sample = rmsnorm(xs, norm_f)
    cmp_p = jnp.stack([s[0] for s in st_p])
    cmp_s = jnp.stack([s[0] for s in st_s])
    slc_p = jnp.stack([s[1] for s in st_p])
    slc_s = jnp.stack([s[1] for s in st_s])
    win_p = jnp.stack([s[2] for s in st_p])
    win_s = jnp.stack([s[2] for s in st_s])
    gla_p = jnp.stack([s[3] for s in st_p])
    gla_s = jnp.stack([s[3] for s in st_s])
    rwkv_p = jnp.stack([s[4] for s in st_p])
    rwkv_s = jnp.stack([s[4] for s in st_s])
    shift_p = jnp.stack([s[5] for s in st_p])
    shift_s = jnp.stack([s[5] for s in st_s])
    conv_p = jnp.stack([s[6] for s in st_p])
    conv_s = jnp.stack([s[6] for s in st_s])
    return (y_prompt, y_sample, cmp_p, cmp_s, slc_p, slc_s, win_p, win_s, gla_p, gla_s,
            rwkv_p, rwkv_s, shift_p, shift_s, conv_p, conv_s)
```

```python
import functools

import jax
import jax.numpy as jnp
import numpy as np
from jax import lax
from jax.experimental import pallas as pl
from jax.experimental.pallas import tpu as pltpu

D_MODEL = 4096
DEPTH = 2
PAGE_SIZE = 128
HEAD_DIM = 128
ROPE_DIM = HEAD_DIM // 4
ROPE_THETA = 500000.0
NORM_EPS = 1e-5
NEG_INF = -1e30

GLA_WIDTH = D_MODEL // 4
GLA_HEADS = 4
GLA_DV = GLA_WIDTH // GLA_HEADS
GLA_DK = GLA_DV // 2
GLA_GATE_RANK = 16
GLA_TAU = 16.0
GLA_CHUNK = 64

NSA_HEADS = D_MODEL // 256
NSA_KV_HEADS = 4
NSA_GROUP = NSA_HEADS // NSA_KV_HEADS
NSA_WIDTH = NSA_HEADS * HEAD_DIM
NSA_KV_WIDTH = NSA_KV_HEADS * HEAD_DIM
CMP_STRIDE = 16
CMP_BLOCK = 2 * CMP_STRIDE
SEL_BLOCK = 64
SEL_TOP = 16
N_INIT_BLOCKS = 1
N_LOCAL_BLOCKS = 2
WINDOW = 512
SEL_Q_BLOCK = 32
WIN_Q_BLOCK = 128
FORCE_SCORE = 1e4

RWKV_WIDTH = D_MODEL // 4
RWKV_N = 64
RWKV_HEADS = RWKV_WIDTH // RWKV_N
RWKV_DECAY_RANK = 64
RWKV_AAA_RANK = 64
RWKV_GATE_RANK = 160
RWKV_SIZES = (RWKV_WIDTH, RWKV_WIDTH, RWKV_WIDTH, RWKV_DECAY_RANK, RWKV_AAA_RANK, RWKV_GATE_RANK)
RWKV_COLS = sum(RWKV_SIZES)
RWKV_LN_EPS = 64e-5

N_BRANCH = 3
D_FF = 256 * ((8 * D_MODEL // 3 + 255) // 256)
CONV_W = 3

IN_SIZES = (GLA_HEADS * GLA_DK, GLA_HEADS * GLA_DK, GLA_WIDTH, GLA_WIDTH, GLA_GATE_RANK,
            NSA_WIDTH, NSA_KV_WIDTH, NSA_KV_WIDTH, NSA_KV_WIDTH, NSA_KV_WIDTH, NSA_KV_WIDTH, NSA_KV_WIDTH,
            NSA_HEADS * 3,
            RWKV_COLS,
            N_BRANCH * D_MODEL)

LANE = 128
VMEM_LIMIT = 48 * 1024 * 1024


def _round_up(n, m):
    return -(-n // m) * m


_SEG_PAD = tuple(_round_up(s, LANE) for s in IN_SIZES)
_SEG_OFF = tuple(int(o) for o in np.concatenate([[0], np.cumsum(_SEG_PAD)[:-1]]))
N_IN_PAD = _round_up(sum(_SEG_PAD), 2048)


def _mm_kernel(x_ref, w_ref, o_ref, acc_ref, *, nk):
    k = pl.program_id(2)
    part = jnp.dot(x_ref[...], w_ref[...], preferred_element_type=jnp.float32)
    if nk == 1:
        o_ref[...] = part.astype(o_ref.dtype)
    else:
        @pl.when(k == 0)
        def _():
            acc_ref[...] = part

        @pl.when(jnp.logical_and(k > 0, k < nk - 1))
        def _():
            acc_ref[...] += part

        @pl.when(k == nk - 1)
        def _():
            o_ref[...] = (acc_ref[...] + part).astype(o_ref.dtype)


def _pick(n, cands):
    for c in cands:
        if n % c == 0:
            return c
    return n


def mm(x, w, out_dtype=jnp.float32):
    M, K = x.shape
    N = w.shape[1]
    tm = _pick(M, (1024, 512, 256, 128, 64, 32, 16, 8))
    tn = _pick(N, (512, 256, 128))
    tk = K if K <= 4096 else _pick(K, (5504, 4096, 2048, 1024, 512))
    nk = K // tk
    return pl.pallas_call(
        functools.partial(_mm_kernel, nk=nk),
        grid=(M // tm, N // tn, nk),
        in_specs=[pl.BlockSpec((tm, tk), lambda i, j, k: (i, k)),
                  pl.BlockSpec((tk, tn), lambda i, j, k: (k, j))],
        out_specs=pl.BlockSpec((tm, tn), lambda i, j, k: (i, j)),
        out_shape=jax.ShapeDtypeStruct((M, N), out_dtype),
        scratch_shapes=[pltpu.VMEM((tm, tn) if nk > 1 else (8, LANE), jnp.float32)],
        compiler_params=pltpu.CompilerParams(
            dimension_semantics=("parallel", "parallel", "arbitrary"),
            vmem_limit_bytes=VMEM_LIMIT),
        name="dense_mm",
    )(x, w)


def dense(x, w_bf16):
    lead = x.shape[:-1]
    y = mm(x.reshape(-1, x.shape[-1]).astype(jnp.bfloat16), w_bf16)
    return y.reshape(*lead, w_bf16.shape[1])


def rmsnorm(x, g):
    xf = x.astype(jnp.float32)
    y = xf * lax.rsqrt(jnp.mean(xf * xf, axis=-1, keepdims=True) + NORM_EPS)
    return (y * g.astype(jnp.float32)).astype(x.dtype)


def split_sizes(h, sizes):
    return jnp.split(h, [int(s) for s in np.cumsum(sizes)[:-1]], axis=-1)


def rope(x, pos):
    half = ROPE_DIM // 2
    inv = ROPE_THETA ** (-jnp.arange(half, dtype=jnp.float32) / half)
    ang = pos.astype(jnp.float32)[:, None] * inv[None, :]
    cos = jnp.cos(ang)[None, :, None, :]
    sin = jnp.sin(ang)[None, :, None, :]
    xr = x[..., :ROPE_DIM].astype(jnp.float32)
    x1, x2 = xr[..., :half], xr[..., half:]
    rot = jnp.concatenate([x1 * cos - x2 * sin, x2 * cos + x1 * sin], axis=-1).astype(x.dtype)
    return jnp.concatenate([rot, x[..., ROPE_DIM:]], axis=-1)


def gla_chunked(q, k, v, log_a, s0):
    B, T, H, DK = q.shape
    DV = v.shape[-1]
    C = GLA_CHUNK if T % GLA_CHUNK == 0 else T
    n = T // C

    def chunks(t):
        return t.astype(jnp.float32).reshape(B, n, C, H, t.shape[-1]).transpose(1, 0, 3, 2, 4)

    qc, kc, vc = chunks(q), chunks(k), chunks(v)
    bc = jnp.cumsum(chunks(log_a), axis=3)
    b_mid = bc[:, :, :, C // 2:C // 2 + 1]
    b_last = bc[:, :, :, C - 1:C]
    causal = jnp.tril(jnp.ones((C, C), dtype=bool))
    att = jnp.einsum('nbhid,nbhjd->nbhij', qc * jnp.exp(bc - b_mid), kc * jnp.exp(b_mid - bc))
    att = jnp.where(causal, att, 0.0)
    o_intra = jnp.einsum('nbhij,nbhjv->nbhiv', att, vc)
    q_dec = qc * jnp.exp(bc)
    k_dec = kc * jnp.exp(b_last - bc)
    a_last = jnp.exp(b_last[:, :, :, 0])

    def step(s, inp):
        qd, kd, vv, al = inp
        o = jnp.einsum('bhcd,bhdv->bhcv', qd, s)
        s = al[..., None] * s + jnp.einsum('bhcd,bhcv->bhdv', kd, vv)
        return s, o

    s_fin, o_inter = lax.scan(step, s0.astype(jnp.float32), (q_dec, k_dec, vc, a_last))
    o = (o_intra + o_inter).transpose(1, 0, 3, 2, 4).reshape(B, T, H, DV)
    return o, s_fin


def rwkv7_scan(r, w_log, k, v, kk, a, s0):
    def step(s, inp):
        rt, wt, kt, vt, kkt, at = inp
        sa = jnp.einsum('bhij,bhj->bhi', s, -kkt)
        s = (s * jnp.exp(wt)[:, :, None, :] + sa[..., None] * (kkt * at)[:, :, None, :]
             + vt[..., None] * kt[:, :, None, :])
        return s, jnp.einsum('bhij,bhj->bhi', s, rt)

    tf = lambda t: jnp.swapaxes(t, 0, 1)
    s_fin, y = lax.scan(step, s0, (tf(r), tf(w_log), tf(k), tf(v), tf(kk), tf(a)))
    return tf(y), s_fin


def compress_blocks(x, w1, w2, pe):
    B, L, G, HD = x.shape
    ns = L // CMP_STRIDE
    seg = x[:, :ns * CMP_STRIDE].reshape(B, ns, CMP_STRIDE, G, HD)
    first = jnp.einsum('bnpgd,pde->bnge', seg, w1[:CMP_STRIDE])
    second = jnp.einsum('bnpgd,pde->bnge', seg, w1[CMP_STRIDE:])
    pos_term = jnp.einsum('pd,pde->e', pe, w1)
    hid = jax.nn.gelu(first[:, :-1] + second[:, 1:] + pos_term)
    return jnp.einsum('bnge,ed->bngd', hid, w2)


def nsa_compressed_selected(q, k_c, v_c, k_s, v_s, gates, pos, w1k, w2k, pek, w1v, w2v, pev):
    B, T, H, HD = q.shape
    L = k_c.shape[1]
    G, R = NSA_KV_HEADS, NSA_GROUP
    scale = HEAD_DIM ** -0.5
    kcmp = compress_blocks(k_c, w1k, w2k, pek)
    vcmp = compress_blocks(v_c, w1v, w2v, pev)
    NC = kcmp.shape[1]
    cmp_end = jnp.arange(NC) * CMP_STRIDE + CMP_BLOCK - 1
    NS = -(-L // SEL_BLOCK)
    pad = NS * SEL_BLOCK - L
    to_blocks = lambda t: jnp.pad(t, ((0, 0), (0, pad), (0, 0), (0, 0))).reshape(
        B, NS, SEL_BLOCK, G, HD).transpose(0, 3, 1, 2, 4)
    kb, vb = to_blocks(k_s), to_blocks(v_s)
    ci = jnp.arange(NC)[:, None] * CMP_STRIDE
    sj = jnp.arange(NS)[None, :] * SEL_BLOCK
    cover = jnp.clip(jnp.minimum(ci + CMP_BLOCK, sj + SEL_BLOCK) - jnp.maximum(ci, sj), 0, None)
    cover = cover.astype(jnp.float32) / CMP_BLOCK
    n_top = min(SEL_TOP, NS)
    blk = jnp.arange(NS)
    bi = jnp.arange(B)[:, None, None, None]
    gi = jnp.arange(G)[None, None, :, None]
    off = jnp.arange(SEL_BLOCK)
    QC = SEL_Q_BLOCK if T % SEL_Q_BLOCK == 0 else T
    nq = T // QC

    def chunk_fn(args):
        qc, gc, pc = args
        qg = qc.reshape(B, QC, G, R, HD)
        s = jnp.einsum('bqgrd,bngd->bqgrn', qg, kcmp).astype(jnp.float32) * scale
        valid = (cmp_end[None, :] <= pc[:, None])[None, :, None, None, :]
        p = jnp.where(valid, jax.nn.softmax(jnp.where(valid, s, NEG_INF), axis=-1), 0.0)
        o_cmp = jnp.einsum('bqgrn,bngd->bqgrd', p.astype(vcmp.dtype), vcmp)
        imp = jnp.einsum('bqgrn,nj->bqgj', p, cover)
        qblk = (pc // SEL_BLOCK)[:, None]
        forced = (blk[None, :] < N_INIT_BLOCKS) | ((blk[None, :] <= qblk) & (blk[None, :] > qblk - N_LOCAL_BLOCKS))
        causal_blk = blk[None, :] <= qblk
        score = jnp.where(forced[None, :, None, :], FORCE_SCORE,
                          jnp.where(causal_blk[None, :, None, :], imp, NEG_INF))
        _, idx = lax.top_k(score, n_top)
        kg = kb[bi, gi, idx].reshape(B, QC, G, n_top * SEL_BLOCK, HD)
        vg = vb[bi, gi, idx].reshape(B, QC, G, n_top * SEL_BLOCK, HD)
        tok = (idx[..., None] * SEL_BLOCK + off).reshape(B, QC, G, n_top * SEL_BLOCK)
        ok = (tok <= pc[None, :, None, None])[:, :, :, None, :]
        s2 = jnp.einsum('bqgrd,bqgkd->bqgrk', qg, kg).astype(jnp.float32) * scale
        p2 = jax.nn.softmax(jnp.where(ok, s2, NEG_INF), axis=-1)
        o_slc = jnp.einsum('bqgrk,bqgkd->bqgrd', p2.astype(vg.dtype), vg)
        return (gc[..., 0:1] * o_cmp.reshape(B, QC, H, HD)
                + gc[..., 1:2] * o_slc.reshape(B, QC, H, HD))

    xs = (q.reshape(B, nq, QC, H, HD).swapaxes(0, 1),
          gates.reshape(B, nq, QC, H, 3).swapaxes(0, 1),
          pos.reshape(nq, QC))
    o = lax.map(chunk_fn, xs)
    return o.swapaxes(0, 1).reshape(B, T, H, HD)


def window_attn(q, k_pad, v_pad, q_pos0):
    B, T, H, HD = q.shape
    G, R = NSA_KV_HEADS, NSA_GROUP
    QB = WIN_Q_BLOCK if T % WIN_Q_BLOCK == 0 else T
    nb = T // QB
    kidx = jnp.arange(nb)[:, None] * QB + jnp.arange(WINDOW + QB)[None, :]
    kb, vb = k_pad[:, kidx], v_pad[:, kidx]
    qb = q.reshape(B, nb, QB, G, R, HD)
    qpos = q_pos0 + jnp.arange(T).reshape(nb, QB)
    kpos = q_pos0 - WINDOW + kidx
    rel = qpos[:, :, None] - kpos[:, None, :]
    ok = (rel >= 0) & (rel < WINDOW) & (kpos[:, None, :] >= 0)
    s = jnp.einsum('bcqgrd,bckgd->bcgrqk', qb, kb).astype(jnp.float32) * HEAD_DIM ** -0.5
    p = jax.nn.softmax(jnp.where(ok[None, :, None, None], s, NEG_INF), axis=-1)
    o = jnp.einsum('bcgrqk,bckgd->bcqgrd', p.astype(vb.dtype), vb)
    return o.reshape(B, T, H, HD)


def trunk_layer(x, pos0, past_cmp, past_slc, win_buf, gla_s0, rwkv_s0, shift0, conv0, lw):
    B, T, _ = x.shape
    dt = x.dtype
    f32 = jnp.float32
    pos = pos0 + jnp.arange(T, dtype=jnp.int32)
    heads = lambda t, n: t.reshape(B, T, n, t.shape[-1] // n)

    xn = rmsnorm(x, lw['norm1'])
    hp_ = dense(xn, lw['w_in'])
    (gq, gk, gv, gog, glo, nq, nkc, nvc, nks, nvs, nkw, nvw, ng, rw, mg) = [
        hp_[..., o:o + s] for o, s in zip(_SEG_OFF, IN_SIZES)]

    log_a = jax.nn.log_sigmoid((glo @ lw['gla_wa2'] + lw['gla_ba']).astype(f32)) / GLA_TAU
    o_gla, gla_s = gla_chunked(heads(gq, GLA_HEADS) * GLA_DK ** -0.5, heads(gk, GLA_HEADS),
                               heads(gv, GLA_HEADS), heads(log_a, GLA_HEADS), gla_s0)
    o_gla = rmsnorm(o_gla, lw['gla_norm']) * jax.nn.silu(heads(gog, GLA_HEADS).astype(f32))
    y_gla = dense(o_gla.reshape(B, T, GLA_WIDTH), lw['w_o_gla'])

    kvh = lambda t: heads(t, NSA_KV_HEADS)
    qn = rope(heads(nq, NSA_HEADS), pos)
    new_cmp = jnp.stack([rope(kvh(nkc), pos), kvh(nvc)], axis=2)
    new_slc = jnp.stack([rope(kvh(nks), pos), kvh(nvs)], axis=2)
    new_win = jnp.stack([rope(kvh(nkw), pos), kvh(nvw)], axis=2)
    cmp_all = jnp.concatenate([past_cmp.astype(dt), new_cmp], axis=1)
    slc_all = jnp.concatenate([past_slc.astype(dt), new_slc], axis=1)
    win_all = jnp.concatenate([win_buf.astype(dt), new_win], axis=1)
    n_win = win_all.shape[1]
    win_pad = jnp.pad(win_all, ((0, 0), (WINDOW + T - n_win, 0), (0, 0), (0, 0), (0, 0)))
    g_nsa = jax.nn.sigmoid(ng.astype(f32)).reshape(B, T, NSA_HEADS, 3)
    o_nsa = nsa_compressed_selected(qn, cmp_all[:, :, 0], cmp_all[:, :, 1], slc_all[:, :, 0], slc_all[:, :, 1],
                                    g_nsa, pos, lw['cmp_w1k'], lw['cmp_w2k'], lw['cmp_pek'],
                                    lw['cmp_w1v'], lw['cmp_w2v'], lw['cmp_pev'])
    o_nsa = o_nsa + g_nsa[..., 2:3] * window_attn(qn, win_pad[:, :, 0], win_pad[:, :, 1], pos0)
    y_nsa = dense(o_nsa.reshape(B, T, NSA_WIDTH), lw['w_o_nsa'])
    win_new = win_all[:, n_win - min(WINDOW, n_win):]

    prev = jnp.concatenate([shift0[:, None].astype(dt), rw[:, :-1]], axis=1)
    rm = (rw + (prev - rw) * lw['rwkv_mu']).astype(f32)
    r_, k_, v_, w_lo, a_lo, g_lo = split_sizes(rm, RWKV_SIZES)
    w_raw = lw['rwkv_w0'] + jnp.tanh(w_lo) @ lw['rwkv_w2']
    w_log = -jnp.exp(-jax.nn.softplus(-w_raw) - 0.5)
    a = jax.nn.sigmoid(lw['rwkv_a0'] + a_lo @ lw['rwkv_a2'])
    gate = jax.nn.sigmoid(g_lo) @ lw['rwkv_g2']
    rh = lambda t: heads(t, RWKV_HEADS)
    kk = rh(k_ * lw['rwkv_kk'])
    kk = kk * lax.rsqrt(jnp.maximum(jnp.sum(kk * kk, axis=-1, keepdims=True), 1e-24))
    k2 = k_ * (1.0 + (a - 1.0) * lw['rwkv_ka'])
    y, rwkv_s = rwkv7_scan(rh(r_), rh(w_log), rh(k2), rh(v_), kk, rh(a), rwkv_s0.astype(f32))
    y_mean = jnp.mean(y, axis=-1, keepdims=True)
    y_var = jnp.mean(jnp.square(y - y_mean), axis=-1, keepdims=True)
    yn = ((y - y_mean) * lax.rsqrt(y_var + RWKV_LN_EPS)).reshape(B, T, RWKV_WIDTH) * lw['rwkv_ln_w'] + lw['rwkv_ln_b']
    bonus = jnp.sum(rh(r_) * rh(k2) * lw['rwkv_rk'], axis=-1, keepdims=True) * rh(v_)
    o_rwkv = (yn + bonus.reshape(B, T, RWKV_WIDTH)) * gate
    y_rwkv = dense(o_rwkv, lw['w_o_rwkv'])
    shift_new = rw[:, -1]

    gm = jax.nn.sigmoid(mg.astype(f32)).reshape(B, T, N_BRANCH, D_MODEL)
    merged = gm[:, :, 0] * y_gla + gm[:, :, 1] * y_nsa + gm[:, :, 2] * y_rwkv
    x = x + dense(merged, lw['w_out'])

    xn2 = rmsnorm(x, lw['norm2'])
    hp = jnp.concatenate([conv0.astype(dt), dense(xn2, lw['ffn_gate'])], axis=1)
    cw = lw['ffn_conv']
    hc = lw['ffn_conv_b'] + sum(hp[:, i:i + T] * cw[i] for i in range(CONV_W))
    x = x + dense(jax.nn.silu(hc) * dense(xn2, lw['ffn_up']), lw['ffn_down'])
    conv_new = hp[:, T:]
    return x, (new_cmp, new_slc, win_new, gla_s, rwkv_s, shift_new, conv_new)


def _repack_w_in(w):
    parts = []
    off = 0
    for s, p in zip(IN_SIZES, _SEG_PAD):
        seg = w[:, off:off + s].astype(jnp.bfloat16)
        parts.append(jnp.pad(seg, ((0, 0), (0, p - s))) if p != s else seg)
        off += s
    tail = N_IN_PAD - sum(_SEG_PAD)
    if tail:
        parts.append(jnp.zeros((w.shape[0], tail), jnp.bfloat16))
    return jnp.concatenate(parts, axis=1)


def kernel(x_prompt, x_sample, cache_cmp_kv, cache_slc_kv, cache_win_kv, state_gla, state_rwkv, state_rwkv_shift, state_ffn_conv, page_table, norm1, w_in, gla_wa2, gla_ba, gla_norm, w_o_gla, cmp_w1k, cmp_w2k, cmp_pek, cmp_w1v, cmp_w2v, cmp_pev, w_o_nsa, rwkv_mu, rwkv_w0, rwkv_w2, rwkv_a0, rwkv_a2, rwkv_g2, rwkv_kk, rwkv_ka, rwkv_rk, rwkv_ln_w, rwkv_ln_b, w_o_rwkv, w_out, norm2, ffn_gate, ffn_conv, ffn_conv_b, ffn_up, ffn_down, norm_f):
    G, HD = NSA_KV_HEADS, HEAD_DIM
    n_db, n_pages = page_table.shape
    past_len = n_pages * PAGE_SIZE
    bp = x_prompt.shape[0]
    dt = x_prompt.dtype
    bf = jnp.bfloat16
    xp, xs = x_prompt, x_sample
    st_p, st_s = [], []
    for l in range(DEPTH):
        lw = {'norm1': norm1[l], 'w_in': _repack_w_in(w_in[l]), 'gla_wa2': gla_wa2[l], 'gla_ba': gla_ba[l],
              'gla_norm': gla_norm[l], 'w_o_gla': w_o_gla[l].astype(bf), 'cmp_w1k': cmp_w1k[l], 'cmp_w2k': cmp_w2k[l],
              'cmp_pek': cmp_pek[l], 'cmp_w1v': cmp_w1v[l], 'cmp_w2v': cmp_w2v[l], 'cmp_pev': cmp_pev[l],
              'w_o_nsa': w_o_nsa[l].astype(bf), 'rwkv_mu': rwkv_mu[l], 'rwkv_w0': rwkv_w0[l], 'rwkv_w2': rwkv_w2[l],
              'rwkv_a0': rwkv_a0[l], 'rwkv_a2': rwkv_a2[l], 'rwkv_g2': rwkv_g2[l], 'rwkv_kk': rwkv_kk[l],
              'rwkv_ka': rwkv_ka[l], 'rwkv_rk': rwkv_rk[l], 'rwkv_ln_w': rwkv_ln_w[l], 'rwkv_ln_b': rwkv_ln_b[l],
              'w_o_rwkv': w_o_rwkv[l].astype(bf), 'w_out': w_out[l].astype(bf), 'norm2': norm2[l],
              'ffn_gate': ffn_gate[l].astype(bf), 'ffn_conv': ffn_conv[l], 'ffn_conv_b': ffn_conv_b[l],
              'ffn_up': ffn_up[l].astype(bf), 'ffn_down': ffn_down[l].astype(bf)}
        empty = jnp.zeros((bp, 0, 2, G, HD), dt)
        xp, sp = trunk_layer(xp, 0, empty, empty, empty,
                             jnp.zeros((bp, GLA_HEADS, GLA_DK, GLA_DV), jnp.float32),
                             jnp.zeros((bp, RWKV_HEADS, RWKV_N, RWKV_N), jnp.float32),
                             jnp.zeros((bp, RWKV_COLS), dt),
                             jnp.zeros((bp, CONV_W - 1, D_FF), dt), lw)
        past_c = cache_cmp_kv[l, page_table].reshape(n_db, past_len, 2, G, HD)
        past_s = cache_slc_kv[l, page_table].reshape(n_db, past_len, 2, G, HD)
        xs, ss = trunk_layer(xs, past_len, past_c, past_s, cache_win_kv[l], state_gla[l], state_rwkv[l],
                             state_rwkv_shift[l], state_ffn_conv[l], lw)
        st_p.append(sp)
        st_s.append(ss)
    y_prompt = rmsnorm(xp, norm_f)
    y_sample = rmsnorm(xs, norm_f)
    outs = [y_prompt, y_sample]
    for i in range(7):
        outs.append(jnp.stack([s[i] for s in st_p]))
        outs.append(jnp.stack([s[i] for s in st_s]))
    return tuple(outs)
```

```python
import functools

import jax
import jax.numpy as jnp
import numpy as np
from jax import lax
from jax.experimental import pallas as pl
from jax.experimental.pallas import tpu as pltpu

D_MODEL = 4096
DEPTH = 2
PAGE_SIZE = 128
HEAD_DIM = 128
ROPE_DIM = HEAD_DIM // 4
ROPE_THETA = 500000.0
NORM_EPS = 1e-5
NEG_INF = -1e30

GLA_WIDTH = D_MODEL // 4
GLA_HEADS = 4
GLA_DV = GLA_WIDTH // GLA_HEADS
GLA_DK = GLA_DV // 2
GLA_GATE_RANK = 16
GLA_TAU = 16.0
GLA_CHUNK = 64

NSA_HEADS = D_MODEL // 256
NSA_KV_HEADS = 4
NSA_GROUP = NSA_HEADS // NSA_KV_HEADS
NSA_WIDTH = NSA_HEADS * HEAD_DIM
NSA_KV_WIDTH = NSA_KV_HEADS * HEAD_DIM
CMP_STRIDE = 16
CMP_BLOCK = 2 * CMP_STRIDE
SEL_BLOCK = 64
SEL_TOP = 16
N_INIT_BLOCKS = 1
N_LOCAL_BLOCKS = 2
WINDOW = 512
SEL_Q_BLOCK = 32
WIN_Q_BLOCK = 128
FORCE_SCORE = 1e4

RWKV_WIDTH = D_MODEL // 4
RWKV_N = 64
RWKV_HEADS = RWKV_WIDTH // RWKV_N
RWKV_DECAY_RANK = 64
RWKV_AAA_RANK = 64
RWKV_GATE_RANK = 160
RWKV_SIZES = (RWKV_WIDTH, RWKV_WIDTH, RWKV_WIDTH, RWKV_DECAY_RANK, RWKV_AAA_RANK, RWKV_GATE_RANK)
RWKV_COLS = sum(RWKV_SIZES)
RWKV_LN_EPS = 64e-5

N_BRANCH = 3
D_FF = 256 * ((8 * D_MODEL // 3 + 255) // 256)
CONV_W = 3

IN_SIZES = (GLA_HEADS * GLA_DK, GLA_HEADS * GLA_DK, GLA_WIDTH, GLA_WIDTH, GLA_GATE_RANK,
            NSA_WIDTH, NSA_KV_WIDTH, NSA_KV_WIDTH, NSA_KV_WIDTH, NSA_KV_WIDTH, NSA_KV_WIDTH, NSA_KV_WIDTH,
            NSA_HEADS * 3,
            RWKV_COLS,
            N_BRANCH * D_MODEL)

LANE = 128
SUBLANE = 8
NSA_TQ = 256
VMEM_LIMIT = 48 * 1024 * 1024


def _round_up(n, m):
    return -(-n // m) * m


_SEG_PAD = tuple(_round_up(s, LANE) for s in IN_SIZES)
_SEG_OFF = tuple(int(o) for o in np.concatenate([[0], np.cumsum(_SEG_PAD)[:-1]]))
N_IN_PAD = _round_up(sum(_SEG_PAD), 2048)


def _mm_kernel(x_ref, w_ref, o_ref, acc_ref, *, nk):
    k = pl.program_id(2)
    part = jnp.dot(x_ref[...], w_ref[...], preferred_element_type=jnp.float32)
    if nk == 1:
        o_ref[...] = part.astype(o_ref.dtype)
    else:
        @pl.when(k == 0)
        def _():
            acc_ref[...] = part

        @pl.when(jnp.logical_and(k > 0, k < nk - 1))
        def _():
            acc_ref[...] += part

        @pl.when(k == nk - 1)
        def _():
            o_ref[...] = (acc_ref[...] + part).astype(o_ref.dtype)


def _pick(n, cands):
    for c in cands:
        if n % c == 0:
            return c
    return n


def mm(x, w, out_dtype=jnp.float32):
    M, K = x.shape
    N = w.shape[1]
    tm = _pick(M, (1024, 512, 256, 128, 64, 32, 16, 8))
    tn = _pick(N, (512, 256, 128))
    tk = K if K <= 4096 else _pick(K, (5504, 4096, 2048, 1024, 512))
    nk = K // tk
    return pl.pallas_call(
        functools.partial(_mm_kernel, nk=nk),
        grid=(M // tm, N // tn, nk),
        in_specs=[pl.BlockSpec((tm, tk), lambda i, j, k: (i, k)),
                  pl.BlockSpec((tk, tn), lambda i, j, k: (k, j))],
        out_specs=pl.BlockSpec((tm, tn), lambda i, j, k: (i, j)),
        out_shape=jax.ShapeDtypeStruct((M, N), out_dtype),
        scratch_shapes=[pltpu.VMEM((tm, tn) if nk > 1 else (8, LANE), jnp.float32)],
        compiler_params=pltpu.CompilerParams(
            dimension_semantics=("parallel", "parallel", "arbitrary"),
            vmem_limit_bytes=VMEM_LIMIT),
        name="dense_mm",
    )(x, w)


def dense(x, w_bf16):
    lead = x.shape[:-1]
    y = mm(x.reshape(-1, x.shape[-1]).astype(jnp.bfloat16), w_bf16)
    return y.reshape(*lead, w_bf16.shape[1])


def _rwkv_kernel(r_ref, wl_ref, k_ref, v_ref, kk_ref, a_ref, s0_ref, y_ref, sout_ref, s_scr, *, NP, Tc):
    c = pl.program_id(1)
    f32, bf16 = jnp.float32, jnp.bfloat16
    U = min(SUBLANE, Tc)

    @pl.when(c == 0)
    def _():
        s_scr[...] = s0_ref[...]

    sub = lax.broadcasted_iota(jnp.int32, (RWKV_N, LANE), 0)
    lane = lax.broadcasted_iota(jnp.int32, (RWKV_N, LANE), 1)
    eye2 = (lane % RWKV_N) == sub
    rr = lax.broadcasted_iota(jnp.int32, (LANE, LANE), 0) // RWKV_N
    cc = lax.broadcasted_iota(jnp.int32, (LANE, LANE), 1) // RWKV_N
    bd = jnp.where(rr == cc, 1.0, 0.0).astype(bf16)

    def ssb(p):
        hi = p.astype(bf16)
        lo = (p - hi.astype(f32)).astype(bf16)
        return (jnp.dot(hi, bd, preferred_element_type=f32)
                + jnp.dot(lo, bd, preferred_element_type=f32))

    eye_all = jnp.concatenate([eye2] * NP, axis=0)

    def bcast(x8, s):
        return jnp.concatenate(
            [jnp.broadcast_to(x8[s:s + 1, p * LANE:(p + 1) * LANE], (RWKV_N, LANE)) for p in range(NP)], axis=0)

    def body(g, carry):
        rows = pl.ds(pl.multiple_of(g * U, U), U)
        r8 = r_ref[rows, :]
        w8 = jnp.exp(wl_ref[rows, :])
        k8 = k_ref[rows, :]
        v8 = v_ref[rows, :]
        kk8 = kk_ref[rows, :]
        ka8 = kk8 * a_ref[rows, :]
        nk8 = -kk8
        S = s_scr[...]
        ys = []
        for s in range(U):
            sa = ssb(S * bcast(nk8, s))
            vb = ssb(jnp.where(eye_all, bcast(v8, s), 0.0))
            S = S * bcast(w8, s) + sa * bcast(ka8, s) + vb * bcast(k8, s)
            yb = jnp.where(eye_all, ssb(S * bcast(r8, s)), 0.0)
            ys.append(jnp.concatenate(
                [jnp.sum(yb[p * RWKV_N:(p + 1) * RWKV_N], axis=0, keepdims=True) for p in range(NP)], axis=1))
        s_scr[...] = S
        y_ref[rows, :] = ys[0] if U == 1 else jnp.concatenate(ys, axis=0)
        return carry

    lax.fori_loop(0, Tc // U, body, 0)

    @pl.when(c == pl.num_programs(1) - 1)
    def _():
        sout_ref[...] = s_scr[...]


def rwkv_scan_pallas(r, w_log, k, v, kk, a, s0):
    B, T, W = r.shape
    H = W // RWKV_N
    NP = H // 2
    Tc = 128 if T % 128 == 0 else T
    s0p = s0.reshape(B, NP, 2, RWKV_N, RWKV_N).transpose(0, 1, 3, 2, 4).reshape(B, NP * RWKV_N, LANE)
    blk = pl.BlockSpec((None, Tc, W), lambda b, c: (b, c, 0))
    sblk = pl.BlockSpec((None, NP * RWKV_N, LANE), lambda b, c: (b, 0, 0))
    y, sp = pl.pallas_call(
        functools.partial(_rwkv_kernel, NP=NP, Tc=Tc),
        grid=(B, T // Tc),
        in_specs=[blk] * 6 + [sblk],
        out_specs=[blk, sblk],
        out_shape=[jax.ShapeDtypeStruct((B, T, W), jnp.float32),
                   jax.ShapeDtypeStruct((B, NP * RWKV_N, LANE), jnp.float32)],
        scratch_shapes=[pltpu.VMEM((NP * RWKV_N, LANE), jnp.float32)],
        compiler_params=pltpu.CompilerParams(dimension_semantics=("parallel", "arbitrary"),
                                             vmem_limit_bytes=VMEM_LIMIT),
        name="rwkv7_scan",
    )(r, w_log, k, v, kk, a, s0p)
    s_fin = sp.reshape(B, NP, RWKV_N, 2, RWKV_N).transpose(0, 1, 3, 2, 4).reshape(B, H, RWKV_N, RWKV_N)
    return y, s_fin


def _dot_nt(a, b):
    return lax.dot_general(a, b, (((1,), (1,)), ((), ())), preferred_element_type=jnp.float32)


def _nsa_kernel(q_ref, kc_ref, vc_ref, ks_ref, vs_ref, kw_ref, vw_ref, g_ref, covt_ref, e_ref, o_ref,
                bias_scr, p4_scr, ocmp_scr, m_scr, l_scr, acc_scr, *, TQ, T, NS, NCP, n_top):
    f32, bf16 = jnp.float32, jnp.bfloat16
    R = NSA_GROUP
    i = pl.program_id(2)
    nchunk = T // TQ
    qpos_col = i * TQ + lax.broadcasted_iota(jnp.int32, (TQ, 1), 0)

    kc = kc_ref[...]
    vc = vc_ref[...]
    cend = lax.broadcasted_iota(jnp.int32, (1, NCP), 1) * CMP_STRIDE + (CMP_BLOCK - 1)
    valid = cend <= qpos_col
    for r in range(R):
        s = _dot_nt(q_ref[:, r * HEAD_DIM:(r + 1) * HEAD_DIM], kc)
        s = jnp.where(valid, s, NEG_INF)
        m = jnp.max(s, axis=-1, keepdims=True)
        p = jnp.where(valid, jnp.exp(s - m), 0.0)
        l = jnp.sum(p, axis=-1, keepdims=True)
        p = (p / jnp.where(l > 0.0, l, 1.0)).astype(bf16)
        p4_scr[:, r * NCP:(r + 1) * NCP] = p
        ocmp_scr[r] = jnp.dot(p, vc, preferred_element_type=f32)

    imp_t = _dot_nt(covt_ref[...], p4_scr[...])
    j = lax.broadcasted_iota(jnp.int32, (NS, TQ), 0)
    qblk = (i * TQ + lax.broadcasted_iota(jnp.int32, (NS, TQ), 1)) // SEL_BLOCK
    forced = (j < N_INIT_BLOCKS) | ((j <= qblk) & (j > qblk - N_LOCAL_BLOCKS))
    score = jnp.where(forced, FORCE_SCORE, jnp.where(j <= qblk, imp_t, NEG_INF))
    rank = jnp.zeros((NS, TQ), f32)
    for a in range(NS):
        row = score[a:a + 1, :]
        beats = (row > score) | ((row == score) & (a < j))
        rank = rank + jnp.where(beats, 1.0, 0.0)
    sel_t = jnp.where(rank < n_top, 1.0, 0.0)
    if NS < LANE:
        sel_t = jnp.concatenate([sel_t, jnp.zeros((LANE - NS, TQ), f32)], axis=0)
    sel = sel_t.T.astype(bf16)
    for c in range(nchunk):
        selexp = jnp.dot(sel, e_ref[:, c * TQ:(c + 1) * TQ], preferred_element_type=f32)
        kpos = c * TQ + lax.broadcasted_iota(jnp.int32, (TQ, TQ), 1)
        bias_scr[c] = jnp.where((selexp > 0.5) & (kpos <= qpos_col), 0.0, NEG_INF)

    def attend(k_ref, v_ref, lo, masker):
        m_scr[...] = jnp.full(m_scr.shape, NEG_INF, f32)
        l_scr[...] = jnp.zeros(l_scr.shape, f32)
        acc_scr[...] = jnp.zeros(acc_scr.shape, f32)

        def chunk(c, carry):
            rows = pl.ds(pl.multiple_of(c * TQ, TQ), TQ)
            k = k_ref[rows, :]
            v = v_ref[rows, :]
            mk = masker(c)
            for r in range(R):
                s = _dot_nt(q_ref[:, r * HEAD_DIM:(r + 1) * HEAD_DIM], k)
                s, keep = mk(s)
                m_prev = m_scr[r]
                m_new = jnp.maximum(m_prev, jnp.max(s, axis=-1, keepdims=True))
                alpha = jnp.exp(m_prev - m_new)
                p = jnp.exp(s - m_new)
                if keep is not None:
                    p = jnp.where(keep, p, 0.0)
                l_scr[r] = alpha * l_scr[r] + jnp.sum(p, axis=-1, keepdims=True)
                acc_scr[r] = alpha * acc_scr[r] + jnp.dot(p.astype(bf16), v, preferred_element_type=f32)
                m_scr[r] = m_new
            return carry

        lax.fori_loop(lo, i + 1, chunk, 0)

    def sel_masker(c):
        b = bias_scr[c]
        return lambda s: (s + b, None)

    attend(ks_ref, vs_ref, 0, sel_masker)
    g = g_ref[...]
    for r in range(R):
        ocmp_scr[r] = (g[:, 3 * r:3 * r + 1] * ocmp_scr[r]
                       + g[:, 3 * r + 1:3 * r + 2] * (acc_scr[r] / l_scr[r]))

    def win_masker(c):
        rel = ((i - c) * TQ + lax.broadcasted_iota(jnp.int32, (TQ, TQ), 0)
               - lax.broadcasted_iota(jnp.int32, (TQ, TQ), 1))
        ok = (rel >= 0) & (rel < WINDOW)
        return lambda s: (jnp.where(ok, s, NEG_INF), ok)

    attend(kw_ref, vw_ref, jnp.maximum(i - (WINDOW // TQ), 0), win_masker)
    for r in range(R):
        o = ocmp_scr[r] + g[:, 3 * r + 2:3 * r + 3] * (acc_scr[r] / l_scr[r])
        o_ref[:, r * HEAD_DIM:(r + 1) * HEAD_DIM] = o.astype(o_ref.dtype)


def nsa_attention_pallas(qr, kcmp, vcmp, ks, vs, kw, vw, gates):
    B, T, _ = qr.shape
    G, R = NSA_KV_HEADS, NSA_GROUP
    TQ = min(NSA_TQ, T)
    NS = T // SEL_BLOCK
    NC = T // CMP_STRIDE - 1
    NCP = kcmp.shape[2]
    n_top = min(SEL_TOP, NS)
    ci = np.arange(NCP)[:, None] * CMP_STRIDE
    sj = np.arange(NS)[None, :] * SEL_BLOCK
    cover = np.clip(np.minimum(ci + CMP_BLOCK, sj + SEL_BLOCK) - np.maximum(ci, sj), 0, None) / CMP_BLOCK
    cover[NC:] = 0.0
    covt = jnp.asarray(np.tile(cover.T, (1, R)), jnp.bfloat16)
    e = jnp.asarray((np.arange(T)[None, :] // SEL_BLOCK) == np.arange(LANE)[:, None], jnp.bfloat16)
    kv_spec = pl.BlockSpec((None, T, HEAD_DIM), lambda b, g, i: (b, 0, g))
    cmp_spec = pl.BlockSpec((None, None, NCP, HEAD_DIM), lambda b, g, i: (b, g, 0, 0))
    return pl.pallas_call(
        functools.partial(_nsa_kernel, TQ=TQ, T=T, NS=NS, NCP=NCP, n_top=n_top),
        grid=(B, G, T // TQ),
        in_specs=[pl.BlockSpec((None, TQ, R * HEAD_DIM), lambda b, g, i: (b, i, g)),
                  cmp_spec, cmp_spec, kv_spec, kv_spec, kv_spec, kv_spec,
                  pl.BlockSpec((None, None, TQ, 3 * R), lambda b, g, i: (b, g, i, 0)),
                  pl.BlockSpec((NS, R * NCP), lambda b, g, i: (0, 0)),
                  pl.BlockSpec((LANE, T), lambda b, g, i: (0, 0))],
        out_specs=pl.BlockSpec((None, TQ, R * HEAD_DIM), lambda b, g, i: (b, i, g)),
        out_shape=jax.ShapeDtypeStruct((B, T, G * R * HEAD_DIM), jnp.bfloat16),
        scratch_shapes=[pltpu.VMEM((T // TQ, TQ, TQ), jnp.float32),
                        pltpu.VMEM((TQ, R * NCP), jnp.bfloat16),
                        pltpu.VMEM((R, TQ, HEAD_DIM), jnp.float32),
                        pltpu.VMEM((R, TQ, 1), jnp.float32),
                        pltpu.VMEM((R, TQ, 1), jnp.float32),
                        pltpu.VMEM((R, TQ, HEAD_DIM), jnp.float32)],
        compiler_params=pltpu.CompilerParams(dimension_semantics=("parallel", "parallel", "arbitrary"),
                                             vmem_limit_bytes=VMEM_LIMIT),
        name="nsa_attention",
    )(qr, kcmp, vcmp, ks, vs, kw, vw, gates, covt, e)


def rmsnorm(x, g):
    xf = x.astype(jnp.float32)
    y = xf * lax.rsqrt(jnp.mean(xf * xf, axis=-1, keepdims=True) + NORM_EPS)
    return (y * g.astype(jnp.float32)).astype(x.dtype)


def split_sizes(h, sizes):
    return jnp.split(h, [int(s) for s in np.cumsum(sizes)[:-1]], axis=-1)


def rope(x, pos):
    half = ROPE_DIM // 2
    inv = ROPE_THETA ** (-jnp.arange(half, dtype=jnp.float32) / half)
    ang = pos.astype(jnp.float32)[:, None] * inv[None, :]
    cos = jnp.cos(ang)[None, :, None, :]
    sin = jnp.sin(ang)[None, :, None, :]
    xr = x[..., :ROPE_DIM].astype(jnp.float32)
    x1, x2 = xr[..., :half], xr[..., half:]
    rot = jnp.concatenate([x1 * cos - x2 * sin, x2 * cos + x1 * sin], axis=-1).astype(x.dtype)
    return jnp.concatenate([rot, x[..., ROPE_DIM:]], axis=-1)


def gla_chunked(q, k, v, log_a, s0):
    B, T, H, DK = q.shape
    DV = v.shape[-1]
    C = GLA_CHUNK if T % GLA_CHUNK == 0 else T
    n = T // C

    def chunks(t):
        return t.astype(jnp.float32).reshape(B, n, C, H, t.shape[-1]).transpose(1, 0, 3, 2, 4)

    qc, kc, vc = chunks(q), chunks(k), chunks(v)
    bc = jnp.cumsum(chunks(log_a), axis=3)
    b_mid = bc[:, :, :, C // 2:C // 2 + 1]
    b_last = bc[:, :, :, C - 1:C]
    causal = jnp.tril(jnp.ones((C, C), dtype=bool))
    att = jnp.einsum('nbhid,nbhjd->nbhij', qc * jnp.exp(bc - b_mid), kc * jnp.exp(b_mid - bc))
    att = jnp.where(causal, att, 0.0)
    o_intra = jnp.einsum('nbhij,nbhjv->nbhiv', att, vc)
    q_dec = qc * jnp.exp(bc)
    k_dec = kc * jnp.exp(b_last - bc)
    a_last = jnp.exp(b_last[:, :, :, 0])

    def step(s, inp):
        qd, kd, vv, al = inp
        o = jnp.einsum('bhcd,bhdv->bhcv', qd, s)
        s = al[..., None] * s + jnp.einsum('bhcd,bhcv->bhdv', kd, vv)
        return s, o

    s_fin, o_inter = lax.scan(step, s0.astype(jnp.float32), (q_dec, k_dec, vc, a_last))
    o = (o_intra + o_inter).transpose(1, 0, 3, 2, 4).reshape(B, T, H, DV)
    return o, s_fin


def compress_blocks(x, w1, w2, pe):
    B, L, G, HD = x.shape
    ns = L // CMP_STRIDE
    seg = x[:, :ns * CMP_STRIDE].reshape(B, ns, CMP_STRIDE, G, HD)
    first = jnp.einsum('bnpgd,pde->bnge', seg, w1[:CMP_STRIDE])
    second = jnp.einsum('bnpgd,pde->bnge', seg, w1[CMP_STRIDE:])
    pos_term = jnp.einsum('pd,pde->e', pe, w1)
    hid = jax.nn.gelu(first[:, :-1] + second[:, 1:] + pos_term)
    return jnp.einsum('bnge,ed->bngd', hid, w2)


def nsa_compressed_selected(q, k_c, v_c, k_s, v_s, gates, pos, w1k, w2k, pek, w1v, w2v, pev):
    B, T, H, HD = q.shape
    L = k_c.shape[1]
    G, R = NSA_KV_HEADS, NSA_GROUP
    scale = HEAD_DIM ** -0.5
    kcmp = compress_blocks(k_c, w1k, w2k, pek)
    vcmp = compress_blocks(v_c, w1v, w2v, pev)
    NC = kcmp.shape[1]
    cmp_end = jnp.arange(NC) * CMP_STRIDE + CMP_BLOCK - 1
    NS = -(-L // SEL_BLOCK)
    pad = NS * SEL_BLOCK - L
    to_blocks = lambda t: jnp.pad(t, ((0, 0), (0, pad), (0, 0), (0, 0))).reshape(
        B, NS, SEL_BLOCK, G, HD).transpose(0, 3, 1, 2, 4)
    kb, vb = to_blocks(k_s), to_blocks(v_s)
    ci = jnp.arange(NC)[:, None] * CMP_STRIDE
    sj = jnp.arange(NS)[None, :] * SEL_BLOCK
    cover = jnp.clip(jnp.minimum(ci + CMP_BLOCK, sj + SEL_BLOCK) - jnp.maximum(ci, sj), 0, None)
    cover = cover.astype(jnp.float32) / CMP_BLOCK
    n_top = min(SEL_TOP, NS)
    blk = jnp.arange(NS)
    bi = jnp.arange(B)[:, None, None, None]
    gi = jnp.arange(G)[None, None, :, None]
    off = jnp.arange(SEL_BLOCK)
    QC = SEL_Q_BLOCK if T % SEL_Q_BLOCK == 0 else T
    nq = T // QC

    def chunk_fn(args):
        qc, gc, pc = args
        qg = qc.reshape(B, QC, G, R, HD)
        s = jnp.einsum('bqgrd,bngd->bqgrn', qg, kcmp).astype(jnp.float32) * scale
        valid = (cmp_end[None, :] <= pc[:, None])[None, :, None, None, :]
        p = jnp.where(valid, jax.nn.softmax(jnp.where(valid, s, NEG_INF), axis=-1), 0.0)
        o_cmp = jnp.einsum('bqgrn,bngd->bqgrd', p.astype(vcmp.dtype), vcmp)
        imp = jnp.einsum('bqgrn,nj->bqgj', p, cover)
        qblk = (pc // SEL_BLOCK)[:, None]
        forced = (blk[None, :] < N_INIT_BLOCKS) | ((blk[None, :] <= qblk) & (blk[None, :] > qblk - N_LOCAL_BLOCKS))
        causal_blk = blk[None, :] <= qblk
        score = jnp.where(forced[None, :, None, :], FORCE_SCORE,
                          jnp.where(causal_blk[None, :, None, :], imp, NEG_INF))
        _, idx = lax.top_k(score, n_top)
        kg = kb[bi, gi, idx].reshape(B, QC, G, n_top * SEL_BLOCK, HD)
        vg = vb[bi, gi, idx].reshape(B, QC, G, n_top * SEL_BLOCK, HD)
        tok = (idx[..., None] * SEL_BLOCK + off).reshape(B, QC, G, n_top * SEL_BLOCK)
        ok = (tok <= pc[None, :, None, None])[:, :, :, None, :]
        s2 = jnp.einsum('bqgrd,bqgkd->bqgrk', qg, kg).astype(jnp.float32) * scale
        p2 = jax.nn.softmax(jnp.where(ok, s2, NEG_INF), axis=-1)
        o_slc = jnp.einsum('bqgrk,bqgkd->bqgrd', p2.astype(vg.dtype), vg)
        return (gc[..., 0:1] * o_cmp.reshape(B, QC, H, HD)
                + gc[..., 1:2] * o_slc.reshape(B, QC, H, HD))

    xs = (q.reshape(B, nq, QC, H, HD).swapaxes(0, 1),
          gates.reshape(B, nq, QC, H, 3).swapaxes(0, 1),
          pos.reshape(nq, QC))
    o = lax.map(chunk_fn, xs)
    return o.swapaxes(0, 1).reshape(B, T, H, HD)


def window_attn(q, k_pad, v_pad, q_pos0):
    B, T, H, HD = q.shape
    G, R = NSA_KV_HEADS, NSA_GROUP
    QB = WIN_Q_BLOCK if T % WIN_Q_BLOCK == 0 else T
    nb = T // QB
    kidx = jnp.arange(nb)[:, None] * QB + jnp.arange(WINDOW + QB)[None, :]
    kb, vb = k_pad[:, kidx], v_pad[:, kidx]
    qb = q.reshape(B, nb, QB, G, R, HD)
    qpos = q_pos0 + jnp.arange(T).reshape(nb, QB)
    kpos = q_pos0 - WINDOW + kidx
    rel = qpos[:, :, None] - kpos[:, None, :]
    ok = (rel >= 0) & (rel < WINDOW) & (kpos[:, None, :] >= 0)
    s = jnp.einsum('bcqgrd,bckgd->bcgrqk', qb, kb).astype(jnp.float32) * HEAD_DIM ** -0.5
    p = jax.nn.softmax(jnp.where(ok[None, :, None, None], s, NEG_INF), axis=-1)
    o = jnp.einsum('bcgrqk,bckgd->bcqgrd', p.astype(vb.dtype), vb)
    return o.reshape(B, T, H, HD)


def trunk_layer(x, pos0, past_cmp, past_slc, win_buf, gla_s0, rwkv_s0, shift0, conv0, lw):
    B, T, _ = x.shape
    dt = x.dtype
    f32 = jnp.float32
    pos = pos0 + jnp.arange(T, dtype=jnp.int32)
    heads = lambda t, n: t.reshape(B, T, n, t.shape[-1] // n)

    xn = rmsnorm(x, lw['norm1'])
    hp_ = dense(xn, lw['w_in'])
    (gq, gk, gv, gog, glo, nq, nkc, nvc, nks, nvs, nkw, nvw, ng, rw, mg) = [
        hp_[..., o:o + s] for o, s in zip(_SEG_OFF, IN_SIZES)]

    log_a = jax.nn.log_sigmoid((glo @ lw['gla_wa2'] + lw['gla_ba']).astype(f32)) / GLA_TAU
    o_gla, gla_s = gla_chunked(heads(gq, GLA_HEADS) * GLA_DK ** -0.5, heads(gk, GLA_HEADS),
                               heads(gv, GLA_HEADS), heads(log_a, GLA_HEADS), gla_s0)
    o_gla = rmsnorm(o_gla, lw['gla_norm']) * jax.nn.silu(heads(gog, GLA_HEADS).astype(f32))
    y_gla = dense(o_gla.reshape(B, T, GLA_WIDTH), lw['w_o_gla'])

    kvh = lambda t: heads(t, NSA_KV_HEADS)
    qn = rope(heads(nq, NSA_HEADS), pos)
    new_cmp = jnp.stack([rope(kvh(nkc), pos), kvh(nvc)], axis=2)
    new_slc = jnp.stack([rope(kvh(nks), pos), kvh(nvs)], axis=2)
    new_win = jnp.stack([rope(kvh(nkw), pos), kvh(nvw)], axis=2)
    cmp_all = jnp.concatenate([past_cmp.astype(dt), new_cmp], axis=1)
    slc_all = jnp.concatenate([past_slc.astype(dt), new_slc], axis=1)
    win_all = jnp.concatenate([win_buf.astype(dt), new_win], axis=1)
    n_win = win_all.shape[1]
    win_pad = jnp.pad(win_all, ((0, 0), (WINDOW + T - n_win, 0), (0, 0), (0, 0), (0, 0)))
    g_nsa = jax.nn.sigmoid(ng.astype(f32)).reshape(B, T, NSA_HEADS, 3)
    if past_cmp.shape[1] == 0 and T % NSA_TQ == 0:
        bf = jnp.bfloat16
        ncp = _round_up(T // CMP_STRIDE, LANE)
        padc = lambda t: jnp.pad(t.transpose(0, 2, 1, 3),
                                 ((0, 0), (0, 0), (0, ncp - t.shape[1]), (0, 0))).astype(bf)
        kcmp = padc(compress_blocks(new_cmp[:, :, 0], lw['cmp_w1k'], lw['cmp_w2k'], lw['cmp_pek']))
        vcmp = padc(compress_blocks(new_cmp[:, :, 1], lw['cmp_w1v'], lw['cmp_w2v'], lw['cmp_pev']))
        flat = lambda t: t.reshape(B, T, NSA_KV_WIDTH).astype(bf)
        o_nsa = nsa_attention_pallas(
            (qn * HEAD_DIM ** -0.5).reshape(B, T, NSA_WIDTH).astype(bf), kcmp, vcmp,
            flat(new_slc[:, :, 0]), flat(new_slc[:, :, 1]), flat(new_win[:, :, 0]), flat(new_win[:, :, 1]),
            g_nsa.reshape(B, T, NSA_KV_HEADS, 3 * NSA_GROUP).transpose(0, 2, 1, 3))
        y_nsa = mm(o_nsa.reshape(B * T, NSA_WIDTH), lw['w_o_nsa']).reshape(B, T, D_MODEL)
    else:
        o_nsa = nsa_compressed_selected(qn, cmp_all[:, :, 0], cmp_all[:, :, 1], slc_all[:, :, 0], slc_all[:, :, 1],
                                        g_nsa, pos, lw['cmp_w1k'], lw['cmp_w2k'], lw['cmp_pek'],
                                        lw['cmp_w1v'], lw['cmp_w2v'], lw['cmp_pev'])
        o_nsa = o_nsa + g_nsa[..., 2:3] * window_attn(qn, win_pad[:, :, 0], win_pad[:, :, 1], pos0)
        y_nsa = dense(o_nsa.reshape(B, T, NSA_WIDTH), lw['w_o_nsa'])
    win_new = win_all[:, n_win - min(WINDOW, n_win):]

    prev = jnp.concatenate([shift0[:, None].astype(dt), rw[:, :-1]], axis=1)
    rm = (rw + (prev - rw) * lw['rwkv_mu']).astype(f32)
    r_, k_, v_, w_lo, a_lo, g_lo = split_sizes(rm, RWKV_SIZES)
    w_raw = lw['rwkv_w0'] + jnp.tanh(w_lo) @ lw['rwkv_w2']
    w_log = -jnp.exp(-jax.nn.softplus(-w_raw) - 0.5)
    a = jax.nn.sigmoid(lw['rwkv_a0'] + a_lo @ lw['rwkv_a2'])
    gate = jax.nn.sigmoid(g_lo) @ lw['rwkv_g2']
    rh = lambda t: heads(t, RWKV_HEADS)
    kk = rh(k_ * lw['rwkv_kk'])
    kk = kk * lax.rsqrt(jnp.maximum(jnp.sum(kk * kk, axis=-1, keepdims=True), 1e-24))
    k2 = k_ * (1.0 + (a - 1.0) * lw['rwkv_ka'])
    y, rwkv_s = rwkv_scan_pallas(r_, w_log, k2, v_, kk.reshape(B, T, RWKV_WIDTH), a, rwkv_s0.astype(f32))
    y = rh(y)
    y_mean = jnp.mean(y, axis=-1, keepdims=True)
    y_var = jnp.mean(jnp.square(y - y_mean), axis=-1, keepdims=True)
    yn = ((y - y_mean) * lax.rsqrt(y_var + RWKV_LN_EPS)).reshape(B, T, RWKV_WIDTH) * lw['rwkv_ln_w'] + lw['rwkv_ln_b']
    bonus = jnp.sum(rh(r_) * rh(k2) * lw['rwkv_rk'], axis=-1, keepdims=True) * rh(v_)
    o_rwkv = (yn + bonus.reshape(B, T, RWKV_WIDTH)) * gate
    y_rwkv = dense(o_rwkv, lw['w_o_rwkv'])
    shift_new = rw[:, -1]

    gm = jax.nn.sigmoid(mg.astype(f32)).reshape(B, T, N_BRANCH, D_MODEL)
    merged = gm[:, :, 0] * y_gla + gm[:, :, 1] * y_nsa + gm[:, :, 2] * y_rwkv
    x = x + dense(merged, lw['w_out'])

    xn2 = rmsnorm(x, lw['norm2'])
    hp = jnp.concatenate([conv0.astype(dt), dense(xn2, lw['ffn_gate'])], axis=1)
    cw = lw['ffn_conv']
    hc = lw['ffn_conv_b'] + sum(hp[:, i:i + T] * cw[i] for i in range(CONV_W))
    x = x + dense(jax.nn.silu(hc) * dense(xn2, lw['ffn_up']), lw['ffn_down'])
    conv_new = hp[:, T:]
    return x, (new_cmp, new_slc, win_new, gla_s, rwkv_s, shift_new, conv_new)


def _repack_w_in(w):
    parts = []
    off = 0
    for s, p in zip(IN_SIZES, _SEG_PAD):
        seg = w[:, off:off + s].astype(jnp.bfloat16)
        parts.append(jnp.pad(seg, ((0, 0), (0, p - s))) if p != s else seg)
        off += s
    tail = N_IN_PAD - sum(_SEG_PAD)
    if tail:
        parts.append(jnp.zeros((w.shape[0], tail), jnp.bfloat16))
    return jnp.concatenate(parts, axis=1)


def kernel(x_prompt, x_sample, cache_cmp_kv, cache_slc_kv, cache_win_kv, state_gla, state_rwkv, state_rwkv_shift, state_ffn_conv, page_table, norm1, w_in, gla_wa2, gla_ba, gla_norm, w_o_gla, cmp_w1k, cmp_w2k, cmp_pek, cmp_w1v, cmp_w2v, cmp_pev, w_o_nsa, rwkv_mu, rwkv_w0, rwkv_w2, rwkv_a0, rwkv_a2, rwkv_g2, rwkv_kk, rwkv_ka, rwkv_rk, rwkv_ln_w, rwkv_ln_b, w_o_rwkv, w_out, norm2, ffn_gate, ffn_conv, ffn_conv_b, ffn_up, ffn_down, norm_f):
    G, HD = NSA_KV_HEADS, HEAD_DIM
    n_db, n_pages = page_table.shape
    past_len = n_pages * PAGE_SIZE
    bp = x_prompt.shape[0]
    dt = x_prompt.dtype
    bf = jnp.bfloat16
    xp, xs = x_prompt, x_sample
    st_p, st_s = [], []
    for l in range(DEPTH):
        lw = {'norm1': norm1[l], 'w_in': _repack_w_in(w_in[l]), 'gla_wa2': gla_wa2[l], 'gla_ba': gla_ba[l],
              'gla_norm': gla_norm[l], 'w_o_gla': w_o_gla[l].astype(bf), 'cmp_w1k': cmp_w1k[l], 'cmp_w2k': cmp_w2k[l],
              'cmp_pek': cmp_pek[l], 'cmp_w1v': cmp_w1v[l], 'cmp_w2v': cmp_w2v[l], 'cmp_pev': cmp_pev[l],
              'w_o_nsa': w_o_nsa[l].astype(bf), 'rwkv_mu': rwkv_mu[l], 'rwkv_w0': rwkv_w0[l], 'rwkv_w2': rwkv_w2[l],
              'rwkv_a0': rwkv_a0[l], 'rwkv_a2': rwkv_a2[l], 'rwkv_g2': rwkv_g2[l], 'rwkv_kk': rwkv_kk[l],
              'rwkv_ka': rwkv_ka[l], 'rwkv_rk': rwkv_rk[l], 'rwkv_ln_w': rwkv_ln_w[l], 'rwkv_ln_b': rwkv_ln_b[l],
              'w_o_rwkv': w_o_rwkv[l].astype(bf), 'w_out': w_out[l].astype(bf), 'norm2': norm2[l],
              'ffn_gate': ffn_gate[l].astype(bf), 'ffn_conv': ffn_conv[l], 'ffn_conv_b': ffn_conv_b[l],
              'ffn_up': ffn_up[l].astype(bf), 'ffn_down': ffn_down[l].astype(bf)}
        empty = jnp.zeros((bp, 0, 2, G, HD), dt)
        xp, sp = trunk_layer(xp, 0, empty, empty, empty,
                             jnp.zeros((bp, GLA_HEADS, GLA_DK, GLA_DV), jnp.float32),
                             jnp.zeros((bp, RWKV_HEADS, RWKV_N, RWKV_N), jnp.float32),
                             jnp.zeros((bp, RWKV_COLS), dt),
                             jnp.zeros((bp, CONV_W - 1, D_FF), dt), lw)
        past_c = cache_cmp_kv[l, page_table].reshape(n_db, past_len, 2, G, HD)
        past_s = cache_slc_kv[l, page_table].reshape(n_db, past_len, 2, G, HD)
        xs, ss = trunk_layer(xs, past_len, past_c, past_s, cache_win_kv[l], state_gla[l], state_rwkv[l],
                             state_rwkv_shift[l], state_ffn_conv[l], lw)
        st_p.append(sp)
        st_s.append(ss)
    y_prompt = rmsnorm(xp, norm_f)
    y_sample = rmsnorm(xs, norm_f)
    outs = [y_prompt, y_sample]
    for i in range(7):
        outs.append(jnp.stack([s[i] for s in st_p]))
        outs.append(jnp.stack([s[i] for s in st_s]))
    return tuple(outs)
```

```python
import functools

import jax
import jax.numpy as jnp
import numpy as np
from jax import lax
from jax.experimental import pallas as pl
from jax.experimental.pallas import tpu as pltpu

D_MODEL = 4096
DEPTH = 2
PAGE_SIZE = 128
HEAD_DIM = 128
ROPE_DIM = HEAD_DIM // 4
ROPE_THETA = 500000.0
NORM_EPS = 1e-5
NEG_INF = -1e30

GLA_WIDTH = D_MODEL // 4
GLA_HEADS = 4
GLA_DV = GLA_WIDTH // GLA_HEADS
GLA_DK = GLA_DV // 2
GLA_GATE_RANK = 16
GLA_TAU = 16.0
GLA_CHUNK = 64

NSA_HEADS = D_MODEL // 256
NSA_KV_HEADS = 4
NSA_GROUP = NSA_HEADS // NSA_KV_HEADS
NSA_WIDTH = NSA_HEADS * HEAD_DIM
NSA_KV_WIDTH = NSA_KV_HEADS * HEAD_DIM
CMP_STRIDE = 16
CMP_BLOCK = 2 * CMP_STRIDE
SEL_BLOCK = 64
SEL_TOP = 16
N_INIT_BLOCKS = 1
N_LOCAL_BLOCKS = 2
WINDOW = 512
SEL_Q_BLOCK = 32
WIN_Q_BLOCK = 128
FORCE_SCORE = 1e4

RWKV_WIDTH = D_MODEL // 4
RWKV_N = 64
RWKV_HEADS = RWKV_WIDTH // RWKV_N
RWKV_DECAY_RANK = 64
RWKV_AAA_RANK = 64
RWKV_GATE_RANK = 160
RWKV_SIZES = (RWKV_WIDTH, RWKV_WIDTH, RWKV_WIDTH, RWKV_DECAY_RANK, RWKV_AAA_RANK, RWKV_GATE_RANK)
RWKV_COLS = sum(RWKV_SIZES)
RWKV_LN_EPS = 64e-5

N_BRANCH = 3
D_FF = 256 * ((8 * D_MODEL // 3 + 255) // 256)
CONV_W = 3

IN_SIZES = (GLA_HEADS * GLA_DK, GLA_HEADS * GLA_DK, GLA_WIDTH, GLA_WIDTH, GLA_GATE_RANK,
            NSA_WIDTH, NSA_KV_WIDTH, NSA_KV_WIDTH, NSA_KV_WIDTH, NSA_KV_WIDTH, NSA_KV_WIDTH, NSA_KV_WIDTH,
            NSA_HEADS * 3,
            RWKV_COLS,
            N_BRANCH * D_MODEL)

LANE = 128
SUBLANE = 8
NSA_TQ = 256
GLA_TC = 256
VMEM_LIMIT = 48 * 1024 * 1024


def _round_up(n, m):
    return -(-n // m) * m


N_MIX_SEG = len(IN_SIZES) - 1
_SEG_PAD = tuple(_round_up(s, LANE) for s in IN_SIZES[:N_MIX_SEG])
_SEG_OFF = tuple(int(o) for o in np.concatenate([[0], np.cumsum(_SEG_PAD)[:-1]]))
N_MIX_PAD = _round_up(sum(_SEG_PAD), 512)


def _pick(n, cands):
    for c in cands:
        if n % c == 0:
            return c
    return n


def _rmsnorm_kernel(x_ref, g_ref, o_ref):
    x = x_ref[...]
    y = x * lax.rsqrt(jnp.mean(x * x, axis=-1, keepdims=True) + NORM_EPS)
    o_ref[...] = (y * g_ref[...]).astype(o_ref.dtype)


def rmsnorm_pallas(x, g, out_dtype=jnp.bfloat16):
    M, D = x.shape
    tm = _pick(M, (256, 128, 64, 32, 16, 8))
    return pl.pallas_call(
        _rmsnorm_kernel,
        grid=(M // tm,),
        in_specs=[pl.BlockSpec((tm, D), lambda i: (i, 0)), pl.BlockSpec((1, D), lambda i: (0, 0))],
        out_specs=pl.BlockSpec((tm, D), lambda i: (i, 0)),
        out_shape=jax.ShapeDtypeStruct((M, D), out_dtype),
        compiler_params=pltpu.CompilerParams(dimension_semantics=("parallel",), vmem_limit_bytes=VMEM_LIMIT),
        name="rmsnorm",
    )(x, g.reshape(1, D))


def _mm_kernel(*refs, nk, has_res):
    x_ref, w_ref = refs[:2]
    res_ref = refs[2] if has_res else None
    o_ref, acc_ref = refs[-2:]
    k = pl.program_id(2)
    part = jnp.dot(x_ref[...], w_ref[...], preferred_element_type=jnp.float32)

    def finish(v):
        if has_res:
            v = v + res_ref[...]
        o_ref[...] = v.astype(o_ref.dtype)

    if nk == 1:
        finish(part)
    else:
        @pl.when(k == 0)
        def _():
            acc_ref[...] = part

        @pl.when(jnp.logical_and(k > 0, k < nk - 1))
        def _():
            acc_ref[...] += part

        @pl.when(k == nk - 1)
        def _():
            finish(acc_ref[...] + part)


def mm(x, w, res=None, out_dtype=jnp.float32):
    M, K = x.shape
    N = w.shape[1]
    tm = _pick(M, (1024, 512, 256, 128, 64, 32, 16, 8))
    tn = _pick(N, (512, 256, 128))
    tk = K if K <= 4096 else _pick(K, (5504, 4096, 2048, 1024, 512))
    nk = K // tk
    in_specs = [pl.BlockSpec((tm, tk), lambda i, j, k: (i, k)),
                pl.BlockSpec((tk, tn), lambda i, j, k: (k, j))]
    args = [x, w]
    if res is not None:
        in_specs.append(pl.BlockSpec((tm, tn), lambda i, j, k: (i, j)))
        args.append(res)
    return pl.pallas_call(
        functools.partial(_mm_kernel, nk=nk, has_res=res is not None),
        grid=(M // tm, N // tn, nk),
        in_specs=in_specs,
        out_specs=pl.BlockSpec((tm, tn), lambda i, j, k: (i, j)),
        out_shape=jax.ShapeDtypeStruct((M, N), out_dtype),
        scratch_shapes=[pltpu.VMEM((tm, tn) if nk > 1 else (SUBLANE, LANE), jnp.float32)],
        compiler_params=pltpu.CompilerParams(
            dimension_semantics=("parallel", "parallel", "arbitrary"),
            vmem_limit_bytes=VMEM_LIMIT),
        name="dense_mm",
    )(*args)


def _merge_kernel(oa_ref, ob_ref, oc_ref, wa_ref, wb_ref, wc_ref, ga_ref, gb_ref, gc_ref, o_ref):
    f32 = jnp.float32
    acc = jax.nn.sigmoid(ga_ref[...]) * jnp.dot(oa_ref[...], wa_ref[...], preferred_element_type=f32)
    acc += jax.nn.sigmoid(gb_ref[...]) * jnp.dot(ob_ref[...], wb_ref[...], preferred_element_type=f32)
    acc += jax.nn.sigmoid(gc_ref[...]) * jnp.dot(oc_ref[...], wc_ref[...], preferred_element_type=f32)
    o_ref[...] = acc.astype(o_ref.dtype)


def merge_mm(o_a, o_b, o_c, w_a, w_b, w_c, mg, out_dtype=jnp.bfloat16):
    M = o_a.shape[0]
    D = w_a.shape[1]
    tm = _pick(M, (512, 256, 128, 64, 32, 16, 8))
    tn = _pick(D, (512, 256, 128))
    nj = D // tn
    o_spec = lambda o: pl.BlockSpec((tm, o.shape[1]), lambda i, j: (i, 0))
    w_spec = lambda w: pl.BlockSpec((w.shape[0], tn), lambda i, j: (0, j))
    g_spec = lambda b: pl.BlockSpec((tm, tn), lambda i, j: (i, b * nj + j))
    return pl.pallas_call(
        _merge_kernel,
        grid=(M // tm, nj),
        in_specs=[o_spec(o_a), o_spec(o_b), o_spec(o_c), w_spec(w_a), w_spec(w_b), w_spec(w_c),
                  g_spec(0), g_spec(1), g_spec(2)],
        out_specs=pl.BlockSpec((tm, tn), lambda i, j: (i, j)),
        out_shape=jax.ShapeDtypeStruct((M, D), out_dtype),
        compiler_params=pltpu.CompilerParams(dimension_semantics=("parallel", "parallel"),
                                             vmem_limit_bytes=VMEM_LIMIT),
        name="merge_mm",
    )(o_a, o_b, o_c, w_a, w_b, w_c, mg, mg, mg)


def _ffn_gate_up_kernel(x_ref, wg_ref, wu_ref, cw_ref, cb_ref, st_ref, act_ref, tail_ref, carry_scr, *, tm, tps, T):
    f32 = jnp.float32
    i, j = pl.program_id(0), pl.program_id(1)
    x = x_ref[...]
    h = jnp.dot(x, wg_ref[...], preferred_element_type=f32)
    u = jnp.dot(x, wu_ref[...], preferred_element_type=f32)
    cw = cw_ref[...]
    if T == 1:
        prev2, prev1 = st_ref[0], st_ref[1]
        tail_ref[0] = prev1
        tail_ref[1] = h
    else:
        tail = jnp.where(i % tps == 0, st_ref[...], carry_scr[j])
        row = lax.broadcasted_iota(jnp.int32, h.shape, 0)
        prev1 = jnp.where(row == 0, tail[7:8], pltpu.roll(h, 1, axis=0))
        prev2 = jnp.where(row == 0, tail[6:7], jnp.where(row == 1, tail[7:8], pltpu.roll(h, 2, axis=0)))
        last = h[tm - SUBLANE:tm]
        carry_scr[j] = last
        tail_ref[...] = last
    hc = cb_ref[...] + prev2 * cw[0:1] + prev1 * cw[1:2] + h * cw[2:3]
    act_ref[...] = (jax.nn.silu(hc) * u).astype(act_ref.dtype)


def ffn_gate_up(xn, w_gate, w_up, conv_w, conv_b, conv0, B, T):
    M, D = xn.shape
    F = w_gate.shape[1]
    tn = _pick(F, (512, 256, 128))
    nj = F // tn
    cw = jnp.pad(conv_w, ((0, SUBLANE - CONV_W), (0, 0)))
    cb = conv_b.reshape(1, F)
    if T == 1:
        tm, tps = M, 1
        st = conv0.transpose(1, 0, 2)
        st_spec = pl.BlockSpec((2, B, tn), lambda i, j: (0, 0, j))
        tail_shape, tail_spec = (2, B, F), pl.BlockSpec((2, B, tn), lambda i, j: (0, 0, j))
    else:
        tm = _pick(T, (1024, 512, 256, 128, 64, 32, 16, 8))
        tps = T // tm
        st = jnp.pad(conv0, ((0, 0), (SUBLANE - 2, 0), (0, 0)))
        st_spec = pl.BlockSpec((None, SUBLANE, tn), lambda i, j: (i // tps, 0, j))
        tail_shape, tail_spec = (B, SUBLANE, F), pl.BlockSpec((None, SUBLANE, tn), lambda i, j: (i // tps, 0, j))
    act, tail = pl.pallas_call(
        functools.partial(_ffn_gate_up_kernel, tm=tm, tps=tps, T=T),
        grid=(M // tm, nj),
        in_specs=[pl.BlockSpec((tm, D), lambda i, j: (i, 0)),
                  pl.BlockSpec((D, tn), lambda i, j: (0, j)),
                  pl.BlockSpec((D, tn), lambda i, j: (0, j)),
                  pl.BlockSpec((SUBLANE, tn), lambda i, j: (0, j)),
                  pl.BlockSpec((1, tn), lambda i, j: (0, j)),
                  st_spec],
        out_specs=[pl.BlockSpec((tm, tn), lambda i, j: (i, j)), tail_spec],
        out_shape=[jax.ShapeDtypeStruct((M, F), jnp.bfloat16), jax.ShapeDtypeStruct(tail_shape, jnp.float32)],
        scratch_shapes=[pltpu.VMEM((nj, SUBLANE, tn), jnp.float32)],
        compiler_params=pltpu.CompilerParams(dimension_semantics=("arbitrary", "arbitrary"),
                                             vmem_limit_bytes=VMEM_LIMIT),
        name="ffn_gate_up",
    )(xn, w_gate, w_up, cw, cb, st)
    conv_new = tail.transpose(1, 0, 2) if T == 1 else tail[:, SUBLANE - 2:]
    return act, conv_new


def _gla_kernel(q_ref, k_ref, v_ref, og_ref, lo_ref, wa_ref, ba_ref, gn_ref, s0_ref, o_ref, sout_ref, s_scr, *, Tc, C):
    f32, bf16 = jnp.float32, jnp.bfloat16
    c = pl.program_id(2)

    @pl.when(c == 0)
    def _():
        s_scr[...] = s0_ref[...]

    row = lax.broadcasted_iota(jnp.int32, (C, GLA_DK), 0)
    tril = lax.broadcasted_iota(jnp.int32, (C, C), 0) >= lax.broadcasted_iota(jnp.int32, (C, C), 1)
    wa = wa_ref[...]
    ba = ba_ref[...]
    gn = gn_ref[...]
    S = s_scr[...]
    for n in range(Tc // C):
        rows = slice(n * C, (n + 1) * C)
        la = jax.nn.log_sigmoid(jnp.dot(lo_ref[rows, :].astype(bf16), wa, preferred_element_type=f32) + ba) / GLA_TAU
        bc = la
        d = 1
        while d < C:
            bc = bc + jnp.where(row >= d, pltpu.roll(bc, d, axis=0), 0.0)
            d *= 2
        b_mid = bc[C // 2:C // 2 + 1]
        b_last = bc[C - 1:C]
        q = q_ref[rows, :] * GLA_DK ** -0.5
        k = k_ref[rows, :]
        v = v_ref[rows, :].astype(bf16)
        att = lax.dot_general((q * jnp.exp(bc - b_mid)).astype(bf16), (k * jnp.exp(b_mid - bc)).astype(bf16),
                              (((1,), (1,)), ((), ())), preferred_element_type=f32)
        att = jnp.where(tril, att, 0.0)
        o = jnp.dot(att.astype(bf16), v, preferred_element_type=f32)
        o = o + jnp.dot((q * jnp.exp(bc)).astype(bf16), S.astype(bf16), preferred_element_type=f32)
        kd = jnp.concatenate([k * jnp.exp(b_last - bc), jnp.broadcast_to(jnp.exp(b_last), (SUBLANE, GLA_DK))], axis=0)
        kdt = kd.T
        S = kdt[:, C:C + 1] * S + jnp.dot(kdt[:, :C].astype(bf16), v, preferred_element_type=f32)
        o = o * lax.rsqrt(jnp.mean(o * o, axis=-1, keepdims=True) + NORM_EPS) * gn
        o_ref[rows, :] = (o * jax.nn.silu(og_ref[rows, :])).astype(o_ref.dtype)
    s_scr[...] = S

    @pl.when(c == pl.num_programs(2) - 1)
    def _():
        sout_ref[...] = S


def gla_pallas(hmix, wa2, ba, gnorm, s0, seg_off):
    B, T, _ = hmix.shape
    H = GLA_HEADS
    Tc = min(GLA_TC, T)
    C = GLA_CHUNK
    oq, ok, ov, og, ol = seg_off
    wa = jnp.pad(wa2, ((0, LANE - wa2.shape[0]), (0, 0))).astype(jnp.bfloat16)
    col = lambda off, w: (lambda b, h, c: (b, c, off // w + h))
    o, s = pl.pallas_call(
        functools.partial(_gla_kernel, Tc=Tc, C=C),
        grid=(B, H, T // Tc),
        in_specs=[pl.BlockSpec((None, Tc, GLA_DK), col(oq, GLA_DK)),
                  pl.BlockSpec((None, Tc, GLA_DK), col(ok, GLA_DK)),
                  pl.BlockSpec((None, Tc, GLA_DV), col(ov, GLA_DV)),
                  pl.BlockSpec((None, Tc, GLA_DV), col(og, GLA_DV)),
                  pl.BlockSpec((None, Tc, LANE), lambda b, h, c: (b, c, ol // LANE)),
                  pl.BlockSpec((LANE, GLA_DK), lambda b, h, c: (0, h)),
                  pl.BlockSpec((1, GLA_DK), lambda b, h, c: (0, h)),
                  pl.BlockSpec((1, GLA_DV), lambda b, h, c: (0, 0)),
                  pl.BlockSpec((None, None, GLA_DK, GLA_DV), lambda b, h, c: (b, h, 0, 0))],
        out_specs=[pl.BlockSpec((None, Tc, GLA_DV), lambda b, h, c: (b, c, h)),
                   pl.BlockSpec((None, None, GLA_DK, GLA_DV), lambda b, h, c: (b, h, 0, 0))],
        out_shape=[jax.ShapeDtypeStruct((B, T, H * GLA_DV), jnp.bfloat16),
                   jax.ShapeDtypeStruct((B, H, GLA_DK, GLA_DV), jnp.float32)],
        scratch_shapes=[pltpu.VMEM((GLA_DK, GLA_DV), jnp.float32)],
        compiler_params=pltpu.CompilerParams(dimension_semantics=("parallel", "parallel", "arbitrary"),
                                             vmem_limit_bytes=VMEM_LIMIT),
        name="gla_chunked",
    )(hmix, hmix, hmix, hmix, hmix, wa, ba.reshape(1, -1), gnorm.reshape(1, -1), s0)
    return o, s


def _rwkv_kernel(r_ref, wl_ref, k_ref, v_ref, kk_ref, a_ref, s0_ref, y_ref, sout_ref, s_scr, *, NP, Tc):
    c = pl.program_id(1)
    f32, bf16 = jnp.float32, jnp.bfloat16
    U = min(SUBLANE, Tc)

    @pl.when(c == 0)
    def _():
        s_scr[...] = s0_ref[...]

    sub = lax.broadcasted_iota(jnp.int32, (RWKV_N, LANE), 0)
    lane = lax.broadcasted_iota(jnp.int32, (RWKV_N, LANE), 1)
    eye2 = (lane % RWKV_N) == sub
    rr = lax.broadcasted_iota(jnp.int32, (LANE, LANE), 0) // RWKV_N
    cc = lax.broadcasted_iota(jnp.int32, (LANE, LANE), 1) // RWKV_N
    bd = jnp.where(rr == cc, 1.0, 0.0).astype(bf16)

    def ssb(p, two_piece=True):
        hi = p.astype(bf16)
        out = jnp.dot(hi, bd, preferred_element_type=f32)
        if two_piece:
            lo = (p - hi.astype(f32)).astype(bf16)
            out = out + jnp.dot(lo, bd, preferred_element_type=f32)
        return out

    eye_all = jnp.concatenate([eye2] * NP, axis=0)

    def bcast(x8, s):
        return jnp.concatenate(
            [jnp.broadcast_to(x8[s:s + 1, p * LANE:(p + 1) * LANE], (RWKV_N, LANE)) for p in range(NP)], axis=0)

    def body(g, carry):
        rows = pl.ds(pl.multiple_of(g * U, U), U)
        r8 = r_ref[rows, :]
        w8 = jnp.exp(wl_ref[rows, :])
        k8 = k_ref[rows, :]
        v8 = v_ref[rows, :]
        kk8 = kk_ref[rows, :]
        ka8 = kk8 * a_ref[rows, :]
        nk8 = -kk8
        S = s_scr[...]
        ys = []
        for s in range(U):
            sa = ssb(S * bcast(nk8, s))
            vb = ssb(jnp.where(eye_all, bcast(v8, s), 0.0))
            S = S * bcast(w8, s) + sa * bcast(ka8, s) + vb * bcast(k8, s)
            yb = jnp.where(eye_all, ssb(S * bcast(r8, s), two_piece=False), 0.0)
            ys.append(jnp.concatenate(
                [jnp.sum(yb[p * RWKV_N:(p + 1) * RWKV_N], axis=0, keepdims=True) for p in range(NP)], axis=1))
        s_scr[...] = S
        y_ref[rows, :] = ys[0] if U == 1 else jnp.concatenate(ys, axis=0)
        return carry

    lax.fori_loop(0, Tc // U, body, 0)

    @pl.when(c == pl.num_programs(1) - 1)
    def _():
        sout_ref[...] = s_scr[...]


def rwkv_scan_pallas(r, w_log, k, v, kk, a, s0):
    B, T, W = r.shape
    H = W // RWKV_N
    NP = H // 2
    Tc = 128 if T % 128 == 0 else T
    s0p = s0.reshape(B, NP, 2, RWKV_N, RWKV_N).transpose(0, 1, 3, 2, 4).reshape(B, NP * RWKV_N, LANE)
    blk = pl.BlockSpec((None, Tc, W), lambda b, c: (b, c, 0))
    sblk = pl.BlockSpec((None, NP * RWKV_N, LANE), lambda b, c: (b, 0, 0))
    y, sp = pl.pallas_call(
        functools.partial(_rwkv_kernel, NP=NP, Tc=Tc),
        grid=(B, T // Tc),
        in_specs=[blk] * 6 + [sblk],
        out_specs=[blk, sblk],
        out_shape=[jax.ShapeDtypeStruct((B, T, W), jnp.float32),
                   jax.ShapeDtypeStruct((B, NP * RWKV_N, LANE), jnp.float32)],
        scratch_shapes=[pltpu.VMEM((NP * RWKV_N, LANE), jnp.float32)],
        compiler_params=pltpu.CompilerParams(dimension_semantics=("parallel", "arbitrary"),
                                             vmem_limit_bytes=VMEM_LIMIT),
        name="rwkv7_scan",
    )(r, w_log, k, v, kk, a, s0p)
    s_fin = sp.reshape(B, NP, RWKV_N, 2, RWKV_N).transpose(0, 1, 3, 2, 4).reshape(B, H, RWKV_N, RWKV_N)
    return y, s_fin


def _dot_nt(a, b):
    return lax.dot_general(a, b, (((1,), (1,)), ((), ())), preferred_element_type=jnp.float32)


def _nsa_kernel(q_ref, kc_ref, vc_ref, ks_ref, vs_ref, kw_ref, vw_ref, g_ref, covt_ref, e_ref, o_ref,
                bias_scr, p4_scr, ocmp_scr, m_scr, l_scr, acc_scr, *, TQ, T, NS, NCP, n_top):
    f32, bf16 = jnp.float32, jnp.bfloat16
    R = NSA_GROUP
    i = pl.program_id(2)
    nchunk = T // TQ
    qpos_col = i * TQ + lax.broadcasted_iota(jnp.int32, (TQ, 1), 0)

    kc = kc_ref[...]
    vc = vc_ref[...]
    cend = lax.broadcasted_iota(jnp.int32, (1, NCP), 1) * CMP_STRIDE + (CMP_BLOCK - 1)
    valid = cend <= qpos_col
    for r in range(R):
        s = _dot_nt(q_ref[:, r * HEAD_DIM:(r + 1) * HEAD_DIM], kc)
        s = jnp.where(valid, s, NEG_INF)
        m = jnp.max(s, axis=-1, keepdims=True)
        p = jnp.where(valid, jnp.exp(s - m), 0.0)
        l = jnp.sum(p, axis=-1, keepdims=True)
        p = (p / jnp.where(l > 0.0, l, 1.0)).astype(bf16)
        p4_scr[:, r * NCP:(r + 1) * NCP] = p
        ocmp_scr[r] = jnp.dot(p, vc, preferred_element_type=f32)

    imp_t = _dot_nt(covt_ref[...], p4_scr[...])
    j = lax.broadcasted_iota(jnp.int32, (NS, TQ), 0)
    qblk = (i * TQ + lax.broadcasted_iota(jnp.int32, (NS, TQ), 1)) // SEL_BLOCK
    forced = (j < N_INIT_BLOCKS) | ((j <= qblk) & (j > qblk - N_LOCAL_BLOCKS))
    score = jnp.where(forced, FORCE_SCORE, jnp.where(j <= qblk, imp_t, NEG_INF))
    rank = jnp.zeros((NS, TQ), f32)
    for a in range(NS):
        row = score[a:a + 1, :]
        beats = (row > score) | ((row == score) & (a < j))
        rank = rank + jnp.where(beats, 1.0, 0.0)
    sel_t = jnp.where(rank < n_top, 1.0, 0.0)
    if NS < LANE:
        sel_t = jnp.concatenate([sel_t, jnp.zeros((LANE - NS, TQ), f32)], axis=0)
    sel = sel_t.T.astype(bf16)
    for c in range(nchunk):
        selexp = jnp.dot(sel, e_ref[:, c * TQ:(c + 1) * TQ], preferred_element_type=f32)
        kpos = c * TQ + lax.broadcasted_iota(jnp.int32, (TQ, TQ), 1)
        bias_scr[c] = jnp.where((selexp > 0.5) & (kpos <= qpos_col), 0.0, NEG_INF)

    def attend(k_ref, v_ref, lo, masker):
        m_scr[...] = jnp.full(m_scr.shape, NEG_INF, f32)
        l_scr[...] = jnp.zeros(l_scr.shape, f32)
        acc_scr[...] = jnp.zeros(acc_scr.shape, f32)

        def chunk(c, carry):
            rows = pl.ds(pl.multiple_of(c * TQ, TQ), TQ)
            k = k_ref[rows, :]
            v = v_ref[rows, :]
            mk = masker(c)
            for r in range(R):
                s = _dot_nt(q_ref[:, r * HEAD_DIM:(r + 1) * HEAD_DIM], k)
                s, keep = mk(s)
                m_prev = m_scr[r]
                m_new = jnp.maximum(m_prev, jnp.max(s, axis=-1, keepdims=True))
                alpha = jnp.exp(m_prev - m_new)
                p = jnp.exp(s - m_new)
                if keep is not None:
                    p = jnp.where(keep, p, 0.0)
                l_scr[r] = alpha * l_scr[r] + jnp.sum(p, axis=-1, keepdims=True)
                acc_scr[r] = alpha * acc_scr[r] + jnp.dot(p.astype(bf16), v, preferred_element_type=f32)
                m_scr[r] = m_new
            return carry

        lax.fori_loop(lo, i + 1, chunk, 0)

    def sel_masker(c):
        b = bias_scr[c]
        return lambda s: (s + b, None)

    attend(ks_ref, vs_ref, 0, sel_masker)
    g = g_ref[...]
    for r in range(R):
        ocmp_scr[r] = (g[:, 3 * r:3 * r + 1] * ocmp_scr[r]
                       + g[:, 3 * r + 1:3 * r + 2] * (acc_scr[r] / l_scr[r]))

    def win_masker(c):
        rel = ((i - c) * TQ + lax.broadcasted_iota(jnp.int32, (TQ, TQ), 0)
               - lax.broadcasted_iota(jnp.int32, (TQ, TQ), 1))
        ok = (rel >= 0) & (rel < WINDOW)
        return lambda s: (jnp.where(ok, s, NEG_INF), ok)

    attend(kw_ref, vw_ref, jnp.maximum(i - (WINDOW // TQ), 0), win_masker)
    for r in range(R):
        o = ocmp_scr[r] + g[:, 3 * r + 2:3 * r + 3] * (acc_scr[r] / l_scr[r])
        o_ref[:, r * HEAD_DIM:(r + 1) * HEAD_DIM] = o.astype(o_ref.dtype)


def nsa_attention_pallas(qr, kcmp, vcmp, ks, vs, kw, vw, gates):
    B, T, _ = qr.shape
    G, R = NSA_KV_HEADS, NSA_GROUP
    TQ = min(NSA_TQ, T)
    NS = T // SEL_BLOCK
    NC = T // CMP_STRIDE - 1
    NCP = kcmp.shape[2]
    n_top = min(SEL_TOP, NS)
    ci = np.arange(NCP)[:, None] * CMP_STRIDE
    sj = np.arange(NS)[None, :] * SEL_BLOCK
    cover = np.clip(np.minimum(ci + CMP_BLOCK, sj + SEL_BLOCK) - np.maximum(ci, sj), 0, None) / CMP_BLOCK
    cover[NC:] = 0.0
    covt = jnp.asarray(np.tile(cover.T, (1, R)), jnp.bfloat16)
    e = jnp.asarray((np.arange(T)[None, :] // SEL_BLOCK) == np.arange(LANE)[:, None], jnp.bfloat16)
    kv_spec = pl.BlockSpec((None, T, HEAD_DIM), lambda b, g, i: (b, 0, g))
    cmp_spec = pl.BlockSpec((None, None, NCP, HEAD_DIM), lambda b, g, i: (b, g, 0, 0))
    return pl.pallas_call(
        functools.partial(_nsa_kernel, TQ=TQ, T=T, NS=NS, NCP=NCP, n_top=n_top),
        grid=(B, G, T // TQ),
        in_specs=[pl.BlockSpec((None, TQ, R * HEAD_DIM), lambda b, g, i: (b, i, g)),
                  cmp_spec, cmp_spec, kv_spec, kv_spec, kv_spec, kv_spec,
                  pl.BlockSpec((None, None, TQ, 3 * R), lambda b, g, i: (b, g, i, 0)),
                  pl.BlockSpec((NS, R * NCP), lambda b, g, i: (0, 0)),
                  pl.BlockSpec((LANE, T), lambda b, g, i: (0, 0))],
        out_specs=pl.BlockSpec((None, TQ, R * HEAD_DIM), lambda b, g, i: (b, i, g)),
        out_shape=jax.ShapeDtypeStruct((B, T, G * R * HEAD_DIM), jnp.bfloat16),
        scratch_shapes=[pltpu.VMEM((T // TQ, TQ, TQ), jnp.float32),
                        pltpu.VMEM((TQ, R * NCP), jnp.bfloat16),
                        pltpu.VMEM((R, TQ, HEAD_DIM), jnp.float32),
                        pltpu.VMEM((R, TQ, 1), jnp.float32),
                        pltpu.VMEM((R, TQ, 1), jnp.float32),
                        pltpu.VMEM((R, TQ, HEAD_DIM), jnp.float32)],
        compiler_params=pltpu.CompilerParams(dimension_semantics=("parallel", "parallel", "arbitrary"),
                                             vmem_limit_bytes=VMEM_LIMIT),
        name="nsa_attention",
    )(qr, kcmp, vcmp, ks, vs, kw, vw, gates, covt, e)


def rmsnorm(x, g):
    xf = x.astype(jnp.float32)
    y = xf * lax.rsqrt(jnp.mean(xf * xf, axis=-1, keepdims=True) + NORM_EPS)
    return (y * g.astype(jnp.float32)).astype(x.dtype)


def split_sizes(h, sizes):
    return jnp.split(h, [int(s) for s in np.cumsum(sizes)[:-1]], axis=-1)


def rope(x, pos):
    half = ROPE_DIM // 2
    inv = ROPE_THETA ** (-jnp.arange(half, dtype=jnp.float32) / half)
    ang = pos.astype(jnp.float32)[:, None] * inv[None, :]
    cos = jnp.cos(ang)[None, :, None, :]
    sin = jnp.sin(ang)[None, :, None, :]
    xr = x[..., :ROPE_DIM].astype(jnp.float32)
    x1, x2 = xr[..., :half], xr[..., half:]
    rot = jnp.concatenate([x1 * cos - x2 * sin, x2 * cos + x1 * sin], axis=-1).astype(x.dtype)
    return jnp.concatenate([rot, x[..., ROPE_DIM:]], axis=-1)


def gla_chunked(q, k, v, log_a, s0):
    B, T, H, DK = q.shape
    DV = v.shape[-1]
    C = GLA_CHUNK if T % GLA_CHUNK == 0 else T
    n = T // C

    def chunks(t):
        return t.astype(jnp.float32).reshape(B, n, C, H, t.shape[-1]).transpose(1, 0, 3, 2, 4)

    qc, kc, vc = chunks(q), chunks(k), chunks(v)
    bc = jnp.cumsum(chunks(log_a), axis=3)
    b_mid = bc[:, :, :, C // 2:C // 2 + 1]
    b_last = bc[:, :, :, C - 1:C]
    causal = jnp.tril(jnp.ones((C, C), dtype=bool))
    att = jnp.einsum('nbhid,nbhjd->nbhij', qc * jnp.exp(bc - b_mid), kc * jnp.exp(b_mid - bc))
    att = jnp.where(causal, att, 0.0)
    o_intra = jnp.einsum('nbhij,nbhjv->nbhiv', att, vc)
    q_dec = qc * jnp.exp(bc)
    k_dec = kc * jnp.exp(b_last - bc)
    a_last = jnp.exp(b_last[:, :, :, 0])

    def step(s, inp):
        qd, kd, vv, al = inp
        o = jnp.einsum('bhcd,bhdv->bhcv', qd, s)
        s = al[..., None] * s + jnp.einsum('bhcd,bhcv->bhdv', kd, vv)
        return s, o

    s_fin, o_inter = lax.scan(step, s0.astype(jnp.float32), (q_dec, k_dec, vc, a_last))
    o = (o_intra + o_inter).transpose(1, 0, 3, 2, 4).reshape(B, T, H, DV)
    return o, s_fin


def compress_blocks(x, w1, w2, pe):
    B, L, G, HD = x.shape
    ns = L // CMP_STRIDE
    seg = x[:, :ns * CMP_STRIDE].reshape(B, ns, CMP_STRIDE, G, HD)
    first = jnp.einsum('bnpgd,pde->bnge', seg, w1[:CMP_STRIDE])
    second = jnp.einsum('bnpgd,pde->bnge', seg, w1[CMP_STRIDE:])
    pos_term = jnp.einsum('pd,pde->e', pe, w1)
    hid = jax.nn.gelu(first[:, :-1] + second[:, 1:] + pos_term)
    return jnp.einsum('bnge,ed->bngd', hid, w2)


def nsa_compressed_selected(q, k_c, v_c, k_s, v_s, gates, pos, w1k, w2k, pek, w1v, w2v, pev):
    B, T, H, HD = q.shape
    L = k_c.shape[1]
    G, R = NSA_KV_HEADS, NSA_GROUP
    scale = HEAD_DIM ** -0.5
    kcmp = compress_blocks(k_c, w1k, w2k, pek)
    vcmp = compress_blocks(v_c, w1v, w2v, pev)
    NC = kcmp.shape[1]
    cmp_end = jnp.arange(NC) * CMP_STRIDE + CMP_BLOCK - 1
    NS = -(-L // SEL_BLOCK)
    pad = NS * SEL_BLOCK - L
    to_blocks = lambda t: jnp.pad(t, ((0, 0), (0, pad), (0, 0), (0, 0))).reshape(
        B, NS, SEL_BLOCK, G, HD).transpose(0, 3, 1, 2, 4)
    kb, vb = to_blocks(k_s), to_blocks(v_s)
    ci = jnp.arange(NC)[:, None] * CMP_STRIDE
    sj = jnp.arange(NS)[None, :] * SEL_BLOCK
    cover = jnp.clip(jnp.minimum(ci + CMP_BLOCK, sj + SEL_BLOCK) - jnp.maximum(ci, sj), 0, None)
    cover = cover.astype(jnp.float32) / CMP_BLOCK
    n_top = min(SEL_TOP, NS)
    blk = jnp.arange(NS)
    bi = jnp.arange(B)[:, None, None, None]
    gi = jnp.arange(G)[None, None, :, None]
    off = jnp.arange(SEL_BLOCK)
    QC = SEL_Q_BLOCK if T % SEL_Q_BLOCK == 0 else T
    nq = T // QC

    def chunk_fn(args):
        qc, gc, pc = args
        qg = qc.reshape(B, QC, G, R, HD)
        s = jnp.einsum('bqgrd,bngd->bqgrn', qg, kcmp).astype(jnp.float32) * scale
        valid = (cmp_end[None, :] <= pc[:, None])[None, :, None, None, :]
        p = jnp.where(valid, jax.nn.softmax(jnp.where(valid, s, NEG_INF), axis=-1), 0.0)
        o_cmp = jnp.einsum('bqgrn,bngd->bqgrd', p.astype(vcmp.dtype), vcmp)
        imp = jnp.einsum('bqgrn,nj->bqgj', p, cover)
        qblk = (pc // SEL_BLOCK)[:, None]
        forced = (blk[None, :] < N_INIT_BLOCKS) | ((blk[None, :] <= qblk) & (blk[None, :] > qblk - N_LOCAL_BLOCKS))
        causal_blk = blk[None, :] <= qblk
        score = jnp.where(forced[None, :, None, :], FORCE_SCORE,
                          jnp.where(causal_blk[None, :, None, :], imp, NEG_INF))
        _, idx = lax.top_k(score, n_top)
        kg = kb[bi, gi, idx].reshape(B, QC, G, n_top * SEL_BLOCK, HD)
        vg = vb[bi, gi, idx].reshape(B, QC, G, n_top * SEL_BLOCK, HD)
        tok = (idx[..., None] * SEL_BLOCK + off).reshape(B, QC, G, n_top * SEL_BLOCK)
        ok = (tok <= pc[None, :, None, None])[:, :, :, None, :]
        s2 = jnp.einsum('bqgrd,bqgkd->bqgrk', qg, kg).astype(jnp.float32) * scale
        p2 = jax.nn.softmax(jnp.where(ok, s2, NEG_INF), axis=-1)
        o_slc = jnp.einsum('bqgrk,bqgkd->bqgrd', p2.astype(vg.dtype), vg)
        return (gc[..., 0:1] * o_cmp.reshape(B, QC, H, HD)
                + gc[..., 1:2] * o_slc.reshape(B, QC, H, HD))

    xs = (q.reshape(B, nq, QC, H, HD).swapaxes(0, 1),
          gates.reshape(B, nq, QC, H, 3).swapaxes(0, 1),
          pos.reshape(nq, QC))
    o = lax.map(chunk_fn, xs)
    return o.swapaxes(0, 1).reshape(B, T, H, HD)


def window_attn(q, k_pad, v_pad, q_pos0):
    B, T, H, HD = q.shape
    G, R = NSA_KV_HEADS, NSA_GROUP
    QB = WIN_Q_BLOCK if T % WIN_Q_BLOCK == 0 else T
    nb = T // QB
    kidx = jnp.arange(nb)[:, None] * QB + jnp.arange(WINDOW + QB)[None, :]
    kb, vb = k_pad[:, kidx], v_pad[:, kidx]
    qb = q.reshape(B, nb, QB, G, R, HD)
    qpos = q_pos0 + jnp.arange(T).reshape(nb, QB)
    kpos = q_pos0 - WINDOW + kidx
    rel = qpos[:, :, None] - kpos[:, None, :]
    ok = (rel >= 0) & (rel < WINDOW) & (kpos[:, None, :] >= 0)
    s = jnp.einsum('bcqgrd,bckgd->bcgrqk', qb, kb).astype(jnp.float32) * HEAD_DIM ** -0.5
    p = jax.nn.softmax(jnp.where(ok[None, :, None, None], s, NEG_INF), axis=-1)
    o = jnp.einsum('bcgrqk,bckgd->bcqgrd', p.astype(vb.dtype), vb)
    return o.reshape(B, T, H, HD)


def trunk_layer(x, pos0, past_cmp, past_slc, win_buf, gla_s0, rwkv_s0, shift0, conv0, lw):
    B, T, _ = x.shape
    dt = x.dtype
    f32 = jnp.float32
    pos = pos0 + jnp.arange(T, dtype=jnp.int32)
    heads = lambda t, n: t.reshape(B, T, n, t.shape[-1] // n)

    bf = jnp.bfloat16
    M = B * T
    x2 = x.reshape(M, D_MODEL)
    xn = rmsnorm_pallas(x2, lw['norm1'])
    hp_ = mm(xn, lw['w_mix']).reshape(B, T, N_MIX_PAD)
    mg = mm(xn, lw['w_mg'])
    (gq, gk, gv, gog, glo, nq, nkc, nvc, nks, nvs, nkw, nvw, ng, rw) = [
        hp_[..., o:o + s] for o, s in zip(_SEG_OFF, IN_SIZES[:N_MIX_SEG])]

    if T % GLA_TC == 0:
        o_gla, gla_s = gla_pallas(hp_, lw['gla_wa2'], lw['gla_ba'], lw['gla_norm'], gla_s0.astype(f32), _SEG_OFF[:5])
    else:
        log_a = jax.nn.log_sigmoid((glo @ lw['gla_wa2'] + lw['gla_ba']).astype(f32)) / GLA_TAU
        o_gla, gla_s = gla_chunked(heads(gq, GLA_HEADS) * GLA_DK ** -0.5, heads(gk, GLA_HEADS),
                                   heads(gv, GLA_HEADS), heads(log_a, GLA_HEADS), gla_s0)
        o_gla = rmsnorm(o_gla, lw['gla_norm']) * jax.nn.silu(heads(gog, GLA_HEADS).astype(f32))
    o_gla = o_gla.reshape(M, GLA_WIDTH).astype(bf)

    kvh = lambda t: heads(t, NSA_KV_HEADS)
    qn = rope(heads(nq, NSA_HEADS), pos)
    new_cmp = jnp.stack([rope(kvh(nkc), pos), kvh(nvc)], axis=2)
    new_slc = jnp.stack([rope(kvh(nks), pos), kvh(nvs)], axis=2)
    new_win = jnp.stack([rope(kvh(nkw), pos), kvh(nvw)], axis=2)
    cmp_all = jnp.concatenate([past_cmp.astype(dt), new_cmp], axis=1)
    slc_all = jnp.concatenate([past_slc.astype(dt), new_slc], axis=1)
    win_all = jnp.concatenate([win_buf.astype(dt), new_win], axis=1)
    n_win = win_all.shape[1]
    win_pad = jnp.pad(win_all, ((0, 0), (WINDOW + T - n_win, 0), (0, 0), (0, 0), (0, 0)))
    g_nsa = jax.nn.sigmoid(ng.astype(f32)).reshape(B, T, NSA_HEADS, 3)
    if past_cmp.shape[1] == 0 and T % NSA_TQ == 0:
        ncp = _round_up(T // CMP_STRIDE, LANE)
        padc = lambda t: jnp.pad(t.transpose(0, 2, 1, 3),
                                 ((0, 0), (0, 0), (0, ncp - t.shape[1]), (0, 0))).astype(bf)
        kcmp = padc(compress_blocks(new_cmp[:, :, 0], lw['cmp_w1k'], lw['cmp_w2k'], lw['cmp_pek']))
        vcmp = padc(compress_blocks(new_cmp[:, :, 1], lw['cmp_w1v'], lw['cmp_w2v'], lw['cmp_pev']))
        flat = lambda t: t.reshape(B, T, NSA_KV_WIDTH).astype(bf)
        o_nsa = nsa_attention_pallas(
            (qn * HEAD_DIM ** -0.5).reshape(B, T, NSA_WIDTH).astype(bf), kcmp, vcmp,
            flat(new_slc[:, :, 0]), flat(new_slc[:, :, 1]), flat(new_win[:, :, 0]), flat(new_win[:, :, 1]),
            g_nsa.reshape(B, T, NSA_KV_HEADS, 3 * NSA_GROUP).transpose(0, 2, 1, 3))
    else:
        o_nsa = nsa_compressed_selected(qn, cmp_all[:, :, 0], cmp_all[:, :, 1], slc_all[:, :, 0], slc_all[:, :, 1],
                                        g_nsa, pos, lw['cmp_w1k'], lw['cmp_w2k'], lw['cmp_pek'],
                                        lw['cmp_w1v'], lw['cmp_w2v'], lw['cmp_pev'])
        o_nsa = o_nsa + g_nsa[..., 2:3] * window_attn(qn, win_pad[:, :, 0], win_pad[:, :, 1], pos0)
    o_nsa = o_nsa.reshape(M, NSA_WIDTH).astype(bf)
    win_new = win_all[:, n_win - min(WINDOW, n_win):]

    prev = jnp.concatenate([shift0[:, None].astype(dt), rw[:, :-1]], axis=1)
    rm = (rw + (prev - rw) * lw['rwkv_mu']).astype(f32)
    r_, k_, v_, w_lo, a_lo, g_lo = split_sizes(rm, RWKV_SIZES)
    w_raw = lw['rwkv_w0'] + jnp.tanh(w_lo) @ lw['rwkv_w2']
    w_log = -jnp.exp(-jax.nn.softplus(-w_raw) - 0.5)
    a = jax.nn.sigmoid(lw['rwkv_a0'] + a_lo @ lw['rwkv_a2'])
    gate = jax.nn.sigmoid(g_lo) @ lw['rwkv_g2']
    rh = lambda t: heads(t, RWKV_HEADS)
    kk = rh(k_ * lw['rwkv_kk'])
    kk = kk * lax.rsqrt(jnp.maximum(jnp.sum(kk * kk, axis=-1, keepdims=True), 1e-24))
    k2 = k_ * (1.0 + (a - 1.0) * lw['rwkv_ka'])
    y, rwkv_s = rwkv_scan_pallas(r_, w_log, k2, v_, kk.reshape(B, T, RWKV_WIDTH), a, rwkv_s0.astype(f32))
    y = rh(y)
    y_mean = jnp.mean(y, axis=-1, keepdims=True)
    y_var = jnp.mean(jnp.square(y - y_mean), axis=-1, keepdims=True)
    yn = ((y - y_mean) * lax.rsqrt(y_var + RWKV_LN_EPS)).reshape(B, T, RWKV_WIDTH) * lw['rwkv_ln_w'] + lw['rwkv_ln_b']
    bonus = jnp.sum(rh(r_) * rh(k2) * lw['rwkv_rk'], axis=-1, keepdims=True) * rh(v_)
    o_rwkv = ((yn + bonus.reshape(B, T, RWKV_WIDTH)) * gate).reshape(M, RWKV_WIDTH).astype(bf)
    shift_new = rw[:, -1]

    merged = merge_mm(o_gla, o_nsa, o_rwkv, lw['w_o_gla'], lw['w_o_nsa'], lw['w_o_rwkv'], mg)
    x2 = mm(merged, lw['w_out'], res=x2)

    xn2 = rmsnorm_pallas(x2, lw['norm2'])
    act, conv_new = ffn_gate_up(xn2, lw['ffn_gate'], lw['ffn_up'], lw['ffn_conv'], lw['ffn_conv_b'],
                                conv0.astype(f32), B, T)
    x2 = mm(act, lw['ffn_down'], res=x2)
    return x2.reshape(B, T, D_MODEL), (new_cmp, new_slc, win_new, gla_s, rwkv_s, shift_new, conv_new)


def _repack_w_mix(w):
    parts = []
    off = 0
    for s, p in zip(IN_SIZES[:N_MIX_SEG], _SEG_PAD):
        seg = w[:, off:off + s].astype(jnp.bfloat16)
        parts.append(jnp.pad(seg, ((0, 0), (0, p - s))) if p != s else seg)
        off += s
    tail = N_MIX_PAD - sum(_SEG_PAD)
    if tail:
        parts.append(jnp.zeros((w.shape[0], tail), jnp.bfloat16))
    return jnp.concatenate(parts, axis=1)


def kernel(x_prompt, x_sample, cache_cmp_kv, cache_slc_kv, cache_win_kv, state_gla, state_rwkv, state_rwkv_shift, state_ffn_conv, page_table, norm1, w_in, gla_wa2, gla_ba, gla_norm, w_o_gla, cmp_w1k, cmp_w2k, cmp_pek, cmp_w1v, cmp_w2v, cmp_pev, w_o_nsa, rwkv_mu, rwkv_w0, rwkv_w2, rwkv_a0, rwkv_a2, rwkv_g2, rwkv_kk, rwkv_ka, rwkv_rk, rwkv_ln_w, rwkv_ln_b, w_o_rwkv, w_out, norm2, ffn_gate, ffn_conv, ffn_conv_b, ffn_up, ffn_down, norm_f):
    G, HD = NSA_KV_HEADS, HEAD_DIM
    n_db, n_pages = page_table.shape
    past_len = n_pages * PAGE_SIZE
    bp = x_prompt.shape[0]
    dt = x_prompt.dtype
    bf = jnp.bfloat16
    xp, xs = x_prompt, x_sample
    st_p, st_s = [], []
    for l in range(DEPTH):
        lw = {'norm1': norm1[l], 'w_mix': _repack_w_mix(w_in[l]),
              'w_mg': w_in[l][:, sum(IN_SIZES[:N_MIX_SEG]):].astype(bf), 'gla_wa2': gla_wa2[l], 'gla_ba': gla_ba[l],
              'gla_norm': gla_norm[l], 'w_o_gla': w_o_gla[l].astype(bf), 'cmp_w1k': cmp_w1k[l], 'cmp_w2k': cmp_w2k[l],
              'cmp_pek': cmp_pek[l], 'cmp_w1v': cmp_w1v[l], 'cmp_w2v': cmp_w2v[l], 'cmp_pev': cmp_pev[l],
              'w_o_nsa': w_o_nsa[l].astype(bf), 'rwkv_mu': rwkv_mu[l], 'rwkv_w0': rwkv_w0[l], 'rwkv_w2': rwkv_w2[l],
              'rwkv_a0': rwkv_a0[l], 'rwkv_a2': rwkv_a2[l], 'rwkv_g2': rwkv_g2[l], 'rwkv_kk': rwkv_kk[l],
              'rwkv_ka': rwkv_ka[l], 'rwkv_rk': rwkv_rk[l], 'rwkv_ln_w': rwkv_ln_w[l], 'rwkv_ln_b': rwkv_ln_b[l],
              'w_o_rwkv': w_o_rwkv[l].astype(bf), 'w_out': w_out[l].astype(bf), 'norm2': norm2[l],
              'ffn_gate': ffn_gate[l].astype(bf), 'ffn_conv': ffn_conv[l], 'ffn_conv_b': ffn_conv_b[l],
              'ffn_up': ffn_up[l].astype(bf), 'ffn_down': ffn_down[l].astype(bf)}
        empty = jnp.zeros((bp, 0, 2, G, HD), dt)
        xp, sp = trunk_layer(xp, 0, empty, empty, empty,
                             jnp.zeros((bp, GLA_HEADS, GLA_DK, GLA_DV), jnp.float32),
                             jnp.zeros((bp, RWKV_HEADS, RWKV_N, RWKV_N), jnp.float32),
                             jnp.zeros((bp, RWKV_COLS), dt),
                             jnp.zeros((bp, CONV_W - 1, D_FF), dt), lw)
        past_c = cache_cmp_kv[l, page_table].reshape(n_db, past_len, 2, G, HD)
        past_s = cache_slc_kv[l, page_table].reshape(n_db, past_len, 2, G, HD)
        xs, ss = trunk_layer(xs, past_len, past_c, past_s, cache_win_kv[l], state_gla[l], state_rwkv[l],
                             state_rwkv_shift[l], state_ffn_conv[l], lw)
        st_p.append(sp)
        st_s.append(ss)
    y_prompt = rmsnorm_pallas(xp.reshape(-1, D_MODEL), norm_f, out_dtype=dt).reshape(xp.shape)
    y_sample = rmsnorm_pallas(xs.reshape(-1, D_MODEL), norm_f, out_dtype=dt).reshape(xs.shape)
    outs = [y_prompt, y_sample]
    for i in range(7):
        outs.append(jnp.stack([s[i] for s in st_p]))
        outs.append(jnp.stack([s[i] for s in st_s]))
    return tuple(outs)
```

```python
import functools

import jax
import jax.numpy as jnp
import numpy as np
from jax import lax
from jax.experimental import pallas as pl
from jax.experimental.pallas import tpu as pltpu

D_MODEL = 4096
DEPTH = 2
PAGE_SIZE = 128
HEAD_DIM = 128
ROPE_DIM = HEAD_DIM // 4
ROPE_THETA = 500000.0
NORM_EPS = 1e-5
NEG_INF = -1e30

GLA_WIDTH = D_MODEL // 4
GLA_HEADS = 4
GLA_DV = GLA_WIDTH // GLA_HEADS
GLA_DK = GLA_DV // 2
GLA_GATE_RANK = 16
GLA_TAU = 16.0
GLA_CHUNK = 64

NSA_HEADS = D_MODEL // 256
NSA_KV_HEADS = 4
NSA_GROUP = NSA_HEADS // NSA_KV_HEADS
NSA_WIDTH = NSA_HEADS * HEAD_DIM
NSA_KV_WIDTH = NSA_KV_HEADS * HEAD_DIM
CMP_STRIDE = 16
CMP_BLOCK = 2 * CMP_STRIDE
SEL_BLOCK = 64
SEL_TOP = 16
N_INIT_BLOCKS = 1
N_LOCAL_BLOCKS = 2
WINDOW = 512
SEL_Q_BLOCK = 32
WIN_Q_BLOCK = 128
FORCE_SCORE = 1e4

RWKV_WIDTH = D_MODEL // 4
RWKV_N = 64
RWKV_HEADS = RWKV_WIDTH // RWKV_N
RWKV_DECAY_RANK = 64
RWKV_AAA_RANK = 64
RWKV_GATE_RANK = 160
RWKV_SIZES = (RWKV_WIDTH, RWKV_WIDTH, RWKV_WIDTH, RWKV_DECAY_RANK, RWKV_AAA_RANK, RWKV_GATE_RANK)
RWKV_COLS = sum(RWKV_SIZES)
RWKV_LN_EPS = 64e-5

N_BRANCH = 3
D_FF = 256 * ((8 * D_MODEL // 3 + 255) // 256)
CONV_W = 3

IN_SIZES = (GLA_HEADS * GLA_DK, GLA_HEADS * GLA_DK, GLA_WIDTH, GLA_WIDTH, GLA_GATE_RANK,
            NSA_WIDTH, NSA_KV_WIDTH, NSA_KV_WIDTH, NSA_KV_WIDTH, NSA_KV_WIDTH, NSA_KV_WIDTH, NSA_KV_WIDTH,
            NSA_HEADS * 3,
            RWKV_COLS,
            N_BRANCH * D_MODEL)

LANE = 128
SUBLANE = 8
NSA_TQ = 256
GLA_TC = 256
VMEM_LIMIT = 48 * 1024 * 1024


def _round_up(n, m):
    return -(-n // m) * m


W_IN_TILE = 512
_C_NSA = sum(IN_SIZES[:5])
_C_RWKV = sum(IN_SIZES[:13])
_C_MG = sum(IN_SIZES[:14])
GLA_OFF = tuple(int(o) for o in np.concatenate([[0], np.cumsum(IN_SIZES[:4])]))
NSA_GATE_OFF = NSA_WIDTH + 6 * NSA_KV_WIDTH


def _pick(n, cands):
    for c in cands:
        if n % c == 0:
            return c
    return n


def _rmsnorm_kernel(x_ref, g_ref, o_ref):
    x = x_ref[...]
    y = x * lax.rsqrt(jnp.mean(x * x, axis=-1, keepdims=True) + NORM_EPS)
    o_ref[...] = (y * g_ref[...]).astype(o_ref.dtype)


def rmsnorm_pallas(x, g, out_dtype=jnp.bfloat16):
    M, D = x.shape
    tm = _pick(M, (256, 128, 64, 32, 16, 8))
    return pl.pallas_call(
        _rmsnorm_kernel,
        grid=(M // tm,),
        in_specs=[pl.BlockSpec((tm, D), lambda i: (i, 0)), pl.BlockSpec((1, D), lambda i: (0, 0))],
        out_specs=pl.BlockSpec((tm, D), lambda i: (i, 0)),
        out_shape=jax.ShapeDtypeStruct((M, D), out_dtype),
        compiler_params=pltpu.CompilerParams(dimension_semantics=("parallel",), vmem_limit_bytes=VMEM_LIMIT),
        name="rmsnorm",
    )(x, g.reshape(1, D))


def _mm_kernel(*refs, nk, has_res):
    x_ref, w_ref = refs[:2]
    res_ref = refs[2] if has_res else None
    o_ref, acc_ref = refs[-2:]
    k = pl.program_id(2)
    part = jnp.dot(x_ref[...], w_ref[...], preferred_element_type=jnp.float32)

    def finish(v):
        if has_res:
            v = v + res_ref[...]
        o_ref[...] = v.astype(o_ref.dtype)

    if nk == 1:
        finish(part)
    else:
        @pl.when(k == 0)
        def _():
            acc_ref[...] = part

        @pl.when(jnp.logical_and(k > 0, k < nk - 1))
        def _():
            acc_ref[...] += part

        @pl.when(k == nk - 1)
        def _():
            finish(acc_ref[...] + part)


def mm(x, w, res=None, out_dtype=jnp.float32):
    M, K = x.shape
    N = w.shape[1]
    tm = _pick(M, (1024, 512, 256, 128, 64, 32, 16, 8))
    tn = _pick(N, (512, 256, 128))
    tk = K if K <= 4096 else _pick(K, (5504, 4096, 2048, 1024, 512))
    nk = K // tk
    in_specs = [pl.BlockSpec((tm, tk), lambda i, j, k: (i, k)),
                pl.BlockSpec((tk, tn), lambda i, j, k: (k, j))]
    args = [x, w]
    if res is not None:
        in_specs.append(pl.BlockSpec((tm, tn), lambda i, j, k: (i, j)))
        args.append(res)
    return pl.pallas_call(
        functools.partial(_mm_kernel, nk=nk, has_res=res is not None),
        grid=(M // tm, N // tn, nk),
        in_specs=in_specs,
        out_specs=pl.BlockSpec((tm, tn), lambda i, j, k: (i, j)),
        out_shape=jax.ShapeDtypeStruct((M, N), out_dtype),
        scratch_shapes=[pltpu.VMEM((tm, tn) if nk > 1 else (SUBLANE, LANE), jnp.float32)],
        compiler_params=pltpu.CompilerParams(
            dimension_semantics=("parallel", "parallel", "arbitrary"),
            vmem_limit_bytes=VMEM_LIMIT),
        name="dense_mm",
    )(*args)


def _merge_kernel(oa_ref, ob_ref, oc_ref, wa_ref, wb_ref, wc_ref, ga_ref, gb_ref, gc_ref, o_ref):
    f32 = jnp.float32
    acc = jax.nn.sigmoid(ga_ref[...]) * jnp.dot(oa_ref[...], wa_ref[...], preferred_element_type=f32)
    acc += jax.nn.sigmoid(gb_ref[...]) * jnp.dot(ob_ref[...], wb_ref[...], preferred_element_type=f32)
    acc += jax.nn.sigmoid(gc_ref[...]) * jnp.dot(oc_ref[...], wc_ref[...], preferred_element_type=f32)
    o_ref[...] = acc.astype(o_ref.dtype)


def merge_mm(o_a, o_b, o_c, w_a, w_b, w_c, mg, out_dtype=jnp.bfloat16):
    M = o_a.shape[0]
    D = w_a.shape[1]
    tm = _pick(M, (512, 256, 128, 64, 32, 16, 8))
    tn = _pick(D, (512, 256, 128))
    nj = D // tn
    o_spec = lambda o: pl.BlockSpec((tm, o.shape[1]), lambda i, j: (i, 0))
    w_spec = lambda w: pl.BlockSpec((w.shape[0], tn), lambda i, j: (0, j))
    g_spec = lambda b: pl.BlockSpec((tm, tn), lambda i, j: (i, b * nj + j))
    return pl.pallas_call(
        _merge_kernel,
        grid=(M // tm, nj),
        in_specs=[o_spec(o_a), o_spec(o_b), o_spec(o_c), w_spec(w_a), w_spec(w_b), w_spec(w_c),
                  g_spec(0), g_spec(1), g_spec(2)],
        out_specs=pl.BlockSpec((tm, tn), lambda i, j: (i, j)),
        out_shape=jax.ShapeDtypeStruct((M, D), out_dtype),
        compiler_params=pltpu.CompilerParams(dimension_semantics=("parallel", "parallel"),
                                             vmem_limit_bytes=VMEM_LIMIT),
        name="merge_mm",
    )(o_a, o_b, o_c, w_a, w_b, w_c, mg, mg, mg)


def _ffn_gate_up_kernel(x_ref, wg_ref, wu_ref, cw_ref, cb_ref, st_ref, act_ref, tail_ref, carry_scr, *, tm, tps, T):
    f32 = jnp.float32
    i, j = pl.program_id(0), pl.program_id(1)
    x = x_ref[...]
    h = jnp.dot(x, wg_ref[...], preferred_element_type=f32)
    u = jnp.dot(x, wu_ref[...], preferred_element_type=f32)
    cw = cw_ref[...]
    if T == 1:
        prev2, prev1 = st_ref[0], st_ref[1]
        tail_ref[0] = prev1
        tail_ref[1] = h
    else:
        tail = jnp.where(i % tps == 0, st_ref[...], carry_scr[j])
        row = lax.broadcasted_iota(jnp.int32, h.shape, 0)
        prev1 = jnp.where(row == 0, tail[7:8], pltpu.roll(h, 1, axis=0))
        prev2 = jnp.where(row == 0, tail[6:7], jnp.where(row == 1, tail[7:8], pltpu.roll(h, 2, axis=0)))
        last = h[tm - SUBLANE:tm]
        carry_scr[j] = last
        tail_ref[...] = last
    hc = cb_ref[...] + prev2 * cw[0:1] + prev1 * cw[1:2] + h * cw[2:3]
    act_ref[...] = (jax.nn.silu(hc) * u).astype(act_ref.dtype)


def ffn_gate_up(xn, w_gate, w_up, conv_w, conv_b, conv0, B, T):
    M, D = xn.shape
    F = w_gate.shape[1]
    tn = _pick(F, (512, 256, 128))
    nj = F // tn
    cw = jnp.pad(conv_w, ((0, SUBLANE - CONV_W), (0, 0)))
    cb = conv_b.reshape(1, F)
    if T == 1:
        tm, tps = M, 1
        st = conv0.transpose(1, 0, 2)
        st_spec = pl.BlockSpec((2, B, tn), lambda i, j: (0, 0, j))
        tail_shape, tail_spec = (2, B, F), pl.BlockSpec((2, B, tn), lambda i, j: (0, 0, j))
    else:
        tm = _pick(T, (1024, 512, 256, 128, 64, 32, 16, 8))
        tps = T // tm
        st = jnp.pad(conv0, ((0, 0), (SUBLANE - 2, 0), (0, 0)))
        st_spec = pl.BlockSpec((None, SUBLANE, tn), lambda i, j: (i // tps, 0, j))
        tail_shape, tail_spec = (B, SUBLANE, F), pl.BlockSpec((None, SUBLANE, tn), lambda i, j: (i // tps, 0, j))
    act, tail = pl.pallas_call(
        functools.partial(_ffn_gate_up_kernel, tm=tm, tps=tps, T=T),
        grid=(M // tm, nj),
        in_specs=[pl.BlockSpec((tm, D), lambda i, j: (i, 0)),
                  pl.BlockSpec((D, tn), lambda i, j: (0, j)),
                  pl.BlockSpec((D, tn), lambda i, j: (0, j)),
                  pl.BlockSpec((SUBLANE, tn), lambda i, j: (0, j)),
                  pl.BlockSpec((1, tn), lambda i, j: (0, j)),
                  st_spec],
        out_specs=[pl.BlockSpec((tm, tn), lambda i, j: (i, j)), tail_spec],
        out_shape=[jax.ShapeDtypeStruct((M, F), jnp.bfloat16), jax.ShapeDtypeStruct(tail_shape, jnp.float32)],
        scratch_shapes=[pltpu.VMEM((nj, SUBLANE, tn), jnp.float32)],
        compiler_params=pltpu.CompilerParams(dimension_semantics=("arbitrary", "arbitrary"),
                                             vmem_limit_bytes=VMEM_LIMIT),
        name="ffn_gate_up",
    )(xn, w_gate, w_up, cw, cb, st)
    conv_new = tail.transpose(1, 0, 2) if T == 1 else tail[:, SUBLANE - 2:]
    return act, conv_new


def _gla_kernel(q_ref, k_ref, v_ref, og_ref, lo_ref, wa_ref, ba_ref, gn_ref, s0_ref, o_ref, sout_ref, s_scr, *, Tc, C):
    f32, bf16 = jnp.float32, jnp.bfloat16
    c = pl.program_id(2)

    @pl.when(c == 0)
    def _():
        s_scr[...] = s0_ref[...]

    row = lax.broadcasted_iota(jnp.int32, (C, GLA_DK), 0)
    tril = lax.broadcasted_iota(jnp.int32, (C, C), 0) >= lax.broadcasted_iota(jnp.int32, (C, C), 1)
    wa = wa_ref[...]
    ba = ba_ref[...]
    gn = gn_ref[...]
    S = s_scr[...]
    for n in range(Tc // C):
        rows = slice(n * C, (n + 1) * C)
        la = jax.nn.log_sigmoid(jnp.dot(lo_ref[rows, :].astype(bf16), wa, preferred_element_type=f32) + ba) / GLA_TAU
        bc = la
        d = 1
        while d < C:
            bc = bc + jnp.where(row >= d, pltpu.roll(bc, d, axis=0), 0.0)
            d *= 2
        b_mid = bc[C // 2:C // 2 + 1]
        b_last = bc[C - 1:C]
        q = q_ref[rows, :] * GLA_DK ** -0.5
        k = k_ref[rows, :]
        v = v_ref[rows, :].astype(bf16)
        att = lax.dot_general((q * jnp.exp(bc - b_mid)).astype(bf16), (k * jnp.exp(b_mid - bc)).astype(bf16),
                              (((1,), (1,)), ((), ())), preferred_element_type=f32)
        att = jnp.where(tril, att, 0.0)
        o = jnp.dot(att.astype(bf16), v, preferred_element_type=f32)
        o = o + jnp.dot((q * jnp.exp(bc)).astype(bf16), S.astype(bf16), preferred_element_type=f32)
        kd = jnp.concatenate([k * jnp.exp(b_last - bc), jnp.broadcast_to(jnp.exp(b_last), (SUBLANE, GLA_DK))], axis=0)
        kdt = kd.T
        S = kdt[:, C:C + 1] * S + jnp.dot(kdt[:, :C].astype(bf16), v, preferred_element_type=f32)
        o = o * lax.rsqrt(jnp.mean(o * o, axis=-1, keepdims=True) + NORM_EPS) * gn
        o_ref[rows, :] = (o * jax.nn.silu(og_ref[rows, :])).astype(o_ref.dtype)
    s_scr[...] = S

    @pl.when(c == pl.num_programs(2) - 1)
    def _():
        sout_ref[...] = S


def gla_pallas(hmix, wa2, ba, gnorm, s0, seg_off):
    B, T, _ = hmix.shape
    H = GLA_HEADS
    Tc = min(GLA_TC, T)
    C = GLA_CHUNK
    oq, ok, ov, og, ol = seg_off
    wa = jnp.pad(wa2, ((0, LANE - wa2.shape[0]), (0, 0))).astype(jnp.bfloat16)
    col = lambda off, w: (lambda b, h, c: (b, c, off // w + h))
    o, s = pl.pallas_call(
        functools.partial(_gla_kernel, Tc=Tc, C=C),
        grid=(B, H, T // Tc),
        in_specs=[pl.BlockSpec((None, Tc, GLA_DK), col(oq, GLA_DK)),
                  pl.BlockSpec((None, Tc, GLA_DK), col(ok, GLA_DK)),
                  pl.BlockSpec((None, Tc, GLA_DV), col(ov, GLA_DV)),
                  pl.BlockSpec((None, Tc, GLA_DV), col(og, GLA_DV)),
                  pl.BlockSpec((None, Tc, LANE), lambda b, h, c: (b, c, ol // LANE)),
                  pl.BlockSpec((LANE, GLA_DK), lambda b, h, c: (0, h)),
                  pl.BlockSpec((1, GLA_DK), lambda b, h, c: (0, h)),
                  pl.BlockSpec((1, GLA_DV), lambda b, h, c: (0, 0)),
                  pl.BlockSpec((None, None, GLA_DK, GLA_DV), lambda b, h, c: (b, h, 0, 0))],
        out_specs=[pl.BlockSpec((None, Tc, GLA_DV), lambda b, h, c: (b, c, h)),
                   pl.BlockSpec((None, None, GLA_DK, GLA_DV), lambda b, h, c: (b, h, 0, 0))],
        out_shape=[jax.ShapeDtypeStruct((B, T, H * GLA_DV), jnp.bfloat16),
                   jax.ShapeDtypeStruct((B, H, GLA_DK, GLA_DV), jnp.float32)],
        scratch_shapes=[pltpu.VMEM((GLA_DK, GLA_DV), jnp.float32)],
        compiler_params=pltpu.CompilerParams(dimension_semantics=("parallel", "parallel", "arbitrary"),
                                             vmem_limit_bytes=VMEM_LIMIT),
        name="gla_chunked",
    )(hmix, hmix, hmix, hmix, hmix, wa, ba.reshape(1, -1), gnorm.reshape(1, -1), s0)
    return o, s


def _head_sums(x, bd):
    f32, bf16 = jnp.float32, jnp.bfloat16
    outs = []
    for t in range(x.shape[1] // LANE):
        p = x[:, t * LANE:(t + 1) * LANE]
        hi = p.astype(bf16)
        lo = (p - hi.astype(f32)).astype(bf16)
        outs.append(jnp.dot(hi, bd, preferred_element_type=f32) + jnp.dot(lo, bd, preferred_element_type=f32))
    return jnp.concatenate(outs, axis=1)


def _block_diag_ones():
    rr = lax.broadcasted_iota(jnp.int32, (LANE, LANE), 0) // RWKV_N
    cc = lax.broadcasted_iota(jnp.int32, (LANE, LANE), 1) // RWKV_N
    return jnp.where(rr == cc, 1.0, 0.0).astype(jnp.bfloat16)


def _rwkv_prep_kernel(x_ref, sh_ref, mu_ref, w0_ref, a0_ref, kkw_ref, ka_ref, rk_ref, w2_ref, a2_ref, g2_ref,
                      r_ref, wl_ref, k_ref, v_ref, kk_ref, a_ref, gate_ref, bonus_ref, tail_ref, carry_scr, *, tm, T):
    f32, bf16 = jnp.float32, jnp.bfloat16
    W = RWKV_WIDTH
    i = pl.program_id(1)
    x = x_ref[...]
    if T == 1:
        prev = sh_ref[...]
        tail_ref[...] = x
    else:
        first = jnp.where(i == 0, sh_ref[SUBLANE - 1:SUBLANE], carry_scr[SUBLANE - 1:SUBLANE])
        row = lax.broadcasted_iota(jnp.int32, x.shape, 0)
        prev = jnp.where(row == 0, first, pltpu.roll(x, 1, axis=0))
        last = x[tm - SUBLANE:tm]
        carry_scr[...] = last
        tail_ref[...] = last
    rm = x + (prev - x) * mu_ref[...]
    r, k, v = rm[:, :W], rm[:, W:2 * W], rm[:, 2 * W:3 * W]
    lo = rm[:, 3 * W:3 * W + LANE]
    glo = rm[:, 3 * W + LANE:]
    w_raw = w0_ref[...] + jnp.dot(jnp.tanh(lo).astype(bf16), w2_ref[...], preferred_element_type=f32)
    wl_ref[...] = -jnp.exp(-jax.nn.softplus(-w_raw) - 0.5)
    a = jax.nn.sigmoid(a0_ref[...] + jnp.dot(lo.astype(bf16), a2_ref[...], preferred_element_type=f32))
    gate_ref[...] = jnp.dot(jax.nn.sigmoid(glo).astype(bf16), g2_ref[...], preferred_element_type=f32)
    bd = _block_diag_ones()
    kk = k * kkw_ref[...]
    kk_ref[...] = kk * lax.rsqrt(jnp.maximum(_head_sums(kk * kk, bd), 1e-24))
    k2 = k * (1.0 + (a - 1.0) * ka_ref[...])
    bonus_ref[...] = _head_sums(r * k2 * rk_ref[...], bd) * v
    r_ref[...] = r
    k_ref[...] = k2
    v_ref[...] = v
    a_ref[...] = a


def rwkv_prep(hr, shift0, mu, w0, w2, a0, a2, g2, kkw, ka, rk):
    B, T, WP = hr.shape
    W = RWKV_WIDTH
    bf16 = jnp.bfloat16
    padc = lambda t: jnp.pad(t, ((0, 0), (0, WP - t.shape[1])))
    w2p = jnp.pad(w2, ((0, LANE - RWKV_DECAY_RANK), (0, 0))).astype(bf16)
    a2p = jnp.pad(a2, ((RWKV_DECAY_RANK, 0), (0, 0))).astype(bf16)
    gpad = WP - 3 * W - LANE
    g2p = jnp.pad(g2, ((0, gpad - RWKV_GATE_RANK), (0, 0))).astype(bf16)
    row = lambda t: t.reshape(1, -1)
    if T == 1:
        tm = 1
        sh = padc(shift0).reshape(B, 1, WP)
        sh_spec = pl.BlockSpec((None, 1, WP), lambda b, i: (b, 0, 0))
        tail_rows = 1
    else:
        tm = _pick(T, (256, 128, 64, 32, 16, 8))
        sh = jnp.broadcast_to(padc(shift0)[:, None, :], (B, SUBLANE, WP))
        sh_spec = pl.BlockSpec((None, SUBLANE, WP), lambda b, i: (b, 0, 0))
        tail_rows = SUBLANE
    vec = lambda n: pl.BlockSpec((1, n), lambda b, i: (0, 0))
    mat = lambda m: pl.BlockSpec(m.shape, lambda b, i: (0, 0))
    o_spec = pl.BlockSpec((None, tm, W), lambda b, i: (b, i, 0))
    o_shape = jax.ShapeDtypeStruct((B, T, W), jnp.float32)
    outs = pl.pallas_call(
        functools.partial(_rwkv_prep_kernel, tm=tm, T=T),
        grid=(B, T // tm),
        in_specs=[pl.BlockSpec((None, tm, WP), lambda b, i: (b, i, 0)), sh_spec, vec(WP),
                  vec(W), vec(W), vec(W), vec(W), vec(W), mat(w2p), mat(a2p), mat(g2p)],
        out_specs=[o_spec] * 8 + [pl.BlockSpec((None, tail_rows, WP), lambda b, i: (b, 0, 0))],
        out_shape=[o_shape] * 8 + [jax.ShapeDtypeStruct((B, tail_rows, WP), jnp.float32)],
        scratch_shapes=[pltpu.VMEM((SUBLANE, WP), jnp.float32)],
        compiler_params=pltpu.CompilerParams(dimension_semantics=("parallel", "arbitrary"),
                                             vmem_limit_bytes=VMEM_LIMIT),
        name="rwkv_prep",
    )(hr, sh, row(padc(mu.reshape(1, -1))), row(w0), row(a0), row(kkw), row(ka), row(rk), w2p, a2p, g2p)
    return outs[:8], outs[8][:, tail_rows - 1, :RWKV_COLS]


def _rwkv_post_kernel(y_ref, bonus_ref, gate_ref, lw_ref, lb_ref, o_ref):
    bd = _block_diag_ones()
    y = y_ref[...]
    d = y - _head_sums(y, bd) * (1.0 / RWKV_N)
    var = _head_sums(d * d, bd) * (1.0 / RWKV_N)
    yn = d * lax.rsqrt(var + RWKV_LN_EPS) * lw_ref[...] + lb_ref[...]
    o_ref[...] = ((yn + bonus_ref[...]) * gate_ref[...]).astype(o_ref.dtype)


def rwkv_post(y, bonus, gate, ln_w, ln_b):
    M, W = y.shape
    tm = _pick(M, (256, 128, 64, 32, 16, 8))
    spec = pl.BlockSpec((tm, W), lambda i: (i, 0))
    vec = pl.BlockSpec((1, W), lambda i: (0, 0))
    return pl.pallas_call(
        _rwkv_post_kernel,
        grid=(M // tm,),
        in_specs=[spec, spec, spec, vec, vec],
        out_specs=spec,
        out_shape=jax.ShapeDtypeStruct((M, W), jnp.bfloat16),
        compiler_params=pltpu.CompilerParams(dimension_semantics=("parallel",), vmem_limit_bytes=VMEM_LIMIT),
        name="rwkv_post",
    )(y, bonus, gate, ln_w.reshape(1, W), ln_b.reshape(1, W))


def _rwkv_kernel(r_ref, wl_ref, k_ref, v_ref, kk_ref, a_ref, s0_ref, y_ref, sout_ref, s_scr, *, NP, Tc):
    c = pl.program_id(1)
    f32, bf16 = jnp.float32, jnp.bfloat16
    U = min(SUBLANE, Tc)

    @pl.when(c == 0)
    def _():
        s_scr[...] = s0_ref[...]

    sub = lax.broadcasted_iota(jnp.int32, (RWKV_N, LANE), 0)
    lane = lax.broadcasted_iota(jnp.int32, (RWKV_N, LANE), 1)
    eye2 = (lane % RWKV_N) == sub
    rr = lax.broadcasted_iota(jnp.int32, (LANE, LANE), 0) // RWKV_N
    cc = lax.broadcasted_iota(jnp.int32, (LANE, LANE), 1) // RWKV_N
    bd = jnp.where(rr == cc, 1.0, 0.0).astype(bf16)

    def ssb(p, two_piece=True):
        hi = p.astype(bf16)
        out = jnp.dot(hi, bd, preferred_element_type=f32)
        if two_piece:
            lo = (p - hi.astype(f32)).astype(bf16)
            out = out + jnp.dot(lo, bd, preferred_element_type=f32)
        return out

    eye_all = jnp.concatenate([eye2] * NP, axis=0)

    def bcast(x8, s):
        return jnp.concatenate(
            [jnp.broadcast_to(x8[s:s + 1, p * LANE:(p + 1) * LANE], (RWKV_N, LANE)) for p in range(NP)], axis=0)

    def body(g, carry):
        rows = pl.ds(pl.multiple_of(g * U, U), U)
        r8 = r_ref[rows, :]
        w8 = jnp.exp(wl_ref[rows, :])
        k8 = k_ref[rows, :]
        v8 = v_ref[rows, :]
        kk8 = kk_ref[rows, :]
        ka8 = kk8 * a_ref[rows, :]
        nk8 = -kk8
        S = s_scr[...]
        ys = []
        for s in range(U):
            sa = ssb(S * bcast(nk8, s))
            vb = ssb(jnp.where(eye_all, bcast(v8, s), 0.0))
            S = S * bcast(w8, s) + sa * bcast(ka8, s) + vb * bcast(k8, s)
            yb = jnp.where(eye_all, ssb(S * bcast(r8, s), two_piece=False), 0.0)
            ys.append(jnp.concatenate(
                [jnp.sum(yb[p * RWKV_N:(p + 1) * RWKV_N], axis=0, keepdims=True) for p in range(NP)], axis=1))
        s_scr[...] = S
        y_ref[rows, :] = ys[0] if U == 1 else jnp.concatenate(ys, axis=0)
        return carry

    lax.fori_loop(0, Tc // U, body, 0)

    @pl.when(c == pl.num_programs(1) - 1)
    def _():
        sout_ref[...] = s_scr[...]


def rwkv_scan_pallas(r, w_log, k, v, kk, a, s0):
    B, T, W = r.shape
    H = W // RWKV_N
    NP = H // 2
    Tc = 128 if T % 128 == 0 else T
    s0p = s0.reshape(B, NP, 2, RWKV_N, RWKV_N).transpose(0, 1, 3, 2, 4).reshape(B, NP * RWKV_N, LANE)
    blk = pl.BlockSpec((None, Tc, W), lambda b, c: (b, c, 0))
    sblk = pl.BlockSpec((None, NP * RWKV_N, LANE), lambda b, c: (b, 0, 0))
    y, sp = pl.pallas_call(
        functools.partial(_rwkv_kernel, NP=NP, Tc=Tc),
        grid=(B, T // Tc),
        in_specs=[blk] * 6 + [sblk],
        out_specs=[blk, sblk],
        out_shape=[jax.ShapeDtypeStruct((B, T, W), jnp.float32),
                   jax.ShapeDtypeStruct((B, NP * RWKV_N, LANE), jnp.float32)],
        scratch_shapes=[pltpu.VMEM((NP * RWKV_N, LANE), jnp.float32)],
        compiler_params=pltpu.CompilerParams(dimension_semantics=("parallel", "arbitrary"),
                                             vmem_limit_bytes=VMEM_LIMIT),
        name="rwkv7_scan",
    )(r, w_log, k, v, kk, a, s0p)
    s_fin = sp.reshape(B, NP, RWKV_N, 2, RWKV_N).transpose(0, 1, 3, 2, 4).reshape(B, H, RWKV_N, RWKV_N)
    return y, s_fin


def rope_tables(pos):
    half = ROPE_DIM // 2
    inv = ROPE_THETA ** (-jnp.arange(half, dtype=jnp.float32) / half)
    ang = pos.astype(jnp.float32)[:, None] * inv[None, :]
    cos, sin = jnp.cos(ang), jnp.sin(ang)
    T = pos.shape[0]
    z = jnp.zeros((T, HEAD_DIM - ROPE_DIM), jnp.float32)
    zh = jnp.zeros((T, half), jnp.float32)
    c = jnp.concatenate([cos, cos, jnp.ones_like(z)], axis=1)
    s_up = jnp.concatenate([-sin, zh, z], axis=1)
    s_dn = jnp.concatenate([zh, sin, z], axis=1)
    return c, s_up, s_dn


def _nsa_prep_kernel(q_ref, c_ref, s_ref, w_ref, tc_ref, tu_ref, td_ref, qo_ref, co_ref, so_ref, wo_ref):
    c, su, sd = tc_ref[...], tu_ref[...], td_ref[...]
    half = ROPE_DIM // 2

    def rot(x):
        return x * c + pltpu.roll(x, HEAD_DIM - half, axis=1) * su + pltpu.roll(x, half, axis=1) * sd

    for h in range(NSA_HEADS):
        cols = slice(h * HEAD_DIM, (h + 1) * HEAD_DIM)
        qo_ref[:, cols] = (rot(q_ref[:, cols]) * HEAD_DIM ** -0.5).astype(qo_ref.dtype)
    for src, dst in ((c_ref, co_ref), (s_ref, so_ref), (w_ref, wo_ref)):
        for g in range(NSA_KV_HEADS):
            cols = slice(g * HEAD_DIM, (g + 1) * HEAD_DIM)
            dst[:, cols] = rot(src[:, cols])
        dst[:, NSA_KV_WIDTH:] = src[:, NSA_KV_WIDTH:]


def nsa_prep(hn, pos):
    B, T, _ = hn.shape
    tm = _pick(T, (256, 128, 64, 32, 16, 8))
    tabs = rope_tables(pos)
    kvw = 2 * NSA_KV_WIDTH
    q_spec = pl.BlockSpec((None, tm, NSA_WIDTH), lambda b, i: (b, i, 0))
    kv_spec = lambda n: pl.BlockSpec((None, tm, kvw), lambda b, i: (b, i, NSA_WIDTH // kvw + n))
    t_spec = pl.BlockSpec((tm, HEAD_DIM), lambda b, i: (i, 0))
    o_spec = pl.BlockSpec((None, tm, kvw), lambda b, i: (b, i, 0))
    kv_shape = jax.ShapeDtypeStruct((B, T, kvw), jnp.float32)
    return pl.pallas_call(
        _nsa_prep_kernel,
        grid=(B, T // tm),
        in_specs=[q_spec, kv_spec(0), kv_spec(1), kv_spec(2), t_spec, t_spec, t_spec],
        out_specs=[q_spec, o_spec, o_spec, o_spec],
        out_shape=[jax.ShapeDtypeStruct((B, T, NSA_WIDTH), jnp.bfloat16), kv_shape, kv_shape, kv_shape],
        compiler_params=pltpu.CompilerParams(dimension_semantics=("parallel", "parallel"),
                                             vmem_limit_bytes=VMEM_LIMIT),
        name="nsa_prep",
    )(hn, hn, hn, hn, *tabs)


def _compress_kernel(x_ref, w1_ref, w2_ref, pe_ref, ko_ref, vo_ref, *, ns):
    f32, bf16 = jnp.float32, jnp.bfloat16
    G, HD = NSA_KV_HEADS, HEAD_DIM
    row_w = 2 * NSA_KV_WIDTH
    for kv, o_ref in ((0, ko_ref), (1, vo_ref)):
        pos = jnp.zeros((SUBLANE, HD), f32)
        for p in range(CMP_STRIDE):
            w = w1_ref[kv, p]
            lo = jnp.broadcast_to(pe_ref[kv, p:p + 1, :], (SUBLANE, HD)).astype(bf16)
            hi = jnp.broadcast_to(pe_ref[kv, CMP_STRIDE + p:CMP_STRIDE + p + 1, :], (SUBLANE, HD)).astype(bf16)
            pos = pos + jnp.dot(lo, w, preferred_element_type=f32)[:, :HD] + jnp.dot(hi, w, preferred_element_type=f32)[:, HD:]
        pos = pos[0:1]
        for g in range(G):
            acc = jnp.zeros((ns, 2 * HD), f32)
            for p in range(CMP_STRIDE):
                c0 = p * row_w + kv * NSA_KV_WIDTH + g * HD
                acc = acc + jnp.dot(x_ref[:, c0:c0 + HD].astype(bf16), w1_ref[kv, p], preferred_element_type=f32)
            nxt = pltpu.roll(acc[:, HD:], ns - 1, axis=0)
            hid = jax.nn.gelu(acc[:, :HD] + nxt + pos)
            o_ref[g] = jnp.dot(hid.astype(bf16), w2_ref[kv], preferred_element_type=f32).astype(o_ref.dtype)


def compress_pallas(kv_rows, w1k, w2k, pek, w1v, w2v, pev):
    B, T, W = kv_rows.shape
    ns = T // CMP_STRIDE
    bf16 = jnp.bfloat16
    x = kv_rows.reshape(B, ns, CMP_STRIDE * W)
    cat = lambda w1: jnp.concatenate([w1[:CMP_STRIDE], w1[CMP_STRIDE:]], axis=-1)
    w1 = jnp.stack([cat(w1k), cat(w1v)]).astype(bf16)
    w2 = jnp.stack([w2k, w2v]).astype(bf16)
    pe = jnp.stack([pek, pev])
    out = jax.ShapeDtypeStruct((B, NSA_KV_HEADS, ns, HEAD_DIM), bf16)
    o_spec = pl.BlockSpec((None, NSA_KV_HEADS, ns, HEAD_DIM), lambda b: (b, 0, 0, 0))
    return pl.pallas_call(
        functools.partial(_compress_kernel, ns=ns),
        grid=(B,),
        in_specs=[pl.BlockSpec((None, ns, CMP_STRIDE * W), lambda b: (b, 0, 0)),
                  pl.BlockSpec(w1.shape, lambda b: (0, 0, 0, 0)),
                  pl.BlockSpec(w2.shape, lambda b: (0, 0, 0)),
                  pl.BlockSpec(pe.shape, lambda b: (0, 0, 0))],
        out_specs=[o_spec, o_spec],
        out_shape=[out, out],
        compiler_params=pltpu.CompilerParams(dimension_semantics=("parallel",), vmem_limit_bytes=VMEM_LIMIT),
        name="nsa_compress",
    )(x, w1, w2, pe)


def _dot_nt(a, b):
    return lax.dot_general(a, b, (((1,), (1,)), ((), ())), preferred_element_type=jnp.float32)


def _nsa_kernel(q_ref, kc_ref, vc_ref, ks_ref, vs_ref, kw_ref, vw_ref, g_ref, covt_ref, e_ref, o_ref,
                bias_scr, p4_scr, ocmp_scr, m_scr, l_scr, acc_scr, *, TQ, T, NS, NCP, n_top):
    f32, bf16 = jnp.float32, jnp.bfloat16
    R = NSA_GROUP
    i = pl.program_id(2)
    nchunk = T // TQ
    qpos_col = i * TQ + lax.broadcasted_iota(jnp.int32, (TQ, 1), 0)

    kc = kc_ref[...]
    vc = vc_ref[...]
    cend = lax.broadcasted_iota(jnp.int32, (1, NCP), 1) * CMP_STRIDE + (CMP_BLOCK - 1)
    valid = cend <= qpos_col
    for r in range(R):
        s = _dot_nt(q_ref[:, r * HEAD_DIM:(r + 1) * HEAD_DIM], kc)
        s = jnp.where(valid, s, NEG_INF)
        m = jnp.max(s, axis=-1, keepdims=True)
        p = jnp.where(valid, jnp.exp(s - m), 0.0)
        l = jnp.sum(p, axis=-1, keepdims=True)
        p = (p / jnp.where(l > 0.0, l, 1.0)).astype(bf16)
        p4_scr[:, r * NCP:(r + 1) * NCP] = p
        ocmp_scr[r] = jnp.dot(p, vc, preferred_element_type=f32)

    imp_t = _dot_nt(covt_ref[...], p4_scr[...])
    j = lax.broadcasted_iota(jnp.int32, (NS, TQ), 0)
    qblk = (i * TQ + lax.broadcasted_iota(jnp.int32, (NS, TQ), 1)) // SEL_BLOCK
    forced = (j < N_INIT_BLOCKS) | ((j <= qblk) & (j > qblk - N_LOCAL_BLOCKS))
    score = jnp.where(forced, FORCE_SCORE, jnp.where(j <= qblk, imp_t, NEG_INF))
    rank = jnp.zeros((NS, TQ), f32)
    for a in range(NS):
        row = score[a:a + 1, :]
        beats = (row > score) | ((row == score) & (a < j))
        rank = rank + jnp.where(beats, 1.0, 0.0)
    sel_t = jnp.where(rank < n_top, 1.0, 0.0)
    if NS < LANE:
        sel_t = jnp.concatenate([sel_t, jnp.zeros((LANE - NS, TQ), f32)], axis=0)
    sel = sel_t.T.astype(bf16)
    for c in range(nchunk):
        selexp = jnp.dot(sel, e_ref[:, c * TQ:(c + 1) * TQ], preferred_element_type=f32)
        kpos = c * TQ + lax.broadcasted_iota(jnp.int32, (TQ, TQ), 1)
        bias_scr[c] = jnp.where((selexp > 0.5) & (kpos <= qpos_col), 0.0, NEG_INF)

    def attend(k_ref, v_ref, lo, masker):
        m_scr[...] = jnp.full(m_scr.shape, NEG_INF, f32)
        l_scr[...] = jnp.zeros(l_scr.shape, f32)
        acc_scr[...] = jnp.zeros(acc_scr.shape, f32)

        def chunk(c, carry):
            rows = pl.ds(pl.multiple_of(c * TQ, TQ), TQ)
            k = k_ref[rows, :].astype(bf16)
            v = v_ref[rows, :].astype(bf16)
            mk = masker(c)
            heads = range(R)
            sk = [mk(_dot_nt(q_ref[:, r * HEAD_DIM:(r + 1) * HEAD_DIM], k)) for r in heads]
            m_prev = [m_scr[r] for r in heads]
            m_new = [jnp.maximum(m_prev[r], jnp.max(sk[r][0], axis=-1, keepdims=True)) for r in heads]
            alpha = [jnp.exp(m_prev[r] - m_new[r]) for r in heads]
            ps = [jnp.exp(sk[r][0] - m_new[r]) for r in heads]
            ps = [p if sk[r][1] is None else jnp.where(sk[r][1], p, 0.0) for r, p in enumerate(ps)]
            pv = [jnp.dot(ps[r].astype(bf16), v, preferred_element_type=f32) for r in heads]
            for r in heads:
                l_scr[r] = alpha[r] * l_scr[r] + jnp.sum(ps[r], axis=-1, keepdims=True)
                acc_scr[r] = alpha[r] * acc_scr[r] + pv[r]
                m_scr[r] = m_new[r]
            return carry

        lax.fori_loop(lo, i + 1, chunk, 0)

    def sel_masker(c):
        b = bias_scr[c]
        return lambda s: (s + b, None)

    attend(ks_ref, vs_ref, 0, sel_masker)
    g = pltpu.roll(jax.nn.sigmoid(g_ref[...]), (LANE - 3 * R * pl.program_id(1)) % LANE, axis=1)
    for r in range(R):
        ocmp_scr[r] = (g[:, 3 * r:3 * r + 1] * ocmp_scr[r]
                       + g[:, 3 * r + 1:3 * r + 2] * (acc_scr[r] / l_scr[r]))

    def win_masker(c):
        rel = ((i - c) * TQ + lax.broadcasted_iota(jnp.int32, (TQ, TQ), 0)
               - lax.broadcasted_iota(jnp.int32, (TQ, TQ), 1))
        ok = (rel >= 0) & (rel < WINDOW)
        return lambda s: (jnp.where(ok, s, NEG_INF), ok)

    attend(kw_ref, vw_ref, jnp.maximum(i - (WINDOW // TQ), 0), win_masker)
    for r in range(R):
        o = ocmp_scr[r] + g[:, 3 * r + 2:3 * r + 3] * (acc_scr[r] / l_scr[r])
        o_ref[:, r * HEAD_DIM:(r + 1) * HEAD_DIM] = o.astype(o_ref.dtype)


def nsa_attention_pallas(qr, kcmp, vcmp, slc, win, hn, gate_col):
    B, T, _ = qr.shape
    G, R = NSA_KV_HEADS, NSA_GROUP
    TQ = min(NSA_TQ, T)
    NS = T // SEL_BLOCK
    NC = T // CMP_STRIDE - 1
    NCP = kcmp.shape[2]
    n_top = min(SEL_TOP, NS)
    ci = np.arange(NCP)[:, None] * CMP_STRIDE
    sj = np.arange(NS)[None, :] * SEL_BLOCK
    cover = np.clip(np.minimum(ci + CMP_BLOCK, sj + SEL_BLOCK) - np.maximum(ci, sj), 0, None) / CMP_BLOCK
    cover[NC:] = 0.0
    covt = jnp.asarray(np.tile(cover.T, (1, R)), jnp.bfloat16)
    e = jnp.asarray((np.arange(T)[None, :] // SEL_BLOCK) == np.arange(LANE)[:, None], jnp.bfloat16)
    k_spec = pl.BlockSpec((None, T, HEAD_DIM), lambda b, g, i: (b, 0, g))
    v_spec = pl.BlockSpec((None, T, HEAD_DIM), lambda b, g, i: (b, 0, G + g))
    cmp_spec = pl.BlockSpec((None, None, NCP, HEAD_DIM), lambda b, g, i: (b, g, 0, 0))
    return pl.pallas_call(
        functools.partial(_nsa_kernel, TQ=TQ, T=T, NS=NS, NCP=NCP, n_top=n_top),
        grid=(B, G, T // TQ),
        in_specs=[pl.BlockSpec((None, TQ, R * HEAD_DIM), lambda b, g, i: (b, i, g)),
                  cmp_spec, cmp_spec, k_spec, v_spec, k_spec, v_spec,
                  pl.BlockSpec((None, TQ, LANE), lambda b, g, i: (b, i, gate_col // LANE)),
                  pl.BlockSpec((NS, R * NCP), lambda b, g, i: (0, 0)),
                  pl.BlockSpec((LANE, T), lambda b, g, i: (0, 0))],
        out_specs=pl.BlockSpec((None, TQ, R * HEAD_DIM), lambda b, g, i: (b, i, g)),
        out_shape=jax.ShapeDtypeStruct((B, T, G * R * HEAD_DIM), jnp.bfloat16),
        scratch_shapes=[pltpu.VMEM((T // TQ, TQ, TQ), jnp.float32),
                        pltpu.VMEM((TQ, R * NCP), jnp.bfloat16),
                        pltpu.VMEM((R, TQ, HEAD_DIM), jnp.float32),
                        pltpu.VMEM((R, TQ, 1), jnp.float32),
                        pltpu.VMEM((R, TQ, 1), jnp.float32),
                        pltpu.VMEM((R, TQ, HEAD_DIM), jnp.float32)],
        compiler_params=pltpu.CompilerParams(dimension_semantics=("parallel", "parallel", "arbitrary"),
                                             vmem_limit_bytes=VMEM_LIMIT),
        name="nsa_attention",
    )(qr, kcmp, vcmp, slc, slc, win, win, hn, covt, e)


def rmsnorm(x, g):
    xf = x.astype(jnp.float32)
    y = xf * lax.rsqrt(jnp.mean(xf * xf, axis=-1, keepdims=True) + NORM_EPS)
    return (y * g.astype(jnp.float32)).astype(x.dtype)


def split_sizes(h, sizes):
    return jnp.split(h, [int(s) for s in np.cumsum(sizes)[:-1]], axis=-1)


def rope(x, pos):
    half = ROPE_DIM // 2
    inv = ROPE_THETA ** (-jnp.arange(half, dtype=jnp.float32) / half)
    ang = pos.astype(jnp.float32)[:, None] * inv[None, :]
    cos = jnp.cos(ang)[None, :, None, :]
    sin = jnp.sin(ang)[None, :, None, :]
    xr = x[..., :ROPE_DIM].astype(jnp.float32)
    x1, x2 = xr[..., :half], xr[..., half:]
    rot = jnp.concatenate([x1 * cos - x2 * sin, x2 * cos + x1 * sin], axis=-1).astype(x.dtype)
    return jnp.concatenate([rot, x[..., ROPE_DIM:]], axis=-1)


def gla_chunked(q, k, v, log_a, s0):
    B, T, H, DK = q.shape
    DV = v.shape[-1]
    C = GLA_CHUNK if T % GLA_CHUNK == 0 else T
    n = T // C

    def chunks(t):
        return t.astype(jnp.float32).reshape(B, n, C, H, t.shape[-1]).transpose(1, 0, 3, 2, 4)

    qc, kc, vc = chunks(q), chunks(k), chunks(v)
    bc = jnp.cumsum(chunks(log_a), axis=3)
    b_mid = bc[:, :, :, C // 2:C // 2 + 1]
    b_last = bc[:, :, :, C - 1:C]
    causal = jnp.tril(jnp.ones((C, C), dtype=bool))
    att = jnp.einsum('nbhid,nbhjd->nbhij', qc * jnp.exp(bc - b_mid), kc * jnp.exp(b_mid - bc))
    att = jnp.where(causal, att, 0.0)
    o_intra = jnp.einsum('nbhij,nbhjv->nbhiv', att, vc)
    q_dec = qc * jnp.exp(bc)
    k_dec = kc * jnp.exp(b_last - bc)
    a_last = jnp.exp(b_last[:, :, :, 0])

    def step(s, inp):
        qd, kd, vv, al = inp
        o = jnp.einsum('bhcd,bhdv->bhcv', qd, s)
        s = al[..., None] * s + jnp.einsum('bhcd,bhcv->bhdv', kd, vv)
        return s, o

    s_fin, o_inter = lax.scan(step, s0.astype(jnp.float32), (q_dec, k_dec, vc, a_last))
    o = (o_intra + o_inter).transpose(1, 0, 3, 2, 4).reshape(B, T, H, DV)
    return o, s_fin


def compress_blocks(x, w1, w2, pe):
    B, L, G, HD = x.shape
    ns = L // CMP_STRIDE
    seg = x[:, :ns * CMP_STRIDE].reshape(B, ns, CMP_STRIDE, G, HD)
    first = jnp.einsum('bnpgd,pde->bnge', seg, w1[:CMP_STRIDE])
    second = jnp.einsum('bnpgd,pde->bnge', seg, w1[CMP_STRIDE:])
    pos_term = jnp.einsum('pd,pde->e', pe, w1)
    hid = jax.nn.gelu(first[:, :-1] + second[:, 1:] + pos_term)
    return jnp.einsum('bnge,ed->bngd', hid, w2)


def nsa_compressed_selected(q, k_c, v_c, k_s, v_s, gates, pos, w1k, w2k, pek, w1v, w2v, pev):
    B, T, H, HD = q.shape
    L = k_c.shape[1]
    G, R = NSA_KV_HEADS, NSA_GROUP
    scale = HEAD_DIM ** -0.5
    kcmp = compress_blocks(k_c, w1k, w2k, pek)
    vcmp = compress_blocks(v_c, w1v, w2v, pev)
    NC = kcmp.shape[1]
    cmp_end = jnp.arange(NC) * CMP_STRIDE + CMP_BLOCK - 1
    NS = -(-L // SEL_BLOCK)
    pad = NS * SEL_BLOCK - L
    to_blocks = lambda t: jnp.pad(t, ((0, 0), (0, pad), (0, 0), (0, 0))).reshape(
        B, NS, SEL_BLOCK, G, HD).transpose(0, 3, 1, 2, 4)
    kb, vb = to_blocks(k_s), to_blocks(v_s)
    ci = jnp.arange(NC)[:, None] * CMP_STRIDE
    sj = jnp.arange(NS)[None, :] * SEL_BLOCK
    cover = jnp.clip(jnp.minimum(ci + CMP_BLOCK, sj + SEL_BLOCK) - jnp.maximum(ci, sj), 0, None)
    cover = cover.astype(jnp.float32) / CMP_BLOCK
    n_top = min(SEL_TOP, NS)
    blk = jnp.arange(NS)
    bi = jnp.arange(B)[:, None, None, None]
    gi = jnp.arange(G)[None, None, :, None]
    off = jnp.arange(SEL_BLOCK)
    QC = SEL_Q_BLOCK if T % SEL_Q_BLOCK == 0 else T
    nq = T // QC

    def chunk_fn(args):
        qc, gc, pc = args
        qg = qc.reshape(B, QC, G, R, HD)
        s = jnp.einsum('bqgrd,bngd->bqgrn', qg, kcmp).astype(jnp.float32) * scale
        valid = (cmp_end[None, :] <= pc[:, None])[None, :, None, None, :]
        p = jnp.where(valid, jax.nn.softmax(jnp.where(valid, s, NEG_INF), axis=-1), 0.0)
        o_cmp = jnp.einsum('bqgrn,bngd->bqgrd', p.astype(vcmp.dtype), vcmp)
        imp = jnp.einsum('bqgrn,nj->bqgj', p, cover)
        qblk = (pc // SEL_BLOCK)[:, None]
        forced = (blk[None, :] < N_INIT_BLOCKS) | ((blk[None, :] <= qblk) & (blk[None, :] > qblk - N_LOCAL_BLOCKS))
        causal_blk = blk[None, :] <= qblk
        score = jnp.where(forced[None, :, None, :], FORCE_SCORE,
                          jnp.where(causal_blk[None, :, None, :], imp, NEG_INF))
        _, idx = lax.top_k(score, n_top)
        kg = kb[bi, gi, idx].reshape(B, QC, G, n_top * SEL_BLOCK, HD)
        vg = vb[bi, gi, idx].reshape(B, QC, G, n_top * SEL_BLOCK, HD)
        tok = (idx[..., None] * SEL_BLOCK + off).reshape(B, QC, G, n_top * SEL_BLOCK)
        ok = (tok <= pc[None, :, None, None])[:, :, :, None, :]
        s2 = jnp.einsum('bqgrd,bqgkd->bqgrk', qg, kg).astype(jnp.float32) * scale
        p2 = jax.nn.softmax(jnp.where(ok, s2, NEG_INF), axis=-1)
        o_slc = jnp.einsum('bqgrk,bqgkd->bqgrd', p2.astype(vg.dtype), vg)
        return (gc[..., 0:1] * o_cmp.reshape(B, QC, H, HD)
                + gc[..., 1:2] * o_slc.reshape(B, QC, H, HD))

    xs = (q.reshape(B, nq, QC, H, HD).swapaxes(0, 1),
          gates.reshape(B, nq, QC, H, 3).swapaxes(0, 1),
          pos.reshape(nq, QC))
    o = lax.map(chunk_fn, xs)
    return o.swapaxes(0, 1).reshape(B, T, H, HD)


def window_attn(q, k_pad, v_pad, q_pos0):
    B, T, H, HD = q.shape
    G, R = NSA_KV_HEADS, NSA_GROUP
    QB = WIN_Q_BLOCK if T % WIN_Q_BLOCK == 0 else T
    nb = T // QB
    kidx = jnp.arange(nb)[:, None] * QB + jnp.arange(WINDOW + QB)[None, :]
    kb, vb = k_pad[:, kidx], v_pad[:, kidx]
    qb = q.reshape(B, nb, QB, G, R, HD)
    qpos = q_pos0 + jnp.arange(T).reshape(nb, QB)
    kpos = q_pos0 - WINDOW + kidx
    rel = qpos[:, :, None] - kpos[:, None, :]
    ok = (rel >= 0) & (rel < WINDOW) & (kpos[:, None, :] >= 0)
    s = jnp.einsum('bcqgrd,bckgd->bcgrqk', qb, kb).astype(jnp.float32) * HEAD_DIM ** -0.5
    p = jax.nn.softmax(jnp.where(ok[None, :, None, None], s, NEG_INF), axis=-1)
    o = jnp.einsum('bcgrqk,bckgd->bcqgrd', p.astype(vb.dtype), vb)
    return o.reshape(B, T, H, HD)


def trunk_layer(x, pos0, past_cmp, past_slc, win_buf, gla_s0, rwkv_s0, shift0, conv0, lw):
    B, T, _ = x.shape
    dt = x.dtype
    f32 = jnp.float32
    pos = pos0 + jnp.arange(T, dtype=jnp.int32)
    heads = lambda t, n: t.reshape(B, T, n, t.shape[-1] // n)

    bf = jnp.bfloat16
    M = B * T
    x2 = x.reshape(M, D_MODEL)
    xn = rmsnorm_pallas(x2, lw['norm1'])
    hg = mm(xn, lw['w_gla']).reshape(B, T, -1)
    hn = mm(xn, lw['w_nsa']).reshape(B, T, -1)
    hr = mm(xn, lw['w_rwkv']).reshape(B, T, -1)
    mg = mm(xn, lw['w_mg'])

    if T % GLA_TC == 0:
        o_gla, gla_s = gla_pallas(hg, lw['gla_wa2'], lw['gla_ba'], lw['gla_norm'], gla_s0.astype(f32), GLA_OFF)
    else:
        gq, gk, gv, gog, glo = [hg[..., o:o + s] for o, s in zip(GLA_OFF, IN_SIZES[:5])]
        log_a = jax.nn.log_sigmoid((glo @ lw['gla_wa2'] + lw['gla_ba']).astype(f32)) / GLA_TAU
        o_gla, gla_s = gla_chunked(heads(gq, GLA_HEADS) * GLA_DK ** -0.5, heads(gk, GLA_HEADS),
                                   heads(gv, GLA_HEADS), heads(log_a, GLA_HEADS), gla_s0)
        o_gla = rmsnorm(o_gla, lw['gla_norm']) * jax.nn.silu(heads(gog, GLA_HEADS).astype(f32))
    o_gla = o_gla.reshape(M, GLA_WIDTH).astype(bf)

    kv5 = lambda t: t.reshape(B, T, 2, NSA_KV_HEADS, HEAD_DIM)
    if past_cmp.shape[1] == 0 and T % NSA_TQ == 0:
        qr, cmp2, slc2, win2 = nsa_prep(hn, pos)
        kcmp, vcmp = compress_pallas(cmp2, lw['cmp_w1k'], lw['cmp_w2k'], lw['cmp_pek'],
                                     lw['cmp_w1v'], lw['cmp_w2v'], lw['cmp_pev'])
        o_nsa = nsa_attention_pallas(qr, kcmp, vcmp, slc2, win2, hn, NSA_GATE_OFF)
        new_cmp, new_slc, win_new = kv5(cmp2), kv5(slc2), kv5(win2)[:, T - min(WINDOW, T):]
    else:
        kvh = lambda t: heads(t, NSA_KV_HEADS)
        nq = hn[..., :NSA_WIDTH]
        nkc, nvc, nks, nvs, nkw, nvw = [hn[..., NSA_WIDTH + n * NSA_KV_WIDTH:NSA_WIDTH + (n + 1) * NSA_KV_WIDTH]
                                        for n in range(6)]
        ng = hn[..., NSA_GATE_OFF:NSA_GATE_OFF + 3 * NSA_HEADS]
        qn = rope(heads(nq, NSA_HEADS), pos)
        new_cmp = jnp.stack([rope(kvh(nkc), pos), kvh(nvc)], axis=2)
        new_slc = jnp.stack([rope(kvh(nks), pos), kvh(nvs)], axis=2)
        new_win = jnp.stack([rope(kvh(nkw), pos), kvh(nvw)], axis=2)
        cmp_all = jnp.concatenate([past_cmp.astype(dt), new_cmp], axis=1)
        slc_all = jnp.concatenate([past_slc.astype(dt), new_slc], axis=1)
        win_all = jnp.concatenate([win_buf.astype(dt), new_win], axis=1)
        n_win = win_all.shape[1]
        win_pad = jnp.pad(win_all, ((0, 0), (WINDOW + T - n_win, 0), (0, 0), (0, 0), (0, 0)))
        g_nsa = jax.nn.sigmoid(ng.astype(f32)).reshape(B, T, NSA_HEADS, 3)
        o_nsa = nsa_compressed_selected(qn, cmp_all[:, :, 0], cmp_all[:, :, 1], slc_all[:, :, 0], slc_all[:, :, 1],
                                        g_nsa, pos, lw['cmp_w1k'], lw['cmp_w2k'], lw['cmp_pek'],
                                        lw['cmp_w1v'], lw['cmp_w2v'], lw['cmp_pev'])
        o_nsa = o_nsa + g_nsa[..., 2:3] * window_attn(qn, win_pad[:, :, 0], win_pad[:, :, 1], pos0)
        win_new = win_all[:, n_win - min(WINDOW, n_win):]
    o_nsa = o_nsa.reshape(M, NSA_WIDTH).astype(bf)

    (r_, w_log, k2, v_, kk, a, gate, bonus), shift_new = rwkv_prep(
        hr, shift0.astype(f32), lw['rwkv_mu'], lw['rwkv_w0'], lw['rwkv_w2'], lw['rwkv_a0'], lw['rwkv_a2'],
        lw['rwkv_g2'], lw['rwkv_kk'], lw['rwkv_ka'], lw['rwkv_rk'])
    y, rwkv_s = rwkv_scan_pallas(r_, w_log, k2, v_, kk, a, rwkv_s0.astype(f32))
    flat = lambda t: t.reshape(M, RWKV_WIDTH)
    o_rwkv = rwkv_post(flat(y), flat(bonus), flat(gate), lw['rwkv_ln_w'], lw['rwkv_ln_b'])

    merged = merge_mm(o_gla, o_nsa, o_rwkv, lw['w_o_gla'], lw['w_o_nsa'], lw['w_o_rwkv'], mg)
    x2 = mm(merged, lw['w_out'], res=x2)

    xn2 = rmsnorm_pallas(x2, lw['norm2'])
    act, conv_new = ffn_gate_up(xn2, lw['ffn_gate'], lw['ffn_up'], lw['ffn_conv'], lw['ffn_conv_b'],
                                conv0.astype(f32), B, T)
    x2 = mm(act, lw['ffn_down'], res=x2)
    return x2.reshape(B, T, D_MODEL), (new_cmp, new_slc, win_new, gla_s, rwkv_s, shift_new, conv_new)


def _w_in_group(w, lo, hi):
    seg = w[:, lo:hi].astype(jnp.bfloat16)
    return jnp.pad(seg, ((0, 0), (0, _round_up(hi - lo, W_IN_TILE) - (hi - lo))))


def kernel(x_prompt, x_sample, cache_cmp_kv, cache_slc_kv, cache_win_kv, state_gla, state_rwkv, state_rwkv_shift, state_ffn_conv, page_table, norm1, w_in, gla_wa2, gla_ba, gla_norm, w_o_gla, cmp_w1k, cmp_w2k, cmp_pek, cmp_w1v, cmp_w2v, cmp_pev, w_o_nsa, rwkv_mu, rwkv_w0, rwkv_w2, rwkv_a0, rwkv_a2, rwkv_g2, rwkv_kk, rwkv_ka, rwkv_rk, rwkv_ln_w, rwkv_ln_b, w_o_rwkv, w_out, norm2, ffn_gate, ffn_conv, ffn_conv_b, ffn_up, ffn_down, norm_f):
    G, HD = NSA_KV_HEADS, HEAD_DIM
    n_db, n_pages = page_table.shape
    past_len = n_pages * PAGE_SIZE
    bp = x_prompt.shape[0]
    dt = x_prompt.dtype
    bf = jnp.bfloat16
    xp, xs = x_prompt, x_sample
    st_p, st_s = [], []
    for l in range(DEPTH):
        lw = {'norm1': norm1[l], 'w_gla': _w_in_group(w_in[l], 0, _C_NSA),
              'w_nsa': _w_in_group(w_in[l], _C_NSA, _C_RWKV), 'w_rwkv': _w_in_group(w_in[l], _C_RWKV, _C_MG),
              'w_mg': _w_in_group(w_in[l], _C_MG, _C_MG + IN_SIZES[14]), 'gla_wa2': gla_wa2[l], 'gla_ba': gla_ba[l],
              'gla_norm': gla_norm[l], 'w_o_gla': w_o_gla[l].astype(bf), 'cmp_w1k': cmp_w1k[l], 'cmp_w2k': cmp_w2k[l],
              'cmp_pek': cmp_pek[l], 'cmp_w1v': cmp_w1v[l], 'cmp_w2v': cmp_w2v[l], 'cmp_pev': cmp_pev[l],
              'w_o_nsa': w_o_nsa[l].astype(bf), 'rwkv_mu': rwkv_mu[l], 'rwkv_w0': rwkv_w0[l], 'rwkv_w2': rwkv_w2[l],
              'rwkv_a0': rwkv_a0[l], 'rwkv_a2': rwkv_a2[l], 'rwkv_g2': rwkv_g2[l], 'rwkv_kk': rwkv_kk[l],
              'rwkv_ka': rwkv_ka[l], 'rwkv_rk': rwkv_rk[l], 'rwkv_ln_w': rwkv_ln_w[l], 'rwkv_ln_b': rwkv_ln_b[l],
              'w_o_rwkv': w_o_rwkv[l].astype(bf), 'w_out': w_out[l].astype(bf), 'norm2': norm2[l],
              'ffn_gate': ffn_gate[l].astype(bf), 'ffn_conv': ffn_conv[l], 'ffn_conv_b': ffn_conv_b[l],
              'ffn_up': ffn_up[l].astype(bf), 'ffn_down': ffn_down[l].astype(bf)}
        empty = jnp.zeros((bp, 0, 2, G, HD), dt)
        xp, sp = trunk_layer(xp, 0, empty, empty, empty,
                             jnp.zeros((bp, GLA_HEADS, GLA_DK, GLA_DV), jnp.float32),
                             jnp.zeros((bp, RWKV_HEADS, RWKV_N, RWKV_N), jnp.float32),
                             jnp.zeros((bp, RWKV_COLS), dt),
                             jnp.zeros((bp, CONV_W - 1, D_FF), dt), lw)
        past_c = cache_cmp_kv[l, page_table].reshape(n_db, past_len, 2, G, HD)
        past_s = cache_slc_kv[l, page_table].reshape(n_db, past_len, 2, G, HD)
        xs, ss = trunk_layer(xs, past_len, past_c, past_s, cache_win_kv[l], state_gla[l], state_rwkv[l],
                             state_rwkv_shift[l], state_ffn_conv[l], lw)
        st_p.append(sp)
        st_s.append(ss)
    y_prompt = rmsnorm_pallas(xp.reshape(-1, D_MODEL), norm_f, out_dtype=dt).reshape(xp.shape)
    y_sample = rmsnorm_pallas(xs.reshape(-1, D_MODEL), norm_f, out_dtype=dt).reshape(xs.shape)
    outs = [y_prompt, y_sample]
    for i in range(7):
        outs.append(jnp.stack([s[i] for s in st_p]))
        outs.append(jnp.stack([s[i] for s in st_s]))
    return tuple(outs)
```

```python
import functools

import jax
import jax.numpy as jnp
import numpy as np
from jax import lax
from jax.experimental import pallas as pl
from jax.experimental.pallas import tpu as pltpu

D_MODEL = 4096
DEPTH = 2
PAGE_SIZE = 128
HEAD_DIM = 128
ROPE_DIM = HEAD_DIM // 4
ROPE_THETA = 500000.0
NORM_EPS = 1e-5
NEG_INF = -1e30

GLA_WIDTH = D_MODEL // 4
GLA_HEADS = 4
GLA_DV = GLA_WIDTH // GLA_HEADS
GLA_DK = GLA_DV // 2
GLA_GATE_RANK = 16
GLA_TAU = 16.0
GLA_CHUNK = 64

NSA_HEADS = D_MODEL // 256
NSA_KV_HEADS = 4
NSA_GROUP = NSA_HEADS // NSA_KV_HEADS
NSA_WIDTH = NSA_HEADS * HEAD_DIM
NSA_KV_WIDTH = NSA_KV_HEADS * HEAD_DIM
CMP_STRIDE = 16
CMP_BLOCK = 2 * CMP_STRIDE
SEL_BLOCK = 64
SEL_TOP = 16
N_INIT_BLOCKS = 1
N_LOCAL_BLOCKS = 2
WINDOW = 512
SEL_Q_BLOCK = 32
WIN_Q_BLOCK = 128
FORCE_SCORE = 1e4

RWKV_WIDTH = D_MODEL // 4
RWKV_N = 64
RWKV_HEADS = RWKV_WIDTH // RWKV_N
RWKV_DECAY_RANK = 64
RWKV_AAA_RANK = 64
RWKV_GATE_RANK = 160
RWKV_SIZES = (RWKV_WIDTH, RWKV_WIDTH, RWKV_WIDTH, RWKV_DECAY_RANK, RWKV_AAA_RANK, RWKV_GATE_RANK)
RWKV_COLS = sum(RWKV_SIZES)
RWKV_LN_EPS = 64e-5

N_BRANCH = 3
D_FF = 256 * ((8 * D_MODEL // 3 + 255) // 256)
CONV_W = 3

IN_SIZES = (GLA_HEADS * GLA_DK, GLA_HEADS * GLA_DK, GLA_WIDTH, GLA_WIDTH, GLA_GATE_RANK,
            NSA_WIDTH, NSA_KV_WIDTH, NSA_KV_WIDTH, NSA_KV_WIDTH, NSA_KV_WIDTH, NSA_KV_WIDTH, NSA_KV_WIDTH,
            NSA_HEADS * 3,
            RWKV_COLS,
            N_BRANCH * D_MODEL)

LANE = 128
SUBLANE = 8
NSA_TQ = 256
GLA_TC = 256
DEC_PAGES = 16
VMEM_LIMIT = 48 * 1024 * 1024


def _round_up(n, m):
    return -(-n // m) * m


W_IN_TILE = 512
_C_NSA = sum(IN_SIZES[:5])
_C_RWKV = sum(IN_SIZES[:13])
_C_MG = sum(IN_SIZES[:14])
GLA_OFF = tuple(int(o) for o in np.concatenate([[0], np.cumsum(IN_SIZES[:4])]))
NSA_GATE_OFF = NSA_WIDTH + 6 * NSA_KV_WIDTH


def _pick(n, cands):
    for c in cands:
        if n % c == 0:
            return c
    return n


def _rmsnorm_kernel(x_ref, g_ref, o_ref):
    x = x_ref[...]
    y = x * lax.rsqrt(jnp.mean(x * x, axis=-1, keepdims=True) + NORM_EPS)
    o_ref[...] = (y * g_ref[...]).astype(o_ref.dtype)


def rmsnorm_pallas(x, g, out_dtype=jnp.bfloat16):
    M, D = x.shape
    tm = _pick(M, (256, 128, 64, 32, 16, 8))
    return pl.pallas_call(
        _rmsnorm_kernel,
        grid=(M // tm,),
        in_specs=[pl.BlockSpec((tm, D), lambda i: (i, 0)), pl.BlockSpec((1, D), lambda i: (0, 0))],
        out_specs=pl.BlockSpec((tm, D), lambda i: (i, 0)),
        out_shape=jax.ShapeDtypeStruct((M, D), out_dtype),
        compiler_params=pltpu.CompilerParams(dimension_semantics=("parallel",), vmem_limit_bytes=VMEM_LIMIT),
        name="rmsnorm",
    )(x, g.reshape(1, D))


def _mm_kernel(*refs, nk, has_res):
    x_ref, w_ref = refs[:2]
    res_ref = refs[2] if has_res else None
    o_ref, acc_ref = refs[-2:]
    k = pl.program_id(2)
    part = jnp.dot(x_ref[...], w_ref[...], preferred_element_type=jnp.float32)

    def finish(v):
        if has_res:
            v = v + res_ref[...]
        o_ref[...] = v.astype(o_ref.dtype)

    if nk == 1:
        finish(part)
    else:
        @pl.when(k == 0)
        def _():
            acc_ref[...] = part

        @pl.when(jnp.logical_and(k > 0, k < nk - 1))
        def _():
            acc_ref[...] += part

        @pl.when(k == nk - 1)
        def _():
            finish(acc_ref[...] + part)


def mm(x, w, res=None, out_dtype=jnp.float32):
    M, K = x.shape
    N = w.shape[1]
    tm = _pick(M, (1024, 512, 256, 128, 64, 32, 16, 8))
    tn = _pick(N, (512, 256, 128))
    tk = K if K <= 4096 else _pick(K, (5504, 4096, 2048, 1024, 512))
    nk = K // tk
    in_specs = [pl.BlockSpec((tm, tk), lambda i, j, k: (i, k)),
                pl.BlockSpec((tk, tn), lambda i, j, k: (k, j))]
    args = [x, w]
    if res is not None:
        in_specs.append(pl.BlockSpec((tm, tn), lambda i, j, k: (i, j)))
        args.append(res)
    return pl.pallas_call(
        functools.partial(_mm_kernel, nk=nk, has_res=res is not None),
        grid=(M // tm, N // tn, nk),
        in_specs=in_specs,
        out_specs=pl.BlockSpec((tm, tn), lambda i, j, k: (i, j)),
        out_shape=jax.ShapeDtypeStruct((M, N), out_dtype),
        scratch_shapes=[pltpu.VMEM((tm, tn) if nk > 1 else (SUBLANE, LANE), jnp.float32)],
        compiler_params=pltpu.CompilerParams(
            dimension_semantics=("parallel", "parallel", "arbitrary"),
            vmem_limit_bytes=VMEM_LIMIT),
        name="dense_mm",
    )(*args)


def _merge_kernel(oa_ref, ob_ref, oc_ref, wa_ref, wb_ref, wc_ref, ga_ref, gb_ref, gc_ref, o_ref):
    f32 = jnp.float32
    acc = jax.nn.sigmoid(ga_ref[...]) * jnp.dot(oa_ref[...], wa_ref[...], preferred_element_type=f32)
    acc += jax.nn.sigmoid(gb_ref[...]) * jnp.dot(ob_ref[...], wb_ref[...], preferred_element_type=f32)
    acc += jax.nn.sigmoid(gc_ref[...]) * jnp.dot(oc_ref[...], wc_ref[...], preferred_element_type=f32)
    o_ref[...] = acc.astype(o_ref.dtype)


def merge_mm(o_a, o_b, o_c, w_a, w_b, w_c, mg, out_dtype=jnp.bfloat16):
    M = o_a.shape[0]
    D = w_a.shape[1]
    tm = _pick(M, (512, 256, 128, 64, 32, 16, 8))
    tn = _pick(D, (512, 256, 128))
    nj = D // tn
    o_spec = lambda o: pl.BlockSpec((tm, o.shape[1]), lambda i, j: (i, 0))
    w_spec = lambda w: pl.BlockSpec((w.shape[0], tn), lambda i, j: (0, j))
    g_spec = lambda b: pl.BlockSpec((tm, tn), lambda i, j: (i, b * nj + j))
    return pl.pallas_call(
        _merge_kernel,
        grid=(M // tm, nj),
        in_specs=[o_spec(o_a), o_spec(o_b), o_spec(o_c), w_spec(w_a), w_spec(w_b), w_spec(w_c),
                  g_spec(0), g_spec(1), g_spec(2)],
        out_specs=pl.BlockSpec((tm, tn), lambda i, j: (i, j)),
        out_shape=jax.ShapeDtypeStruct((M, D), out_dtype),
        compiler_params=pltpu.CompilerParams(dimension_semantics=("parallel", "parallel"),
                                             vmem_limit_bytes=VMEM_LIMIT),
        name="merge_mm",
    )(o_a, o_b, o_c, w_a, w_b, w_c, mg, mg, mg)


def _ffn_gate_up_kernel(x_ref, wg_ref, wu_ref, cw_ref, cb_ref, st_ref, act_ref, tail_ref, carry_scr, *, tm, tps, T):
    f32 = jnp.float32
    i, j = pl.program_id(0), pl.program_id(1)
    x = x_ref[...]
    h = jnp.dot(x, wg_ref[...].astype(x.dtype), preferred_element_type=f32)
    u = jnp.dot(x, wu_ref[...].astype(x.dtype), preferred_element_type=f32)
    cw = cw_ref[...]
    if T == 1:
        prev2, prev1 = st_ref[0], st_ref[1]
        tail_ref[0] = prev1
        tail_ref[1] = h
    else:
        tail = jnp.where(i % tps == 0, st_ref[...], carry_scr[j])
        row = lax.broadcasted_iota(jnp.int32, h.shape, 0)
        prev1 = jnp.where(row == 0, tail[7:8], pltpu.roll(h, 1, axis=0))
        prev2 = jnp.where(row == 0, tail[6:7], jnp.where(row == 1, tail[7:8], pltpu.roll(h, 2, axis=0)))
        last = h[tm - SUBLANE:tm]
        carry_scr[j] = last
        tail_ref[...] = last
    hc = cb_ref[...] + prev2 * cw[0:1] + prev1 * cw[1:2] + h * cw[2:3]
    act_ref[...] = (jax.nn.silu(hc) * u).astype(act_ref.dtype)


def ffn_gate_up(xn, w_gate, w_up, conv_w, conv_b, conv0, B, T):
    M, D = xn.shape
    F = w_gate.shape[1]
    tn = _pick(F, (512, 256, 128))
    nj = F // tn
    cw = jnp.pad(conv_w, ((0, SUBLANE - CONV_W), (0, 0)))
    cb = conv_b.reshape(1, F)
    if T == 1:
        tm, tps = M, 1
        st = conv0.transpose(1, 0, 2)
        st_spec = pl.BlockSpec((2, B, tn), lambda i, j: (0, 0, j))
        tail_shape, tail_spec = (2, B, F), pl.BlockSpec((2, B, tn), lambda i, j: (0, 0, j))
    else:
        tm = _pick(T, (1024, 512, 256, 128, 64, 32, 16, 8))
        tps = T // tm
        st = jnp.pad(conv0, ((0, 0), (SUBLANE - 2, 0), (0, 0)))
        st_spec = pl.BlockSpec((None, SUBLANE, tn), lambda i, j: (i // tps, 0, j))
        tail_shape, tail_spec = (B, SUBLANE, F), pl.BlockSpec((None, SUBLANE, tn), lambda i, j: (i // tps, 0, j))
    act, tail = pl.pallas_call(
        functools.partial(_ffn_gate_up_kernel, tm=tm, tps=tps, T=T),
        grid=(M // tm, nj),
        in_specs=[pl.BlockSpec((tm, D), lambda i, j: (i, 0)),
                  pl.BlockSpec((D, tn), lambda i, j: (0, j)),
                  pl.BlockSpec((D, tn), lambda i, j: (0, j)),
                  pl.BlockSpec((SUBLANE, tn), lambda i, j: (0, j)),
                  pl.BlockSpec((1, tn), lambda i, j: (0, j)),
                  st_spec],
        out_specs=[pl.BlockSpec((tm, tn), lambda i, j: (i, j)), tail_spec],
        out_shape=[jax.ShapeDtypeStruct((M, F), jnp.bfloat16), jax.ShapeDtypeStruct(tail_shape, jnp.float32)],
        scratch_shapes=[pltpu.VMEM((nj, SUBLANE, tn), jnp.float32)],
        compiler_params=pltpu.CompilerParams(dimension_semantics=("arbitrary", "arbitrary"),
                                             vmem_limit_bytes=VMEM_LIMIT),
        name="ffn_gate_up",
    )(xn, w_gate, w_up, cw, cb, st)
    conv_new = tail.transpose(1, 0, 2) if T == 1 else tail[:, SUBLANE - 2:]
    return act, conv_new


def _gla_kernel(q_ref, k_ref, v_ref, og_ref, lo_ref, wa_ref, ba_ref, gn_ref, s0_ref, o_ref, sout_ref, s_scr, *, Tc, C):
    f32, bf16 = jnp.float32, jnp.bfloat16
    c = pl.program_id(2)

    @pl.when(c == 0)
    def _():
        s_scr[...] = s0_ref[...]

    row = lax.broadcasted_iota(jnp.int32, (C, GLA_DK), 0)
    tril = lax.broadcasted_iota(jnp.int32, (C, C), 0) >= lax.broadcasted_iota(jnp.int32, (C, C), 1)
    wa = wa_ref[...]
    ba = ba_ref[...]
    gn = gn_ref[...]
    S = s_scr[...]
    for n in range(Tc // C):
        rows = slice(n * C, (n + 1) * C)
        la = jax.nn.log_sigmoid(jnp.dot(lo_ref[rows, :].astype(bf16), wa, preferred_element_type=f32) + ba) / GLA_TAU
        bc = la
        d = 1
        while d < C:
            bc = bc + jnp.where(row >= d, pltpu.roll(bc, d, axis=0), 0.0)
            d *= 2
        b_mid = bc[C // 2:C // 2 + 1]
        b_last = bc[C - 1:C]
        q = q_ref[rows, :] * GLA_DK ** -0.5
        k = k_ref[rows, :]
        v = v_ref[rows, :].astype(bf16)
        att = lax.dot_general((q * jnp.exp(bc - b_mid)).astype(bf16), (k * jnp.exp(b_mid - bc)).astype(bf16),
                              (((1,), (1,)), ((), ())), preferred_element_type=f32)
        att = jnp.where(tril, att, 0.0)
        o = jnp.dot(att.astype(bf16), v, preferred_element_type=f32)
        o = o + jnp.dot((q * jnp.exp(bc)).astype(bf16), S.astype(bf16), preferred_element_type=f32)
        kd = jnp.concatenate([k * jnp.exp(b_last - bc), jnp.broadcast_to(jnp.exp(b_last), (SUBLANE, GLA_DK))], axis=0)
        kdt = kd.T
        S = kdt[:, C:C + 1] * S + jnp.dot(kdt[:, :C].astype(bf16), v, preferred_element_type=f32)
        o = o * lax.rsqrt(jnp.mean(o * o, axis=-1, keepdims=True) + NORM_EPS) * gn
        o_ref[rows, :] = (o * jax.nn.silu(og_ref[rows, :])).astype(o_ref.dtype)
    s_scr[...] = S

    @pl.when(c == pl.num_programs(2) - 1)
    def _():
        sout_ref[...] = S


def gla_pallas(hmix, wa2, ba, gnorm, s0, seg_off):
    B, T, _ = hmix.shape
    H = GLA_HEADS
    Tc = min(GLA_TC, T)
    C = GLA_CHUNK
    oq, ok, ov, og, ol = seg_off
    wa = jnp.pad(wa2, ((0, LANE - wa2.shape[0]), (0, 0))).astype(jnp.bfloat16)
    col = lambda off, w: (lambda b, h, c: (b, c, off // w + h))
    o, s = pl.pallas_call(
        functools.partial(_gla_kernel, Tc=Tc, C=C),
        grid=(B, H, T // Tc),
        in_specs=[pl.BlockSpec((None, Tc, GLA_DK), col(oq, GLA_DK)),
                  pl.BlockSpec((None, Tc, GLA_DK), col(ok, GLA_DK)),
                  pl.BlockSpec((None, Tc, GLA_DV), col(ov, GLA_DV)),
                  pl.BlockSpec((None, Tc, GLA_DV), col(og, GLA_DV)),
                  pl.BlockSpec((None, Tc, LANE), lambda b, h, c: (b, c, ol // LANE)),
                  pl.BlockSpec((LANE, GLA_DK), lambda b, h, c: (0, h)),
                  pl.BlockSpec((1, GLA_DK), lambda b, h, c: (0, h)),
                  pl.BlockSpec((1, GLA_DV), lambda b, h, c: (0, 0)),
                  pl.BlockSpec((None, None, GLA_DK, GLA_DV), lambda b, h, c: (b, h, 0, 0))],
        out_specs=[pl.BlockSpec((None, Tc, GLA_DV), lambda b, h, c: (b, c, h)),
                   pl.BlockSpec((None, None, GLA_DK, GLA_DV), lambda b, h, c: (b, h, 0, 0))],
        out_shape=[jax.ShapeDtypeStruct((B, T, H * GLA_DV), jnp.bfloat16),
                   jax.ShapeDtypeStruct((B, H, GLA_DK, GLA_DV), jnp.float32)],
        scratch_shapes=[pltpu.VMEM((GLA_DK, GLA_DV), jnp.float32)],
        compiler_params=pltpu.CompilerParams(dimension_semantics=("parallel", "parallel", "arbitrary"),
                                             vmem_limit_bytes=VMEM_LIMIT),
        name="gla_chunked",
    )(hmix, hmix, hmix, hmix, hmix, wa, ba.reshape(1, -1), gnorm.reshape(1, -1), s0)
    return o, s


def _head_sums(x, bd):
    f32, bf16 = jnp.float32, jnp.bfloat16
    outs = []
    for t in range(x.shape[1] // LANE):
        p = x[:, t * LANE:(t + 1) * LANE]
        hi = p.astype(bf16)
        lo = (p - hi.astype(f32)).astype(bf16)
        outs.append(jnp.dot(hi, bd, preferred_element_type=f32) + jnp.dot(lo, bd, preferred_element_type=f32))
    return jnp.concatenate(outs, axis=1)


def _block_diag_ones():
    rr = lax.broadcasted_iota(jnp.int32, (LANE, LANE), 0) // RWKV_N
    cc = lax.broadcasted_iota(jnp.int32, (LANE, LANE), 1) // RWKV_N
    return jnp.where(rr == cc, 1.0, 0.0).astype(jnp.bfloat16)


def _rwkv_prep_kernel(x_ref, sh_ref, mu_ref, w0_ref, a0_ref, kkw_ref, ka_ref, rk_ref, w2_ref, a2_ref, g2_ref,
                      r_ref, wl_ref, k_ref, v_ref, kk_ref, a_ref, gate_ref, bonus_ref, tail_ref, carry_scr, *, tm, T):
    f32, bf16 = jnp.float32, jnp.bfloat16
    W = RWKV_WIDTH
    i = pl.program_id(1)
    x = x_ref[...]
    if T == 1:
        prev = sh_ref[...]
        tail_ref[...] = x
    else:
        first = jnp.where(i == 0, sh_ref[SUBLANE - 1:SUBLANE], carry_scr[SUBLANE - 1:SUBLANE])
        row = lax.broadcasted_iota(jnp.int32, x.shape, 0)
        prev = jnp.where(row == 0, first, pltpu.roll(x, 1, axis=0))
        last = x[tm - SUBLANE:tm]
        carry_scr[...] = last
        tail_ref[...] = last
    rm = x + (prev - x) * mu_ref[...]
    r, k, v = rm[:, :W], rm[:, W:2 * W], rm[:, 2 * W:3 * W]
    lo = rm[:, 3 * W:3 * W + LANE]
    glo = rm[:, 3 * W + LANE:]
    w_raw = w0_ref[...] + jnp.dot(jnp.tanh(lo).astype(bf16), w2_ref[...], preferred_element_type=f32)
    wl_ref[...] = -jnp.exp(-jax.nn.softplus(-w_raw) - 0.5)
    a = jax.nn.sigmoid(a0_ref[...] + jnp.dot(lo.astype(bf16), a2_ref[...], preferred_element_type=f32))
    gate_ref[...] = jnp.dot(jax.nn.sigmoid(glo).astype(bf16), g2_ref[...], preferred_element_type=f32)
    bd = _block_diag_ones()
    kk = k * kkw_ref[...]
    kk_ref[...] = kk * lax.rsqrt(jnp.maximum(_head_sums(kk * kk, bd), 1e-24))
    k2 = k * (1.0 + (a - 1.0) * ka_ref[...])
    bonus_ref[...] = _head_sums(r * k2 * rk_ref[...], bd) * v
    r_ref[...] = r
    k_ref[...] = k2
    v_ref[...] = v
    a_ref[...] = a


def rwkv_prep(hr, shift0, mu, w0, w2, a0, a2, g2, kkw, ka, rk):
    B, T, WP = hr.shape
    W = RWKV_WIDTH
    bf16 = jnp.bfloat16
    padc = lambda t: jnp.pad(t, ((0, 0), (0, WP - t.shape[1])))
    w2p = jnp.pad(w2, ((0, LANE - RWKV_DECAY_RANK), (0, 0))).astype(bf16)
    a2p = jnp.pad(a2, ((RWKV_DECAY_RANK, 0), (0, 0))).astype(bf16)
    gpad = WP - 3 * W - LANE
    g2p = jnp.pad(g2, ((0, gpad - RWKV_GATE_RANK), (0, 0))).astype(bf16)
    row = lambda t: t.reshape(1, -1)
    if T == 1:
        tm = 1
        sh = padc(shift0).reshape(B, 1, WP)
        sh_spec = pl.BlockSpec((None, 1, WP), lambda b, i: (b, 0, 0))
        tail_rows = 1
    else:
        tm = _pick(T, (256, 128, 64, 32, 16, 8))
        sh = jnp.broadcast_to(padc(shift0)[:, None, :], (B, SUBLANE, WP))
        sh_spec = pl.BlockSpec((None, SUBLANE, WP), lambda b, i: (b, 0, 0))
        tail_rows = SUBLANE
    vec = lambda n: pl.BlockSpec((1, n), lambda b, i: (0, 0))
    mat = lambda m: pl.BlockSpec(m.shape, lambda b, i: (0, 0))
    o_spec = pl.BlockSpec((None, tm, W), lambda b, i: (b, i, 0))
    o_shape = jax.ShapeDtypeStruct((B, T, W), jnp.float32)
    outs = pl.pallas_call(
        functools.partial(_rwkv_prep_kernel, tm=tm, T=T),
        grid=(B, T // tm),
        in_specs=[pl.BlockSpec((None, tm, WP), lambda b, i: (b, i, 0)), sh_spec, vec(WP),
                  vec(W), vec(W), vec(W), vec(W), vec(W), mat(w2p), mat(a2p), mat(g2p)],
        out_specs=[o_spec] * 8 + [pl.BlockSpec((None, tail_rows, WP), lambda b, i: (b, 0, 0))],
        out_shape=[o_shape] * 8 + [jax.ShapeDtypeStruct((B, tail_rows, WP), jnp.float32)],
        scratch_shapes=[pltpu.VMEM((SUBLANE, WP), jnp.float32)],
        compiler_params=pltpu.CompilerParams(dimension_semantics=("parallel", "arbitrary"),
                                             vmem_limit_bytes=VMEM_LIMIT),
        name="rwkv_prep",
    )(hr, sh, row(padc(mu.reshape(1, -1))), row(w0), row(a0), row(kkw), row(ka), row(rk), w2p, a2p, g2p)
    return outs[:8], outs[8][:, tail_rows - 1, :RWKV_COLS]


def _rwkv_post_kernel(y_ref, bonus_ref, gate_ref, lw_ref, lb_ref, o_ref):
    bd = _block_diag_ones()
    y = y_ref[...]
    d = y - _head_sums(y, bd) * (1.0 / RWKV_N)
    var = _head_sums(d * d, bd) * (1.0 / RWKV_N)
    yn = d * lax.rsqrt(var + RWKV_LN_EPS) * lw_ref[...] + lb_ref[...]
    o_ref[...] = ((yn + bonus_ref[...]) * gate_ref[...]).astype(o_ref.dtype)


def rwkv_post(y, bonus, gate, ln_w, ln_b):
    M, W = y.shape
    tm = _pick(M, (256, 128, 64, 32, 16, 8))
    spec = pl.BlockSpec((tm, W), lambda i: (i, 0))
    vec = pl.BlockSpec((1, W), lambda i: (0, 0))
    return pl.pallas_call(
        _rwkv_post_kernel,
        grid=(M // tm,),
        in_specs=[spec, spec, spec, vec, vec],
        out_specs=spec,
        out_shape=jax.ShapeDtypeStruct((M, W), jnp.bfloat16),
        compiler_params=pltpu.CompilerParams(dimension_semantics=("parallel",), vmem_limit_bytes=VMEM_LIMIT),
        name="rwkv_post",
    )(y, bonus, gate, ln_w.reshape(1, W), ln_b.reshape(1, W))


def _rwkv_kernel(r_ref, wl_ref, k_ref, v_ref, kk_ref, a_ref, s0_ref, y_ref, sout_ref, s_scr, *, NP, Tc):
    c = pl.program_id(1)
    f32, bf16 = jnp.float32, jnp.bfloat16
    U = min(SUBLANE, Tc)

    @pl.when(c == 0)
    def _():
        s_scr[...] = s0_ref[...]

    sub = lax.broadcasted_iota(jnp.int32, (RWKV_N, LANE), 0)
    lane = lax.broadcasted_iota(jnp.int32, (RWKV_N, LANE), 1)
    eye2 = (lane % RWKV_N) == sub
    rr = lax.broadcasted_iota(jnp.int32, (LANE, LANE), 0) // RWKV_N
    cc = lax.broadcasted_iota(jnp.int32, (LANE, LANE), 1) // RWKV_N
    bd = jnp.where(rr == cc, 1.0, 0.0).astype(bf16)

    def ssb(p, two_piece=True):
        hi = p.astype(bf16)
        out = jnp.dot(hi, bd, preferred_element_type=f32)
        if two_piece:
            lo = (p - hi.astype(f32)).astype(bf16)
            out = out + jnp.dot(lo, bd, preferred_element_type=f32)
        return out

    eye_all = jnp.concatenate([eye2] * NP, axis=0)

    def bcast(x8, s):
        return jnp.concatenate(
            [jnp.broadcast_to(x8[s:s + 1, p * LANE:(p + 1) * LANE], (RWKV_N, LANE)) for p in range(NP)], axis=0)

    def body(g, carry):
        rows = pl.ds(pl.multiple_of(g * U, U), U)
        r8 = r_ref[rows, :]
        w8 = jnp.exp(wl_ref[rows, :])
        k8 = k_ref[rows, :]
        v8 = v_ref[rows, :]
        kk8 = kk_ref[rows, :]
        ka8 = kk8 * a_ref[rows, :]
        nk8 = -kk8
        S = s_scr[...]
        ys = []
        for s in range(U):
            sa = ssb(S * bcast(nk8, s))
            vb = ssb(jnp.where(eye_all, bcast(v8, s), 0.0))
            S = S * bcast(w8, s) + sa * bcast(ka8, s) + vb * bcast(k8, s)
            yb = jnp.where(eye_all, ssb(S * bcast(r8, s), two_piece=False), 0.0)
            ys.append(jnp.concatenate(
                [jnp.sum(yb[p * RWKV_N:(p + 1) * RWKV_N], axis=0, keepdims=True) for p in range(NP)], axis=1))
        s_scr[...] = S
        y_ref[rows, :] = ys[0] if U == 1 else jnp.concatenate(ys, axis=0)
        return carry

    lax.fori_loop(0, Tc // U, body, 0)

    @pl.when(c == pl.num_programs(1) - 1)
    def _():
        sout_ref[...] = s_scr[...]


def rwkv_scan_pallas(r, w_log, k, v, kk, a, s0):
    B, T, W = r.shape
    H = W // RWKV_N
    NP = H // 2
    Tc = 128 if T % 128 == 0 else T
    s0p = s0.reshape(B, NP, 2, RWKV_N, RWKV_N).transpose(0, 1, 3, 2, 4).reshape(B, NP * RWKV_N, LANE)
    blk = pl.BlockSpec((None, Tc, W), lambda b, c: (b, c, 0))
    sblk = pl.BlockSpec((None, NP * RWKV_N, LANE), lambda b, c: (b, 0, 0))
    y, sp = pl.pallas_call(
        functools.partial(_rwkv_kernel, NP=NP, Tc=Tc),
        grid=(B, T // Tc),
        in_specs=[blk] * 6 + [sblk],
        out_specs=[blk, sblk],
        out_shape=[jax.ShapeDtypeStruct((B, T, W), jnp.float32),
                   jax.ShapeDtypeStruct((B, NP * RWKV_N, LANE), jnp.float32)],
        scratch_shapes=[pltpu.VMEM((NP * RWKV_N, LANE), jnp.float32)],
        compiler_params=pltpu.CompilerParams(dimension_semantics=("parallel", "arbitrary"),
                                             vmem_limit_bytes=VMEM_LIMIT),
        name="rwkv7_scan",
    )(r, w_log, k, v, kk, a, s0p)
    s_fin = sp.reshape(B, NP, RWKV_N, 2, RWKV_N).transpose(0, 1, 3, 2, 4).reshape(B, H, RWKV_N, RWKV_N)
    return y, s_fin


def rope_tables(pos):
    half = ROPE_DIM // 2
    inv = ROPE_THETA ** (-jnp.arange(half, dtype=jnp.float32) / half)
    ang = pos.astype(jnp.float32)[:, None] * inv[None, :]
    cos, sin = jnp.cos(ang), jnp.sin(ang)
    T = pos.shape[0]
    z = jnp.zeros((T, HEAD_DIM - ROPE_DIM), jnp.float32)
    zh = jnp.zeros((T, half), jnp.float32)
    c = jnp.concatenate([cos, cos, jnp.ones_like(z)], axis=1)
    s_up = jnp.concatenate([-sin, zh, z], axis=1)
    s_dn = jnp.concatenate([zh, sin, z], axis=1)
    return c, s_up, s_dn


def _nsa_prep_kernel(q_ref, c_ref, s_ref, w_ref, tc_ref, tu_ref, td_ref, qo_ref, co_ref, so_ref, wo_ref):
    c, su, sd = tc_ref[...], tu_ref[...], td_ref[...]
    half = ROPE_DIM // 2

    def rot(x):
        return x * c + pltpu.roll(x, HEAD_DIM - half, axis=1) * su + pltpu.roll(x, half, axis=1) * sd

    for h in range(NSA_HEADS):
        cols = slice(h * HEAD_DIM, (h + 1) * HEAD_DIM)
        qo_ref[:, cols] = (rot(q_ref[:, cols]) * HEAD_DIM ** -0.5).astype(qo_ref.dtype)
    for src, dst in ((c_ref, co_ref), (s_ref, so_ref), (w_ref, wo_ref)):
        for g in range(NSA_KV_HEADS):
            cols = slice(g * HEAD_DIM, (g + 1) * HEAD_DIM)
            dst[:, cols] = rot(src[:, cols])
        dst[:, NSA_KV_WIDTH:] = src[:, NSA_KV_WIDTH:]


def nsa_prep(hn, pos):
    B, T, _ = hn.shape
    tm = _pick(T, (256, 128, 64, 32, 16, 8))
    tabs = rope_tables(pos)
    kvw = 2 * NSA_KV_WIDTH
    q_spec = pl.BlockSpec((None, tm, NSA_WIDTH), lambda b, i: (b, i, 0))
    kv_spec = lambda n: pl.BlockSpec((None, tm, kvw), lambda b, i: (b, i, NSA_WIDTH // kvw + n))
    t_spec = pl.BlockSpec((tm, HEAD_DIM), lambda b, i: (i, 0))
    o_spec = pl.BlockSpec((None, tm, kvw), lambda b, i: (b, i, 0))
    kv_shape = jax.ShapeDtypeStruct((B, T, kvw), jnp.float32)
    return pl.pallas_call(
        _nsa_prep_kernel,
        grid=(B, T // tm),
        in_specs=[q_spec, kv_spec(0), kv_spec(1), kv_spec(2), t_spec, t_spec, t_spec],
        out_specs=[q_spec, o_spec, o_spec, o_spec],
        out_shape=[jax.ShapeDtypeStruct((B, T, NSA_WIDTH), jnp.bfloat16), kv_shape, kv_shape, kv_shape],
        compiler_params=pltpu.CompilerParams(dimension_semantics=("parallel", "parallel"),
                                             vmem_limit_bytes=VMEM_LIMIT),
        name="nsa_prep",
    )(hn, hn, hn, hn, *tabs)


def _compress_kernel(x_ref, w1_ref, w2_ref, pe_ref, ko_ref, vo_ref, *, ns):
    f32, bf16 = jnp.float32, jnp.bfloat16
    G, HD = NSA_KV_HEADS, HEAD_DIM
    row_w = 2 * NSA_KV_WIDTH
    for kv, o_ref in ((0, ko_ref), (1, vo_ref)):
        pos = jnp.zeros((SUBLANE, HD), f32)
        for p in range(CMP_STRIDE):
            w = w1_ref[kv, p]
            lo = jnp.broadcast_to(pe_ref[kv, p:p + 1, :], (SUBLANE, HD)).astype(bf16)
            hi = jnp.broadcast_to(pe_ref[kv, CMP_STRIDE + p:CMP_STRIDE + p + 1, :], (SUBLANE, HD)).astype(bf16)
            pos = pos + jnp.dot(lo, w, preferred_element_type=f32)[:, :HD] + jnp.dot(hi, w, preferred_element_type=f32)[:, HD:]
        pos = pos[0:1]
        for g in range(G):
            acc = jnp.zeros((ns, 2 * HD), f32)
            for p in range(CMP_STRIDE):
                c0 = p * row_w + kv * NSA_KV_WIDTH + g * HD
                acc = acc + jnp.dot(x_ref[:, c0:c0 + HD].astype(bf16), w1_ref[kv, p], preferred_element_type=f32)
            nxt = pltpu.roll(acc[:, HD:], ns - 1, axis=0)
            hid = jax.nn.gelu(acc[:, :HD] + nxt + pos)
            o_ref[g] = jnp.dot(hid.astype(bf16), w2_ref[kv], preferred_element_type=f32).astype(o_ref.dtype)


def compress_weights(w1k, w2k, pek, w1v, w2v, pev):
    bf16 = jnp.bfloat16
    cat = lambda w1: jnp.concatenate([w1[:CMP_STRIDE], w1[CMP_STRIDE:]], axis=-1)
    return (jnp.stack([cat(w1k), cat(w1v)]).astype(bf16), jnp.stack([w2k, w2v]).astype(bf16), jnp.stack([pek, pev]))


def compress_pallas(kv_rows, w1, w2, pe):
    B, T, W = kv_rows.shape
    ns = T // CMP_STRIDE
    bf16 = jnp.bfloat16
    x = kv_rows.reshape(B, ns, CMP_STRIDE * W)
    out = jax.ShapeDtypeStruct((B, NSA_KV_HEADS, ns, HEAD_DIM), bf16)
    o_spec = pl.BlockSpec((None, NSA_KV_HEADS, ns, HEAD_DIM), lambda b: (b, 0, 0, 0))
    return pl.pallas_call(
        functools.partial(_compress_kernel, ns=ns),
        grid=(B,),
        in_specs=[pl.BlockSpec((None, ns, CMP_STRIDE * W), lambda b: (b, 0, 0)),
                  pl.BlockSpec(w1.shape, lambda b: (0, 0, 0, 0)),
                  pl.BlockSpec(w2.shape, lambda b: (0, 0, 0)),
                  pl.BlockSpec(pe.shape, lambda b: (0, 0, 0))],
        out_specs=[o_spec, o_spec],
        out_shape=[out, out],
        compiler_params=pltpu.CompilerParams(dimension_semantics=("parallel",), vmem_limit_bytes=VMEM_LIMIT),
        name="nsa_compress",
    )(x, w1, w2, pe)


def _dot_nt(a, b):
    return lax.dot_general(a, b, (((1,), (1,)), ((), ())), preferred_element_type=jnp.float32)


def _nsa_kernel(q_ref, kc_ref, vc_ref, ks_ref, vs_ref, kw_ref, vw_ref, g_ref, covt_ref, e_ref, o_ref,
                bias_scr, p4_scr, ocmp_scr, m_scr, l_scr, acc_scr, *, TQ, T, NS, NCP, n_top):
    f32, bf16 = jnp.float32, jnp.bfloat16
    R = NSA_GROUP
    i = pl.program_id(2)
    nchunk = T // TQ
    qpos_col = i * TQ + lax.broadcasted_iota(jnp.int32, (TQ, 1), 0)

    kc = kc_ref[...]
    vc = vc_ref[...]
    cend = lax.broadcasted_iota(jnp.int32, (1, NCP), 1) * CMP_STRIDE + (CMP_BLOCK - 1)
    valid = cend <= qpos_col
    for r in range(R):
        s = _dot_nt(q_ref[:, r * HEAD_DIM:(r + 1) * HEAD_DIM], kc)
        s = jnp.where(valid, s, NEG_INF)
        m = jnp.max(s, axis=-1, keepdims=True)
        p = jnp.where(valid, jnp.exp(s - m), 0.0)
        l = jnp.sum(p, axis=-1, keepdims=True)
        p = (p / jnp.where(l > 0.0, l, 1.0)).astype(bf16)
        p4_scr[:, r * NCP:(r + 1) * NCP] = p
        ocmp_scr[r] = jnp.dot(p, vc, preferred_element_type=f32)

    imp_t = _dot_nt(covt_ref[...], p4_scr[...])
    j = lax.broadcasted_iota(jnp.int32, (NS, TQ), 0)
    qblk = (i * TQ + lax.broadcasted_iota(jnp.int32, (NS, TQ), 1)) // SEL_BLOCK
    forced = (j < N_INIT_BLOCKS) | ((j <= qblk) & (j > qblk - N_LOCAL_BLOCKS))
    score = jnp.where(forced, FORCE_SCORE, jnp.where(j <= qblk, imp_t, NEG_INF))
    rank = jnp.zeros((NS, TQ), f32)
    for a in range(NS):
        row = score[a:a + 1, :]
        beats = (row > score) | ((row == score) & (a < j))
        rank = rank + jnp.where(beats, 1.0, 0.0)
    sel_t = jnp.where(rank < n_top, 1.0, 0.0)
    if NS < LANE:
        sel_t = jnp.concatenate([sel_t, jnp.zeros((LANE - NS, TQ), f32)], axis=0)
    sel = sel_t.T.astype(bf16)
    for c in range(nchunk):
        selexp = jnp.dot(sel, e_ref[:, c * TQ:(c + 1) * TQ], preferred_element_type=f32)
        kpos = c * TQ + lax.broadcasted_iota(jnp.int32, (TQ, TQ), 1)
        bias_scr[c] = jnp.where((selexp > 0.5) & (kpos <= qpos_col), 0.0, NEG_INF)

    def attend(k_ref, v_ref, lo, masker):
        m_scr[...] = jnp.full(m_scr.shape, NEG_INF, f32)
        l_scr[...] = jnp.zeros(l_scr.shape, f32)
        acc_scr[...] = jnp.zeros(acc_scr.shape, f32)

        def chunk(c, carry):
            rows = pl.ds(pl.multiple_of(c * TQ, TQ), TQ)
            k = k_ref[rows, :].astype(bf16)
            v = v_ref[rows, :].astype(bf16)
            mk = masker(c)
            heads = range(R)
            sk = [mk(_dot_nt(q_ref[:, r * HEAD_DIM:(r + 1) * HEAD_DIM], k)) for r in heads]
            m_prev = [m_scr[r] for r in heads]
            m_new = [jnp.maximum(m_prev[r], jnp.max(sk[r][0], axis=-1, keepdims=True)) for r in heads]
            alpha = [jnp.exp(m_prev[r] - m_new[r]) for r in heads]
            ps = [jnp.exp(sk[r][0] - m_new[r]) for r in heads]
            ps = [p if sk[r][1] is None else jnp.where(sk[r][1], p, 0.0) for r, p in enumerate(ps)]
            pv = [jnp.dot(ps[r].astype(bf16), v, preferred_element_type=f32) for r in heads]
            for r in heads:
                l_scr[r] = alpha[r] * l_scr[r] + jnp.sum(ps[r], axis=-1, keepdims=True)
                acc_scr[r] = alpha[r] * acc_scr[r] + pv[r]
                m_scr[r] = m_new[r]
            return carry

        lax.fori_loop(lo, i + 1, chunk, 0)

    def sel_masker(c):
        b = bias_scr[c]
        return lambda s: (s + b, None)

    attend(ks_ref, vs_ref, 0, sel_masker)
    g = pltpu.roll(jax.nn.sigmoid(g_ref[...]), (LANE - 3 * R * pl.program_id(1)) % LANE, axis=1)
    for r in range(R):
        ocmp_scr[r] = (g[:, 3 * r:3 * r + 1] * ocmp_scr[r]
                       + g[:, 3 * r + 1:3 * r + 2] * (acc_scr[r] / l_scr[r]))

    def win_masker(c):
        rel = ((i - c) * TQ + lax.broadcasted_iota(jnp.int32, (TQ, TQ), 0)
               - lax.broadcasted_iota(jnp.int32, (TQ, TQ), 1))
        ok = (rel >= 0) & (rel < WINDOW)
        return lambda s: (jnp.where(ok, s, NEG_INF), ok)

    attend(kw_ref, vw_ref, jnp.maximum(i - (WINDOW // TQ), 0), win_masker)
    for r in range(R):
        o = ocmp_scr[r] + g[:, 3 * r + 2:3 * r + 3] * (acc_scr[r] / l_scr[r])
        o_ref[:, r * HEAD_DIM:(r + 1) * HEAD_DIM] = o.astype(o_ref.dtype)


def nsa_attention_pallas(qr, kcmp, vcmp, slc, win, hn, gate_col):
    B, T, _ = qr.shape
    G, R = NSA_KV_HEADS, NSA_GROUP
    TQ = min(NSA_TQ, T)
    NS = T // SEL_BLOCK
    NC = T // CMP_STRIDE - 1
    NCP = kcmp.shape[2]
    n_top = min(SEL_TOP, NS)
    ci = np.arange(NCP)[:, None] * CMP_STRIDE
    sj = np.arange(NS)[None, :] * SEL_BLOCK
    cover = np.clip(np.minimum(ci + CMP_BLOCK, sj + SEL_BLOCK) - np.maximum(ci, sj), 0, None) / CMP_BLOCK
    cover[NC:] = 0.0
    covt = jnp.asarray(np.tile(cover.T, (1, R)), jnp.bfloat16)
    e = jnp.asarray((np.arange(T)[None, :] // SEL_BLOCK) == np.arange(LANE)[:, None], jnp.bfloat16)
    k_spec = pl.BlockSpec((None, T, HEAD_DIM), lambda b, g, i: (b, 0, g))
    v_spec = pl.BlockSpec((None, T, HEAD_DIM), lambda b, g, i: (b, 0, G + g))
    cmp_spec = pl.BlockSpec((None, None, NCP, HEAD_DIM), lambda b, g, i: (b, g, 0, 0))
    return pl.pallas_call(
        functools.partial(_nsa_kernel, TQ=TQ, T=T, NS=NS, NCP=NCP, n_top=n_top),
        grid=(B, G, T // TQ),
        in_specs=[pl.BlockSpec((None, TQ, R * HEAD_DIM), lambda b, g, i: (b, i, g)),
                  cmp_spec, cmp_spec, k_spec, v_spec, k_spec, v_spec,
                  pl.BlockSpec((None, TQ, LANE), lambda b, g, i: (b, i, gate_col // LANE)),
                  pl.BlockSpec((NS, R * NCP), lambda b, g, i: (0, 0)),
                  pl.BlockSpec((LANE, T), lambda b, g, i: (0, 0))],
        out_specs=pl.BlockSpec((None, TQ, R * HEAD_DIM), lambda b, g, i: (b, i, g)),
        out_shape=jax.ShapeDtypeStruct((B, T, G * R * HEAD_DIM), jnp.bfloat16),
        scratch_shapes=[pltpu.VMEM((T // TQ, TQ, TQ), jnp.float32),
                        pltpu.VMEM((TQ, R * NCP), jnp.bfloat16),
                        pltpu.VMEM((R, TQ, HEAD_DIM), jnp.float32),
                        pltpu.VMEM((R, TQ, 1), jnp.float32),
                        pltpu.VMEM((R, TQ, 1), jnp.float32),
                        pltpu.VMEM((R, TQ, HEAD_DIM), jnp.float32)],
        compiler_params=pltpu.CompilerParams(dimension_semantics=("parallel", "parallel", "arbitrary"),
                                             vmem_limit_bytes=VMEM_LIMIT),
        name="nsa_attention",
    )(qr, kcmp, vcmp, slc, slc, win, win, hn, covt, e)


def _dec_compress_kernel(pt_ref, *refs):
    f32, bf16 = jnp.float32, jnp.bfloat16
    pages, w1_ref, o_ref = refs[:DEC_PAGES], refs[DEC_PAGES], refs[DEC_PAGES + 1]
    HD = HEAD_DIM
    row_w = 2 * NSA_KV_WIDTH
    for kv in range(2):
        for g in range(NSA_KV_HEADS):
            acc = None
            for p in range(0, CMP_STRIDE, 2):
                c0 = p * row_w + kv * NSA_KV_WIDTH + g * HD
                lhs = jnp.concatenate(
                    [jnp.concatenate([pg[:, c0:c0 + HD], pg[:, c0 + row_w:c0 + row_w + HD]], axis=1) for pg in pages],
                    axis=0).astype(bf16)
                w = jnp.concatenate([w1_ref[kv, p], w1_ref[kv, p + 1]], axis=0)
                part = jnp.dot(lhs, w, preferred_element_type=f32)
                acc = part if acc is None else acc + part
            o_ref[:, (kv * NSA_KV_HEADS + g) * 2 * HD:(kv * NSA_KV_HEADS + g + 1) * 2 * HD] = acc


def dec_compress(cache, page_table, w1):
    n_phys, page, W = cache.shape
    B, n_pages = page_table.shape
    seg = page // CMP_STRIDE
    c2 = cache.reshape(n_phys, seg, CMP_STRIDE * W)
    steps = n_pages // DEC_PAGES
    page_spec = lambda k: pl.BlockSpec((None, seg, CMP_STRIDE * W), lambda b, s, pt: (pt[b, s * DEC_PAGES + k], 0, 0))
    ow = 2 * NSA_KV_HEADS * 2 * HEAD_DIM
    return pl.pallas_call(
        _dec_compress_kernel,
        grid_spec=pltpu.PrefetchScalarGridSpec(
            num_scalar_prefetch=1,
            grid=(B, steps),
            in_specs=[page_spec(k) for k in range(DEC_PAGES)] + [pl.BlockSpec(w1.shape, lambda b, s, pt: (0, 0, 0, 0))],
            out_specs=pl.BlockSpec((None, DEC_PAGES * seg, ow), lambda b, s, pt: (b, s, 0))),
        out_shape=jax.ShapeDtypeStruct((B, n_pages * seg, ow), jnp.float32),
        compiler_params=pltpu.CompilerParams(dimension_semantics=("parallel", "arbitrary"),
                                             vmem_limit_bytes=VMEM_LIMIT),
        name="nsa_dec_compress",
    )(page_table, *([c2] * DEC_PAGES), w1)


def _dec_select_kernel(q_ref, fk_ref, fv_ref, w1_ref, w2_ref, pe_ref, covt_ref, ocmp_ref, idx_ref, *, NSEG, NS, NSP, n_top):
    f32, bf16 = jnp.float32, jnp.bfloat16
    HD, R = HEAD_DIM, NSA_GROUP
    NC = NSEG - 1

    def compressed(kv, f_ref):
        pos = jnp.zeros((SUBLANE, HD), f32)
        for p in range(CMP_STRIDE):
            w = w1_ref[kv, p]
            lo = jnp.broadcast_to(pe_ref[kv, p:p + 1, :], (SUBLANE, HD)).astype(bf16)
            hi = jnp.broadcast_to(pe_ref[kv, CMP_STRIDE + p:CMP_STRIDE + p + 1, :], (SUBLANE, HD)).astype(bf16)
            pos = pos + jnp.dot(lo, w, preferred_element_type=f32)[:, :HD] + jnp.dot(hi, w, preferred_element_type=f32)[:, HD:]
        nxt = pltpu.roll(f_ref[:, HD:], NSEG - 1, axis=0)
        hid = jax.nn.gelu(f_ref[:, :HD] + nxt + pos[0:1])
        return jnp.dot(hid.astype(bf16), w2_ref[kv], preferred_element_type=f32).astype(bf16)

    kc = compressed(0, fk_ref)
    vc = compressed(1, fv_ref)
    q = q_ref[...]
    q4 = jnp.concatenate([q[:, r * HD:(r + 1) * HD] for r in range(R)] + [jnp.zeros((SUBLANE - R, HD), bf16)], axis=0)
    s = _dot_nt(q4, kc)
    valid = lax.broadcasted_iota(jnp.int32, s.shape, 1) < NC
    s = jnp.where(valid, s, NEG_INF)
    p = jnp.where(valid, jnp.exp(s - jnp.max(s, axis=-1, keepdims=True)), 0.0)
    p = (p / jnp.sum(p, axis=-1, keepdims=True)).astype(bf16)
    ocmp_ref[...] = jnp.dot(p, vc, preferred_element_type=f32)
    head = lax.broadcasted_iota(jnp.int32, p.shape, 0) < R
    imp = _dot_nt(covt_ref[...], jnp.where(head, p, jnp.zeros_like(p)))
    imp = jnp.sum(imp, axis=1, keepdims=True)
    j_col = lax.broadcasted_iota(jnp.int32, (NSP, 1), 0)
    qblk = NS - 1
    forced = (j_col < N_INIT_BLOCKS) | ((j_col <= qblk) & (j_col > qblk - N_LOCAL_BLOCKS))
    score_col = jnp.where(forced, FORCE_SCORE, jnp.where(j_col <= qblk, imp, -3e38))
    score_cb = jnp.broadcast_to(score_col, (NSP, LANE))
    score_row = score_cb.T[0:1, :]
    ii = lax.broadcasted_iota(jnp.int32, (NSP, NSP), 0)
    jj = lax.broadcasted_iota(jnp.int32, (NSP, NSP), 1)
    beats = (score_col > score_row) | ((score_col == score_row) & (ii < jj))
    rank = jnp.sum(jnp.where(beats, 1.0, 0.0), axis=0, keepdims=True)
    t_col = lax.broadcasted_iota(jnp.int32, (n_top, NSP), 0).astype(f32)
    j_row = lax.broadcasted_iota(jnp.int32, (n_top, NSP), 1).astype(f32)
    ids = jnp.sum(jnp.where(rank == t_col, j_row, 0.0), axis=1, keepdims=True)
    idx_ref[...] = jnp.broadcast_to(ids, (n_top, LANE)).astype(jnp.int32)


def dec_select(qr, fs, w1, w2, pe, past_len):
    B = qr.shape[0]
    G, R, HD = NSA_KV_HEADS, NSA_GROUP, HEAD_DIM
    NSEG = fs.shape[1]
    NC = NSEG - 1
    NS = -(-(past_len + 1) // SEL_BLOCK)
    NSP = _round_up(NS, LANE)
    n_top = min(SEL_TOP, NS)
    ci = np.arange(NSEG)[:, None] * CMP_STRIDE
    sj = np.arange(NSP)[None, :] * SEL_BLOCK
    cover = np.clip(np.minimum(ci + CMP_BLOCK, sj + SEL_BLOCK) - np.maximum(ci, sj), 0, None) / CMP_BLOCK
    cover[NC:] = 0.0
    cover[:, NS:] = 0.0
    covt = jnp.asarray(cover.T, jnp.bfloat16)
    f_spec = lambda kv: pl.BlockSpec((None, NSEG, 2 * HD), lambda b, g: (b, 0, kv * G + g))
    full = lambda a: pl.BlockSpec(a.shape, lambda b, g: (0,) * a.ndim)
    ocmp, idx = pl.pallas_call(
        functools.partial(_dec_select_kernel, NSEG=NSEG, NS=NS, NSP=NSP, n_top=n_top),
        grid=(B, G),
        in_specs=[pl.BlockSpec((None, 1, R * HD), lambda b, g: (b, 0, g)), f_spec(0), f_spec(1),
                  full(w1), full(w2), full(pe), full(covt)],
        out_specs=[pl.BlockSpec((None, None, SUBLANE, HD), lambda b, g: (b, g, 0, 0)),
                   pl.BlockSpec((None, None, n_top, LANE), lambda b, g: (b, g, 0, 0))],
        out_shape=[jax.ShapeDtypeStruct((B, G, SUBLANE, HD), jnp.float32),
                   jax.ShapeDtypeStruct((B, G, n_top, LANE), jnp.int32)],
        compiler_params=pltpu.CompilerParams(dimension_semantics=("parallel", "parallel"),
                                             vmem_limit_bytes=VMEM_LIMIT),
        name="nsa_dec_select",
    )(qr, fs, fs, w1, w2, pe, covt)
    return ocmp, idx[:, :, :, 0]


def _dec_attend_kernel(pt_ref, idx_ref, q_ref, ks_ref, vs_ref, nks_ref, nvs_ref, kw_ref, vw_ref, nkw_ref, nvw_ref,
                       ocmp_ref, g_ref, o_ref, m_scr, l_scr, acc_scr, ow_scr, *, NS, n_top):
    f32, bf16 = jnp.float32, jnp.bfloat16
    HD, R = HEAD_DIM, NSA_GROUP
    b, g, t = pl.program_id(0), pl.program_id(1), pl.program_id(2)
    q = q_ref[...]
    q4 = jnp.concatenate([q[:, r * HD:(r + 1) * HD] for r in range(R)] + [jnp.zeros((SUBLANE - R, HD), bf16)], axis=0)

    def new_score(k_row_ref):
        k_new = jnp.broadcast_to(k_row_ref[...], (SUBLANE, HD)).astype(bf16)
        return jnp.sum(q4.astype(f32) * k_new.astype(f32), axis=-1, keepdims=True)

    @pl.when(t == 0)
    def _():
        m_scr[...] = new_score(nks_ref)
        l_scr[...] = jnp.ones(l_scr.shape, f32)
        acc_scr[...] = jnp.broadcast_to(nvs_ref[...], (SUBLANE, HD)).astype(bf16).astype(f32)
        nwin = kw_ref.shape[0]
        s = _dot_nt(q4, kw_ref[...].astype(bf16))
        ok = lax.broadcasted_iota(jnp.int32, s.shape, 1) > nwin - WINDOW
        s = jnp.where(ok, s, NEG_INF)
        s_new = new_score(nkw_ref)
        m = jnp.maximum(jnp.max(s, axis=-1, keepdims=True), s_new)
        p = jnp.where(ok, jnp.exp(s - m), 0.0)
        p_new = jnp.exp(s_new - m)
        l = jnp.sum(p, axis=-1, keepdims=True) + p_new
        v_new = jnp.broadcast_to(nvw_ref[...], (SUBLANE, HD)).astype(bf16).astype(f32)
        ow_scr[...] = (jnp.dot(p.astype(bf16), vw_ref[...].astype(bf16), preferred_element_type=f32)
                       + p_new.astype(bf16).astype(f32) * v_new) / l

    @pl.when(idx_ref[b, g, t] != NS - 1)
    def _():
        s = _dot_nt(q4, ks_ref[...].astype(bf16))
        m_prev = m_scr[...]
        m_new = jnp.maximum(m_prev, jnp.max(s, axis=-1, keepdims=True))
        alpha = jnp.exp(m_prev - m_new)
        p = jnp.exp(s - m_new)
        l_scr[...] = alpha * l_scr[...] + jnp.sum(p, axis=-1, keepdims=True)
        acc_scr[...] = alpha * acc_scr[...] + jnp.dot(p.astype(bf16), vs_ref[...].astype(bf16), preferred_element_type=f32)
        m_scr[...] = m_new

    @pl.when(t == n_top - 1)
    def _():
        gate = pltpu.roll(jax.nn.sigmoid(jnp.broadcast_to(g_ref[...], (SUBLANE, LANE))), (LANE - 3 * R * g) % LANE, axis=1)
        o_slc = acc_scr[...] / l_scr[...]
        rows = []
        for r in range(R):
            o = (gate[r:r + 1, 3 * r:3 * r + 1] * ocmp_ref[r:r + 1, :] + gate[r:r + 1, 3 * r + 1:3 * r + 2] * o_slc[r:r + 1, :]
                 + gate[r:r + 1, 3 * r + 2:3 * r + 3] * ow_scr[r:r + 1, :])
            rows.append(o)
        o_ref[...] = jnp.concatenate(rows, axis=1).astype(o_ref.dtype)


def dec_attend(qr, slc_cache, page_table, idx, new_slc, win_cache, new_win, ocmp, hn, gate_col):
    B = qr.shape[0]
    G, R, HD = NSA_KV_HEADS, NSA_GROUP, HEAD_DIM
    n_phys, page, W = slc_cache.shape
    n_top = idx.shape[2]
    NS = -(-(page_table.shape[1] * page + 1) // SEL_BLOCK)
    per = page // SEL_BLOCK
    blocks = slc_cache.reshape(n_phys * per, SEL_BLOCK, W)
    nwin = win_cache.shape[1]

    def blk(col):
        def index(b, g, t, pt, ix):
            j = jnp.minimum(ix[b, g, t], NS - 2)
            return (pt[b, j // per] * per + j % per, 0, col(g))
        return pl.BlockSpec((None, SEL_BLOCK, HD), index)

    row = lambda col: pl.BlockSpec((None, 1, HD), lambda b, g, t, pt, ix: (b, 0, col(g)))
    wsp = lambda col: pl.BlockSpec((None, nwin, HD), lambda b, g, t, pt, ix: (b, 0, col(g)))
    kcol, vcol = (lambda g: g), (lambda g: G + g)
    return pl.pallas_call(
        functools.partial(_dec_attend_kernel, NS=NS, n_top=n_top),
        grid_spec=pltpu.PrefetchScalarGridSpec(
            num_scalar_prefetch=2,
            grid=(B, G, n_top),
            in_specs=[pl.BlockSpec((None, 1, R * HD), lambda b, g, t, pt, ix: (b, 0, g)),
                      blk(kcol), blk(vcol), row(kcol), row(vcol), wsp(kcol), wsp(vcol), row(kcol), row(vcol),
                      pl.BlockSpec((None, None, SUBLANE, HD), lambda b, g, t, pt, ix: (b, g, 0, 0)),
                      pl.BlockSpec((None, 1, LANE), lambda b, g, t, pt, ix: (b, 0, gate_col // LANE))],
            out_specs=pl.BlockSpec((None, 1, R * HD), lambda b, g, t, pt, ix: (b, 0, g)),
            scratch_shapes=[pltpu.VMEM((SUBLANE, 1), jnp.float32), pltpu.VMEM((SUBLANE, 1), jnp.float32),
                            pltpu.VMEM((SUBLANE, HD), jnp.float32), pltpu.VMEM((SUBLANE, HD), jnp.float32)]),
        out_shape=jax.ShapeDtypeStruct((B, 1, G * R * HD), jnp.bfloat16),
        compiler_params=pltpu.CompilerParams(dimension_semantics=("parallel", "parallel", "arbitrary"),
                                             vmem_limit_bytes=VMEM_LIMIT),
        name="nsa_dec_attend",
    )(page_table, idx, qr, blocks, blocks, new_slc, new_slc, win_cache, win_cache, new_win, new_win, ocmp, hn)


def rmsnorm(x, g):
    xf = x.astype(jnp.float32)
    y = xf * lax.rsqrt(jnp.mean(xf * xf, axis=-1, keepdims=True) + NORM_EPS)
    return (y * g.astype(jnp.float32)).astype(x.dtype)


def gla_chunked(q, k, v, log_a, s0):
    B, T, H, DK = q.shape
    DV = v.shape[-1]
    C = GLA_CHUNK if T % GLA_CHUNK == 0 else T
    n = T // C

    def chunks(t):
        return t.astype(jnp.float32).reshape(B, n, C, H, t.shape[-1]).transpose(1, 0, 3, 2, 4)

    qc, kc, vc = chunks(q), chunks(k), chunks(v)
    bc = jnp.cumsum(chunks(log_a), axis=3)
    b_mid = bc[:, :, :, C // 2:C // 2 + 1]
    b_last = bc[:, :, :, C - 1:C]
    causal = jnp.tril(jnp.ones((C, C), dtype=bool))
    att = jnp.einsum('nbhid,nbhjd->nbhij', qc * jnp.exp(bc - b_mid), kc * jnp.exp(b_mid - bc))
    att = jnp.where(causal, att, 0.0)
    o_intra = jnp.einsum('nbhij,nbhjv->nbhiv', att, vc)
    q_dec = qc * jnp.exp(bc)
    k_dec = kc * jnp.exp(b_last - bc)
    a_last = jnp.exp(b_last[:, :, :, 0])

    def step(s, inp):
        qd, kd, vv, al = inp
        o = jnp.einsum('bhcd,bhdv->bhcv', qd, s)
        s = al[..., None] * s + jnp.einsum('bhcd,bhcv->bhdv', kd, vv)
        return s, o

    s_fin, o_inter = lax.scan(step, s0.astype(jnp.float32), (q_dec, k_dec, vc, a_last))
    o = (o_intra + o_inter).transpose(1, 0, 3, 2, 4).reshape(B, T, H, DV)
    return o, s_fin


def trunk_layer(x, pos0, paged, gla_s0, rwkv_s0, shift0, conv0, lw):
    B, T, _ = x.shape
    dt = x.dtype
    f32 = jnp.float32
    pos = pos0 + jnp.arange(T, dtype=jnp.int32)
    heads = lambda t, n: t.reshape(B, T, n, t.shape[-1] // n)

    bf = jnp.bfloat16
    M = B * T
    x2 = x.reshape(M, D_MODEL)
    xn = rmsnorm_pallas(x2, lw['norm1'])
    hg = mm(xn, lw['w_gla']).reshape(B, T, -1)
    hn = mm(xn, lw['w_nsa']).reshape(B, T, -1)
    hr = mm(xn, lw['w_rwkv']).reshape(B, T, -1)
    mg = mm(xn, lw['w_mg'])

    if T % GLA_TC == 0:
        o_gla, gla_s = gla_pallas(hg, lw['gla_wa2'], lw['gla_ba'], lw['gla_norm'], gla_s0.astype(f32), GLA_OFF)
    else:
        gq, gk, gv, gog, glo = [hg[..., o:o + s] for o, s in zip(GLA_OFF, IN_SIZES[:5])]
        log_a = jax.nn.log_sigmoid((glo @ lw['gla_wa2'] + lw['gla_ba']).astype(f32)) / GLA_TAU
        o_gla, gla_s = gla_chunked(heads(gq, GLA_HEADS) * GLA_DK ** -0.5, heads(gk, GLA_HEADS),
                                   heads(gv, GLA_HEADS), heads(log_a, GLA_HEADS), gla_s0)
        o_gla = rmsnorm(o_gla, lw['gla_norm']) * jax.nn.silu(heads(gog, GLA_HEADS).astype(f32))
    o_gla = o_gla.reshape(M, GLA_WIDTH).astype(bf)

    kv5 = lambda t: t.reshape(B, T, 2, NSA_KV_HEADS, HEAD_DIM)
    cw1, cw2, cpe = compress_weights(lw['cmp_w1k'], lw['cmp_w2k'], lw['cmp_pek'],
                                     lw['cmp_w1v'], lw['cmp_w2v'], lw['cmp_pev'])
    if paged is None:
        assert T % NSA_TQ == 0
        qr, cmp2, slc2, win2 = nsa_prep(hn, pos)
        kcmp, vcmp = compress_pallas(cmp2, cw1, cw2, cpe)
        o_nsa = nsa_attention_pallas(qr, kcmp, vcmp, slc2, win2, hn, NSA_GATE_OFF)
        new_cmp, new_slc, win_new = kv5(cmp2), kv5(slc2), kv5(win2)[:, T - min(WINDOW, T):]
    else:
        assert T == 1
        cache_cmp, cache_slc, win_buf, page_table = paged
        qr, cmp2, slc2, win2 = [t.reshape(B, 1, -1) for t in
                                nsa_prep(hn.reshape(1, B, -1), jnp.full((B,), pos0, jnp.int32))]
        rows = lambda c: c.reshape(c.shape[0], c.shape[1], 2 * NSA_KV_WIDTH)
        fs = dec_compress(rows(cache_cmp), page_table, cw1)
        ocmp, sel_ids = dec_select(qr, fs, cw1, cw2, cpe, pos0)
        o_nsa = dec_attend(qr, rows(cache_slc), page_table, sel_ids, slc2, rows(win_buf), win2, ocmp, hn, NSA_GATE_OFF)
        new_cmp, new_slc = kv5(cmp2), kv5(slc2)
        win_all = jnp.concatenate([win_buf.astype(dt), kv5(win2)], axis=1)
        win_new = win_all[:, win_all.shape[1] - min(WINDOW, win_all.shape[1]):]
    o_nsa = o_nsa.reshape(M, NSA_WIDTH)

    (r_, w_log, k2, v_, kk, a, gate, bonus), shift_new = rwkv_prep(
        hr, shift0.astype(f32), lw['rwkv_mu'], lw['rwkv_w0'], lw['rwkv_w2'], lw['rwkv_a0'], lw['rwkv_a2'],
        lw['rwkv_g2'], lw['rwkv_kk'], lw['rwkv_ka'], lw['rwkv_rk'])
    y, rwkv_s = rwkv_scan_pallas(r_, w_log, k2, v_, kk, a, rwkv_s0.astype(f32))
    flat = lambda t: t.reshape(M, RWKV_WIDTH)
    o_rwkv = rwkv_post(flat(y), flat(bonus), flat(gate), lw['rwkv_ln_w'], lw['rwkv_ln_b'])

    merged = merge_mm(o_gla, o_nsa, o_rwkv, lw['w_o_gla'], lw['w_o_nsa'], lw['w_o_rwkv'], mg)
    x2 = mm(merged, lw['w_out'], res=x2)

    xn2 = rmsnorm_pallas(x2, lw['norm2'])
    act, conv_new = ffn_gate_up(xn2, lw['ffn_gate'], lw['ffn_up'], lw['ffn_conv'], lw['ffn_conv_b'],
                                conv0.astype(f32), B, T)
    x2 = mm(act, lw['ffn_down'], res=x2)
    return x2.reshape(B, T, D_MODEL), (new_cmp, new_slc, win_new, gla_s, rwkv_s, shift_new, conv_new)


def _w_in_group(w, lo, hi):
    seg = w[:, lo:hi].astype(jnp.bfloat16)
    return jnp.pad(seg, ((0, 0), (0, _round_up(hi - lo, W_IN_TILE) - (hi - lo))))


def kernel(x_prompt, x_sample, cache_cmp_kv, cache_slc_kv, cache_win_kv, state_gla, state_rwkv, state_rwkv_shift, state_ffn_conv, page_table, norm1, w_in, gla_wa2, gla_ba, gla_norm, w_o_gla, cmp_w1k, cmp_w2k, cmp_pek, cmp_w1v, cmp_w2v, cmp_pev, w_o_nsa, rwkv_mu, rwkv_w0, rwkv_w2, rwkv_a0, rwkv_a2, rwkv_g2, rwkv_kk, rwkv_ka, rwkv_rk, rwkv_ln_w, rwkv_ln_b, w_o_rwkv, w_out, norm2, ffn_gate, ffn_conv, ffn_conv_b, ffn_up, ffn_down, norm_f):
    G, HD = NSA_KV_HEADS, HEAD_DIM
    n_db, n_pages = page_table.shape
    past_len = n_pages * PAGE_SIZE
    bp = x_prompt.shape[0]
    dt = x_prompt.dtype
    bf = jnp.bfloat16
    xp, xs = x_prompt, x_sample
    st_p, st_s = [], []
    for l in range(DEPTH):
        lw = {'norm1': norm1[l], 'w_gla': _w_in_group(w_in[l], 0, _C_NSA),
              'w_nsa': _w_in_group(w_in[l], _C_NSA, _C_RWKV), 'w_rwkv': _w_in_group(w_in[l], _C_RWKV, _C_MG),
              'w_mg': _w_in_group(w_in[l], _C_MG, _C_MG + IN_SIZES[14]), 'gla_wa2': gla_wa2[l], 'gla_ba': gla_ba[l],
              'gla_norm': gla_norm[l], 'w_o_gla': w_o_gla[l].astype(bf), 'cmp_w1k': cmp_w1k[l], 'cmp_w2k': cmp_w2k[l],
              'cmp_pek': cmp_pek[l], 'cmp_w1v': cmp_w1v[l], 'cmp_w2v': cmp_w2v[l], 'cmp_pev': cmp_pev[l],
              'w_o_nsa': w_o_nsa[l].astype(bf), 'rwkv_mu': rwkv_mu[l], 'rwkv_w0': rwkv_w0[l], 'rwkv_w2': rwkv_w2[l],
              'rwkv_a0': rwkv_a0[l], 'rwkv_a2': rwkv_a2[l], 'rwkv_g2': rwkv_g2[l], 'rwkv_kk': rwkv_kk[l],
              'rwkv_ka': rwkv_ka[l], 'rwkv_rk': rwkv_rk[l], 'rwkv_ln_w': rwkv_ln_w[l], 'rwkv_ln_b': rwkv_ln_b[l],
              'w_o_rwkv': w_o_rwkv[l].astype(bf), 'w_out': w_out[l].astype(bf), 'norm2': norm2[l],
              'ffn_gate': ffn_gate[l], 'ffn_conv': ffn_conv[l], 'ffn_conv_b': ffn_conv_b[l],
              'ffn_up': ffn_up[l], 'ffn_down': ffn_down[l].astype(bf)}
        xp, sp = trunk_layer(xp, 0, None,
                             jnp.zeros((bp, GLA_HEADS, GLA_DK, GLA_DV), jnp.float32),
                             jnp.zeros((bp, RWKV_HEADS, RWKV_N, RWKV_N), jnp.float32),
                             jnp.zeros((bp, RWKV_COLS), dt),
                             jnp.zeros((bp, CONV_W - 1, D_FF), dt), lw)
        paged = (cache_cmp_kv[l], cache_slc_kv[l], cache_win_kv[l], page_table)
        xs, ss = trunk_layer(xs, past_len, paged, state_gla[l], state_rwkv[l],
                             state_rwkv_shift[l], state_ffn_conv[l], lw)
        st_p.append(sp)
        st_s.append(ss)
    y_prompt = rmsnorm_pallas(xp.reshape(-1, D_MODEL), norm_f, out_dtype=dt).reshape(xp.shape)
    y_sample = rmsnorm_pallas(xs.reshape(-1, D_MODEL), norm_f, out_dtype=dt).reshape(xs.shape)
    outs = [y_prompt, y_sample]
    for i in range(7):
        outs.append(jnp.stack([s[i] for s in st_p]))
        outs.append(jnp.stack([s[i] for s in st_s]))
    return tuple(outs)
```

```python
import functools

import jax
import jax.numpy as jnp
import numpy as np
from jax import lax
from jax.experimental import pallas as pl
from jax.experimental.pallas import tpu as pltpu

D_MODEL = 4096
DEPTH = 2
PAGE_SIZE = 128
HEAD_DIM = 128
ROPE_DIM = HEAD_DIM // 4
ROPE_THETA = 500000.0
NORM_EPS = 1e-5
NEG_INF = -1e30

GLA_WIDTH = D_MODEL // 4
GLA_HEADS = 4
GLA_DV = GLA_WIDTH // GLA_HEADS
GLA_DK = GLA_DV // 2
GLA_GATE_RANK = 16
GLA_TAU = 16.0
GLA_CHUNK = 64

NSA_HEADS = D_MODEL // 256
NSA_KV_HEADS = 4
NSA_GROUP = NSA_HEADS // NSA_KV_HEADS
NSA_WIDTH = NSA_HEADS * HEAD_DIM
NSA_KV_WIDTH = NSA_KV_HEADS * HEAD_DIM
CMP_STRIDE = 16
CMP_BLOCK = 2 * CMP_STRIDE
SEL_BLOCK = 64
SEL_TOP = 16
N_INIT_BLOCKS = 1
N_LOCAL_BLOCKS = 2
WINDOW = 512
SEL_Q_BLOCK = 32
WIN_Q_BLOCK = 128
FORCE_SCORE = 1e4

RWKV_WIDTH = D_MODEL // 4
RWKV_N = 64
RWKV_HEADS = RWKV_WIDTH // RWKV_N
RWKV_DECAY_RANK = 64
RWKV_AAA_RANK = 64
RWKV_GATE_RANK = 160
RWKV_SIZES = (RWKV_WIDTH, RWKV_WIDTH, RWKV_WIDTH, RWKV_DECAY_RANK, RWKV_AAA_RANK, RWKV_GATE_RANK)
RWKV_COLS = sum(RWKV_SIZES)
RWKV_LN_EPS = 64e-5

N_BRANCH = 3
D_FF = 256 * ((8 * D_MODEL // 3 + 255) // 256)
CONV_W = 3

IN_SIZES = (GLA_HEADS * GLA_DK, GLA_HEADS * GLA_DK, GLA_WIDTH, GLA_WIDTH, GLA_GATE_RANK,
            NSA_WIDTH, NSA_KV_WIDTH, NSA_KV_WIDTH, NSA_KV_WIDTH, NSA_KV_WIDTH, NSA_KV_WIDTH, NSA_KV_WIDTH,
            NSA_HEADS * 3,
            RWKV_COLS,
            N_BRANCH * D_MODEL)

LANE = 128
SUBLANE = 8
NSA_TQ = 256
NSA_TK = 512
GLA_TC = 256
DEC_PAGES = 16
RWKV_NB = 2
VMEM_LIMIT = 48 * 1024 * 1024


def _round_up(n, m):
    return -(-n // m) * m


W_IN_TILE = 512
_C_NSA = sum(IN_SIZES[:5])
_C_RWKV = sum(IN_SIZES[:13])
_C_MG = sum(IN_SIZES[:14])
GLA_OFF = tuple(int(o) for o in np.concatenate([[0], np.cumsum(IN_SIZES[:4])]))
NSA_GATE_OFF = NSA_WIDTH + 6 * NSA_KV_WIDTH


def _pick(n, cands):
    for c in cands:
        if n % c == 0:
            return c
    return n


def _rmsnorm_kernel(x_ref, g_ref, o_ref):
    x = x_ref[...]
    y = x * lax.rsqrt(jnp.mean(x * x, axis=-1, keepdims=True) + NORM_EPS)
    o_ref[...] = (y * g_ref[...]).astype(o_ref.dtype)


def rmsnorm_pallas(x, g, out_dtype=jnp.bfloat16):
    M, D = x.shape
    tm = _pick(M, (256, 128, 64, 32, 16, 8))
    return pl.pallas_call(
        _rmsnorm_kernel,
        grid=(M // tm,),
        in_specs=[pl.BlockSpec((tm, D), lambda i: (i, 0)), pl.BlockSpec((1, D), lambda i: (0, 0))],
        out_specs=pl.BlockSpec((tm, D), lambda i: (i, 0)),
        out_shape=jax.ShapeDtypeStruct((M, D), out_dtype),
        compiler_params=pltpu.CompilerParams(dimension_semantics=("parallel",), vmem_limit_bytes=VMEM_LIMIT),
        name="rmsnorm",
    )(x, g.reshape(1, D))


def _mm_kernel(*refs, nk, has_res):
    x_ref, w_ref = refs[:2]
    res_ref = refs[2] if has_res else None
    o_ref, acc_ref = refs[-2:]
    k = pl.program_id(2)
    part = jnp.dot(x_ref[...], w_ref[...], preferred_element_type=jnp.float32)

    def finish(v):
        if has_res:
            v = v + res_ref[...]
        o_ref[...] = v.astype(o_ref.dtype)

    if nk == 1:
        finish(part)
    else:
        @pl.when(k == 0)
        def _():
            acc_ref[...] = part

        @pl.when(jnp.logical_and(k > 0, k < nk - 1))
        def _():
            acc_ref[...] += part

        @pl.when(k == nk - 1)
        def _():
            finish(acc_ref[...] + part)


def mm(x, w, res=None, out_dtype=jnp.float32):
    M, K = x.shape
    N = w.shape[1]
    tm = _pick(M, (1024, 512, 256, 128, 64, 32, 16, 8))
    tn = _pick(N, (512, 256, 128))
    tk = K if K <= 4096 else _pick(K, (5504, 4096, 2048, 1024, 512))
    nk = K // tk
    in_specs = [pl.BlockSpec((tm, tk), lambda i, j, k: (i, k)),
                pl.BlockSpec((tk, tn), lambda i, j, k: (k, j))]
    args = [x, w]
    if res is not None:
        in_specs.append(pl.BlockSpec((tm, tn), lambda i, j, k: (i, j)))
        args.append(res)
    return pl.pallas_call(
        functools.partial(_mm_kernel, nk=nk, has_res=res is not None),
        grid=(M // tm, N // tn, nk),
        in_specs=in_specs,
        out_specs=pl.BlockSpec((tm, tn), lambda i, j, k: (i, j)),
        out_shape=jax.ShapeDtypeStruct((M, N), out_dtype),
        scratch_shapes=[pltpu.VMEM((tm, tn) if nk > 1 else (SUBLANE, LANE), jnp.float32)],
        compiler_params=pltpu.CompilerParams(
            dimension_semantics=("parallel", "parallel", "arbitrary"),
            vmem_limit_bytes=VMEM_LIMIT),
        name="dense_mm",
    )(*args)


def _merge_kernel(oa_ref, ob_ref, oc_ref, wa_ref, wb_ref, wc_ref, ga_ref, gb_ref, gc_ref, o_ref):
    f32 = jnp.float32
    acc = jax.nn.sigmoid(ga_ref[...]) * jnp.dot(oa_ref[...], wa_ref[...], preferred_element_type=f32)
    acc += jax.nn.sigmoid(gb_ref[...]) * jnp.dot(ob_ref[...], wb_ref[...], preferred_element_type=f32)
    acc += jax.nn.sigmoid(gc_ref[...]) * jnp.dot(oc_ref[...], wc_ref[...], preferred_element_type=f32)
    o_ref[...] = acc.astype(o_ref.dtype)


def merge_mm(o_a, o_b, o_c, w_a, w_b, w_c, mg, out_dtype=jnp.bfloat16):
    M = o_a.shape[0]
    D = w_a.shape[1]
    tm = _pick(M, (512, 256, 128, 64, 32, 16, 8))
    tn = _pick(D, (512, 256, 128))
    nj = D // tn
    o_spec = lambda o: pl.BlockSpec((tm, o.shape[1]), lambda i, j: (i, 0))
    w_spec = lambda w: pl.BlockSpec((w.shape[0], tn), lambda i, j: (0, j))
    g_spec = lambda b: pl.BlockSpec((tm, tn), lambda i, j: (i, b * nj + j))
    return pl.pallas_call(
        _merge_kernel,
        grid=(M // tm, nj),
        in_specs=[o_spec(o_a), o_spec(o_b), o_spec(o_c), w_spec(w_a), w_spec(w_b), w_spec(w_c),
                  g_spec(0), g_spec(1), g_spec(2)],
        out_specs=pl.BlockSpec((tm, tn), lambda i, j: (i, j)),
        out_shape=jax.ShapeDtypeStruct((M, D), out_dtype),
        compiler_params=pltpu.CompilerParams(dimension_semantics=("parallel", "parallel"),
                                             vmem_limit_bytes=VMEM_LIMIT),
        name="merge_mm",
    )(o_a, o_b, o_c, w_a, w_b, w_c, mg, mg, mg)


def _ffn_gate_up_kernel(x_ref, wg_ref, wu_ref, cw_ref, cb_ref, st_ref, act_ref, tail_ref, carry_scr, *, tm, tps, T):
    f32 = jnp.float32
    i, j = pl.program_id(0), pl.program_id(1)
    x = x_ref[...]
    h = jnp.dot(x, wg_ref[...].astype(x.dtype), preferred_element_type=f32)
    u = jnp.dot(x, wu_ref[...].astype(x.dtype), preferred_element_type=f32)
    cw = cw_ref[...]
    if T == 1:
        prev2, prev1 = st_ref[0], st_ref[1]
        tail_ref[0] = prev1
        tail_ref[1] = h
    else:
        tail = jnp.where(i % tps == 0, st_ref[...], carry_scr[j])
        row = lax.broadcasted_iota(jnp.int32, h.shape, 0)
        prev1 = jnp.where(row == 0, tail[7:8], pltpu.roll(h, 1, axis=0))
        prev2 = jnp.where(row == 0, tail[6:7], jnp.where(row == 1, tail[7:8], pltpu.roll(h, 2, axis=0)))
        last = h[tm - SUBLANE:tm]
        carry_scr[j] = last
        tail_ref[...] = last
    hc = cb_ref[...] + prev2 * cw[0:1] + prev1 * cw[1:2] + h * cw[2:3]
    act_ref[...] = (jax.nn.silu(hc) * u).astype(act_ref.dtype)


def ffn_gate_up(xn, w_gate, w_up, conv_w, conv_b, conv0, B, T):
    M, D = xn.shape
    F = w_gate.shape[1]
    tn = _pick(F, (512, 256, 128))
    nj = F // tn
    cw = jnp.pad(conv_w, ((0, SUBLANE - CONV_W), (0, 0)))
    cb = conv_b.reshape(1, F)
    if T == 1:
        tm, tps = M, 1
        st = conv0.transpose(1, 0, 2)
        st_spec = pl.BlockSpec((2, B, tn), lambda i, j: (0, 0, j))
        tail_shape, tail_spec = (2, B, F), pl.BlockSpec((2, B, tn), lambda i, j: (0, 0, j))
    else:
        tm = _pick(T, (1024, 512, 256, 128, 64, 32, 16, 8))
        tps = T // tm
        st = jnp.pad(conv0, ((0, 0), (SUBLANE - 2, 0), (0, 0)))
        st_spec = pl.BlockSpec((None, SUBLANE, tn), lambda i, j: (i // tps, 0, j))
        tail_shape, tail_spec = (B, SUBLANE, F), pl.BlockSpec((None, SUBLANE, tn), lambda i, j: (i // tps, 0, j))
    act, tail = pl.pallas_call(
        functools.partial(_ffn_gate_up_kernel, tm=tm, tps=tps, T=T),
        grid=(M // tm, nj),
        in_specs=[pl.BlockSpec((tm, D), lambda i, j: (i, 0)),
                  pl.BlockSpec((D, tn), lambda i, j: (0, j)),
                  pl.BlockSpec((D, tn), lambda i, j: (0, j)),
                  pl.BlockSpec((SUBLANE, tn), lambda i, j: (0, j)),
                  pl.BlockSpec((1, tn), lambda i, j: (0, j)),
                  st_spec],
        out_specs=[pl.BlockSpec((tm, tn), lambda i, j: (i, j)), tail_spec],
        out_shape=[jax.ShapeDtypeStruct((M, F), jnp.bfloat16), jax.ShapeDtypeStruct(tail_shape, jnp.float32)],
        scratch_shapes=[pltpu.VMEM((nj, SUBLANE, tn), jnp.float32)],
        compiler_params=pltpu.CompilerParams(dimension_semantics=("arbitrary", "arbitrary"),
                                             vmem_limit_bytes=VMEM_LIMIT),
        name="ffn_gate_up",
    )(xn, w_gate, w_up, cw, cb, st)
    conv_new = tail.transpose(1, 0, 2) if T == 1 else tail[:, SUBLANE - 2:]
    return act, conv_new


def _gla_kernel(q_ref, k_ref, v_ref, og_ref, lo_ref, wa_ref, ba_ref, gn_ref, s0_ref, o_ref, sout_ref, s_scr, *, Tc, C):
    f32, bf16 = jnp.float32, jnp.bfloat16
    c = pl.program_id(2)

    @pl.when(c == 0)
    def _():
        s_scr[...] = s0_ref[...]

    row = lax.broadcasted_iota(jnp.int32, (C, GLA_DK), 0)
    tril = lax.broadcasted_iota(jnp.int32, (C, C), 0) >= lax.broadcasted_iota(jnp.int32, (C, C), 1)
    wa = wa_ref[...]
    ba = ba_ref[...]
    gn = gn_ref[...]
    S = s_scr[...]
    for n in range(Tc // C):
        rows = slice(n * C, (n + 1) * C)
        la = jax.nn.log_sigmoid(jnp.dot(lo_ref[rows, :].astype(bf16), wa, preferred_element_type=f32) + ba) / GLA_TAU
        bc = la
        d = 1
        while d < C:
            bc = bc + jnp.where(row >= d, pltpu.roll(bc, d, axis=0), 0.0)
            d *= 2
        b_mid = bc[C // 2:C // 2 + 1]
        b_last = bc[C - 1:C]
        q = q_ref[rows, :] * GLA_DK ** -0.5
        k = k_ref[rows, :]
        v = v_ref[rows, :].astype(bf16)
        att = lax.dot_general((q * jnp.exp(bc - b_mid)).astype(bf16), (k * jnp.exp(b_mid - bc)).astype(bf16),
                              (((1,), (1,)), ((), ())), preferred_element_type=f32)
        att = jnp.where(tril, att, 0.0)
        o = jnp.dot(att.astype(bf16), v, preferred_element_type=f32)
        o = o + jnp.dot((q * jnp.exp(bc)).astype(bf16), S.astype(bf16), preferred_element_type=f32)
        kd = jnp.concatenate([k * jnp.exp(b_last - bc), jnp.broadcast_to(jnp.exp(b_last), (SUBLANE, GLA_DK))], axis=0)
        kdt = kd.T
        S = kdt[:, C:C + 1] * S + jnp.dot(kdt[:, :C].astype(bf16), v, preferred_element_type=f32)
        o = o * lax.rsqrt(jnp.mean(o * o, axis=-1, keepdims=True) + NORM_EPS) * gn
        o_ref[rows, :] = (o * jax.nn.silu(og_ref[rows, :])).astype(o_ref.dtype)
    s_scr[...] = S

    @pl.when(c == pl.num_programs(2) - 1)
    def _():
        sout_ref[...] = S


def gla_pallas(hmix, wa2, ba, gnorm, s0, seg_off):
    B, T, _ = hmix.shape
    H = GLA_HEADS
    Tc = min(GLA_TC, T)
    C = GLA_CHUNK
    oq, ok, ov, og, ol = seg_off
    wa = jnp.pad(wa2, ((0, LANE - wa2.shape[0]), (0, 0))).astype(jnp.bfloat16)
    col = lambda off, w: (lambda b, h, c: (b, c, off // w + h))
    o, s = pl.pallas_call(
        functools.partial(_gla_kernel, Tc=Tc, C=C),
        grid=(B, H, T // Tc),
        in_specs=[pl.BlockSpec((None, Tc, GLA_DK), col(oq, GLA_DK)),
                  pl.BlockSpec((None, Tc, GLA_DK), col(ok, GLA_DK)),
                  pl.BlockSpec((None, Tc, GLA_DV), col(ov, GLA_DV)),
                  pl.BlockSpec((None, Tc, GLA_DV), col(og, GLA_DV)),
                  pl.BlockSpec((None, Tc, LANE), lambda b, h, c: (b, c, ol // LANE)),
                  pl.BlockSpec((LANE, GLA_DK), lambda b, h, c: (0, h)),
                  pl.BlockSpec((1, GLA_DK), lambda b, h, c: (0, h)),
                  pl.BlockSpec((1, GLA_DV), lambda b, h, c: (0, 0)),
                  pl.BlockSpec((None, None, GLA_DK, GLA_DV), lambda b, h, c: (b, h, 0, 0))],
        out_specs=[pl.BlockSpec((None, Tc, GLA_DV), lambda b, h, c: (b, c, h)),
                   pl.BlockSpec((None, None, GLA_DK, GLA_DV), lambda b, h, c: (b, h, 0, 0))],
        out_shape=[jax.ShapeDtypeStruct((B, T, H * GLA_DV), jnp.bfloat16),
                   jax.ShapeDtypeStruct((B, H, GLA_DK, GLA_DV), jnp.float32)],
        scratch_shapes=[pltpu.VMEM((GLA_DK, GLA_DV), jnp.float32)],
        compiler_params=pltpu.CompilerParams(dimension_semantics=("parallel", "parallel", "arbitrary"),
                                             vmem_limit_bytes=VMEM_LIMIT),
        name="gla_chunked",
    )(hmix, hmix, hmix, hmix, hmix, wa, ba.reshape(1, -1), gnorm.reshape(1, -1), s0)
    return o, s


def _head_sums(x, bd):
    f32, bf16 = jnp.float32, jnp.bfloat16
    outs = []
    for t in range(x.shape[1] // LANE):
        p = x[:, t * LANE:(t + 1) * LANE]
        hi = p.astype(bf16)
        lo = (p - hi.astype(f32)).astype(bf16)
        outs.append(jnp.dot(hi, bd, preferred_element_type=f32) + jnp.dot(lo, bd, preferred_element_type=f32))
    return jnp.concatenate(outs, axis=1)


def _block_diag_ones():
    rr = lax.broadcasted_iota(jnp.int32, (LANE, LANE), 0) // RWKV_N
    cc = lax.broadcasted_iota(jnp.int32, (LANE, LANE), 1) // RWKV_N
    return jnp.where(rr == cc, 1.0, 0.0).astype(jnp.bfloat16)


def _rwkv_prep_kernel(x_ref, sh_ref, mu_ref, w0_ref, a0_ref, kkw_ref, ka_ref, rk_ref, w2_ref, a2_ref, g2_ref,
                      r_ref, wl_ref, k_ref, v_ref, kk_ref, a_ref, gate_ref, bonus_ref, tail_ref, carry_scr, *, tm, T):
    f32, bf16 = jnp.float32, jnp.bfloat16
    W = RWKV_WIDTH
    i = pl.program_id(1)
    x = x_ref[...]
    if T == 1:
        prev = sh_ref[...]
        tail_ref[...] = x
    else:
        first = jnp.where(i == 0, sh_ref[SUBLANE - 1:SUBLANE], carry_scr[SUBLANE - 1:SUBLANE])
        row = lax.broadcasted_iota(jnp.int32, x.shape, 0)
        prev = jnp.where(row == 0, first, pltpu.roll(x, 1, axis=0))
        last = x[tm - SUBLANE:tm]
        carry_scr[...] = last
        tail_ref[...] = last
    rm = x + (prev - x) * mu_ref[...]
    r, k, v = rm[:, :W], rm[:, W:2 * W], rm[:, 2 * W:3 * W]
    lo = rm[:, 3 * W:3 * W + LANE]
    glo = rm[:, 3 * W + LANE:]
    w_raw = w0_ref[...] + jnp.dot(jnp.tanh(lo).astype(bf16), w2_ref[...], preferred_element_type=f32)
    wl_ref[...] = -jnp.exp(-jax.nn.softplus(-w_raw) - 0.5)
    a = jax.nn.sigmoid(a0_ref[...] + jnp.dot(lo.astype(bf16), a2_ref[...], preferred_element_type=f32))
    gate_ref[...] = jnp.dot(jax.nn.sigmoid(glo).astype(bf16), g2_ref[...], preferred_element_type=f32)
    bd = _block_diag_ones()
    kk = k * kkw_ref[...]
    kk_ref[...] = kk * lax.rsqrt(jnp.maximum(_head_sums(kk * kk, bd), 1e-24))
    k2 = k * (1.0 + (a - 1.0) * ka_ref[...])
    bonus_ref[...] = _head_sums(r * k2 * rk_ref[...], bd) * v
    r_ref[...] = r
    k_ref[...] = k2
    v_ref[...] = v
    a_ref[...] = a


def rwkv_prep(hr, shift0, mu, w0, w2, a0, a2, g2, kkw, ka, rk):
    B, T, WP = hr.shape
    W = RWKV_WIDTH
    bf16 = jnp.bfloat16
    padc = lambda t: jnp.pad(t, ((0, 0), (0, WP - t.shape[1])))
    w2p = jnp.pad(w2, ((0, LANE - RWKV_DECAY_RANK), (0, 0))).astype(bf16)
    a2p = jnp.pad(a2, ((RWKV_DECAY_RANK, 0), (0, 0))).astype(bf16)
    gpad = WP - 3 * W - LANE
    g2p = jnp.pad(g2, ((0, gpad - RWKV_GATE_RANK), (0, 0))).astype(bf16)
    row = lambda t: t.reshape(1, -1)
    if T == 1:
        tm = 1
        sh = padc(shift0).reshape(B, 1, WP)
        sh_spec = pl.BlockSpec((None, 1, WP), lambda b, i: (b, 0, 0))
        tail_rows = 1
    else:
        tm = _pick(T, (256, 128, 64, 32, 16, 8))
        sh = jnp.broadcast_to(padc(shift0)[:, None, :], (B, SUBLANE, WP))
        sh_spec = pl.BlockSpec((None, SUBLANE, WP), lambda b, i: (b, 0, 0))
        tail_rows = SUBLANE
    vec = lambda n: pl.BlockSpec((1, n), lambda b, i: (0, 0))
    mat = lambda m: pl.BlockSpec(m.shape, lambda b, i: (0, 0))
    o_spec = pl.BlockSpec((None, tm, W), lambda b, i: (b, i, 0))
    o_shape = jax.ShapeDtypeStruct((B, T, W), jnp.float32)
    outs = pl.pallas_call(
        functools.partial(_rwkv_prep_kernel, tm=tm, T=T),
        grid=(B, T // tm),
        in_specs=[pl.BlockSpec((None, tm, WP), lambda b, i: (b, i, 0)), sh_spec, vec(WP),
                  vec(W), vec(W), vec(W), vec(W), vec(W), mat(w2p), mat(a2p), mat(g2p)],
        out_specs=[o_spec] * 8 + [pl.BlockSpec((None, tail_rows, WP), lambda b, i: (b, 0, 0))],
        out_shape=[o_shape] * 8 + [jax.ShapeDtypeStruct((B, tail_rows, WP), jnp.float32)],
        scratch_shapes=[pltpu.VMEM((SUBLANE, WP), jnp.float32)],
        compiler_params=pltpu.CompilerParams(dimension_semantics=("parallel", "arbitrary"),
                                             vmem_limit_bytes=VMEM_LIMIT),
        name="rwkv_prep",
    )(hr, sh, row(padc(mu.reshape(1, -1))), row(w0), row(a0), row(kkw), row(ka), row(rk), w2p, a2p, g2p)
    return outs[:8], outs[8][:, tail_rows - 1, :RWKV_COLS]


def _rwkv_post_kernel(y_ref, bonus_ref, gate_ref, lw_ref, lb_ref, o_ref):
    bd = _block_diag_ones()
    y = y_ref[...]
    d = y - _head_sums(y, bd) * (1.0 / RWKV_N)
    var = _head_sums(d * d, bd) * (1.0 / RWKV_N)
    yn = d * lax.rsqrt(var + RWKV_LN_EPS) * lw_ref[...] + lb_ref[...]
    o_ref[...] = ((yn + bonus_ref[...]) * gate_ref[...]).astype(o_ref.dtype)


def rwkv_post(y, bonus, gate, ln_w, ln_b):
    M, W = y.shape
    tm = _pick(M, (256, 128, 64, 32, 16, 8))
    spec = pl.BlockSpec((tm, W), lambda i: (i, 0))
    vec = pl.BlockSpec((1, W), lambda i: (0, 0))
    return pl.pallas_call(
        _rwkv_post_kernel,
        grid=(M // tm,),
        in_specs=[spec, spec, spec, vec, vec],
        out_specs=spec,
        out_shape=jax.ShapeDtypeStruct((M, W), jnp.bfloat16),
        compiler_params=pltpu.CompilerParams(dimension_semantics=("parallel",), vmem_limit_bytes=VMEM_LIMIT),
        name="rwkv_post",
    )(y, bonus, gate, ln_w.reshape(1, W), ln_b.reshape(1, W))


def _rwkv_kernel(r_ref, wl_ref, k_ref, v_ref, kk_ref, a_ref, s0_ref, y_ref, sout_ref, s_scr, *, NB, NP, Tc):
    c = pl.program_id(1)
    f32, bf16 = jnp.float32, jnp.bfloat16
    U = min(SUBLANE, Tc)

    @pl.when(c == 0)
    def _():
        s_scr[...] = s0_ref[...]

    sub = lax.broadcasted_iota(jnp.int32, (RWKV_N, LANE), 0)
    lane = lax.broadcasted_iota(jnp.int32, (RWKV_N, LANE), 1)
    eye2 = (lane % RWKV_N) == sub
    left = lane < RWKV_N
    rr = lax.broadcasted_iota(jnp.int32, (LANE, LANE), 0) // RWKV_N
    cc = lax.broadcasted_iota(jnp.int32, (LANE, LANE), 1) // RWKV_N
    bd = jnp.where(rr == cc, 1.0, 0.0).astype(bf16)
    bd2 = jnp.concatenate([bd, bd], axis=0)

    def ssb(p, two_piece=True):
        hi = p.astype(bf16)
        if not two_piece:
            return jnp.dot(hi, bd, preferred_element_type=f32)
        lo = (p - hi.astype(f32)).astype(bf16)
        return jnp.dot(jnp.concatenate([hi, lo], axis=1), bd2, preferred_element_type=f32)

    eye_all = jnp.concatenate([eye2] * NP, axis=0)

    def bcast(x8, s):
        return jnp.concatenate(
            [jnp.broadcast_to(x8[s:s + 1, p * LANE:(p + 1) * LANE], (RWKV_N, LANE)) for p in range(NP)], axis=0)

    def vcols(vt, v8, s):
        if vt is None:
            return ssb(jnp.where(eye_all, bcast(v8, s), 0.0))
        return jnp.concatenate(
            [jnp.where(left, jnp.broadcast_to(vt[p][:RWKV_N, s:s + 1], (RWKV_N, LANE)),
                       jnp.broadcast_to(vt[p][RWKV_N:, s:s + 1], (RWKV_N, LANE))) for p in range(NP)], axis=0)

    def body(g, carry):
        rows = pl.ds(pl.multiple_of(g * U, U), U)
        tiles = []
        for nb in range(NB):
            kk8 = kk_ref[nb, rows, :]
            v8 = v_ref[nb, rows, :]
            vt = [v8[:, p * LANE:(p + 1) * LANE].T for p in range(NP)] if U == SUBLANE else None
            tiles.append(dict(r=r_ref[nb, rows, :], w=jnp.exp(wl_ref[nb, rows, :]), k=k_ref[nb, rows, :], v=v8, vt=vt,
                              ka=kk8 * a_ref[nb, rows, :], nk=-kk8))
        S = [s_scr[nb] for nb in range(NB)]
        ys = [[] for _ in range(NB)]
        for s in range(U):
            sa = [ssb(S[nb] * bcast(t['nk'], s)) for nb, t in enumerate(tiles)]
            for nb, t in enumerate(tiles):
                S[nb] = S[nb] * bcast(t['w'], s) + sa[nb] * bcast(t['ka'], s) + vcols(t['vt'], t['v'], s) * bcast(t['k'], s)
            yb = [jnp.where(eye_all, ssb(S[nb] * bcast(t['r'], s), two_piece=False), 0.0) for nb, t in enumerate(tiles)]
            for nb in range(NB):
                ys[nb].append(jnp.concatenate(
                    [jnp.sum(yb[nb][p * RWKV_N:(p + 1) * RWKV_N], axis=0, keepdims=True) for p in range(NP)], axis=1))
        for nb in range(NB):
            s_scr[nb] = S[nb]
            y_ref[nb, rows, :] = ys[nb][0] if U == 1 else jnp.concatenate(ys[nb], axis=0)
        return carry

    lax.fori_loop(0, Tc // U, body, 0)

    @pl.when(c == pl.num_programs(1) - 1)
    def _():
        sout_ref[...] = s_scr[...]


def rwkv_scan_pallas(r, w_log, k, v, kk, a, s0):
    B, T, W = r.shape
    H = W // RWKV_N
    NP = H // 2
    NB = RWKV_NB if B % RWKV_NB == 0 else 1
    Tc = 128 if T % 128 == 0 else T
    s0p = s0.reshape(B, NP, 2, RWKV_N, RWKV_N).transpose(0, 1, 3, 2, 4).reshape(B, NP * RWKV_N, LANE)
    blk = pl.BlockSpec((NB, Tc, W), lambda b, c: (b, c, 0))
    sblk = pl.BlockSpec((NB, NP * RWKV_N, LANE), lambda b, c: (b, 0, 0))
    y, sp = pl.pallas_call(
        functools.partial(_rwkv_kernel, NB=NB, NP=NP, Tc=Tc),
        grid=(B // NB, T // Tc),
        in_specs=[blk] * 6 + [sblk],
        out_specs=[blk, sblk],
        out_shape=[jax.ShapeDtypeStruct((B, T, W), jnp.float32),
                   jax.ShapeDtypeStruct((B, NP * RWKV_N, LANE), jnp.float32)],
        scratch_shapes=[pltpu.VMEM((NB, NP * RWKV_N, LANE), jnp.float32)],
        compiler_params=pltpu.CompilerParams(dimension_semantics=("parallel", "arbitrary"),
                                             vmem_limit_bytes=VMEM_LIMIT),
        name="rwkv7_scan",
    )(r, w_log, k, v, kk, a, s0p)
    s_fin = sp.reshape(B, NP, RWKV_N, 2, RWKV_N).transpose(0, 1, 3, 2, 4).reshape(B, H, RWKV_N, RWKV_N)
    return y, s_fin


def rope_tables(pos):
    half = ROPE_DIM // 2
    inv = ROPE_THETA ** (-jnp.arange(half, dtype=jnp.float32) / half)
    ang = pos.astype(jnp.float32)[:, None] * inv[None, :]
    cos, sin = jnp.cos(ang), jnp.sin(ang)
    T = pos.shape[0]
    z = jnp.zeros((T, HEAD_DIM - ROPE_DIM), jnp.float32)
    zh = jnp.zeros((T, half), jnp.float32)
    c = jnp.concatenate([cos, cos, jnp.ones_like(z)], axis=1)
    s_up = jnp.concatenate([-sin, zh, z], axis=1)
    s_dn = jnp.concatenate([zh, sin, z], axis=1)
    return c, s_up, s_dn


def _nsa_prep_kernel(q_ref, c_ref, s_ref, w_ref, tc_ref, tu_ref, td_ref, qo_ref, co_ref, so_ref, wo_ref):
    c, su, sd = tc_ref[...], tu_ref[...], td_ref[...]
    half = ROPE_DIM // 2

    def rot(x):
        return x * c + pltpu.roll(x, HEAD_DIM - half, axis=1) * su + pltpu.roll(x, half, axis=1) * sd

    for h in range(NSA_HEADS):
        cols = slice(h * HEAD_DIM, (h + 1) * HEAD_DIM)
        qo_ref[:, cols] = (rot(q_ref[:, cols]) * HEAD_DIM ** -0.5).astype(qo_ref.dtype)
    for src, dst in ((c_ref, co_ref), (s_ref, so_ref), (w_ref, wo_ref)):
        for g in range(NSA_KV_HEADS):
            cols = slice(g * HEAD_DIM, (g + 1) * HEAD_DIM)
            dst[:, cols] = rot(src[:, cols])
        dst[:, NSA_KV_WIDTH:] = src[:, NSA_KV_WIDTH:]


def nsa_prep(hn, pos):
    B, T, _ = hn.shape
    tm = _pick(T, (256, 128, 64, 32, 16, 8))
    tabs = rope_tables(pos)
    kvw = 2 * NSA_KV_WIDTH
    q_spec = pl.BlockSpec((None, tm, NSA_WIDTH), lambda b, i: (b, i, 0))
    kv_spec = lambda n: pl.BlockSpec((None, tm, kvw), lambda b, i: (b, i, NSA_WIDTH // kvw + n))
    t_spec = pl.BlockSpec((tm, HEAD_DIM), lambda b, i: (i, 0))
    o_spec = pl.BlockSpec((None, tm, kvw), lambda b, i: (b, i, 0))
    kv_shape = jax.ShapeDtypeStruct((B, T, kvw), jnp.float32)
    return pl.pallas_call(
        _nsa_prep_kernel,
        grid=(B, T // tm),
        in_specs=[q_spec, kv_spec(0), kv_spec(1), kv_spec(2), t_spec, t_spec, t_spec],
        out_specs=[q_spec, o_spec, o_spec, o_spec],
        out_shape=[jax.ShapeDtypeStruct((B, T, NSA_WIDTH), jnp.bfloat16), kv_shape, kv_shape, kv_shape],
        compiler_params=pltpu.CompilerParams(dimension_semantics=("parallel", "parallel"),
                                             vmem_limit_bytes=VMEM_LIMIT),
        name="nsa_prep",
    )(hn, hn, hn, hn, *tabs)


def _compress_kernel(x_ref, w1_ref, w2_ref, pe_ref, ko_ref, vo_ref, *, ns):
    f32, bf16 = jnp.float32, jnp.bfloat16
    G, HD = NSA_KV_HEADS, HEAD_DIM
    row_w = 2 * NSA_KV_WIDTH
    for kv, o_ref in ((0, ko_ref), (1, vo_ref)):
        pos = jnp.zeros((SUBLANE, HD), f32)
        for p in range(CMP_STRIDE):
            w = w1_ref[kv, p]
            lo = jnp.broadcast_to(pe_ref[kv, p:p + 1, :], (SUBLANE, HD)).astype(bf16)
            hi = jnp.broadcast_to(pe_ref[kv, CMP_STRIDE + p:CMP_STRIDE + p + 1, :], (SUBLANE, HD)).astype(bf16)
            pos = pos + jnp.dot(lo, w, preferred_element_type=f32)[:, :HD] + jnp.dot(hi, w, preferred_element_type=f32)[:, HD:]
        pos = pos[0:1]
        for g in range(G):
            acc = jnp.zeros((ns, 2 * HD), f32)
            for p in range(CMP_STRIDE):
                c0 = p * row_w + kv * NSA_KV_WIDTH + g * HD
                acc = acc + jnp.dot(x_ref[:, c0:c0 + HD].astype(bf16), w1_ref[kv, p], preferred_element_type=f32)
            nxt = pltpu.roll(acc[:, HD:], ns - 1, axis=0)
            hid = jax.nn.gelu(acc[:, :HD] + nxt + pos)
            o_ref[g] = jnp.dot(hid.astype(bf16), w2_ref[kv], preferred_element_type=f32).astype(o_ref.dtype)


def compress_weights(w1k, w2k, pek, w1v, w2v, pev):
    bf16 = jnp.bfloat16
    cat = lambda w1: jnp.concatenate([w1[:CMP_STRIDE], w1[CMP_STRIDE:]], axis=-1)
    return (jnp.stack([cat(w1k), cat(w1v)]).astype(bf16), jnp.stack([w2k, w2v]).astype(bf16), jnp.stack([pek, pev]))


def compress_pallas(kv_rows, w1, w2, pe):
    B, T, W = kv_rows.shape
    ns = T // CMP_STRIDE
    bf16 = jnp.bfloat16
    x = kv_rows.reshape(B, ns, CMP_STRIDE * W)
    out = jax.ShapeDtypeStruct((B, NSA_KV_HEADS, ns, HEAD_DIM), bf16)
    o_spec = pl.BlockSpec((None, NSA_KV_HEADS, ns, HEAD_DIM), lambda b: (b, 0, 0, 0))
    return pl.pallas_call(
        functools.partial(_compress_kernel, ns=ns),
        grid=(B,),
        in_specs=[pl.BlockSpec((None, ns, CMP_STRIDE * W), lambda b: (b, 0, 0)),
                  pl.BlockSpec(w1.shape, lambda b: (0, 0, 0, 0)),
                  pl.BlockSpec(w2.shape, lambda b: (0, 0, 0)),
                  pl.BlockSpec(pe.shape, lambda b: (0, 0, 0))],
        out_specs=[o_spec, o_spec],
        out_shape=[out, out],
        compiler_params=pltpu.CompilerParams(dimension_semantics=("parallel",), vmem_limit_bytes=VMEM_LIMIT),
        name="nsa_compress",
    )(x, w1, w2, pe)


def _dot_nt(a, b):
    return lax.dot_general(a, b, (((1,), (1,)), ((), ())), preferred_element_type=jnp.float32)


def _nsa_kernel(q_ref, kc_ref, vc_ref, ks_ref, vs_ref, kw_ref, vw_ref, g_ref, covt_ref, e_ref, o_ref,
                bias_scr, p4_scr, ocmp_scr, m_scr, l_scr, acc_scr, *, TQ, TK, T, NS, NCP, n_top):
    f32, bf16 = jnp.float32, jnp.bfloat16
    R = NSA_GROUP
    i = pl.program_id(2)
    nchunk = T // TK
    qpos_col = i * TQ + lax.broadcasted_iota(jnp.int32, (TQ, 1), 0)

    kc = kc_ref[...]
    vc = vc_ref[...]
    cend = lax.broadcasted_iota(jnp.int32, (1, NCP), 1) * CMP_STRIDE + (CMP_BLOCK - 1)
    valid = cend <= qpos_col
    for r in range(R):
        s = _dot_nt(q_ref[:, r * HEAD_DIM:(r + 1) * HEAD_DIM], kc)
        s = jnp.where(valid, s, NEG_INF)
        m = jnp.max(s, axis=-1, keepdims=True)
        p = jnp.where(valid, jnp.exp(s - m), 0.0)
        l = jnp.sum(p, axis=-1, keepdims=True)
        p = (p / jnp.where(l > 0.0, l, 1.0)).astype(bf16)
        p4_scr[:, r * NCP:(r + 1) * NCP] = p
        ocmp_scr[r] = jnp.dot(p, vc, preferred_element_type=f32)

    imp_t = _dot_nt(covt_ref[...], p4_scr[...])
    j = lax.broadcasted_iota(jnp.int32, (NS, TQ), 0)
    qblk = (i * TQ + lax.broadcasted_iota(jnp.int32, (NS, TQ), 1)) // SEL_BLOCK
    forced = (j < N_INIT_BLOCKS) | ((j <= qblk) & (j > qblk - N_LOCAL_BLOCKS))
    score = jnp.where(forced, FORCE_SCORE, jnp.where(j <= qblk, imp_t, NEG_INF))
    rank = jnp.zeros((NS, TQ), f32)
    for a in range(NS):
        row = score[a:a + 1, :]
        beats = (row > score) | ((row == score) & (a < j))
        rank = rank + jnp.where(beats, 1.0, 0.0)
    sel_t = jnp.where(rank < n_top, 1.0, 0.0)
    if NS < LANE:
        sel_t = jnp.concatenate([sel_t, jnp.zeros((LANE - NS, TQ), f32)], axis=0)
    sel = sel_t.T.astype(bf16)
    for c in range(nchunk):
        selexp = jnp.dot(sel, e_ref[:, c * TK:(c + 1) * TK], preferred_element_type=f32)
        kpos = c * TK + lax.broadcasted_iota(jnp.int32, (TQ, TK), 1)
        bias_scr[c] = jnp.where((selexp > 0.5) & (kpos <= qpos_col), 0.0, NEG_INF)

    hi = (i * TQ + TQ - 1) // TK + 1

    def attend(k_ref, v_ref, lo, masker):
        m_scr[...] = jnp.full(m_scr.shape, NEG_INF, f32)
        l_scr[...] = jnp.zeros(l_scr.shape, f32)
        acc_scr[...] = jnp.zeros(acc_scr.shape, f32)

        def chunk(c, carry):
            rows = pl.ds(pl.multiple_of(c * TK, TK), TK)
            k = k_ref[rows, :].astype(bf16)
            v = v_ref[rows, :].astype(bf16)
            mk = masker(c)
            heads = range(R)
            sk = [mk(_dot_nt(q_ref[:, r * HEAD_DIM:(r + 1) * HEAD_DIM], k)) for r in heads]
            m_prev = [m_scr[r] for r in heads]
            m_new = [jnp.maximum(m_prev[r], jnp.max(sk[r][0], axis=-1, keepdims=True)) for r in heads]
            alpha = [jnp.exp(m_prev[r] - m_new[r]) for r in heads]
            ps = [jnp.exp(sk[r][0] - m_new[r]) for r in heads]
            ps = [p if sk[r][1] is None else jnp.where(sk[r][1], p, 0.0) for r, p in enumerate(ps)]
            pv = [jnp.dot(ps[r].astype(bf16), v, preferred_element_type=f32) for r in heads]
            for r in heads:
                l_scr[r] = alpha[r] * l_scr[r] + jnp.sum(ps[r], axis=-1, keepdims=True)
                acc_scr[r] = alpha[r] * acc_scr[r] + pv[r]
                m_scr[r] = m_new[r]
            return carry

        lax.fori_loop(lo, hi, chunk, 0)

    def sel_masker(c):
        b = bias_scr[c]
        return lambda s: (s + b, None)

    attend(ks_ref, vs_ref, 0, sel_masker)
    g = pltpu.roll(jax.nn.sigmoid(g_ref[...]), (LANE - 3 * R * pl.program_id(1)) % LANE, axis=1)
    for r in range(R):
        ocmp_scr[r] = (g[:, 3 * r:3 * r + 1] * ocmp_scr[r]
                       + g[:, 3 * r + 1:3 * r + 2] * (acc_scr[r] / l_scr[r]))

    def win_masker(c):
        rel = (i * TQ - c * TK + lax.broadcasted_iota(jnp.int32, (TQ, TK), 0)
               - lax.broadcasted_iota(jnp.int32, (TQ, TK), 1))
        ok = (rel >= 0) & (rel < WINDOW)
        return lambda s: (jnp.where(ok, s, NEG_INF), ok)

    attend(kw_ref, vw_ref, jnp.maximum(i * TQ - (WINDOW - 1), 0) // TK, win_masker)
    for r in range(R):
        o = ocmp_scr[r] + g[:, 3 * r + 2:3 * r + 3] * (acc_scr[r] / l_scr[r])
        o_ref[:, r * HEAD_DIM:(r + 1) * HEAD_DIM] = o.astype(o_ref.dtype)


def nsa_attention_pallas(qr, kcmp, vcmp, slc, win, hn, gate_col):
    B, T, _ = qr.shape
    G, R = NSA_KV_HEADS, NSA_GROUP
    TQ = min(NSA_TQ, T)
    TK = min(NSA_TK, T)
    NS = T // SEL_BLOCK
    NC = T // CMP_STRIDE - 1
    NCP = kcmp.shape[2]
    n_top = min(SEL_TOP, NS)
    ci = np.arange(NCP)[:, None] * CMP_STRIDE
    sj = np.arange(NS)[None, :] * SEL_BLOCK
    cover = np.clip(np.minimum(ci + CMP_BLOCK, sj + SEL_BLOCK) - np.maximum(ci, sj), 0, None) / CMP_BLOCK
    cover[NC:] = 0.0
    covt = jnp.asarray(np.tile(cover.T, (1, R)), jnp.bfloat16)
    e = jnp.asarray((np.arange(T)[None, :] // SEL_BLOCK) == np.arange(LANE)[:, None], jnp.bfloat16)
    k_spec = pl.BlockSpec((None, T, HEAD_DIM), lambda b, g, i: (b, 0, g))
    v_spec = pl.BlockSpec((None, T, HEAD_DIM), lambda b, g, i: (b, 0, G + g))
    cmp_spec = pl.BlockSpec((None, None, NCP, HEAD_DIM), lambda b, g, i: (b, g, 0, 0))
    return pl.pallas_call(
        functools.partial(_nsa_kernel, TQ=TQ, TK=TK, T=T, NS=NS, NCP=NCP, n_top=n_top),
        grid=(B, G, T // TQ),
        in_specs=[pl.BlockSpec((None, TQ, R * HEAD_DIM), lambda b, g, i: (b, i, g)),
                  cmp_spec, cmp_spec, k_spec, v_spec, k_spec, v_spec,
                  pl.BlockSpec((None, TQ, LANE), lambda b, g, i: (b, i, gate_col // LANE)),
                  pl.BlockSpec((NS, R * NCP), lambda b, g, i: (0, 0)),
                  pl.BlockSpec((LANE, T), lambda b, g, i: (0, 0))],
        out_specs=pl.BlockSpec((None, TQ, R * HEAD_DIM), lambda b, g, i: (b, i, g)),
        out_shape=jax.ShapeDtypeStruct((B, T, G * R * HEAD_DIM), jnp.bfloat16),
        scratch_shapes=[pltpu.VMEM((T // TK, TQ, TK), jnp.float32),
                        pltpu.VMEM((TQ, R * NCP), jnp.bfloat16),
                        pltpu.VMEM((R, TQ, HEAD_DIM), jnp.float32),
                        pltpu.VMEM((R, TQ, 1), jnp.float32),
                        pltpu.VMEM((R, TQ, 1), jnp.float32),
                        pltpu.VMEM((R, TQ, HEAD_DIM), jnp.float32)],
        compiler_params=pltpu.CompilerParams(dimension_semantics=("parallel", "parallel", "arbitrary"),
                                             vmem_limit_bytes=VMEM_LIMIT),
        name="nsa_attention",
    )(qr, kcmp, vcmp, slc, slc, win, win, hn, covt, e)


def _dec_compress_kernel(pt_ref, *refs):
    f32, bf16 = jnp.float32, jnp.bfloat16
    pages, w1_ref, o_ref = refs[:DEC_PAGES], refs[DEC_PAGES], refs[DEC_PAGES + 1]
    HD = HEAD_DIM
    nc = 2 * NSA_KV_HEADS
    seg = pages[0].shape[0] // (nc * CMP_STRIDE)
    for c in range(nc):
        rows = lambda p: jnp.concatenate(
            [pg[pl.ds(p * nc + c, seg, stride=CMP_STRIDE * nc), :] for pg in pages], axis=0)
        acc = None
        for p in range(0, CMP_STRIDE, 2):
            lhs = jnp.concatenate([rows(p), rows(p + 1)], axis=1).astype(bf16)
            w = jnp.concatenate([w1_ref[c // NSA_KV_HEADS, p], w1_ref[c // NSA_KV_HEADS, p + 1]], axis=0)
            part = jnp.dot(lhs, w, preferred_element_type=f32)
            acc = part if acc is None else acc + part
        o_ref[:, c * 2 * HD:(c + 1) * 2 * HD] = acc


def dec_compress(cache, layer, page_table, w1):
    L, n_phys, page = cache.shape[:3]
    nc = 2 * NSA_KV_HEADS
    B, n_pages = page_table.shape
    seg = page // CMP_STRIDE
    steps = n_pages // DEC_PAGES
    c4 = cache.reshape(L, n_phys, page * nc, HEAD_DIM)
    page_spec = lambda k: pl.BlockSpec((None, None, page * nc, HEAD_DIM),
                                       lambda b, s, pt: (layer, pt[b, s * DEC_PAGES + k], 0, 0))
    ow = nc * 2 * HEAD_DIM
    return pl.pallas_call(
        _dec_compress_kernel,
        grid_spec=pltpu.PrefetchScalarGridSpec(
            num_scalar_prefetch=1,
            grid=(B, steps),
            in_specs=[page_spec(k) for k in range(DEC_PAGES)] + [pl.BlockSpec(w1.shape, lambda b, s, pt: (0, 0, 0, 0))],
            out_specs=pl.BlockSpec((None, DEC_PAGES * seg, ow), lambda b, s, pt: (b, s, 0))),
        out_shape=jax.ShapeDtypeStruct((B, n_pages * seg, ow), jnp.float32),
        compiler_params=pltpu.CompilerParams(dimension_semantics=("parallel", "arbitrary"),
                                             vmem_limit_bytes=VMEM_LIMIT),
        name="nsa_dec_compress",
    )(page_table, *([c4] * DEC_PAGES), w1)


def _dec_select_kernel(q_ref, fk_ref, fv_ref, w1_ref, w2_ref, pe_ref, covt_ref, ocmp_ref, idx_ref, *, NSEG, NS, NSP, n_top):
    f32, bf16 = jnp.float32, jnp.bfloat16
    HD, R = HEAD_DIM, NSA_GROUP
    NC = NSEG - 1

    def compressed(kv, f_ref):
        pos = jnp.zeros((SUBLANE, HD), f32)
        for p in range(CMP_STRIDE):
            w = w1_ref[kv, p]
            lo = jnp.broadcast_to(pe_ref[kv, p:p + 1, :], (SUBLANE, HD)).astype(bf16)
            hi = jnp.broadcast_to(pe_ref[kv, CMP_STRIDE + p:CMP_STRIDE + p + 1, :], (SUBLANE, HD)).astype(bf16)
            pos = pos + jnp.dot(lo, w, preferred_element_type=f32)[:, :HD] + jnp.dot(hi, w, preferred_element_type=f32)[:, HD:]
        nxt = pltpu.roll(f_ref[:, HD:], NSEG - 1, axis=0)
        hid = jax.nn.gelu(f_ref[:, :HD] + nxt + pos[0:1])
        return jnp.dot(hid.astype(bf16), w2_ref[kv], preferred_element_type=f32).astype(bf16)

    kc = compressed(0, fk_ref)
    vc = compressed(1, fv_ref)
    q = q_ref[...]
    q4 = jnp.concatenate([q[:, r * HD:(r + 1) * HD] for r in range(R)] + [jnp.zeros((SUBLANE - R, HD), bf16)], axis=0)
    s = _dot_nt(q4, kc)
    valid = lax.broadcasted_iota(jnp.int32, s.shape, 1) < NC
    s = jnp.where(valid, s, NEG_INF)
    p = jnp.where(valid, jnp.exp(s - jnp.max(s, axis=-1, keepdims=True)), 0.0)
    p = (p / jnp.sum(p, axis=-1, keepdims=True)).astype(bf16)
    ocmp_ref[...] = jnp.dot(p, vc, preferred_element_type=f32)
    head = lax.broadcasted_iota(jnp.int32, p.shape, 0) < R
    imp = _dot_nt(covt_ref[...], jnp.where(head, p, jnp.zeros_like(p)))
    imp = jnp.sum(imp, axis=1, keepdims=True)
    j_col = lax.broadcasted_iota(jnp.int32, (NSP, 1), 0)
    qblk = NS - 1
    forced = (j_col < N_INIT_BLOCKS) | ((j_col <= qblk) & (j_col > qblk - N_LOCAL_BLOCKS))
    score_col = jnp.where(forced, FORCE_SCORE, jnp.where(j_col <= qblk, imp, -3e38))
    score_cb = jnp.broadcast_to(score_col, (NSP, LANE))
    score_row = score_cb.T[0:1, :]
    ii = lax.broadcasted_iota(jnp.int32, (NSP, NSP), 0)
    jj = lax.broadcasted_iota(jnp.int32, (NSP, NSP), 1)
    beats = (score_col > score_row) | ((score_col == score_row) & (ii < jj))
    rank = jnp.sum(jnp.where(beats, 1.0, 0.0), axis=0, keepdims=True)
    t_col = lax.broadcasted_iota(jnp.int32, (n_top, NSP), 0).astype(f32)
    j_row = lax.broadcasted_iota(jnp.int32, (n_top, NSP), 1).astype(f32)
    ids = jnp.sum(jnp.where(rank == t_col, j_row, 0.0), axis=1, keepdims=True)
    idx_ref[...] = jnp.broadcast_to(ids, (n_top, LANE)).astype(jnp.int32)


def dec_select(qr, fs, w1, w2, pe, past_len):
    B = qr.shape[0]
    G, R, HD = NSA_KV_HEADS, NSA_GROUP, HEAD_DIM
    NSEG = fs.shape[1]
    NC = NSEG - 1
    NS = -(-(past_len + 1) // SEL_BLOCK)
    NSP = _round_up(NS, LANE)
    n_top = min(SEL_TOP, NS)
    ci = np.arange(NSEG)[:, None] * CMP_STRIDE
    sj = np.arange(NSP)[None, :] * SEL_BLOCK
    cover = np.clip(np.minimum(ci + CMP_BLOCK, sj + SEL_BLOCK) - np.maximum(ci, sj), 0, None) / CMP_BLOCK
    cover[NC:] = 0.0
    cover[:, NS:] = 0.0
    covt = jnp.asarray(cover.T, jnp.bfloat16)
    f_spec = lambda kv: pl.BlockSpec((None, NSEG, 2 * HD), lambda b, g: (b, 0, kv * G + g))
    full = lambda a: pl.BlockSpec(a.shape, lambda b, g: (0,) * a.ndim)
    ocmp, idx = pl.pallas_call(
        functools.partial(_dec_select_kernel, NSEG=NSEG, NS=NS, NSP=NSP, n_top=n_top),
        grid=(B, G),
        in_specs=[pl.BlockSpec((None, 1, R * HD), lambda b, g: (b, 0, g)), f_spec(0), f_spec(1),
                  full(w1), full(w2), full(pe), full(covt)],
        out_specs=[pl.BlockSpec((None, None, SUBLANE, HD), lambda b, g: (b, g, 0, 0)),
                   pl.BlockSpec((None, None, n_top, LANE), lambda b, g: (b, g, 0, 0))],
        out_shape=[jax.ShapeDtypeStruct((B, G, SUBLANE, HD), jnp.float32),
                   jax.ShapeDtypeStruct((B, G, n_top, LANE), jnp.int32)],
        compiler_params=pltpu.CompilerParams(dimension_semantics=("parallel", "parallel"),
                                             vmem_limit_bytes=VMEM_LIMIT),
        name="nsa_dec_select",
    )(qr, fs, fs, w1, w2, pe, covt)
    return ocmp, idx[:, :, :, 0]


def _dec_attend_kernel(pt_ref, idx_ref, q_ref, sb_ref, nks_ref, nvs_ref, wb_ref, nkw_ref, nvw_ref,
                       ocmp_ref, g_ref, o_ref, m_scr, l_scr, acc_scr, ow_scr, *, NS, n_top):
    f32, bf16 = jnp.float32, jnp.bfloat16
    HD, R, G = HEAD_DIM, NSA_GROUP, NSA_KV_HEADS
    b, g, t = pl.program_id(0), pl.program_id(1), pl.program_id(2)
    q = q_ref[...]

    def head_rows(ref, kv):
        return ref[pl.ds(kv * G + g, ref.shape[0] // (2 * G), stride=2 * G), :].astype(bf16)

    q4 = jnp.concatenate([q[:, r * HD:(r + 1) * HD] for r in range(R)] + [jnp.zeros((SUBLANE - R, HD), bf16)], axis=0)

    def new_score(k_row_ref):
        k_new = jnp.broadcast_to(k_row_ref[...], (SUBLANE, HD)).astype(bf16)
        return jnp.sum(q4.astype(f32) * k_new.astype(f32), axis=-1, keepdims=True)

    @pl.when(t == 0)
    def _():
        m_scr[...] = new_score(nks_ref)
        l_scr[...] = jnp.ones(l_scr.shape, f32)
        acc_scr[...] = jnp.broadcast_to(nvs_ref[...], (SUBLANE, HD)).astype(bf16).astype(f32)
        nwin = wb_ref.shape[0] // (2 * G)
        s = _dot_nt(q4, head_rows(wb_ref, 0))
        ok = lax.broadcasted_iota(jnp.int32, s.shape, 1) > nwin - WINDOW
        s = jnp.where(ok, s, NEG_INF)
        s_new = new_score(nkw_ref)
        m = jnp.maximum(jnp.max(s, axis=-1, keepdims=True), s_new)
        p = jnp.where(ok, jnp.exp(s - m), 0.0)
        p_new = jnp.exp(s_new - m)
        l = jnp.sum(p, axis=-1, keepdims=True) + p_new
        v_new = jnp.broadcast_to(nvw_ref[...], (SUBLANE, HD)).astype(bf16).astype(f32)
        ow_scr[...] = (jnp.dot(p.astype(bf16), head_rows(wb_ref, 1), preferred_element_type=f32)
                       + p_new.astype(bf16).astype(f32) * v_new) / l

    @pl.when(idx_ref[b, g, t] != NS - 1)
    def _():
        s = _dot_nt(q4, head_rows(sb_ref, 0))
        m_prev = m_scr[...]
        m_new = jnp.maximum(m_prev, jnp.max(s, axis=-1, keepdims=True))
        alpha = jnp.exp(m_prev - m_new)
        p = jnp.exp(s - m_new)
        l_scr[...] = alpha * l_scr[...] + jnp.sum(p, axis=-1, keepdims=True)
        acc_scr[...] = alpha * acc_scr[...] + jnp.dot(p.astype(bf16), head_rows(sb_ref, 1), preferred_element_type=f32)
        m_scr[...] = m_new

    @pl.when(t == n_top - 1)
    def _():
        gate = pltpu.roll(jax.nn.sigmoid(jnp.broadcast_to(g_ref[...], (SUBLANE, LANE))), (LANE - 3 * R * g) % LANE, axis=1)
        o_slc = acc_scr[...] / l_scr[...]
        rows = []
        for r in range(R):
            o = (gate[r:r + 1, 3 * r:3 * r + 1] * ocmp_ref[r:r + 1, :] + gate[r:r + 1, 3 * r + 1:3 * r + 2] * o_slc[r:r + 1, :]
                 + gate[r:r + 1, 3 * r + 2:3 * r + 3] * ow_scr[r:r + 1, :])
            rows.append(o)
        o_ref[...] = jnp.concatenate(rows, axis=1).astype(o_ref.dtype)


def dec_attend(qr, slc_cache, win_cache, layer, page_table, idx, new_slc, new_win, ocmp, hn, gate_col):
    B = qr.shape[0]
    G, R, HD = NSA_KV_HEADS, NSA_GROUP, HEAD_DIM
    L, n_phys, page = slc_cache.shape[:3]
    nc = 2 * G
    n_top = idx.shape[2]
    NS = -(-(page_table.shape[1] * page + 1) // SEL_BLOCK)
    per = page // SEL_BLOCK
    blocks = slc_cache.reshape(L, n_phys * per, SEL_BLOCK * nc, HD)
    nwin = win_cache.shape[2]
    wins = win_cache.reshape(L, B, nwin * nc, HD)

    def blk_index(b, g, t, pt, ix):
        j = jnp.minimum(ix[b, g, t], NS - 2)
        return (layer, pt[b, j // per] * per + j % per, 0, 0)

    row = lambda col: pl.BlockSpec((None, 1, HD), lambda b, g, t, pt, ix: (b, 0, col(g)))
    kcol, vcol = (lambda g: g), (lambda g: G + g)
    return pl.pallas_call(
        functools.partial(_dec_attend_kernel, NS=NS, n_top=n_top),
        grid_spec=pltpu.PrefetchScalarGridSpec(
            num_scalar_prefetch=2,
            grid=(B, G, n_top),
            in_specs=[pl.BlockSpec((None, 1, R * HD), lambda b, g, t, pt, ix: (b, 0, g)),
                      pl.BlockSpec((None, None, SEL_BLOCK * nc, HD), blk_index), row(kcol), row(vcol),
                      pl.BlockSpec((None, None, nwin * nc, HD), lambda b, g, t, pt, ix: (layer, b, 0, 0)),
                      row(kcol), row(vcol),
                      pl.BlockSpec((None, None, SUBLANE, HD), lambda b, g, t, pt, ix: (b, g, 0, 0)),
                      pl.BlockSpec((None, 1, LANE), lambda b, g, t, pt, ix: (b, 0, gate_col // LANE))],
            out_specs=pl.BlockSpec((None, 1, R * HD), lambda b, g, t, pt, ix: (b, 0, g)),
            scratch_shapes=[pltpu.VMEM((SUBLANE, 1), jnp.float32), pltpu.VMEM((SUBLANE, 1), jnp.float32),
                            pltpu.VMEM((SUBLANE, HD), jnp.float32), pltpu.VMEM((SUBLANE, HD), jnp.float32)]),
        out_shape=jax.ShapeDtypeStruct((B, 1, G * R * HD), jnp.bfloat16),
        compiler_params=pltpu.CompilerParams(dimension_semantics=("parallel", "parallel", "arbitrary"),
                                             vmem_limit_bytes=VMEM_LIMIT),
        name="nsa_dec_attend",
    )(page_table, idx, qr, blocks, new_slc, new_slc, wins, new_win, new_win, ocmp, hn)


def rmsnorm(x, g):
    xf = x.astype(jnp.float32)
    y = xf * lax.rsqrt(jnp.mean(xf * xf, axis=-1, keepdims=True) + NORM_EPS)
    return (y * g.astype(jnp.float32)).astype(x.dtype)


def gla_chunked(q, k, v, log_a, s0):
    B, T, H, DK = q.shape
    DV = v.shape[-1]
    C = GLA_CHUNK if T % GLA_CHUNK == 0 else T
    n = T // C

    def chunks(t):
        return t.astype(jnp.float32).reshape(B, n, C, H, t.shape[-1]).transpose(1, 0, 3, 2, 4)

    qc, kc, vc = chunks(q), chunks(k), chunks(v)
    bc = jnp.cumsum(chunks(log_a), axis=3)
    b_mid = bc[:, :, :, C // 2:C // 2 + 1]
    b_last = bc[:, :, :, C - 1:C]
    causal = jnp.tril(jnp.ones((C, C), dtype=bool))
    att = jnp.einsum('nbhid,nbhjd->nbhij', qc * jnp.exp(bc - b_mid), kc * jnp.exp(b_mid - bc))
    att = jnp.where(causal, att, 0.0)
    o_intra = jnp.einsum('nbhij,nbhjv->nbhiv', att, vc)
    q_dec = qc * jnp.exp(bc)
    k_dec = kc * jnp.exp(b_last - bc)
    a_last = jnp.exp(b_last[:, :, :, 0])

    def step(s, inp):
        qd, kd, vv, al = inp
        o = jnp.einsum('bhcd,bhdv->bhcv', qd, s)
        s = al[..., None] * s + jnp.einsum('bhcd,bhcv->bhdv', kd, vv)
        return s, o

    s_fin, o_inter = lax.scan(step, s0.astype(jnp.float32), (q_dec, k_dec, vc, a_last))
    o = (o_intra + o_inter).transpose(1, 0, 3, 2, 4).reshape(B, T, H, DV)
    return o, s_fin


def trunk_layer(x, pos0, paged, gla_s0, rwkv_s0, shift0, conv0, lw):
    B, T, _ = x.shape
    dt = x.dtype
    f32 = jnp.float32
    pos = pos0 + jnp.arange(T, dtype=jnp.int32)
    heads = lambda t, n: t.reshape(B, T, n, t.shape[-1] // n)

    bf = jnp.bfloat16
    M = B * T
    x2 = x.reshape(M, D_MODEL)
    xn = rmsnorm_pallas(x2, lw['norm1'])
    hg = mm(xn, lw['w_gla']).reshape(B, T, -1)
    hn = mm(xn, lw['w_nsa']).reshape(B, T, -1)
    hr = mm(xn, lw['w_rwkv']).reshape(B, T, -1)
    mg = mm(xn, lw['w_mg'])

    if T % GLA_TC == 0:
        o_gla, gla_s = gla_pallas(hg, lw['gla_wa2'], lw['gla_ba'], lw['gla_norm'], gla_s0.astype(f32), GLA_OFF)
    else:
        gq, gk, gv, gog, glo = [hg[..., o:o + s] for o, s in zip(GLA_OFF, IN_SIZES[:5])]
        log_a = jax.nn.log_sigmoid((glo @ lw['gla_wa2'] + lw['gla_ba']).astype(f32)) / GLA_TAU
        o_gla, gla_s = gla_chunked(heads(gq, GLA_HEADS) * GLA_DK ** -0.5, heads(gk, GLA_HEADS),
                                   heads(gv, GLA_HEADS), heads(log_a, GLA_HEADS), gla_s0)
        o_gla = rmsnorm(o_gla, lw['gla_norm']) * jax.nn.silu(heads(gog, GLA_HEADS).astype(f32))
    o_gla = o_gla.reshape(M, GLA_WIDTH).astype(bf)

    kv5 = lambda t: t.reshape(B, T, 2, NSA_KV_HEADS, HEAD_DIM)
    cw1, cw2, cpe = compress_weights(lw['cmp_w1k'], lw['cmp_w2k'], lw['cmp_pek'],
                                     lw['cmp_w1v'], lw['cmp_w2v'], lw['cmp_pev'])
    if paged is None:
        assert T % NSA_TQ == 0
        qr, cmp2, slc2, win2 = nsa_prep(hn, pos)
        kcmp, vcmp = compress_pallas(cmp2, cw1, cw2, cpe)
        o_nsa = nsa_attention_pallas(qr, kcmp, vcmp, slc2, win2, hn, NSA_GATE_OFF)
        new_cmp, new_slc, win_new = kv5(cmp2), kv5(slc2), kv5(win2)[:, T - min(WINDOW, T):]
    else:
        assert T == 1
        cache_cmp, cache_slc, cache_win, layer, page_table = paged
        qr, cmp2, slc2, win2 = [t.reshape(B, 1, -1) for t in
                                nsa_prep(hn.reshape(1, B, -1), jnp.full((B,), pos0, jnp.int32))]
        fs = dec_compress(cache_cmp, layer, page_table, cw1)
        ocmp, sel_ids = dec_select(qr, fs, cw1, cw2, cpe, pos0)
        o_nsa = dec_attend(qr, cache_slc, cache_win, layer, page_table, sel_ids, slc2, win2, ocmp, hn, NSA_GATE_OFF)
        new_cmp, new_slc = kv5(cmp2), kv5(slc2)
        win_all = jnp.concatenate([cache_win[layer].astype(dt), kv5(win2)], axis=1)
        win_new = win_all[:, win_all.shape[1] - min(WINDOW, win_all.shape[1]):]
    o_nsa = o_nsa.reshape(M, NSA_WIDTH)

    (r_, w_log, k2, v_, kk, a, gate, bonus), shift_new = rwkv_prep(
        hr, shift0.astype(f32), lw['rwkv_mu'], lw['rwkv_w0'], lw['rwkv_w2'], lw['rwkv_a0'], lw['rwkv_a2'],
        lw['rwkv_g2'], lw['rwkv_kk'], lw['rwkv_ka'], lw['rwkv_rk'])
    y, rwkv_s = rwkv_scan_pallas(r_, w_log, k2, v_, kk, a, rwkv_s0.astype(f32))
    flat = lambda t: t.reshape(M, RWKV_WIDTH)
    o_rwkv = rwkv_post(flat(y), flat(bonus), flat(gate), lw['rwkv_ln_w'], lw['rwkv_ln_b'])

    merged = merge_mm(o_gla, o_nsa, o_rwkv, lw['w_o_gla'], lw['w_o_nsa'], lw['w_o_rwkv'], mg)
    x2 = mm(merged, lw['w_out'], res=x2)

    xn2 = rmsnorm_pallas(x2, lw['norm2'])
    act, conv_new = ffn_gate_up(xn2, lw['ffn_gate'], lw['ffn_up'], lw['ffn_conv'], lw['ffn_conv_b'],
                                conv0.astype(f32), B, T)
    x2 = mm(act, lw['ffn_down'], res=x2)
    return x2.reshape(B, T, D_MODEL), (new_cmp, new_slc, win_new, gla_s, rwkv_s, shift_new, conv_new)


def _w_in_group(w, lo, hi):
    seg = w[:, lo:hi].astype(jnp.bfloat16)
    return jnp.pad(seg, ((0, 0), (0, _round_up(hi - lo, W_IN_TILE) - (hi - lo))))


def kernel(x_prompt, x_sample, cache_cmp_kv, cache_slc_kv, cache_win_kv, state_gla, state_rwkv, state_rwkv_shift, state_ffn_conv, page_table, norm1, w_in, gla_wa2, gla_ba, gla_norm, w_o_gla, cmp_w1k, cmp_w2k, cmp_pek, cmp_w1v, cmp_w2v, cmp_pev, w_o_nsa, rwkv_mu, rwkv_w0, rwkv_w2, rwkv_a0, rwkv_a2, rwkv_g2, rwkv_kk, rwkv_ka, rwkv_rk, rwkv_ln_w, rwkv_ln_b, w_o_rwkv, w_out, norm2, ffn_gate, ffn_conv, ffn_conv_b, ffn_up, ffn_down, norm_f):
    G, HD = NSA_KV_HEADS, HEAD_DIM
    n_db, n_pages = page_table.shape
    past_len = n_pages * PAGE_SIZE
    bp = x_prompt.shape[0]
    dt = x_prompt.dtype
    bf = jnp.bfloat16
    xp, xs = x_prompt, x_sample
    st_p, st_s = [], []
    for l in range(DEPTH):
        lw = {'norm1': norm1[l], 'w_gla': _w_in_group(w_in[l], 0, _C_NSA),
              'w_nsa': _w_in_group(w_in[l], _C_NSA, _C_RWKV), 'w_rwkv': _w_in_group(w_in[l], _C_RWKV, _C_MG),
              'w_mg': _w_in_group(w_in[l], _C_MG, _C_MG + IN_SIZES[14]), 'gla_wa2': gla_wa2[l], 'gla_ba': gla_ba[l],
              'gla_norm': gla_norm[l], 'w_o_gla': w_o_gla[l].astype(bf), 'cmp_w1k': cmp_w1k[l], 'cmp_w2k': cmp_w2k[l],
              'cmp_pek': cmp_pek[l], 'cmp_w1v': cmp_w1v[l], 'cmp_w2v': cmp_w2v[l], 'cmp_pev': cmp_pev[l],
              'w_o_nsa': w_o_nsa[l].astype(bf), 'rwkv_mu': rwkv_mu[l], 'rwkv_w0': rwkv_w0[l], 'rwkv_w2': rwkv_w2[l],
              'rwkv_a0': rwkv_a0[l], 'rwkv_a2': rwkv_a2[l], 'rwkv_g2': rwkv_g2[l], 'rwkv_kk': rwkv_kk[l],
              'rwkv_ka': rwkv_ka[l], 'rwkv_rk': rwkv_rk[l], 'rwkv_ln_w': rwkv_ln_w[l], 'rwkv_ln_b': rwkv_ln_b[l],
              'w_o_rwkv': w_o_rwkv[l].astype(bf), 'w_out': w_out[l].astype(bf), 'norm2': norm2[l],
              'ffn_gate': ffn_gate[l], 'ffn_conv': ffn_conv[l], 'ffn_conv_b': ffn_conv_b[l],
              'ffn_up': ffn_up[l], 'ffn_down': ffn_down[l].astype(bf)}
        xp, sp = trunk_layer(xp, 0, None,
                             jnp.zeros((bp, GLA_HEADS, GLA_DK, GLA_DV), jnp.float32),
                             jnp.zeros((bp, RWKV_HEADS, RWKV_N, RWKV_N), jnp.float32),
                             jnp.zeros((bp, RWKV_COLS), dt),
                             jnp.zeros((bp, CONV_W - 1, D_FF), dt), lw)
        paged = (cache_cmp_kv, cache_slc_kv, cache_win_kv, l, page_table)
        xs, ss = trunk_layer(xs, past_len, paged, state_gla[l], state_rwkv[l],
                             state_rwkv_shift[l], state_ffn_conv[l], lw)
        st_p.append(sp)
        st_s.append(ss)
    y_prompt = rmsnorm_pallas(xp.reshape(-1, D_MODEL), norm_f, out_dtype=dt).reshape(xp.shape)
    y_sample = rmsnorm_pallas(xs.reshape(-1, D_MODEL), norm_f, out_dtype=dt).reshape(xs.shape)
    outs = [y_prompt, y_sample]
    for i in range(7):
        outs.append(jnp.stack([s[i] for s in st_p]))
        outs.append(jnp.stack([s[i] for s in st_s]))
    return tuple(outs)
```

```python
import functools

import jax
import jax.numpy as jnp
import numpy as np
from jax import lax
from jax.experimental import pallas as pl
from jax.experimental.pallas import tpu as pltpu

D_MODEL = 4096
DEPTH = 2
PAGE_SIZE = 128
HEAD_DIM = 128
ROPE_DIM = HEAD_DIM // 4
ROPE_THETA = 500000.0
NORM_EPS = 1e-5
NEG_INF = -1e30

GLA_WIDTH = D_MODEL // 4
GLA_HEADS = 4
GLA_DV = GLA_WIDTH // GLA_HEADS
GLA_DK = GLA_DV // 2
GLA_GATE_RANK = 16
GLA_TAU = 16.0
GLA_CHUNK = 64

NSA_HEADS = D_MODEL // 256
NSA_KV_HEADS = 4
NSA_GROUP = NSA_HEADS // NSA_KV_HEADS
NSA_WIDTH = NSA_HEADS * HEAD_DIM
NSA_KV_WIDTH = NSA_KV_HEADS * HEAD_DIM
CMP_STRIDE = 16
CMP_BLOCK = 2 * CMP_STRIDE
SEL_BLOCK = 64
SEL_TOP = 16
N_INIT_BLOCKS = 1
N_LOCAL_BLOCKS = 2
WINDOW = 512
SEL_Q_BLOCK = 32
WIN_Q_BLOCK = 128
FORCE_SCORE = 1e4

RWKV_WIDTH = D_MODEL // 4
RWKV_N = 64
RWKV_HEADS = RWKV_WIDTH // RWKV_N
RWKV_DECAY_RANK = 64
RWKV_AAA_RANK = 64
RWKV_GATE_RANK = 160
RWKV_SIZES = (RWKV_WIDTH, RWKV_WIDTH, RWKV_WIDTH, RWKV_DECAY_RANK, RWKV_AAA_RANK, RWKV_GATE_RANK)
RWKV_COLS = sum(RWKV_SIZES)
RWKV_LN_EPS = 64e-5

N_BRANCH = 3
D_FF = 256 * ((8 * D_MODEL // 3 + 255) // 256)
CONV_W = 3

IN_SIZES = (GLA_HEADS * GLA_DK, GLA_HEADS * GLA_DK, GLA_WIDTH, GLA_WIDTH, GLA_GATE_RANK,
            NSA_WIDTH, NSA_KV_WIDTH, NSA_KV_WIDTH, NSA_KV_WIDTH, NSA_KV_WIDTH, NSA_KV_WIDTH, NSA_KV_WIDTH,
            NSA_HEADS * 3,
            RWKV_COLS,
            N_BRANCH * D_MODEL)

LANE = 128
SUBLANE = 8
NSA_TQ = 512
NSA_TK = 512
GLA_TC = 256
DEC_PAGES = 16
RWKV_NB = 2
VMEM_LIMIT = 48 * 1024 * 1024


def _round_up(n, m):
    return -(-n // m) * m


W_IN_TILE = 512
_C_NSA = sum(IN_SIZES[:5])
_C_RWKV = sum(IN_SIZES[:13])
_C_MG = sum(IN_SIZES[:14])
GLA_OFF = tuple(int(o) for o in np.concatenate([[0], np.cumsum(IN_SIZES[:4])]))
NSA_GATE_OFF = NSA_WIDTH + 6 * NSA_KV_WIDTH


def _pick(n, cands):
    for c in cands:
        if n % c == 0:
            return c
    return n


def _rmsnorm_kernel(x_ref, g_ref, o_ref):
    x = x_ref[...]
    y = x * lax.rsqrt(jnp.mean(x * x, axis=-1, keepdims=True) + NORM_EPS)
    o_ref[...] = (y * g_ref[...]).astype(o_ref.dtype)


def rmsnorm_pallas(x, g, out_dtype=jnp.bfloat16):
    M, D = x.shape
    tm = _pick(M, (256, 128, 64, 32, 16, 8))
    return pl.pallas_call(
        _rmsnorm_kernel,
        grid=(M // tm,),
        in_specs=[pl.BlockSpec((tm, D), lambda i: (i, 0)), pl.BlockSpec((1, D), lambda i: (0, 0))],
        out_specs=pl.BlockSpec((tm, D), lambda i: (i, 0)),
        out_shape=jax.ShapeDtypeStruct((M, D), out_dtype),
        compiler_params=pltpu.CompilerParams(dimension_semantics=("parallel",), vmem_limit_bytes=VMEM_LIMIT),
        name="rmsnorm",
    )(x, g.reshape(1, D))


def _mm_kernel(*refs, nk, has_res):
    x_ref, w_ref = refs[:2]
    res_ref = refs[2] if has_res else None
    o_ref, acc_ref = refs[-2:]
    k = pl.program_id(2)
    part = jnp.dot(x_ref[...], w_ref[...], preferred_element_type=jnp.float32)

    def finish(v):
        if has_res:
            v = v + res_ref[...]
        o_ref[...] = v.astype(o_ref.dtype)

    if nk == 1:
        finish(part)
    else:
        @pl.when(k == 0)
        def _():
            acc_ref[...] = part

        @pl.when(jnp.logical_and(k > 0, k < nk - 1))
        def _():
            acc_ref[...] += part

        @pl.when(k == nk - 1)
        def _():
            finish(acc_ref[...] + part)


def mm(x, w, res=None, out_dtype=jnp.float32):
    M, K = x.shape
    N = w.shape[1]
    tm = _pick(M, (1024, 512, 256, 128, 64, 32, 16, 8))
    tn = _pick(N, (512, 256, 128))
    tk = K if K <= 4096 else _pick(K, (5504, 4096, 2048, 1024, 512))
    nk = K // tk
    in_specs = [pl.BlockSpec((tm, tk), lambda i, j, k: (i, k)),
                pl.BlockSpec((tk, tn), lambda i, j, k: (k, j))]
    args = [x, w]
    if res is not None:
        in_specs.append(pl.BlockSpec((tm, tn), lambda i, j, k: (i, j)))
        args.append(res)
    return pl.pallas_call(
        functools.partial(_mm_kernel, nk=nk, has_res=res is not None),
        grid=(M // tm, N // tn, nk),
        in_specs=in_specs,
        out_specs=pl.BlockSpec((tm, tn), lambda i, j, k: (i, j)),
        out_shape=jax.ShapeDtypeStruct((M, N), out_dtype),
        scratch_shapes=[pltpu.VMEM((tm, tn) if nk > 1 else (SUBLANE, LANE), jnp.float32)],
        compiler_params=pltpu.CompilerParams(
            dimension_semantics=("parallel", "parallel", "arbitrary"),
            vmem_limit_bytes=VMEM_LIMIT),
        name="dense_mm",
    )(*args)


def _merge_kernel(oa_ref, ob_ref, oc_ref, wa_ref, wb_ref, wc_ref, ga_ref, gb_ref, gc_ref, o_ref):
    f32 = jnp.float32
    acc = jax.nn.sigmoid(ga_ref[...]) * jnp.dot(oa_ref[...], wa_ref[...], preferred_element_type=f32)
    acc += jax.nn.sigmoid(gb_ref[...]) * jnp.dot(ob_ref[...], wb_ref[...], preferred_element_type=f32)
    acc += jax.nn.sigmoid(gc_ref[...]) * jnp.dot(oc_ref[...], wc_ref[...], preferred_element_type=f32)
    o_ref[...] = acc.astype(o_ref.dtype)


def merge_mm(o_a, o_b, o_c, w_a, w_b, w_c, mg, out_dtype=jnp.bfloat16):
    M = o_a.shape[0]
    D = w_a.shape[1]
    tm = _pick(M, (512, 256, 128, 64, 32, 16, 8))
    tn = _pick(D, (512, 256, 128))
    nj = D // tn
    o_spec = lambda o: pl.BlockSpec((tm, o.shape[1]), lambda i, j: (i, 0))
    w_spec = lambda w: pl.BlockSpec((w.shape[0], tn), lambda i, j: (0, j))
    g_spec = lambda b: pl.BlockSpec((tm, tn), lambda i, j: (i, b * nj + j))
    return pl.pallas_call(
        _merge_kernel,
        grid=(M // tm, nj),
        in_specs=[o_spec(o_a), o_spec(o_b), o_spec(o_c), w_spec(w_a), w_spec(w_b), w_spec(w_c),
                  g_spec(0), g_spec(1), g_spec(2)],
        out_specs=pl.BlockSpec((tm, tn), lambda i, j: (i, j)),
        out_shape=jax.ShapeDtypeStruct((M, D), out_dtype),
        compiler_params=pltpu.CompilerParams(dimension_semantics=("parallel", "parallel"),
                                             vmem_limit_bytes=VMEM_LIMIT),
        name="merge_mm",
    )(o_a, o_b, o_c, w_a, w_b, w_c, mg, mg, mg)


def _ffn_gate_up_kernel(x_ref, wg_ref, wu_ref, cw_ref, cb_ref, st_ref, act_ref, tail_ref, carry_scr, *, tm, tps, T):
    f32 = jnp.float32
    i, j = pl.program_id(0), pl.program_id(1)
    x = x_ref[...]
    h = jnp.dot(x, wg_ref[...].astype(x.dtype), preferred_element_type=f32)
    u = jnp.dot(x, wu_ref[...].astype(x.dtype), preferred_element_type=f32)
    cw = cw_ref[...]
    if T == 1:
        prev2, prev1 = st_ref[0], st_ref[1]
        tail_ref[0] = prev1
        tail_ref[1] = h
    else:
        tail = jnp.where(i % tps == 0, st_ref[...], carry_scr[j])
        row = lax.broadcasted_iota(jnp.int32, h.shape, 0)
        prev1 = jnp.where(row == 0, tail[7:8], pltpu.roll(h, 1, axis=0))
        prev2 = jnp.where(row == 0, tail[6:7], jnp.where(row == 1, tail[7:8], pltpu.roll(h, 2, axis=0)))
        last = h[tm - SUBLANE:tm]
        carry_scr[j] = last
        tail_ref[...] = last
    hc = cb_ref[...] + prev2 * cw[0:1] + prev1 * cw[1:2] + h * cw[2:3]
    act_ref[...] = (jax.nn.silu(hc) * u).astype(act_ref.dtype)


def ffn_gate_up(xn, w_gate, w_up, conv_w, conv_b, conv0, B, T):
    M, D = xn.shape
    F = w_gate.shape[1]
    tn = _pick(F, (512, 256, 128))
    nj = F // tn
    cw = jnp.pad(conv_w, ((0, SUBLANE - CONV_W), (0, 0)))
    cb = conv_b.reshape(1, F)
    if T == 1:
        tm, tps = M, 1
        st = conv0.transpose(1, 0, 2)
        st_spec = pl.BlockSpec((2, B, tn), lambda i, j: (0, 0, j))
        tail_shape, tail_spec = (2, B, F), pl.BlockSpec((2, B, tn), lambda i, j: (0, 0, j))
    else:
        tm = _pick(T, (1024, 512, 256, 128, 64, 32, 16, 8))
        tps = T // tm
        st = jnp.pad(conv0, ((0, 0), (SUBLANE - 2, 0), (0, 0)))
        st_spec = pl.BlockSpec((None, SUBLANE, tn), lambda i, j: (i // tps, 0, j))
        tail_shape, tail_spec = (B, SUBLANE, F), pl.BlockSpec((None, SUBLANE, tn), lambda i, j: (i // tps, 0, j))
    act, tail = pl.pallas_call(
        functools.partial(_ffn_gate_up_kernel, tm=tm, tps=tps, T=T),
        grid=(M // tm, nj),
        in_specs=[pl.BlockSpec((tm, D), lambda i, j: (i, 0)),
                  pl.BlockSpec((D, tn), lambda i, j: (0, j)),
                  pl.BlockSpec((D, tn), lambda i, j: (0, j)),
                  pl.BlockSpec((SUBLANE, tn), lambda i, j: (0, j)),
                  pl.BlockSpec((1, tn), lambda i, j: (0, j)),
                  st_spec],
        out_specs=[pl.BlockSpec((tm, tn), lambda i, j: (i, j)), tail_spec],
        out_shape=[jax.ShapeDtypeStruct((M, F), jnp.bfloat16), jax.ShapeDtypeStruct(tail_shape, jnp.float32)],
        scratch_shapes=[pltpu.VMEM((nj, SUBLANE, tn), jnp.float32)],
        compiler_params=pltpu.CompilerParams(dimension_semantics=("arbitrary", "arbitrary"),
                                             vmem_limit_bytes=VMEM_LIMIT),
        name="ffn_gate_up",
    )(xn, w_gate, w_up, cw, cb, st)
    conv_new = tail.transpose(1, 0, 2) if T == 1 else tail[:, SUBLANE - 2:]
    return act, conv_new


def _gla_kernel(q_ref, k_ref, v_ref, og_ref, lo_ref, wa_ref, ba_ref, gn_ref, s0_ref, o_ref, sout_ref, s_scr, *, Tc, C, valid):
    f32, bf16 = jnp.float32, jnp.bfloat16
    c = pl.program_id(2)

    @pl.when(c == 0)
    def _():
        s_scr[...] = s0_ref[...]

    row = lax.broadcasted_iota(jnp.int32, (C, GLA_DK), 0)
    tril = lax.broadcasted_iota(jnp.int32, (C, C), 0) >= lax.broadcasted_iota(jnp.int32, (C, C), 1)
    wa = wa_ref[...]
    ba = ba_ref[...]
    gn = gn_ref[...]
    S = s_scr[...]
    for n in range(Tc // C):
        rows = slice(n * C, (n + 1) * C)
        la = jax.nn.log_sigmoid(jnp.dot(lo_ref[rows, :].astype(bf16), wa, preferred_element_type=f32) + ba) / GLA_TAU
        if valid < C:
            la = jnp.where(row < valid, la, 0.0)
        bc = la
        d = 1
        while d < C:
            bc = bc + jnp.where(row >= d, pltpu.roll(bc, d, axis=0), 0.0)
            d *= 2
        b_mid = bc[C // 2:C // 2 + 1]
        b_last = bc[C - 1:C]
        q = q_ref[rows, :] * GLA_DK ** -0.5
        k = k_ref[rows, :]
        v = v_ref[rows, :].astype(bf16)
        att = lax.dot_general((q * jnp.exp(bc - b_mid)).astype(bf16), (k * jnp.exp(b_mid - bc)).astype(bf16),
                              (((1,), (1,)), ((), ())), preferred_element_type=f32)
        att = jnp.where(tril, att, 0.0)
        o = jnp.dot(att.astype(bf16), v, preferred_element_type=f32)
        o = o + jnp.dot((q * jnp.exp(bc)).astype(bf16), S.astype(bf16), preferred_element_type=f32)
        kd = jnp.concatenate([k * jnp.exp(b_last - bc), jnp.broadcast_to(jnp.exp(b_last), (SUBLANE, GLA_DK))], axis=0)
        kdt = kd.T
        S = kdt[:, C:C + 1] * S + jnp.dot(kdt[:, :C].astype(bf16), v, preferred_element_type=f32)
        o = o * lax.rsqrt(jnp.mean(o * o, axis=-1, keepdims=True) + NORM_EPS) * gn
        o_ref[rows, :] = (o * jax.nn.silu(og_ref[rows, :])).astype(o_ref.dtype)
    s_scr[...] = S

    @pl.when(c == pl.num_programs(2) - 1)
    def _():
        sout_ref[...] = S


def gla_pallas(hmix, wa2, ba, gnorm, s0, seg_off):
    B, T, _ = hmix.shape
    H = GLA_HEADS
    n_tok = T
    if T % GLA_CHUNK == 0:
        Tc, C = min(GLA_TC, T), GLA_CHUNK
    else:
        assert T < SUBLANE
        Tc = C = SUBLANE
        hmix = jnp.pad(hmix, ((0, 0), (0, SUBLANE - T), (0, 0)))
        T = SUBLANE
    oq, ok, ov, og, ol = seg_off
    wa = jnp.pad(wa2, ((0, LANE - wa2.shape[0]), (0, 0))).astype(jnp.bfloat16)
    col = lambda off, w: (lambda b, h, c: (b, c, off // w + h))
    o, s = pl.pallas_call(
        functools.partial(_gla_kernel, Tc=Tc, C=C, valid=min(n_tok, C)),
        grid=(B, H, T // Tc),
        in_specs=[pl.BlockSpec((None, Tc, GLA_DK), col(oq, GLA_DK)),
                  pl.BlockSpec((None, Tc, GLA_DK), col(ok, GLA_DK)),
                  pl.BlockSpec((None, Tc, GLA_DV), col(ov, GLA_DV)),
                  pl.BlockSpec((None, Tc, GLA_DV), col(og, GLA_DV)),
                  pl.BlockSpec((None, Tc, LANE), lambda b, h, c: (b, c, ol // LANE)),
                  pl.BlockSpec((LANE, GLA_DK), lambda b, h, c: (0, h)),
                  pl.BlockSpec((1, GLA_DK), lambda b, h, c: (0, h)),
                  pl.BlockSpec((1, GLA_DV), lambda b, h, c: (0, 0)),
                  pl.BlockSpec((None, None, GLA_DK, GLA_DV), lambda b, h, c: (b, h, 0, 0))],
        out_specs=[pl.BlockSpec((None, Tc, GLA_DV), lambda b, h, c: (b, c, h)),
                   pl.BlockSpec((None, None, GLA_DK, GLA_DV), lambda b, h, c: (b, h, 0, 0))],
        out_shape=[jax.ShapeDtypeStruct((B, T, H * GLA_DV), jnp.bfloat16),
                   jax.ShapeDtypeStruct((B, H, GLA_DK, GLA_DV), jnp.float32)],
        scratch_shapes=[pltpu.VMEM((GLA_DK, GLA_DV), jnp.float32)],
        compiler_params=pltpu.CompilerParams(dimension_semantics=("parallel", "parallel", "arbitrary"),
                                             vmem_limit_bytes=VMEM_LIMIT),
        name="gla_chunked",
    )(hmix, hmix, hmix, hmix, hmix, wa, ba.reshape(1, -1), gnorm.reshape(1, -1), s0)
    return o[:, :n_tok], s


def _head_sums(x, bd):
    f32, bf16 = jnp.float32, jnp.bfloat16
    outs = []
    for t in range(x.shape[1] // LANE):
        p = x[:, t * LANE:(t + 1) * LANE]
        hi = p.astype(bf16)
        lo = (p - hi.astype(f32)).astype(bf16)
        outs.append(jnp.dot(hi, bd, preferred_element_type=f32) + jnp.dot(lo, bd, preferred_element_type=f32))
    return jnp.concatenate(outs, axis=1)


def _block_diag_ones():
    rr = lax.broadcasted_iota(jnp.int32, (LANE, LANE), 0) // RWKV_N
    cc = lax.broadcasted_iota(jnp.int32, (LANE, LANE), 1) // RWKV_N
    return jnp.where(rr == cc, 1.0, 0.0).astype(jnp.bfloat16)


def _rwkv_prep_kernel(x_ref, sh_ref, mu_ref, w0_ref, a0_ref, kkw_ref, ka_ref, rk_ref, w2_ref, a2_ref, g2_ref,
                      r_ref, wl_ref, k_ref, v_ref, kk_ref, a_ref, gate_ref, bonus_ref, tail_ref, carry_scr, *, tm, T):
    f32, bf16 = jnp.float32, jnp.bfloat16
    W = RWKV_WIDTH
    i = pl.program_id(1)
    x = x_ref[...]
    if T == 1:
        prev = sh_ref[...]
        tail_ref[...] = x
    else:
        first = jnp.where(i == 0, sh_ref[SUBLANE - 1:SUBLANE], carry_scr[SUBLANE - 1:SUBLANE])
        row = lax.broadcasted_iota(jnp.int32, x.shape, 0)
        prev = jnp.where(row == 0, first, pltpu.roll(x, 1, axis=0))
        last = x[tm - SUBLANE:tm]
        carry_scr[...] = last
        tail_ref[...] = last
    rm = x + (prev - x) * mu_ref[...]
    r, k, v = rm[:, :W], rm[:, W:2 * W], rm[:, 2 * W:3 * W]
    lo = rm[:, 3 * W:3 * W + LANE]
    glo = rm[:, 3 * W + LANE:]
    w_raw = w0_ref[...] + jnp.dot(jnp.tanh(lo).astype(bf16), w2_ref[...], preferred_element_type=f32)
    wl_ref[...] = -jnp.exp(-jax.nn.softplus(-w_raw) - 0.5)
    a = jax.nn.sigmoid(a0_ref[...] + jnp.dot(lo.astype(bf16), a2_ref[...], preferred_element_type=f32))
    gate_ref[...] = jnp.dot(jax.nn.sigmoid(glo).astype(bf16), g2_ref[...], preferred_element_type=f32)
    bd = _block_diag_ones()
    kk = k * kkw_ref[...]
    kk_ref[...] = kk * lax.rsqrt(jnp.maximum(_head_sums(kk * kk, bd), 1e-24))
    k2 = k * (1.0 + (a - 1.0) * ka_ref[...])
    bonus_ref[...] = _head_sums(r * k2 * rk_ref[...], bd) * v
    r_ref[...] = r
    k_ref[...] = k2
    v_ref[...] = v
    a_ref[...] = a


def rwkv_prep(hr, shift0, mu, w0, w2, a0, a2, g2, kkw, ka, rk):
    B, T, WP = hr.shape
    W = RWKV_WIDTH
    bf16 = jnp.bfloat16
    padc = lambda t: jnp.pad(t, ((0, 0), (0, WP - t.shape[1])))
    w2p = jnp.pad(w2, ((0, LANE - RWKV_DECAY_RANK), (0, 0))).astype(bf16)
    a2p = jnp.pad(a2, ((RWKV_DECAY_RANK, 0), (0, 0))).astype(bf16)
    gpad = WP - 3 * W - LANE
    g2p = jnp.pad(g2, ((0, gpad - RWKV_GATE_RANK), (0, 0))).astype(bf16)
    row = lambda t: t.reshape(1, -1)
    if T == 1:
        tm = 1
        sh = padc(shift0).reshape(B, 1, WP)
        sh_spec = pl.BlockSpec((None, 1, WP), lambda b, i: (b, 0, 0))
        tail_rows = 1
    else:
        tm = _pick(T, (256, 128, 64, 32, 16, 8))
        sh = jnp.broadcast_to(padc(shift0)[:, None, :], (B, SUBLANE, WP))
        sh_spec = pl.BlockSpec((None, SUBLANE, WP), lambda b, i: (b, 0, 0))
        tail_rows = SUBLANE
    vec = lambda n: pl.BlockSpec((1, n), lambda b, i: (0, 0))
    mat = lambda m: pl.BlockSpec(m.shape, lambda b, i: (0, 0))
    o_spec = pl.BlockSpec((None, tm, W), lambda b, i: (b, i, 0))
    o_shape = jax.ShapeDtypeStruct((B, T, W), jnp.float32)
    outs = pl.pallas_call(
        functools.partial(_rwkv_prep_kernel, tm=tm, T=T),
        grid=(B, T // tm),
        in_specs=[pl.BlockSpec((None, tm, WP), lambda b, i: (b, i, 0)), sh_spec, vec(WP),
                  vec(W), vec(W), vec(W), vec(W), vec(W), mat(w2p), mat(a2p), mat(g2p)],
        out_specs=[o_spec] * 8 + [pl.BlockSpec((None, tail_rows, WP), lambda b, i: (b, 0, 0))],
        out_shape=[o_shape] * 8 + [jax.ShapeDtypeStruct((B, tail_rows, WP), jnp.float32)],
        scratch_shapes=[pltpu.VMEM((SUBLANE, WP), jnp.float32)],
        compiler_params=pltpu.CompilerParams(dimension_semantics=("parallel", "arbitrary"),
                                             vmem_limit_bytes=VMEM_LIMIT),
        name="rwkv_prep",
    )(hr, sh, row(padc(mu.reshape(1, -1))), row(w0), row(a0), row(kkw), row(ka), row(rk), w2p, a2p, g2p)
    return outs[:8], outs[8][:, tail_rows - 1, :RWKV_COLS]


def _rwkv_post_kernel(y_ref, bonus_ref, gate_ref, lw_ref, lb_ref, o_ref):
    bd = _block_diag_ones()
    y = y_ref[...]
    d = y - _head_sums(y, bd) * (1.0 / RWKV_N)
    var = _head_sums(d * d, bd) * (1.0 / RWKV_N)
    yn = d * lax.rsqrt(var + RWKV_LN_EPS) * lw_ref[...] + lb_ref[...]
    o_ref[...] = ((yn + bonus_ref[...]) * gate_ref[...]).astype(o_ref.dtype)


def rwkv_post(y, bonus, gate, ln_w, ln_b):
    M, W = y.shape
    tm = _pick(M, (256, 128, 64, 32, 16, 8))
    spec = pl.BlockSpec((tm, W), lambda i: (i, 0))
    vec = pl.BlockSpec((1, W), lambda i: (0, 0))
    return pl.pallas_call(
        _rwkv_post_kernel,
        grid=(M // tm,),
        in_specs=[spec, spec, spec, vec, vec],
        out_specs=spec,
        out_shape=jax.ShapeDtypeStruct((M, W), jnp.bfloat16),
        compiler_params=pltpu.CompilerParams(dimension_semantics=("parallel",), vmem_limit_bytes=VMEM_LIMIT),
        name="rwkv_post",
    )(y, bonus, gate, ln_w.reshape(1, W), ln_b.reshape(1, W))


def _rwkv_kernel(r_ref, wl_ref, k_ref, v_ref, kk_ref, a_ref, s0_ref, y_ref, sout_ref, s_scr, *, NB, NP, Tc):
    c = pl.program_id(1)
    f32, bf16 = jnp.float32, jnp.bfloat16
    U = min(SUBLANE, Tc)

    @pl.when(c == 0)
    def _():
        s_scr[...] = s0_ref[...]

    sub = lax.broadcasted_iota(jnp.int32, (RWKV_N, LANE), 0)
    lane = lax.broadcasted_iota(jnp.int32, (RWKV_N, LANE), 1)
    eye2 = (lane % RWKV_N) == sub
    left = lane < RWKV_N
    rr = lax.broadcasted_iota(jnp.int32, (LANE, LANE), 0) // RWKV_N
    cc = lax.broadcasted_iota(jnp.int32, (LANE, LANE), 1) // RWKV_N
    bd = jnp.where(rr == cc, 1.0, 0.0).astype(bf16)
    bd2 = jnp.concatenate([bd, bd], axis=0)

    def ssb(p, two_piece=True):
        hi = p.astype(bf16)
        if not two_piece:
            return jnp.dot(hi, bd, preferred_element_type=f32)
        lo = (p - hi.astype(f32)).astype(bf16)
        return jnp.dot(jnp.concatenate([hi, lo], axis=1), bd2, preferred_element_type=f32)

    eye_all = jnp.concatenate([eye2] * NP, axis=0)

    def bcast(x8, s):
        return jnp.concatenate(
            [jnp.broadcast_to(x8[s:s + 1, p * LANE:(p + 1) * LANE], (RWKV_N, LANE)) for p in range(NP)], axis=0)

    def vcols(vt, v8, s):
        if vt is None:
            return ssb(jnp.where(eye_all, bcast(v8, s), 0.0))
        return jnp.concatenate(
            [jnp.where(left, jnp.broadcast_to(vt[p][:RWKV_N, s:s + 1], (RWKV_N, LANE)),
                       jnp.broadcast_to(vt[p][RWKV_N:, s:s + 1], (RWKV_N, LANE))) for p in range(NP)], axis=0)

    def body(g, carry):
        rows = pl.ds(pl.multiple_of(g * U, U), U)
        tiles = []
        for nb in range(NB):
            kk8 = kk_ref[nb, rows, :]
            v8 = v_ref[nb, rows, :]
            vt = [v8[:, p * LANE:(p + 1) * LANE].T for p in range(NP)] if U == SUBLANE else None
            tiles.append(dict(r=r_ref[nb, rows, :], w=jnp.exp(wl_ref[nb, rows, :]), k=k_ref[nb, rows, :], v=v8, vt=vt,
                              ka=kk8 * a_ref[nb, rows, :], nk=-kk8))
        S = [s_scr[nb] for nb in range(NB)]
        ys = [[] for _ in range(NB)]
        for s in range(U):
            sa = [ssb(S[nb] * bcast(t['nk'], s)) for nb, t in enumerate(tiles)]
            for nb, t in enumerate(tiles):
                S[nb] = S[nb] * bcast(t['w'], s) + sa[nb] * bcast(t['ka'], s) + vcols(t['vt'], t['v'], s) * bcast(t['k'], s)
            yb = [jnp.where(eye_all, ssb(S[nb] * bcast(t['r'], s), two_piece=False), 0.0) for nb, t in enumerate(tiles)]
            for nb in range(NB):
                ys[nb].append(jnp.concatenate(
                    [jnp.sum(yb[nb][p * RWKV_N:(p + 1) * RWKV_N], axis=0, keepdims=True) for p in range(NP)], axis=1))
        for nb in range(NB):
            s_scr[nb] = S[nb]
            y_ref[nb, rows, :] = ys[nb][0] if U == 1 else jnp.concatenate(ys[nb], axis=0)
        return carry

    lax.fori_loop(0, Tc // U, body, 0)

    @pl.when(c == pl.num_programs(1) - 1)
    def _():
        sout_ref[...] = s_scr[...]


def rwkv_scan_pallas(r, w_log, k, v, kk, a, s0):
    B, T, W = r.shape
    H = W // RWKV_N
    NP = H // 2
    NB = RWKV_NB if B % RWKV_NB == 0 else 1
    Tc = 128 if T % 128 == 0 else T
    s0p = s0.reshape(B, NP, 2, RWKV_N, RWKV_N).transpose(0, 1, 3, 2, 4).reshape(B, NP * RWKV_N, LANE)
    blk = pl.BlockSpec((NB, Tc, W), lambda b, c: (b, c, 0))
    sblk = pl.BlockSpec((NB, NP * RWKV_N, LANE), lambda b, c: (b, 0, 0))
    y, sp = pl.pallas_call(
        functools.partial(_rwkv_kernel, NB=NB, NP=NP, Tc=Tc),
        grid=(B // NB, T // Tc),
        in_specs=[blk] * 6 + [sblk],
        out_specs=[blk, sblk],
        out_shape=[jax.ShapeDtypeStruct((B, T, W), jnp.float32),
                   jax.ShapeDtypeStruct((B, NP * RWKV_N, LANE), jnp.float32)],
        scratch_shapes=[pltpu.VMEM((NB, NP * RWKV_N, LANE), jnp.float32)],
        compiler_params=pltpu.CompilerParams(dimension_semantics=("parallel", "arbitrary"),
                                             vmem_limit_bytes=VMEM_LIMIT),
        name="rwkv7_scan",
    )(r, w_log, k, v, kk, a, s0p)
    s_fin = sp.reshape(B, NP, RWKV_N, 2, RWKV_N).transpose(0, 1, 3, 2, 4).reshape(B, H, RWKV_N, RWKV_N)
    return y, s_fin


def rope_tables(pos):
    half = ROPE_DIM // 2
    inv = ROPE_THETA ** (-jnp.arange(half, dtype=jnp.float32) / half)
    ang = pos.astype(jnp.float32)[:, None] * inv[None, :]
    cos, sin = jnp.cos(ang), jnp.sin(ang)
    T = pos.shape[0]
    z = jnp.zeros((T, HEAD_DIM - ROPE_DIM), jnp.float32)
    zh = jnp.zeros((T, half), jnp.float32)
    c = jnp.concatenate([cos, cos, jnp.ones_like(z)], axis=1)
    s_up = jnp.concatenate([-sin, zh, z], axis=1)
    s_dn = jnp.concatenate([zh, sin, z], axis=1)
    return c, s_up, s_dn


def _nsa_prep_kernel(q_ref, c_ref, s_ref, w_ref, tc_ref, tu_ref, td_ref, qo_ref, co_ref, so_ref, wo_ref):
    c, su, sd = tc_ref[...], tu_ref[...], td_ref[...]
    half = ROPE_DIM // 2

    def rot(x):
        return x * c + pltpu.roll(x, HEAD_DIM - half, axis=1) * su + pltpu.roll(x, half, axis=1) * sd

    for h in range(NSA_HEADS):
        cols = slice(h * HEAD_DIM, (h + 1) * HEAD_DIM)
        qo_ref[:, cols] = (rot(q_ref[:, cols]) * HEAD_DIM ** -0.5).astype(qo_ref.dtype)
    for src, dst in ((c_ref, co_ref), (s_ref, so_ref), (w_ref, wo_ref)):
        for g in range(NSA_KV_HEADS):
            cols = slice(g * HEAD_DIM, (g + 1) * HEAD_DIM)
            dst[:, cols] = rot(src[:, cols])
        dst[:, NSA_KV_WIDTH:] = src[:, NSA_KV_WIDTH:]


def nsa_prep(hn, pos):
    B, T, _ = hn.shape
    tm = _pick(T, (256, 128, 64, 32, 16, 8))
    tabs = rope_tables(pos)
    kvw = 2 * NSA_KV_WIDTH
    q_spec = pl.BlockSpec((None, tm, NSA_WIDTH), lambda b, i: (b, i, 0))
    kv_spec = lambda n: pl.BlockSpec((None, tm, kvw), lambda b, i: (b, i, NSA_WIDTH // kvw + n))
    t_spec = pl.BlockSpec((tm, HEAD_DIM), lambda b, i: (i, 0))
    o_spec = pl.BlockSpec((None, tm, kvw), lambda b, i: (b, i, 0))
    kv_shape = jax.ShapeDtypeStruct((B, T, kvw), jnp.float32)
    return pl.pallas_call(
        _nsa_prep_kernel,
        grid=(B, T // tm),
        in_specs=[q_spec, kv_spec(0), kv_spec(1), kv_spec(2), t_spec, t_spec, t_spec],
        out_specs=[q_spec, o_spec, o_spec, o_spec],
        out_shape=[jax.ShapeDtypeStruct((B, T, NSA_WIDTH), jnp.bfloat16), kv_shape, kv_shape, kv_shape],
        compiler_params=pltpu.CompilerParams(dimension_semantics=("parallel", "parallel"),
                                             vmem_limit_bytes=VMEM_LIMIT),
        name="nsa_prep",
    )(hn, hn, hn, hn, *tabs)


def _compress_kernel(x_ref, w1_ref, w2_ref, pe_ref, ko_ref, vo_ref, *, ns):
    f32, bf16 = jnp.float32, jnp.bfloat16
    G, HD = NSA_KV_HEADS, HEAD_DIM
    row_w = 2 * NSA_KV_WIDTH
    for kv, o_ref in ((0, ko_ref), (1, vo_ref)):
        pos = jnp.zeros((SUBLANE, HD), f32)
        for p in range(CMP_STRIDE):
            w = w1_ref[kv, p]
            lo = jnp.broadcast_to(pe_ref[kv, p:p + 1, :], (SUBLANE, HD)).astype(bf16)
            hi = jnp.broadcast_to(pe_ref[kv, CMP_STRIDE + p:CMP_STRIDE + p + 1, :], (SUBLANE, HD)).astype(bf16)
            pos = pos + jnp.dot(lo, w, preferred_element_type=f32)[:, :HD] + jnp.dot(hi, w, preferred_element_type=f32)[:, HD:]
        pos = pos[0:1]
        for g in range(G):
            acc = jnp.zeros((ns, 2 * HD), f32)
            for p in range(CMP_STRIDE):
                c0 = p * row_w + kv * NSA_KV_WIDTH + g * HD
                acc = acc + jnp.dot(x_ref[:, c0:c0 + HD].astype(bf16), w1_ref[kv, p], preferred_element_type=f32)
            nxt = pltpu.roll(acc[:, HD:], ns - 1, axis=0)
            hid = jax.nn.gelu(acc[:, :HD] + nxt + pos)
            o_ref[g] = jnp.dot(hid.astype(bf16), w2_ref[kv], preferred_element_type=f32).astype(o_ref.dtype)


def compress_weights(w1k, w2k, pek, w1v, w2v, pev):
    bf16 = jnp.bfloat16
    cat = lambda w1: jnp.concatenate([w1[:CMP_STRIDE], w1[CMP_STRIDE:]], axis=-1)
    return (jnp.stack([cat(w1k), cat(w1v)]).astype(bf16), jnp.stack([w2k, w2v]).astype(bf16), jnp.stack([pek, pev]))


def compress_pallas(kv_rows, w1, w2, pe):
    B, T, W = kv_rows.shape
    ns = T // CMP_STRIDE
    bf16 = jnp.bfloat16
    x = kv_rows.reshape(B, ns, CMP_STRIDE * W)
    out = jax.ShapeDtypeStruct((B, NSA_KV_HEADS, ns, HEAD_DIM), bf16)
    o_spec = pl.BlockSpec((None, NSA_KV_HEADS, ns, HEAD_DIM), lambda b: (b, 0, 0, 0))
    return pl.pallas_call(
        functools.partial(_compress_kernel, ns=ns),
        grid=(B,),
        in_specs=[pl.BlockSpec((None, ns, CMP_STRIDE * W), lambda b: (b, 0, 0)),
                  pl.BlockSpec(w1.shape, lambda b: (0, 0, 0, 0)),
                  pl.BlockSpec(w2.shape, lambda b: (0, 0, 0)),
                  pl.BlockSpec(pe.shape, lambda b: (0, 0, 0))],
        out_specs=[o_spec, o_spec],
        out_shape=[out, out],
        compiler_params=pltpu.CompilerParams(dimension_semantics=("parallel",), vmem_limit_bytes=VMEM_LIMIT),
        name="nsa_compress",
    )(x, w1, w2, pe)


def _dot_nt(a, b):
    return lax.dot_general(a, b, (((1,), (1,)), ((), ())), preferred_element_type=jnp.float32)


def _nsa_kernel(q_ref, kc_ref, vc_ref, ks_ref, vs_ref, kw_ref, vw_ref, g_ref, covt_ref, e_ref, o_ref,
                bias_scr, p4_scr, ocmp_scr, m_scr, l_scr, acc_scr, *, TQ, TK, T, NS, NCP, n_top):
    f32, bf16 = jnp.float32, jnp.bfloat16
    R = NSA_GROUP
    i = pl.program_id(2)
    nchunk = T // TK
    qpos_col = i * TQ + lax.broadcasted_iota(jnp.int32, (TQ, 1), 0)

    kc = kc_ref[...]
    vc = vc_ref[...]
    cend = lax.broadcasted_iota(jnp.int32, (1, NCP), 1) * CMP_STRIDE + (CMP_BLOCK - 1)
    valid = cend <= qpos_col
    for r in range(R):
        s = _dot_nt(q_ref[:, r * HEAD_DIM:(r + 1) * HEAD_DIM], kc)
        s = jnp.where(valid, s, NEG_INF)
        m = jnp.max(s, axis=-1, keepdims=True)
        p = jnp.where(valid, jnp.exp(s - m), 0.0)
        l = jnp.sum(p, axis=-1, keepdims=True)
        p = (p / jnp.where(l > 0.0, l, 1.0)).astype(bf16)
        p4_scr[:, r * NCP:(r + 1) * NCP] = p
        ocmp_scr[r] = jnp.dot(p, vc, preferred_element_type=f32)

    imp_t = _dot_nt(covt_ref[...], p4_scr[...])
    j = lax.broadcasted_iota(jnp.int32, (NS, TQ), 0)
    qblk = (i * TQ + lax.broadcasted_iota(jnp.int32, (NS, TQ), 1)) // SEL_BLOCK
    forced = (j < N_INIT_BLOCKS) | ((j <= qblk) & (j > qblk - N_LOCAL_BLOCKS))
    score = jnp.where(forced, FORCE_SCORE, jnp.where(j <= qblk, imp_t, NEG_INF))
    rank = jnp.zeros((NS, TQ), f32)
    for a in range(NS):
        row = score[a:a + 1, :]
        beats = (row > score) | ((row == score) & (a < j))
        rank = rank + jnp.where(beats, 1.0, 0.0)
    sel_t = jnp.where(rank < n_top, 1.0, 0.0)
    if NS < LANE:
        sel_t = jnp.concatenate([sel_t, jnp.zeros((LANE - NS, TQ), f32)], axis=0)
    sel = sel_t.T.astype(bf16)
    for c in range(nchunk):
        selexp = jnp.dot(sel, e_ref[:, c * TK:(c + 1) * TK], preferred_element_type=f32)
        kpos = c * TK + lax.broadcasted_iota(jnp.int32, (TQ, TK), 1)
        bias_scr[c] = jnp.where((selexp > 0.5) & (kpos <= qpos_col), 0.0, NEG_INF)

    hi = (i * TQ + TQ - 1) // TK + 1

    def attend(k_ref, v_ref, lo, masker):
        m_scr[...] = jnp.full(m_scr.shape, NEG_INF, f32)
        l_scr[...] = jnp.zeros(l_scr.shape, f32)
        acc_scr[...] = jnp.zeros(acc_scr.shape, f32)

        def chunk(c, carry):
            rows = pl.ds(pl.multiple_of(c * TK, TK), TK)
            k = k_ref[rows, :].astype(bf16)
            v = v_ref[rows, :].astype(bf16)
            mk = masker(c)
            heads = range(R)
            sk = [mk(_dot_nt(q_ref[:, r * HEAD_DIM:(r + 1) * HEAD_DIM], k)) for r in heads]
            m_prev = [m_scr[r] for r in heads]
            m_new = [jnp.maximum(m_prev[r], jnp.max(sk[r][0], axis=-1, keepdims=True)) for r in heads]
            alpha = [jnp.exp(m_prev[r] - m_new[r]) for r in heads]
            ps = [jnp.exp(sk[r][0] - m_new[r]) for r in heads]
            ps = [p if sk[r][1] is None else jnp.where(sk[r][1], p, 0.0) for r, p in enumerate(ps)]
            pv = [jnp.dot(ps[r].astype(bf16), v, preferred_element_type=f32) for r in heads]
            for r in heads:
                l_scr[r] = alpha[r] * l_scr[r] + jnp.sum(ps[r], axis=-1, keepdims=True)
                acc_scr[r] = alpha[r] * acc_scr[r] + pv[r]
                m_scr[r] = m_new[r]
            return carry

        lax.fori_loop(lo, hi, chunk, 0)

    def sel_masker(c):
        b = bias_scr[c]
        return lambda s: (s + b, None)

    attend(ks_ref, vs_ref, 0, sel_masker)
    g = pltpu.roll(jax.nn.sigmoid(g_ref[...]), (LANE - 3 * R * pl.program_id(1)) % LANE, axis=1)
    for r in range(R):
        ocmp_scr[r] = (g[:, 3 * r:3 * r + 1] * ocmp_scr[r]
                       + g[:, 3 * r + 1:3 * r + 2] * (acc_scr[r] / l_scr[r]))

    def win_masker(c):
        rel = (i * TQ - c * TK + lax.broadcasted_iota(jnp.int32, (TQ, TK), 0)
               - lax.broadcasted_iota(jnp.int32, (TQ, TK), 1))
        ok = (rel >= 0) & (rel < WINDOW)
        return lambda s: (jnp.where(ok, s, NEG_INF), ok)

    attend(kw_ref, vw_ref, jnp.maximum(i * TQ - (WINDOW - 1), 0) // TK, win_masker)
    for r in range(R):
        o = ocmp_scr[r] + g[:, 3 * r + 2:3 * r + 3] * (acc_scr[r] / l_scr[r])
        o_ref[:, r * HEAD_DIM:(r + 1) * HEAD_DIM] = o.astype(o_ref.dtype)


def nsa_attention_pallas(qr, kcmp, vcmp, slc, win, hn, gate_col):
    B, T, _ = qr.shape
    G, R = NSA_KV_HEADS, NSA_GROUP
    TQ = min(NSA_TQ, T)
    TK = min(NSA_TK, T)
    NS = T // SEL_BLOCK
    NC = T // CMP_STRIDE - 1
    NCP = kcmp.shape[2]
    n_top = min(SEL_TOP, NS)
    ci = np.arange(NCP)[:, None] * CMP_STRIDE
    sj = np.arange(NS)[None, :] * SEL_BLOCK
    cover = np.clip(np.minimum(ci + CMP_BLOCK, sj + SEL_BLOCK) - np.maximum(ci, sj), 0, None) / CMP_BLOCK
    cover[NC:] = 0.0
    covt = jnp.asarray(np.tile(cover.T, (1, R)), jnp.bfloat16)
    e = jnp.asarray((np.arange(T)[None, :] // SEL_BLOCK) == np.arange(LANE)[:, None], jnp.bfloat16)
    k_spec = pl.BlockSpec((None, T, HEAD_DIM), lambda b, g, i: (b, 0, g))
    v_spec = pl.BlockSpec((None, T, HEAD_DIM), lambda b, g, i: (b, 0, G + g))
    cmp_spec = pl.BlockSpec((None, None, NCP, HEAD_DIM), lambda b, g, i: (b, g, 0, 0))
    return pl.pallas_call(
        functools.partial(_nsa_kernel, TQ=TQ, TK=TK, T=T, NS=NS, NCP=NCP, n_top=n_top),
        grid=(B, G, T // TQ),
        in_specs=[pl.BlockSpec((None, TQ, R * HEAD_DIM), lambda b, g, i: (b, i, g)),
                  cmp_spec, cmp_spec, k_spec, v_spec, k_spec, v_spec,
                  pl.BlockSpec((None, TQ, LANE), lambda b, g, i: (b, i, gate_col // LANE)),
                  pl.BlockSpec((NS, R * NCP), lambda b, g, i: (0, 0)),
                  pl.BlockSpec((LANE, T), lambda b, g, i: (0, 0))],
        out_specs=pl.BlockSpec((None, TQ, R * HEAD_DIM), lambda b, g, i: (b, i, g)),
        out_shape=jax.ShapeDtypeStruct((B, T, G * R * HEAD_DIM), jnp.bfloat16),
        scratch_shapes=[pltpu.VMEM((T // TK, TQ, TK), jnp.float32),
                        pltpu.VMEM((TQ, R * NCP), jnp.bfloat16),
                        pltpu.VMEM((R, TQ, HEAD_DIM), jnp.float32),
                        pltpu.VMEM((R, TQ, 1), jnp.float32),
                        pltpu.VMEM((R, TQ, 1), jnp.float32),
                        pltpu.VMEM((R, TQ, HEAD_DIM), jnp.float32)],
        compiler_params=pltpu.CompilerParams(dimension_semantics=("parallel", "parallel", "arbitrary"),
                                             vmem_limit_bytes=VMEM_LIMIT),
        name="nsa_attention",
    )(qr, kcmp, vcmp, slc, slc, win, win, hn, covt, e)


def _dec_compress_kernel(pt_ref, *refs):
    f32, bf16 = jnp.float32, jnp.bfloat16
    pages, w1_ref, o_ref = refs[:DEC_PAGES], refs[DEC_PAGES], refs[DEC_PAGES + 1]
    HD = HEAD_DIM
    nc = 2 * NSA_KV_HEADS
    seg = pages[0].shape[0] // (nc * CMP_STRIDE)
    for c in range(nc):
        rows = lambda p: jnp.concatenate(
            [pg[pl.ds(p * nc + c, seg, stride=CMP_STRIDE * nc), :] for pg in pages], axis=0)
        acc = None
        for p in range(0, CMP_STRIDE, 2):
            lhs = jnp.concatenate([rows(p), rows(p + 1)], axis=1).astype(bf16)
            w = jnp.concatenate([w1_ref[c // NSA_KV_HEADS, p], w1_ref[c // NSA_KV_HEADS, p + 1]], axis=0)
            part = jnp.dot(lhs, w, preferred_element_type=f32)
            acc = part if acc is None else acc + part
        o_ref[:, c * 2 * HD:(c + 1) * 2 * HD] = acc


def dec_compress(cache, layer, page_table, w1):
    L, n_phys, page = cache.shape[:3]
    nc = 2 * NSA_KV_HEADS
    B, n_pages = page_table.shape
    seg = page // CMP_STRIDE
    steps = n_pages // DEC_PAGES
    c4 = cache.reshape(L, n_phys, page * nc, HEAD_DIM)
    page_spec = lambda k: pl.BlockSpec((None, None, page * nc, HEAD_DIM),
                                       lambda b, s, pt: (layer, pt[b, s * DEC_PAGES + k], 0, 0))
    ow = nc * 2 * HEAD_DIM
    return pl.pallas_call(
        _dec_compress_kernel,
        grid_spec=pltpu.PrefetchScalarGridSpec(
            num_scalar_prefetch=1,
            grid=(B, steps),
            in_specs=[page_spec(k) for k in range(DEC_PAGES)] + [pl.BlockSpec(w1.shape, lambda b, s, pt: (0, 0, 0, 0))],
            out_specs=pl.BlockSpec((None, DEC_PAGES * seg, ow), lambda b, s, pt: (b, s, 0))),
        out_shape=jax.ShapeDtypeStruct((B, n_pages * seg, ow), jnp.float32),
        compiler_params=pltpu.CompilerParams(dimension_semantics=("parallel", "arbitrary"),
                                             vmem_limit_bytes=VMEM_LIMIT),
        name="nsa_dec_compress",
    )(page_table, *([c4] * DEC_PAGES), w1)


def _dec_select_kernel(q_ref, fk_ref, fv_ref, w1_ref, w2_ref, pe_ref, covt_ref, ocmp_ref, idx_ref, *, NSEG, NS, NSP, n_top):
    f32, bf16 = jnp.float32, jnp.bfloat16
    HD, R = HEAD_DIM, NSA_GROUP
    NC = NSEG - 1

    def compressed(kv, f_ref):
        pos = jnp.zeros((SUBLANE, HD), f32)
        for p in range(CMP_STRIDE):
            w = w1_ref[kv, p]
            lo = jnp.broadcast_to(pe_ref[kv, p:p + 1, :], (SUBLANE, HD)).astype(bf16)
            hi = jnp.broadcast_to(pe_ref[kv, CMP_STRIDE + p:CMP_STRIDE + p + 1, :], (SUBLANE, HD)).astype(bf16)
            pos = pos + jnp.dot(lo, w, preferred_element_type=f32)[:, :HD] + jnp.dot(hi, w, preferred_element_type=f32)[:, HD:]
        nxt = pltpu.roll(f_ref[:, HD:], NSEG - 1, axis=0)
        hid = jax.nn.gelu(f_ref[:, :HD] + nxt + pos[0:1])
        return jnp.dot(hid.astype(bf16), w2_ref[kv], preferred_element_type=f32).astype(bf16)

    kc = compressed(0, fk_ref)
    vc = compressed(1, fv_ref)
    q = q_ref[...]
    q4 = jnp.concatenate([q[:, r * HD:(r + 1) * HD] for r in range(R)] + [jnp.zeros((SUBLANE - R, HD), bf16)], axis=0)
    s = _dot_nt(q4, kc)
    valid = lax.broadcasted_iota(jnp.int32, s.shape, 1) < NC
    s = jnp.where(valid, s, NEG_INF)
    p = jnp.where(valid, jnp.exp(s - jnp.max(s, axis=-1, keepdims=True)), 0.0)
    p = (p / jnp.sum(p, axis=-1, keepdims=True)).astype(bf16)
    ocmp_ref[...] = jnp.dot(p, vc, preferred_element_type=f32)
    head = lax.broadcasted_iota(jnp.int32, p.shape, 0) < R
    imp = _dot_nt(covt_ref[...], jnp.where(head, p, jnp.zeros_like(p)))
    imp = jnp.sum(imp, axis=1, keepdims=True)
    j_col = lax.broadcasted_iota(jnp.int32, (NSP, 1), 0)
    qblk = NS - 1
    forced = (j_col < N_INIT_BLOCKS) | ((j_col <= qblk) & (j_col > qblk - N_LOCAL_BLOCKS))
    score_col = jnp.where(forced, FORCE_SCORE, jnp.where(j_col <= qblk, imp, -3e38))
    score_cb = jnp.broadcast_to(score_col, (NSP, LANE))
    score_row = score_cb.T[0:1, :]
    ii = lax.broadcasted_iota(jnp.int32, (NSP, NSP), 0)
    jj = lax.broadcasted_iota(jnp.int32, (NSP, NSP), 1)
    beats = (score_col > score_row) | ((score_col == score_row) & (ii < jj))
    rank = jnp.sum(jnp.where(beats, 1.0, 0.0), axis=0, keepdims=True)
    t_col = lax.broadcasted_iota(jnp.int32, (n_top, NSP), 0).astype(f32)
    j_row = lax.broadcasted_iota(jnp.int32, (n_top, NSP), 1).astype(f32)
    ids = jnp.sum(jnp.where(rank == t_col, j_row, 0.0), axis=1, keepdims=True)
    idx_ref[...] = jnp.broadcast_to(ids, (n_top, LANE)).astype(jnp.int32)


def dec_select(qr, fs, w1, w2, pe, past_len):
    B = qr.shape[0]
    G, R, HD = NSA_KV_HEADS, NSA_GROUP, HEAD_DIM
    NSEG = fs.shape[1]
    NC = NSEG - 1
    NS = -(-(past_len + 1) // SEL_BLOCK)
    NSP = _round_up(NS, LANE)
    n_top = min(SEL_TOP, NS)
    ci = np.arange(NSEG)[:, None] * CMP_STRIDE
    sj = np.arange(NSP)[None, :] * SEL_BLOCK
    cover = np.clip(np.minimum(ci + CMP_BLOCK, sj + SEL_BLOCK) - np.maximum(ci, sj), 0, None) / CMP_BLOCK
    cover[NC:] = 0.0
    cover[:, NS:] = 0.0
    covt = jnp.asarray(cover.T, jnp.bfloat16)
    f_spec = lambda kv: pl.BlockSpec((None, NSEG, 2 * HD), lambda b, g: (b, 0, kv * G + g))
    full = lambda a: pl.BlockSpec(a.shape, lambda b, g: (0,) * a.ndim)
    ocmp, idx = pl.pallas_call(
        functools.partial(_dec_select_kernel, NSEG=NSEG, NS=NS, NSP=NSP, n_top=n_top),
        grid=(B, G),
        in_specs=[pl.BlockSpec((None, 1, R * HD), lambda b, g: (b, 0, g)), f_spec(0), f_spec(1),
                  full(w1), full(w2), full(pe), full(covt)],
        out_specs=[pl.BlockSpec((None, None, SUBLANE, HD), lambda b, g: (b, g, 0, 0)),
                   pl.BlockSpec((None, None, n_top, LANE), lambda b, g: (b, g, 0, 0))],
        out_shape=[jax.ShapeDtypeStruct((B, G, SUBLANE, HD), jnp.float32),
                   jax.ShapeDtypeStruct((B, G, n_top, LANE), jnp.int32)],
        compiler_params=pltpu.CompilerParams(dimension_semantics=("parallel", "parallel"),
                                             vmem_limit_bytes=VMEM_LIMIT),
        name="nsa_dec_select",
    )(qr, fs, fs, w1, w2, pe, covt)
    return ocmp, idx[:, :, :, 0]


def _dec_attend_kernel(pt_ref, idx_ref, q_ref, *refs, NS, n_top):
    f32, bf16 = jnp.float32, jnp.bfloat16
    HD, R, G = HEAD_DIM, NSA_GROUP, NSA_KV_HEADS
    blocks = refs[:n_top]
    nks_ref, nvs_ref, wb_ref, nkw_ref, nvw_ref, ocmp_ref, g_ref, o_ref = refs[n_top:]
    b, g = pl.program_id(0), pl.program_id(1)
    q = q_ref[...]
    q4 = jnp.concatenate([q[:, r * HD:(r + 1) * HD] for r in range(R)] + [jnp.zeros((SUBLANE - R, HD), bf16)], axis=0)

    def head_rows(ref, kv):
        return ref[pl.ds(kv * G + g, ref.shape[0] // (2 * G), stride=2 * G), :].astype(bf16)

    def attend(keys, vals, bias, k_new_ref, v_new_ref):
        k_new = jnp.broadcast_to(k_new_ref[...], (SUBLANE, HD)).astype(bf16).astype(f32)
        v_new = jnp.broadcast_to(v_new_ref[...], (SUBLANE, HD)).astype(bf16).astype(f32)
        s = _dot_nt(q4, keys) + bias
        s_new = jnp.sum(q4.astype(f32) * k_new, axis=-1, keepdims=True)
        m = jnp.maximum(jnp.max(s, axis=-1, keepdims=True), s_new)
        p = jnp.exp(s - m)
        p_new = jnp.exp(s_new - m)
        l = jnp.sum(p, axis=-1, keepdims=True) + p_new
        return (jnp.dot(p.astype(bf16), vals, preferred_element_type=f32) + p_new.astype(bf16).astype(f32) * v_new) / l

    keys = jnp.concatenate([head_rows(r, 0) for r in blocks], axis=0)
    vals = jnp.concatenate([head_rows(r, 1) for r in blocks], axis=0)
    bias = jnp.concatenate(
        [jnp.broadcast_to(jnp.where(idx_ref[b, g, t] != NS - 1, 0.0, NEG_INF), (SUBLANE, SEL_BLOCK)) for t in range(n_top)],
        axis=1)
    o_slc = attend(keys, vals, bias, nks_ref, nvs_ref)
    nwin = wb_ref.shape[0] // (2 * G)
    ok = lax.broadcasted_iota(jnp.int32, (SUBLANE, nwin), 1) > nwin - WINDOW
    o_win = attend(head_rows(wb_ref, 0), head_rows(wb_ref, 1), jnp.where(ok, 0.0, NEG_INF), nkw_ref, nvw_ref)
    gate = pltpu.roll(jax.nn.sigmoid(jnp.broadcast_to(g_ref[...], (SUBLANE, LANE))), (LANE - 3 * R * g) % LANE, axis=1)
    rows = []
    for r in range(R):
        rows.append(gate[r:r + 1, 3 * r:3 * r + 1] * ocmp_ref[r:r + 1, :] + gate[r:r + 1, 3 * r + 1:3 * r + 2] * o_slc[r:r + 1, :]
                    + gate[r:r + 1, 3 * r + 2:3 * r + 3] * o_win[r:r + 1, :])
    o_ref[...] = jnp.concatenate(rows, axis=1).astype(o_ref.dtype)


def dec_attend(qr, slc_cache, win_cache, layer, page_table, idx, new_slc, new_win, ocmp, hn, gate_col):
    B = qr.shape[0]
    G, R, HD = NSA_KV_HEADS, NSA_GROUP, HEAD_DIM
    L, n_phys, page = slc_cache.shape[:3]
    nc = 2 * G
    n_top = idx.shape[2]
    NS = -(-(page_table.shape[1] * page + 1) // SEL_BLOCK)
    per = page // SEL_BLOCK
    blocks = slc_cache.reshape(L, n_phys * per, SEL_BLOCK * nc, HD)
    nwin = win_cache.shape[2]
    wins = win_cache.reshape(L, B, nwin * nc, HD)

    def blk_spec(t):
        def index(b, g, pt, ix):
            j = jnp.minimum(ix[b, g, t], NS - 2)
            return (layer, pt[b, j // per] * per + j % per, 0, 0)
        return pl.BlockSpec((None, None, SEL_BLOCK * nc, HD), index)

    row = lambda col: pl.BlockSpec((None, 1, HD), lambda b, g, pt, ix: (b, 0, col(g)))
    kcol, vcol = (lambda g: g), (lambda g: G + g)
    return pl.pallas_call(
        functools.partial(_dec_attend_kernel, NS=NS, n_top=n_top),
        grid_spec=pltpu.PrefetchScalarGridSpec(
            num_scalar_prefetch=2,
            grid=(B, G),
            in_specs=[pl.BlockSpec((None, 1, R * HD), lambda b, g, pt, ix: (b, 0, g))]
                     + [blk_spec(t) for t in range(n_top)]
                     + [row(kcol), row(vcol),
                        pl.BlockSpec((None, None, nwin * nc, HD), lambda b, g, pt, ix: (layer, b, 0, 0)),
                        row(kcol), row(vcol),
                        pl.BlockSpec((None, None, SUBLANE, HD), lambda b, g, pt, ix: (b, g, 0, 0)),
                        pl.BlockSpec((None, 1, LANE), lambda b, g, pt, ix: (b, 0, gate_col // LANE))],
            out_specs=pl.BlockSpec((None, 1, R * HD), lambda b, g, pt, ix: (b, 0, g))),
        out_shape=jax.ShapeDtypeStruct((B, 1, G * R * HD), jnp.bfloat16),
        compiler_params=pltpu.CompilerParams(dimension_semantics=("parallel", "parallel"),
                                             vmem_limit_bytes=VMEM_LIMIT),
        name="nsa_dec_attend",
    )(page_table, idx, qr, *([blocks] * n_top), new_slc, new_slc, wins, new_win, new_win, ocmp, hn)


def trunk_layer(x, pos0, paged, gla_s0, rwkv_s0, shift0, conv0, lw):
    B, T, _ = x.shape
    dt = x.dtype
    f32 = jnp.float32
    pos = pos0 + jnp.arange(T, dtype=jnp.int32)
    heads = lambda t, n: t.reshape(B, T, n, t.shape[-1] // n)

    bf = jnp.bfloat16
    M = B * T
    x2 = x.reshape(M, D_MODEL)
    xn = rmsnorm_pallas(x2, lw['norm1'])
    hg = mm(xn, lw['w_gla']).reshape(B, T, -1)
    hn = mm(xn, lw['w_nsa']).reshape(B, T, -1)
    hr = mm(xn, lw['w_rwkv']).reshape(B, T, -1)
    mg = mm(xn, lw['w_mg'])

    o_gla, gla_s = gla_pallas(hg, lw['gla_wa2'], lw['gla_ba'], lw['gla_norm'], gla_s0.astype(f32), GLA_OFF)
    o_gla = o_gla.reshape(M, GLA_WIDTH)

    kv5 = lambda t: t.reshape(B, T, 2, NSA_KV_HEADS, HEAD_DIM)
    cw1, cw2, cpe = compress_weights(lw['cmp_w1k'], lw['cmp_w2k'], lw['cmp_pek'],
                                     lw['cmp_w1v'], lw['cmp_w2v'], lw['cmp_pev'])
    if paged is None:
        assert T % NSA_TQ == 0
        qr, cmp2, slc2, win2 = nsa_prep(hn, pos)
        kcmp, vcmp = compress_pallas(cmp2, cw1, cw2, cpe)
        o_nsa = nsa_attention_pallas(qr, kcmp, vcmp, slc2, win2, hn, NSA_GATE_OFF)
        new_cmp, new_slc, win_new = kv5(cmp2), kv5(slc2), kv5(win2)[:, T - min(WINDOW, T):]
    else:
        assert T == 1
        cache_cmp, cache_slc, cache_win, layer, page_table = paged
        qr, cmp2, slc2, win2 = [t.reshape(B, 1, -1) for t in
                                nsa_prep(hn.reshape(1, B, -1), jnp.full((B,), pos0, jnp.int32))]
        fs = dec_compress(cache_cmp, layer, page_table, cw1)
        ocmp, sel_ids = dec_select(qr, fs, cw1, cw2, cpe, pos0)
        o_nsa = dec_attend(qr, cache_slc, cache_win, layer, page_table, sel_ids, slc2, win2, ocmp, hn, NSA_GATE_OFF)
        new_cmp, new_slc = kv5(cmp2), kv5(slc2)
        win_all = jnp.concatenate([cache_win[layer].astype(dt), kv5(win2)], axis=1)
        win_new = win_all[:, win_all.shape[1] - min(WINDOW, win_all.shape[1]):]
    o_nsa = o_nsa.reshape(M, NSA_WIDTH)

    (r_, w_log, k2, v_, kk, a, gate, bonus), shift_new = rwkv_prep(
        hr, shift0.astype(f32), lw['rwkv_mu'], lw['rwkv_w0'], lw['rwkv_w2'], lw['rwkv_a0'], lw['rwkv_a2'],
        lw['rwkv_g2'], lw['rwkv_kk'], lw['rwkv_ka'], lw['rwkv_rk'])
    y, rwkv_s = rwkv_scan_pallas(r_, w_log, k2, v_, kk, a, rwkv_s0.astype(f32))
    flat = lambda t: t.reshape(M, RWKV_WIDTH)
    o_rwkv = rwkv_post(flat(y), flat(bonus), flat(gate), lw['rwkv_ln_w'], lw['rwkv_ln_b'])

    merged = merge_mm(o_gla, o_nsa, o_rwkv, lw['w_o_gla'], lw['w_o_nsa'], lw['w_o_rwkv'], mg)
    x2 = mm(merged, lw['w_out'], res=x2)

    xn2 = rmsnorm_pallas(x2, lw['norm2'])
    act, conv_new = ffn_gate_up(xn2, lw['ffn_gate'], lw['ffn_up'], lw['ffn_conv'], lw['ffn_conv_b'],
                                conv0.astype(f32), B, T)
    x2 = mm(act, lw['ffn_down'], res=x2)
    return x2.reshape(B, T, D_MODEL), (new_cmp, new_slc, win_new, gla_s, rwkv_s, shift_new, conv_new)


def _w_in_group(w, lo, hi):
    seg = w[:, lo:hi].astype(jnp.bfloat16)
    return jnp.pad(seg, ((0, 0), (0, _round_up(hi - lo, W_IN_TILE) - (hi - lo))))


def kernel(x_prompt, x_sample, cache_cmp_kv, cache_slc_kv, cache_win_kv, state_gla, state_rwkv, state_rwkv_shift, state_ffn_conv, page_table, norm1, w_in, gla_wa2, gla_ba, gla_norm, w_o_gla, cmp_w1k, cmp_w2k, cmp_pek, cmp_w1v, cmp_w2v, cmp_pev, w_o_nsa, rwkv_mu, rwkv_w0, rwkv_w2, rwkv_a0, rwkv_a2, rwkv_g2, rwkv_kk, rwkv_ka, rwkv_rk, rwkv_ln_w, rwkv_ln_b, w_o_rwkv, w_out, norm2, ffn_gate, ffn_conv, ffn_conv_b, ffn_up, ffn_down, norm_f):
    G, HD = NSA_KV_HEADS, HEAD_DIM
    n_db, n_pages = page_table.shape
    past_len = n_pages * PAGE_SIZE
    bp = x_prompt.shape[0]
    dt = x_prompt.dtype
    bf = jnp.bfloat16
    xp, xs = x_prompt, x_sample
    st_p, st_s = [], []
    for l in range(DEPTH):
        lw = {'norm1': norm1[l], 'w_gla': _w_in_group(w_in[l], 0, _C_NSA),
              'w_nsa': _w_in_group(w_in[l], _C_NSA, _C_RWKV), 'w_rwkv': _w_in_group(w_in[l], _C_RWKV, _C_MG),
              'w_mg': _w_in_group(w_in[l], _C_MG, _C_MG + IN_SIZES[14]), 'gla_wa2': gla_wa2[l], 'gla_ba': gla_ba[l],
              'gla_norm': gla_norm[l], 'w_o_gla': w_o_gla[l].astype(bf), 'cmp_w1k': cmp_w1k[l], 'cmp_w2k': cmp_w2k[l],
              'cmp_pek': cmp_pek[l], 'cmp_w1v': cmp_w1v[l], 'cmp_w2v': cmp_w2v[l], 'cmp_pev': cmp_pev[l],
              'w_o_nsa': w_o_nsa[l].astype(bf), 'rwkv_mu': rwkv_mu[l], 'rwkv_w0': rwkv_w0[l], 'rwkv_w2': rwkv_w2[l],
              'rwkv_a0': rwkv_a0[l], 'rwkv_a2': rwkv_a2[l], 'rwkv_g2': rwkv_g2[l], 'rwkv_kk': rwkv_kk[l],
              'rwkv_ka': rwkv_ka[l], 'rwkv_rk': rwkv_rk[l], 'rwkv_ln_w': rwkv_ln_w[l], 'rwkv_ln_b': rwkv_ln_b[l],
              'w_o_rwkv': w_o_rwkv[l].astype(bf), 'w_out': w_out[l].astype(bf), 'norm2': norm2[l],
              'ffn_gate': ffn_gate[l], 'ffn_conv': ffn_conv[l], 'ffn_conv_b': ffn_conv_b[l],
              'ffn_up': ffn_up[l], 'ffn_down': ffn_down[l].astype(bf)}
        xp, sp = trunk_layer(xp, 0, None,
                             jnp.zeros((bp, GLA_HEADS, GLA_DK, GLA_DV), jnp.float32),
                             jnp.zeros((bp, RWKV_HEADS, RWKV_N, RWKV_N), jnp.float32),
                             jnp.zeros((bp, RWKV_COLS), dt),
                             jnp.zeros((bp, CONV_W - 1, D_FF), dt), lw)
        paged = (cache_cmp_kv, cache_slc_kv, cache_win_kv, l, page_table)
        xs, ss = trunk_layer(xs, past_len, paged, state_gla[l], state_rwkv[l],
                             state_rwkv_shift[l], state_ffn_conv[l], lw)
        st_p.append(sp)
        st_s.append(ss)
    y_prompt = rmsnorm_pallas(xp.reshape(-1, D_MODEL), norm_f, out_dtype=dt).reshape(xp.shape)
    y_sample = rmsnorm_pallas(xs.reshape(-1, D_MODEL), norm_f, out_dtype=dt).reshape(xs.shape)
    outs = [y_prompt, y_sample]
    for i in range(7):
        outs.append(jnp.stack([s[i] for s in st_p]))
        outs.append(jnp.stack([s[i] for s in st_s]))
    return tuple(outs)
```

```python
import functools

import jax
import jax.numpy as jnp
import numpy as np
from jax import lax
from jax.experimental import pallas as pl
from jax.experimental.pallas import tpu as pltpu

D_MODEL = 4096
DEPTH = 2
PAGE_SIZE = 128
HEAD_DIM = 128
ROPE_DIM = HEAD_DIM // 4
ROPE_THETA = 500000.0
NORM_EPS = 1e-5
NEG_INF = -1e30

GLA_WIDTH = D_MODEL // 4
GLA_HEADS = 4
GLA_DV = GLA_WIDTH // GLA_HEADS
GLA_DK = GLA_DV // 2
GLA_GATE_RANK = 16
GLA_TAU = 16.0
GLA_CHUNK = 64

NSA_HEADS = D_MODEL // 256
NSA_KV_HEADS = 4
NSA_GROUP = NSA_HEADS // NSA_KV_HEADS
NSA_WIDTH = NSA_HEADS * HEAD_DIM
NSA_KV_WIDTH = NSA_KV_HEADS * HEAD_DIM
CMP_STRIDE = 16
CMP_BLOCK = 2 * CMP_STRIDE
SEL_BLOCK = 64
SEL_TOP = 16
N_INIT_BLOCKS = 1
N_LOCAL_BLOCKS = 2
WINDOW = 512
SEL_Q_BLOCK = 32
WIN_Q_BLOCK = 128
FORCE_SCORE = 1e4

RWKV_WIDTH = D_MODEL // 4
RWKV_N = 64
RWKV_HEADS = RWKV_WIDTH // RWKV_N
RWKV_DECAY_RANK = 64
RWKV_AAA_RANK = 64
RWKV_GATE_RANK = 160
RWKV_SIZES = (RWKV_WIDTH, RWKV_WIDTH, RWKV_WIDTH, RWKV_DECAY_RANK, RWKV_AAA_RANK, RWKV_GATE_RANK)
RWKV_COLS = sum(RWKV_SIZES)
RWKV_LN_EPS = 64e-5

N_BRANCH = 3
D_FF = 256 * ((8 * D_MODEL // 3 + 255) // 256)
CONV_W = 3

IN_SIZES = (GLA_HEADS * GLA_DK, GLA_HEADS * GLA_DK, GLA_WIDTH, GLA_WIDTH, GLA_GATE_RANK,
            NSA_WIDTH, NSA_KV_WIDTH, NSA_KV_WIDTH, NSA_KV_WIDTH, NSA_KV_WIDTH, NSA_KV_WIDTH, NSA_KV_WIDTH,
            NSA_HEADS * 3,
            RWKV_COLS,
            N_BRANCH * D_MODEL)

LANE = 128
SUBLANE = 8
NSA_TQ = 512
NSA_TK = 512
GLA_TC = 256
DEC_PAGES = 16
RWKV_NB = 2
VMEM_LIMIT = 48 * 1024 * 1024


def _round_up(n, m):
    return -(-n // m) * m


W_IN_TILE = 512
_C_NSA = sum(IN_SIZES[:5])
_C_RWKV = sum(IN_SIZES[:13])
_C_MG = sum(IN_SIZES[:14])
GLA_OFF = tuple(int(o) for o in np.concatenate([[0], np.cumsum(IN_SIZES[:4])]))
NSA_GATE_OFF = NSA_WIDTH + 6 * NSA_KV_WIDTH


def _pick(n, cands):
    for c in cands:
        if n % c == 0:
            return c
    return n


def _rmsnorm_kernel(x_ref, g_ref, o_ref):
    x = x_ref[...]
    y = x * lax.rsqrt(jnp.mean(x * x, axis=-1, keepdims=True) + NORM_EPS)
    o_ref[...] = (y * g_ref[...]).astype(o_ref.dtype)


def rmsnorm_pallas(x, g, out_dtype=jnp.bfloat16):
    M, D = x.shape
    tm = _pick(M, (256, 128, 64, 32, 16, 8))
    return pl.pallas_call(
        _rmsnorm_kernel,
        grid=(M // tm,),
        in_specs=[pl.BlockSpec((tm, D), lambda i: (i, 0)), pl.BlockSpec((1, D), lambda i: (0, 0))],
        out_specs=pl.BlockSpec((tm, D), lambda i: (i, 0)),
        out_shape=jax.ShapeDtypeStruct((M, D), out_dtype),
        compiler_params=pltpu.CompilerParams(dimension_semantics=("parallel",), vmem_limit_bytes=VMEM_LIMIT),
        name="rmsnorm",
    )(x, g.reshape(1, D))


def _mm_kernel(*refs, nk, has_res):
    x_ref, w_ref = refs[:2]
    res_ref = refs[2] if has_res else None
    o_ref, acc_ref = refs[-2:]
    k = pl.program_id(2)
    part = jnp.dot(x_ref[...], w_ref[...], preferred_element_type=jnp.float32)

    def finish(v):
        if has_res:
            v = v + res_ref[...]
        o_ref[...] = v.astype(o_ref.dtype)

    if nk == 1:
        finish(part)
    else:
        @pl.when(k == 0)
        def _():
            acc_ref[...] = part

        @pl.when(jnp.logical_and(k > 0, k < nk - 1))
        def _():
            acc_ref[...] += part

        @pl.when(k == nk - 1)
        def _():
            finish(acc_ref[...] + part)


def mm(x, w, res=None, out_dtype=jnp.float32):
    M, K = x.shape
    N = w.shape[1]
    tm = _pick(M, (1024, 512, 256, 128, 64, 32, 16, 8))
    tn = _pick(N, (512, 256, 128))
    tk = K if K <= 4096 else _pick(K, (5504, 4096, 2048, 1024, 512))
    nk = K // tk
    in_specs = [pl.BlockSpec((tm, tk), lambda i, j, k: (i, k)),
                pl.BlockSpec((tk, tn), lambda i, j, k: (k, j))]
    args = [x, w]
    if res is not None:
        in_specs.append(pl.BlockSpec((tm, tn), lambda i, j, k: (i, j)))
        args.append(res)
    return pl.pallas_call(
        functools.partial(_mm_kernel, nk=nk, has_res=res is not None),
        grid=(M // tm, N // tn, nk),
        in_specs=in_specs,
        out_specs=pl.BlockSpec((tm, tn), lambda i, j, k: (i, j)),
        out_shape=jax.ShapeDtypeStruct((M, N), out_dtype),
        scratch_shapes=[pltpu.VMEM((tm, tn) if nk > 1 else (SUBLANE, LANE), jnp.float32)],
        compiler_params=pltpu.CompilerParams(
            dimension_semantics=("parallel", "parallel", "arbitrary"),
            vmem_limit_bytes=VMEM_LIMIT),
        name="dense_mm",
    )(*args)


def _merge_kernel(oa_ref, ob_ref, oc_ref, wa_ref, wb_ref, wc_ref, ga_ref, gb_ref, gc_ref, o_ref):
    f32 = jnp.float32
    acc = jax.nn.sigmoid(ga_ref[...]) * jnp.dot(oa_ref[...], wa_ref[...], preferred_element_type=f32)
    acc += jax.nn.sigmoid(gb_ref[...]) * jnp.dot(ob_ref[...], wb_ref[...], preferred_element_type=f32)
    acc += jax.nn.sigmoid(gc_ref[...]) * jnp.dot(oc_ref[...], wc_ref[...], preferred_element_type=f32)
    o_ref[...] = acc.astype(o_ref.dtype)


def merge_mm(o_a, o_b, o_c, w_a, w_b, w_c, mg, out_dtype=jnp.bfloat16):
    M = o_a.shape[0]
    D = w_a.shape[1]
    tm = _pick(M, (1024, 512, 256, 128, 64, 32, 16, 8))
    tn = _pick(D, (512, 256, 128))
    nj = D // tn
    o_spec = lambda o: pl.BlockSpec((tm, o.shape[1]), lambda i, j: (i, 0))
    w_spec = lambda w: pl.BlockSpec((w.shape[0], tn), lambda i, j: (0, j))
    g_spec = lambda b: pl.BlockSpec((tm, tn), lambda i, j: (i, b * nj + j))
    return pl.pallas_call(
        _merge_kernel,
        grid=(M // tm, nj),
        in_specs=[o_spec(o_a), o_spec(o_b), o_spec(o_c), w_spec(w_a), w_spec(w_b), w_spec(w_c),
                  g_spec(0), g_spec(1), g_spec(2)],
        out_specs=pl.BlockSpec((tm, tn), lambda i, j: (i, j)),
        out_shape=jax.ShapeDtypeStruct((M, D), out_dtype),
        compiler_params=pltpu.CompilerParams(dimension_semantics=("parallel", "parallel"),
                                             vmem_limit_bytes=VMEM_LIMIT),
        name="merge_mm",
    )(o_a, o_b, o_c, w_a, w_b, w_c, mg, mg, mg)


def _ffn_gate_up_kernel(x_ref, wg_ref, wu_ref, cw_ref, cb_ref, st_ref, act_ref, tail_ref, carry_scr, *, tm, tps, T):
    f32 = jnp.float32
    i, j = pl.program_id(0), pl.program_id(1)
    x = x_ref[...]
    h = jnp.dot(x, wg_ref[...].astype(x.dtype), preferred_element_type=f32)
    u = jnp.dot(x, wu_ref[...].astype(x.dtype), preferred_element_type=f32)
    cw = cw_ref[...]
    if T == 1:
        prev2, prev1 = st_ref[0], st_ref[1]
        tail_ref[0] = prev1
        tail_ref[1] = h
    else:
        tail = jnp.where(i % tps == 0, st_ref[...], carry_scr[j])
        row = lax.broadcasted_iota(jnp.int32, h.shape, 0)
        prev1 = jnp.where(row == 0, tail[7:8], pltpu.roll(h, 1, axis=0))
        prev2 = jnp.where(row == 0, tail[6:7], jnp.where(row == 1, tail[7:8], pltpu.roll(h, 2, axis=0)))
        last = h[tm - SUBLANE:tm]
        carry_scr[j] = last
        tail_ref[...] = last
    hc = cb_ref[...] + prev2 * cw[0:1] + prev1 * cw[1:2] + h * cw[2:3]
    act_ref[...] = (jax.nn.silu(hc) * u).astype(act_ref.dtype)


def ffn_gate_up(xn, w_gate, w_up, conv_w, conv_b, conv0, B, T):
    M, D = xn.shape
    F = w_gate.shape[1]
    tn = _pick(F, (512, 256, 128))
    nj = F // tn
    cw = jnp.pad(conv_w, ((0, SUBLANE - CONV_W), (0, 0)))
    cb = conv_b.reshape(1, F)
    if T == 1:
        tm, tps = M, 1
        st = conv0.transpose(1, 0, 2)
        st_spec = pl.BlockSpec((2, B, tn), lambda i, j: (0, 0, j))
        tail_shape, tail_spec = (2, B, F), pl.BlockSpec((2, B, tn), lambda i, j: (0, 0, j))
    else:
        tm = _pick(T, (1024, 512, 256, 128, 64, 32, 16, 8))
        tps = T // tm
        st = jnp.pad(conv0, ((0, 0), (SUBLANE - 2, 0), (0, 0)))
        st_spec = pl.BlockSpec((None, SUBLANE, tn), lambda i, j: (i // tps, 0, j))
        tail_shape, tail_spec = (B, SUBLANE, F), pl.BlockSpec((None, SUBLANE, tn), lambda i, j: (i // tps, 0, j))
    act, tail = pl.pallas_call(
        functools.partial(_ffn_gate_up_kernel, tm=tm, tps=tps, T=T),
        grid=(M // tm, nj),
        in_specs=[pl.BlockSpec((tm, D), lambda i, j: (i, 0)),
                  pl.BlockSpec((D, tn), lambda i, j: (0, j)),
                  pl.BlockSpec((D, tn), lambda i, j: (0, j)),
                  pl.BlockSpec((SUBLANE, tn), lambda i, j: (0, j)),
                  pl.BlockSpec((1, tn), lambda i, j: (0, j)),
                  st_spec],
        out_specs=[pl.BlockSpec((tm, tn), lambda i, j: (i, j)), tail_spec],
        out_shape=[jax.ShapeDtypeStruct((M, F), jnp.bfloat16), jax.ShapeDtypeStruct(tail_shape, jnp.float32)],
        scratch_shapes=[pltpu.VMEM((nj, SUBLANE, tn), jnp.float32)],
        compiler_params=pltpu.CompilerParams(dimension_semantics=("arbitrary", "arbitrary"),
                                             vmem_limit_bytes=VMEM_LIMIT),
        name="ffn_gate_up",
    )(xn, w_gate, w_up, cw, cb, st)
    conv_new = tail.transpose(1, 0, 2) if T == 1 else tail[:, SUBLANE - 2:]
    return act, conv_new


def _gla_kernel(q_ref, k_ref, v_ref, og_ref, lo_ref, wa_ref, ba_ref, gn_ref, s0_ref, o_ref, sout_ref, s_scr, *, Tc, C, valid):
    f32, bf16 = jnp.float32, jnp.bfloat16
    c = pl.program_id(2)

    @pl.when(c == 0)
    def _():
        s_scr[...] = s0_ref[...]

    row = lax.broadcasted_iota(jnp.int32, (C, GLA_DK), 0)
    tril = lax.broadcasted_iota(jnp.int32, (C, C), 0) >= lax.broadcasted_iota(jnp.int32, (C, C), 1)
    wa = wa_ref[...]
    ba = ba_ref[...]
    gn = gn_ref[...]
    S = s_scr[...]
    for n in range(Tc // C):
        rows = slice(n * C, (n + 1) * C)
        la = jax.nn.log_sigmoid(jnp.dot(lo_ref[rows, :].astype(bf16), wa, preferred_element_type=f32) + ba) / GLA_TAU
        if valid < C:
            la = jnp.where(row < valid, la, 0.0)
        bc = la
        d = 1
        while d < C:
            bc = bc + jnp.where(row >= d, pltpu.roll(bc, d, axis=0), 0.0)
            d *= 2
        b_mid = bc[C // 2:C // 2 + 1]
        b_last = bc[C - 1:C]
        q = q_ref[rows, :] * GLA_DK ** -0.5
        k = k_ref[rows, :]
        v = v_ref[rows, :].astype(bf16)
        att = lax.dot_general((q * jnp.exp(bc - b_mid)).astype(bf16), (k * jnp.exp(b_mid - bc)).astype(bf16),
                              (((1,), (1,)), ((), ())), preferred_element_type=f32)
        att = jnp.where(tril, att, 0.0)
        o = jnp.dot(att.astype(bf16), v, preferred_element_type=f32)
        o = o + jnp.dot((q * jnp.exp(bc)).astype(bf16), S.astype(bf16), preferred_element_type=f32)
        kd = jnp.concatenate([k * jnp.exp(b_last - bc), jnp.broadcast_to(jnp.exp(b_last), (SUBLANE, GLA_DK))], axis=0)
        kdt = kd.T
        S = kdt[:, C:C + 1] * S + jnp.dot(kdt[:, :C].astype(bf16), v, preferred_element_type=f32)
        o = o * lax.rsqrt(jnp.mean(o * o, axis=-1, keepdims=True) + NORM_EPS) * gn
        o_ref[rows, :] = (o * jax.nn.silu(og_ref[rows, :])).astype(o_ref.dtype)
    s_scr[...] = S

    @pl.when(c == pl.num_programs(2) - 1)
    def _():
        sout_ref[...] = S


def gla_pallas(hmix, wa2, ba, gnorm, s0, seg_off):
    B, T, _ = hmix.shape
    H = GLA_HEADS
    n_tok = T
    if T % GLA_CHUNK == 0:
        Tc, C = min(GLA_TC, T), GLA_CHUNK
    else:
        assert T < SUBLANE
        Tc = C = SUBLANE
        hmix = jnp.pad(hmix, ((0, 0), (0, SUBLANE - T), (0, 0)))
        T = SUBLANE
    oq, ok, ov, og, ol = seg_off
    wa = jnp.pad(wa2, ((0, LANE - wa2.shape[0]), (0, 0))).astype(jnp.bfloat16)
    col = lambda off, w: (lambda b, h, c: (b, c, off // w + h))
    o, s = pl.pallas_call(
        functools.partial(_gla_kernel, Tc=Tc, C=C, valid=min(n_tok, C)),
        grid=(B, H, T // Tc),
        in_specs=[pl.BlockSpec((None, Tc, GLA_DK), col(oq, GLA_DK)),
                  pl.BlockSpec((None, Tc, GLA_DK), col(ok, GLA_DK)),
                  pl.BlockSpec((None, Tc, GLA_DV), col(ov, GLA_DV)),
                  pl.BlockSpec((None, Tc, GLA_DV), col(og, GLA_DV)),
                  pl.BlockSpec((None, Tc, LANE), lambda b, h, c: (b, c, ol // LANE)),
                  pl.BlockSpec((LANE, GLA_DK), lambda b, h, c: (0, h)),
                  pl.BlockSpec((1, GLA_DK), lambda b, h, c: (0, h)),
                  pl.BlockSpec((1, GLA_DV), lambda b, h, c: (0, 0)),
                  pl.BlockSpec((None, None, GLA_DK, GLA_DV), lambda b, h, c: (b, h, 0, 0))],
        out_specs=[pl.BlockSpec((None, Tc, GLA_DV), lambda b, h, c: (b, c, h)),
                   pl.BlockSpec((None, None, GLA_DK, GLA_DV), lambda b, h, c: (b, h, 0, 0))],
        out_shape=[jax.ShapeDtypeStruct((B, T, H * GLA_DV), jnp.bfloat16),
                   jax.ShapeDtypeStruct((B, H, GLA_DK, GLA_DV), jnp.float32)],
        scratch_shapes=[pltpu.VMEM((GLA_DK, GLA_DV), jnp.float32)],
        compiler_params=pltpu.CompilerParams(dimension_semantics=("parallel", "parallel", "arbitrary"),
                                             vmem_limit_bytes=VMEM_LIMIT),
        name="gla_chunked",
    )(hmix, hmix, hmix, hmix, hmix, wa, ba.reshape(1, -1), gnorm.reshape(1, -1), s0)
    return o[:, :n_tok], s


def _head_sums(x, bd):
    f32, bf16 = jnp.float32, jnp.bfloat16
    outs = []
    for t in range(x.shape[1] // LANE):
        p = x[:, t * LANE:(t + 1) * LANE]
        hi = p.astype(bf16)
        lo = (p - hi.astype(f32)).astype(bf16)
        outs.append(jnp.dot(hi, bd, preferred_element_type=f32) + jnp.dot(lo, bd, preferred_element_type=f32))
    return jnp.concatenate(outs, axis=1)


def _block_diag_ones():
    rr = lax.broadcasted_iota(jnp.int32, (LANE, LANE), 0) // RWKV_N
    cc = lax.broadcasted_iota(jnp.int32, (LANE, LANE), 1) // RWKV_N
    return jnp.where(rr == cc, 1.0, 0.0).astype(jnp.bfloat16)


def _rwkv_prep_kernel(x_ref, sh_ref, mu_ref, w0_ref, a0_ref, kkw_ref, ka_ref, rk_ref, w2_ref, a2_ref, g2_ref,
                      r_ref, wl_ref, k_ref, v_ref, kk_ref, a_ref, gate_ref, bonus_ref, tail_ref, carry_scr, *, tm, T):
    f32, bf16 = jnp.float32, jnp.bfloat16
    W = RWKV_WIDTH
    i = pl.program_id(1)
    x = x_ref[...]
    if T == 1:
        prev = sh_ref[...]
        tail_ref[...] = x
    else:
        first = jnp.where(i == 0, sh_ref[SUBLANE - 1:SUBLANE], carry_scr[SUBLANE - 1:SUBLANE])
        row = lax.broadcasted_iota(jnp.int32, x.shape, 0)
        prev = jnp.where(row == 0, first, pltpu.roll(x, 1, axis=0))
        last = x[tm - SUBLANE:tm]
        carry_scr[...] = last
        tail_ref[...] = last
    rm = x + (prev - x) * mu_ref[...]
    r, k, v = rm[:, :W], rm[:, W:2 * W], rm[:, 2 * W:3 * W]
    lo = rm[:, 3 * W:3 * W + LANE]
    glo = rm[:, 3 * W + LANE:]
    w_raw = w0_ref[...] + jnp.dot(jnp.tanh(lo).astype(bf16), w2_ref[...], preferred_element_type=f32)
    wl_ref[...] = -jnp.exp(-jax.nn.softplus(-w_raw) - 0.5)
    a = jax.nn.sigmoid(a0_ref[...] + jnp.dot(lo.astype(bf16), a2_ref[...], preferred_element_type=f32))
    gate_ref[...] = jnp.dot(jax.nn.sigmoid(glo).astype(bf16), g2_ref[...], preferred_element_type=f32)
    bd = _block_diag_ones()
    kk = k * kkw_ref[...]
    kk_ref[...] = kk * lax.rsqrt(jnp.maximum(_head_sums(kk * kk, bd), 1e-24))
    k2 = k * (1.0 + (a - 1.0) * ka_ref[...])
    bonus_ref[...] = _head_sums(r * k2 * rk_ref[...], bd) * v
    r_ref[...] = r
    k_ref[...] = k2
    v_ref[...] = v
    a_ref[...] = a


def rwkv_prep(hr, shift0, mu, w0, w2, a0, a2, g2, kkw, ka, rk):
    B, T, WP = hr.shape
    W = RWKV_WIDTH
    bf16 = jnp.bfloat16
    padc = lambda t: jnp.pad(t, ((0, 0), (0, WP - t.shape[1])))
    w2p = jnp.pad(w2, ((0, LANE - RWKV_DECAY_RANK), (0, 0))).astype(bf16)
    a2p = jnp.pad(a2, ((RWKV_DECAY_RANK, 0), (0, 0))).astype(bf16)
    gpad = WP - 3 * W - LANE
    g2p = jnp.pad(g2, ((0, gpad - RWKV_GATE_RANK), (0, 0))).astype(bf16)
    row = lambda t: t.reshape(1, -1)
    if T == 1:
        tm = 1
        sh = padc(shift0).reshape(B, 1, WP)
        sh_spec = pl.BlockSpec((None, 1, WP), lambda b, i: (b, 0, 0))
        tail_rows = 1
    else:
        tm = _pick(T, (256, 128, 64, 32, 16, 8))
        sh = jnp.broadcast_to(padc(shift0)[:, None, :], (B, SUBLANE, WP))
        sh_spec = pl.BlockSpec((None, SUBLANE, WP), lambda b, i: (b, 0, 0))
        tail_rows = SUBLANE
    vec = lambda n: pl.BlockSpec((1, n), lambda b, i: (0, 0))
    mat = lambda m: pl.BlockSpec(m.shape, lambda b, i: (0, 0))
    o_spec = pl.BlockSpec((None, tm, W), lambda b, i: (b, i, 0))
    o_shape = jax.ShapeDtypeStruct((B, T, W), jnp.float32)
    outs = pl.pallas_call(
        functools.partial(_rwkv_prep_kernel, tm=tm, T=T),
        grid=(B, T // tm),
        in_specs=[pl.BlockSpec((None, tm, WP), lambda b, i: (b, i, 0)), sh_spec, vec(WP),
                  vec(W), vec(W), vec(W), vec(W), vec(W), mat(w2p), mat(a2p), mat(g2p)],
        out_specs=[o_spec] * 8 + [pl.BlockSpec((None, tail_rows, WP), lambda b, i: (b, 0, 0))],
        out_shape=[o_shape] * 8 + [jax.ShapeDtypeStruct((B, tail_rows, WP), jnp.float32)],
        scratch_shapes=[pltpu.VMEM((SUBLANE, WP), jnp.float32)],
        compiler_params=pltpu.CompilerParams(dimension_semantics=("parallel", "arbitrary"),
                                             vmem_limit_bytes=VMEM_LIMIT),
        name="rwkv_prep",
    )(hr, sh, row(padc(mu.reshape(1, -1))), row(w0), row(a0), row(kkw), row(ka), row(rk), w2p, a2p, g2p)
    return outs[:8], outs[8][:, tail_rows - 1, :RWKV_COLS]


def _rwkv_post_kernel(y_ref, bonus_ref, gate_ref, lw_ref, lb_ref, o_ref):
    bd = _block_diag_ones()
    y = y_ref[...]
    d = y - _head_sums(y, bd) * (1.0 / RWKV_N)
    var = _head_sums(d * d, bd) * (1.0 / RWKV_N)
    yn = d * lax.rsqrt(var + RWKV_LN_EPS) * lw_ref[...] + lb_ref[...]
    o_ref[...] = ((yn + bonus_ref[...]) * gate_ref[...]).astype(o_ref.dtype)


def rwkv_post(y, bonus, gate, ln_w, ln_b):
    M, W = y.shape
    tm = _pick(M, (256, 128, 64, 32, 16, 8))
    spec = pl.BlockSpec((tm, W), lambda i: (i, 0))
    vec = pl.BlockSpec((1, W), lambda i: (0, 0))
    return pl.pallas_call(
        _rwkv_post_kernel,
        grid=(M // tm,),
        in_specs=[spec, spec, spec, vec, vec],
        out_specs=spec,
        out_shape=jax.ShapeDtypeStruct((M, W), jnp.bfloat16),
        compiler_params=pltpu.CompilerParams(dimension_semantics=("parallel",), vmem_limit_bytes=VMEM_LIMIT),
        name="rwkv_post",
    )(y, bonus, gate, ln_w.reshape(1, W), ln_b.reshape(1, W))


def _rwkv_kernel(r_ref, wl_ref, k_ref, v_ref, kk_ref, a_ref, s0_ref, y_ref, sout_ref, s_scr, *, NB, NP, Tc):
    c = pl.program_id(1)
    f32, bf16 = jnp.float32, jnp.bfloat16
    U = min(SUBLANE, Tc)

    @pl.when(c == 0)
    def _():
        s_scr[...] = s0_ref[...]

    sub = lax.broadcasted_iota(jnp.int32, (RWKV_N, LANE), 0)
    lane = lax.broadcasted_iota(jnp.int32, (RWKV_N, LANE), 1)
    eye2 = (lane % RWKV_N) == sub
    left = lane < RWKV_N
    rr = lax.broadcasted_iota(jnp.int32, (LANE, LANE), 0) // RWKV_N
    cc = lax.broadcasted_iota(jnp.int32, (LANE, LANE), 1) // RWKV_N
    bd = jnp.where(rr == cc, 1.0, 0.0).astype(bf16)
    bd2 = jnp.concatenate([bd, bd], axis=0)

    def ssb(p, two_piece=True):
        hi = p.astype(bf16)
        if not two_piece:
            return jnp.dot(hi, bd, preferred_element_type=f32)
        lo = (p - hi.astype(f32)).astype(bf16)
        return jnp.dot(jnp.concatenate([hi, lo], axis=1), bd2, preferred_element_type=f32)

    eye_all = jnp.concatenate([eye2] * NP, axis=0)

    def bcast(x8, s):
        return jnp.concatenate(
            [jnp.broadcast_to(x8[s:s + 1, p * LANE:(p + 1) * LANE], (RWKV_N, LANE)) for p in range(NP)], axis=0)

    def vcols(vt, v8, s):
        if vt is None:
            return ssb(jnp.where(eye_all, bcast(v8, s), 0.0))
        return jnp.concatenate(
            [jnp.where(left, jnp.broadcast_to(vt[p][:RWKV_N, s:s + 1], (RWKV_N, LANE)),
                       jnp.broadcast_to(vt[p][RWKV_N:, s:s + 1], (RWKV_N, LANE))) for p in range(NP)], axis=0)

    def body(g, carry):
        rows = pl.ds(pl.multiple_of(g * U, U), U)
        tiles = []
        for nb in range(NB):
            kk8 = kk_ref[nb, rows, :]
            v8 = v_ref[nb, rows, :]
            vt = [v8[:, p * LANE:(p + 1) * LANE].T for p in range(NP)] if U == SUBLANE else None
            tiles.append(dict(r=r_ref[nb, rows, :], w=jnp.exp(wl_ref[nb, rows, :]), k=k_ref[nb, rows, :], v=v8, vt=vt,
                              ka=kk8 * a_ref[nb, rows, :], nk=-kk8))
        S = [s_scr[nb] for nb in range(NB)]
        ys = [[] for _ in range(NB)]
        for s in range(U):
            sa = [ssb(S[nb] * bcast(t['nk'], s)) for nb, t in enumerate(tiles)]
            for nb, t in enumerate(tiles):
                S[nb] = S[nb] * bcast(t['w'], s) + sa[nb] * bcast(t['ka'], s) + vcols(t['vt'], t['v'], s) * bcast(t['k'], s)
            yb = [jnp.where(eye_all, ssb(S[nb] * bcast(t['r'], s), two_piece=False), 0.0) for nb, t in enumerate(tiles)]
            for nb in range(NB):
                ys[nb].append(jnp.concatenate(
                    [jnp.sum(yb[nb][p * RWKV_N:(p + 1) * RWKV_N], axis=0, keepdims=True) for p in range(NP)], axis=1))
        for nb in range(NB):
            s_scr[nb] = S[nb]
            y_ref[nb, rows, :] = ys[nb][0] if U == 1 else jnp.concatenate(ys[nb], axis=0)
        return carry

    lax.fori_loop(0, Tc // U, body, 0)

    @pl.when(c == pl.num_programs(1) - 1)
    def _():
        sout_ref[...] = s_scr[...]


def rwkv_scan_pallas(r, w_log, k, v, kk, a, s0):
    B, T, W = r.shape
    H = W // RWKV_N
    NP = H // 2
    NB = RWKV_NB if B % RWKV_NB == 0 else 1
    Tc = 128 if T % 128 == 0 else T
    s0p = s0.reshape(B, NP, 2, RWKV_N, RWKV_N).transpose(0, 1, 3, 2, 4).reshape(B, NP * RWKV_N, LANE)
    blk = pl.BlockSpec((NB, Tc, W), lambda b, c: (b, c, 0))
    sblk = pl.BlockSpec((NB, NP * RWKV_N, LANE), lambda b, c: (b, 0, 0))
    y, sp = pl.pallas_call(
        functools.partial(_rwkv_kernel, NB=NB, NP=NP, Tc=Tc),
        grid=(B // NB, T // Tc),
        in_specs=[blk] * 6 + [sblk],
        out_specs=[blk, sblk],
        out_shape=[jax.ShapeDtypeStruct((B, T, W), jnp.float32),
                   jax.ShapeDtypeStruct((B, NP * RWKV_N, LANE), jnp.float32)],
        scratch_shapes=[pltpu.VMEM((NB, NP * RWKV_N, LANE), jnp.float32)],
        compiler_params=pltpu.CompilerParams(dimension_semantics=("parallel", "arbitrary"),
                                             vmem_limit_bytes=VMEM_LIMIT),
        name="rwkv7_scan",
    )(r, w_log, k, v, kk, a, s0p)
    s_fin = sp.reshape(B, NP, RWKV_N, 2, RWKV_N).transpose(0, 1, 3, 2, 4).reshape(B, H, RWKV_N, RWKV_N)
    return y, s_fin


def rope_tables(pos):
    half = ROPE_DIM // 2
    inv = ROPE_THETA ** (-jnp.arange(half, dtype=jnp.float32) / half)
    ang = pos.astype(jnp.float32)[:, None] * inv[None, :]
    cos, sin = jnp.cos(ang), jnp.sin(ang)
    T = pos.shape[0]
    z = jnp.zeros((T, HEAD_DIM - ROPE_DIM), jnp.float32)
    zh = jnp.zeros((T, half), jnp.float32)
    c = jnp.concatenate([cos, cos, jnp.ones_like(z)], axis=1)
    s_up = jnp.concatenate([-sin, zh, z], axis=1)
    s_dn = jnp.concatenate([zh, sin, z], axis=1)
    return c, s_up, s_dn


def _nsa_prep_kernel(q_ref, c_ref, s_ref, w_ref, tc_ref, tu_ref, td_ref, qo_ref, co_ref, so_ref, wo_ref):
    c, su, sd = tc_ref[...], tu_ref[...], td_ref[...]
    half = ROPE_DIM // 2

    def rot(x):
        return x * c + pltpu.roll(x, HEAD_DIM - half, axis=1) * su + pltpu.roll(x, half, axis=1) * sd

    for h in range(NSA_HEADS):
        cols = slice(h * HEAD_DIM, (h + 1) * HEAD_DIM)
        qo_ref[:, cols] = (rot(q_ref[:, cols]) * HEAD_DIM ** -0.5).astype(qo_ref.dtype)
    for src, dst in ((c_ref, co_ref), (s_ref, so_ref), (w_ref, wo_ref)):
        for g in range(NSA_KV_HEADS):
            cols = slice(g * HEAD_DIM, (g + 1) * HEAD_DIM)
            dst[:, cols] = rot(src[:, cols])
        dst[:, NSA_KV_WIDTH:] = src[:, NSA_KV_WIDTH:]


def nsa_prep(hn, pos):
    B, T, _ = hn.shape
    tm = _pick(T, (256, 128, 64, 32, 16, 8))
    tabs = rope_tables(pos)
    kvw = 2 * NSA_KV_WIDTH
    q_spec = pl.BlockSpec((None, tm, NSA_WIDTH), lambda b, i: (b, i, 0))
    kv_spec = lambda n: pl.BlockSpec((None, tm, kvw), lambda b, i: (b, i, NSA_WIDTH // kvw + n))
    t_spec = pl.BlockSpec((tm, HEAD_DIM), lambda b, i: (i, 0))
    o_spec = pl.BlockSpec((None, tm, kvw), lambda b, i: (b, i, 0))
    kv_shape = jax.ShapeDtypeStruct((B, T, kvw), jnp.float32)
    return pl.pallas_call(
        _nsa_prep_kernel,
        grid=(B, T // tm),
        in_specs=[q_spec, kv_spec(0), kv_spec(1), kv_spec(2), t_spec, t_spec, t_spec],
        out_specs=[q_spec, o_spec, o_spec, o_spec],
        out_shape=[jax.ShapeDtypeStruct((B, T, NSA_WIDTH), jnp.bfloat16), kv_shape, kv_shape, kv_shape],
        compiler_params=pltpu.CompilerParams(dimension_semantics=("parallel", "parallel"),
                                             vmem_limit_bytes=VMEM_LIMIT),
        name="nsa_prep",
    )(hn, hn, hn, hn, *tabs)


def _compress_kernel(x_ref, w1_ref, w2_ref, pe_ref, ko_ref, vo_ref, *, ns):
    f32, bf16 = jnp.float32, jnp.bfloat16
    G, HD = NSA_KV_HEADS, HEAD_DIM
    row_w = 2 * NSA_KV_WIDTH
    for kv, o_ref in ((0, ko_ref), (1, vo_ref)):
        pos = jnp.zeros((SUBLANE, HD), f32)
        for p in range(CMP_STRIDE):
            w = w1_ref[kv, p]
            lo = jnp.broadcast_to(pe_ref[kv, p:p + 1, :], (SUBLANE, HD)).astype(bf16)
            hi = jnp.broadcast_to(pe_ref[kv, CMP_STRIDE + p:CMP_STRIDE + p + 1, :], (SUBLANE, HD)).astype(bf16)
            pos = pos + jnp.dot(lo, w, preferred_element_type=f32)[:, :HD] + jnp.dot(hi, w, preferred_element_type=f32)[:, HD:]
        pos = pos[0:1]
        for g in range(G):
            acc = jnp.zeros((ns, 2 * HD), f32)
            for p in range(CMP_STRIDE):
                c0 = p * row_w + kv * NSA_KV_WIDTH + g * HD
                acc = acc + jnp.dot(x_ref[:, c0:c0 + HD].astype(bf16), w1_ref[kv, p], preferred_element_type=f32)
            nxt = pltpu.roll(acc[:, HD:], ns - 1, axis=0)
            hid = jax.nn.gelu(acc[:, :HD] + nxt + pos)
            o_ref[g] = jnp.dot(hid.astype(bf16), w2_ref[kv], preferred_element_type=f32).astype(o_ref.dtype)


def compress_weights(w1k, w2k, pek, w1v, w2v, pev):
    bf16 = jnp.bfloat16
    cat = lambda w1: jnp.concatenate([w1[:CMP_STRIDE], w1[CMP_STRIDE:]], axis=-1)
    return (jnp.stack([cat(w1k), cat(w1v)]).astype(bf16), jnp.stack([w2k, w2v]).astype(bf16), jnp.stack([pek, pev]))


def compress_pallas(kv_rows, w1, w2, pe):
    B, T, W = kv_rows.shape
    ns = T // CMP_STRIDE
    bf16 = jnp.bfloat16
    x = kv_rows.reshape(B, ns, CMP_STRIDE * W)
    out = jax.ShapeDtypeStruct((B, NSA_KV_HEADS, ns, HEAD_DIM), bf16)
    o_spec = pl.BlockSpec((None, NSA_KV_HEADS, ns, HEAD_DIM), lambda b: (b, 0, 0, 0))
    return pl.pallas_call(
        functools.partial(_compress_kernel, ns=ns),
        grid=(B,),
        in_specs=[pl.BlockSpec((None, ns, CMP_STRIDE * W), lambda b: (b, 0, 0)),
                  pl.BlockSpec(w1.shape, lambda b: (0, 0, 0, 0)),
                  pl.BlockSpec(w2.shape, lambda b: (0, 0, 0)),
                  pl.BlockSpec(pe.shape, lambda b: (0, 0, 0))],
        out_specs=[o_spec, o_spec],
        out_shape=[out, out],
        compiler_params=pltpu.CompilerParams(dimension_semantics=("parallel",), vmem_limit_bytes=VMEM_LIMIT),
        name="nsa_compress",
    )(x, w1, w2, pe)


def _dot_nt(a, b):
    return lax.dot_general(a, b, (((1,), (1,)), ((), ())), preferred_element_type=jnp.float32)


def _nsa_kernel(q_ref, kc_ref, vc_ref, ks_ref, vs_ref, kw_ref, vw_ref, g_ref, covt_ref, e_ref, o_ref,
                bias_scr, p4_scr, ocmp_scr, m_scr, acc_scr, *, TQ, TK, T, NS, NCP, n_top):
    f32, bf16 = jnp.float32, jnp.bfloat16
    R = NSA_GROUP
    i = pl.program_id(2)
    nchunk = T // TK
    qpos_col = i * TQ + lax.broadcasted_iota(jnp.int32, (TQ, 1), 0)

    kc = kc_ref[...]
    vc = vc_ref[...]
    cend = lax.broadcasted_iota(jnp.int32, (1, NCP), 1) * CMP_STRIDE + (CMP_BLOCK - 1)
    valid = cend <= qpos_col
    for r in range(R):
        s = _dot_nt(q_ref[:, r * HEAD_DIM:(r + 1) * HEAD_DIM], kc)
        s = jnp.where(valid, s, NEG_INF)
        m = jnp.max(s, axis=-1, keepdims=True)
        p = jnp.where(valid, jnp.exp(s - m), 0.0)
        l = jnp.sum(p, axis=-1, keepdims=True)
        p = (p / jnp.where(l > 0.0, l, 1.0)).astype(bf16)
        p4_scr[:, r * NCP:(r + 1) * NCP] = p
        ocmp_scr[r] = jnp.dot(p, vc, preferred_element_type=f32)

    imp_t = _dot_nt(covt_ref[...], p4_scr[...])
    j = lax.broadcasted_iota(jnp.int32, (NS, TQ), 0)
    qblk = (i * TQ + lax.broadcasted_iota(jnp.int32, (NS, TQ), 1)) // SEL_BLOCK
    forced = (j < N_INIT_BLOCKS) | ((j <= qblk) & (j > qblk - N_LOCAL_BLOCKS))
    score = jnp.where(forced, FORCE_SCORE, jnp.where(j <= qblk, imp_t, NEG_INF))
    rank = jnp.zeros((NS, TQ), f32)
    for a in range(NS):
        row = score[a:a + 1, :]
        beats = (row > score) | ((row == score) & (a < j))
        rank = rank + jnp.where(beats, 1.0, 0.0)
    sel_t = jnp.where(rank < n_top, 1.0, 0.0)
    if NS < LANE:
        sel_t = jnp.concatenate([sel_t, jnp.zeros((LANE - NS, TQ), f32)], axis=0)
    sel = sel_t.T.astype(bf16)
    for c in range(nchunk):
        selexp = jnp.dot(sel, e_ref[:, c * TK:(c + 1) * TK], preferred_element_type=f32)
        kpos = c * TK + lax.broadcasted_iota(jnp.int32, (TQ, TK), 1)
        bias_scr[c] = jnp.where((selexp > 0.5) & (kpos <= qpos_col), 0.0, NEG_INF)

    hi = (i * TQ + TQ - 1) // TK + 1

    def attend(k_ref, v_ref, lo, masker):
        m_scr[...] = jnp.full(m_scr.shape, NEG_INF, f32)
        acc_scr[...] = jnp.zeros(acc_scr.shape, f32)

        def chunk(c, carry):
            rows = pl.ds(pl.multiple_of(c * TK, TK), TK)
            k = k_ref[rows, :].astype(bf16)
            v = jnp.concatenate([v_ref[rows, :].astype(bf16), jnp.ones((TK, HEAD_DIM), bf16)], axis=1)
            mk = masker(c)
            heads = range(R)
            sk = [mk(_dot_nt(q_ref[:, r * HEAD_DIM:(r + 1) * HEAD_DIM], k)) for r in heads]
            m_prev = [m_scr[r] for r in heads]
            m_new = [jnp.maximum(m_prev[r], jnp.max(sk[r][0], axis=-1, keepdims=True)) for r in heads]
            alpha = [jnp.exp(m_prev[r] - m_new[r]) for r in heads]
            ps = [jnp.exp(sk[r][0] - m_new[r]) for r in heads]
            ps = [p if sk[r][1] is None else jnp.where(sk[r][1], p, 0.0) for r, p in enumerate(ps)]
            pv = [jnp.dot(ps[r].astype(bf16), v, preferred_element_type=f32) for r in heads]
            for r in heads:
                acc_scr[r] = alpha[r] * acc_scr[r] + pv[r]
                m_scr[r] = m_new[r]
            return carry

        lax.fori_loop(lo, hi, chunk, 0)

    def normalised(r):
        return acc_scr[r, :, :HEAD_DIM] / acc_scr[r, :, HEAD_DIM:]

    def sel_masker(c):
        b = bias_scr[c]
        return lambda s: (s + b, None)

    attend(ks_ref, vs_ref, 0, sel_masker)
    g = pltpu.roll(jax.nn.sigmoid(g_ref[...]), (LANE - 3 * R * pl.program_id(1)) % LANE, axis=1)
    for r in range(R):
        ocmp_scr[r] = (g[:, 3 * r:3 * r + 1] * ocmp_scr[r]
                       + g[:, 3 * r + 1:3 * r + 2] * normalised(r))

    def win_masker(c):
        rel = (i * TQ - c * TK + lax.broadcasted_iota(jnp.int32, (TQ, TK), 0)
               - lax.broadcasted_iota(jnp.int32, (TQ, TK), 1))
        ok = (rel >= 0) & (rel < WINDOW)
        return lambda s: (jnp.where(ok, s, NEG_INF), ok)

    attend(kw_ref, vw_ref, jnp.maximum(i * TQ - (WINDOW - 1), 0) // TK, win_masker)
    for r in range(R):
        o = ocmp_scr[r] + g[:, 3 * r + 2:3 * r + 3] * normalised(r)
        o_ref[:, r * HEAD_DIM:(r + 1) * HEAD_DIM] = o.astype(o_ref.dtype)


def nsa_attention_pallas(qr, kcmp, vcmp, slc, win, hn, gate_col):
    B, T, _ = qr.shape
    G, R = NSA_KV_HEADS, NSA_GROUP
    TQ = min(NSA_TQ, T)
    TK = min(NSA_TK, T)
    NS = T // SEL_BLOCK
    NC = T // CMP_STRIDE - 1
    NCP = kcmp.shape[2]
    n_top = min(SEL_TOP, NS)
    ci = np.arange(NCP)[:, None] * CMP_STRIDE
    sj = np.arange(NS)[None, :] * SEL_BLOCK
    cover = np.clip(np.minimum(ci + CMP_BLOCK, sj + SEL_BLOCK) - np.maximum(ci, sj), 0, None) / CMP_BLOCK
    cover[NC:] = 0.0
    covt = jnp.asarray(np.tile(cover.T, (1, R)), jnp.bfloat16)
    e = jnp.asarray((np.arange(T)[None, :] // SEL_BLOCK) == np.arange(LANE)[:, None], jnp.bfloat16)
    k_spec = pl.BlockSpec((None, T, HEAD_DIM), lambda b, g, i: (b, 0, g))
    v_spec = pl.BlockSpec((None, T, HEAD_DIM), lambda b, g, i: (b, 0, G + g))
    cmp_spec = pl.BlockSpec((None, None, NCP, HEAD_DIM), lambda b, g, i: (b, g, 0, 0))
    return pl.pallas_call(
        functools.partial(_nsa_kernel, TQ=TQ, TK=TK, T=T, NS=NS, NCP=NCP, n_top=n_top),
        grid=(B, G, T // TQ),
        in_specs=[pl.BlockSpec((None, TQ, R * HEAD_DIM), lambda b, g, i: (b, i, g)),
                  cmp_spec, cmp_spec, k_spec, v_spec, k_spec, v_spec,
                  pl.BlockSpec((None, TQ, LANE), lambda b, g, i: (b, i, gate_col // LANE)),
                  pl.BlockSpec((NS, R * NCP), lambda b, g, i: (0, 0)),
                  pl.BlockSpec((LANE, T), lambda b, g, i: (0, 0))],
        out_specs=pl.BlockSpec((None, TQ, R * HEAD_DIM), lambda b, g, i: (b, i, g)),
        out_shape=jax.ShapeDtypeStruct((B, T, G * R * HEAD_DIM), jnp.bfloat16),
        scratch_shapes=[pltpu.VMEM((T // TK, TQ, TK), jnp.float32),
                        pltpu.VMEM((TQ, R * NCP), jnp.bfloat16),
                        pltpu.VMEM((R, TQ, HEAD_DIM), jnp.float32),
                        pltpu.VMEM((R, TQ, 1), jnp.float32),
                        pltpu.VMEM((R, TQ, 2 * HEAD_DIM), jnp.float32)],
        compiler_params=pltpu.CompilerParams(dimension_semantics=("parallel", "parallel", "arbitrary"),
                                             vmem_limit_bytes=VMEM_LIMIT),
        name="nsa_attention",
    )(qr, kcmp, vcmp, slc, slc, win, win, hn, covt, e)


def _dec_compress_kernel(pt_ref, *refs):
    f32, bf16 = jnp.float32, jnp.bfloat16
    pages, w1_ref, o_ref, reg_scr = refs[:DEC_PAGES], refs[DEC_PAGES], refs[DEC_PAGES + 1], refs[DEC_PAGES + 2]
    HD = HEAD_DIM
    nc = 2 * NSA_KV_HEADS
    seg = pages[0].shape[0] // (nc * CMP_STRIDE)
    acc = None
    for p in range(CMP_STRIDE):
        lhs = jnp.concatenate([pg[(CMP_STRIDE * n + p) * nc:(CMP_STRIDE * n + p + 1) * nc, :]
                               for pg in pages for n in range(seg)], axis=0).astype(bf16)
        w = jnp.concatenate([w1_ref[0, p], w1_ref[1, p]], axis=1)
        part = jnp.dot(lhs, w, preferred_element_type=f32)
        acc = part if acc is None else acc + part
    is_k = (lax.broadcasted_iota(jnp.int32, (acc.shape[0], 2 * HD), 0) % nc) < NSA_KV_HEADS
    fs = jnp.where(is_k, acc[:, :2 * HD], acc[:, 2 * HD:])
    reg_scr[0] = fs[:, :HD]
    reg_scr[1] = fs[:, HD:]
    nseg = DEC_PAGES * seg
    for c in range(nc):
        o_ref[:, c * 2 * HD:c * 2 * HD + HD] = reg_scr[0, pl.ds(c, nseg, stride=nc), :]
        o_ref[:, c * 2 * HD + HD:(c + 1) * 2 * HD] = reg_scr[1, pl.ds(c, nseg, stride=nc), :]


def dec_compress(cache, layer, page_table, w1):
    L, n_phys, page = cache.shape[:3]
    nc = 2 * NSA_KV_HEADS
    B, n_pages = page_table.shape
    seg = page // CMP_STRIDE
    steps = n_pages // DEC_PAGES
    c4 = cache.reshape(L, n_phys, page * nc, HEAD_DIM)
    page_spec = lambda k: pl.BlockSpec((None, None, page * nc, HEAD_DIM),
                                       lambda b, s, pt: (layer, pt[b, s * DEC_PAGES + k], 0, 0))
    ow = nc * 2 * HEAD_DIM
    return pl.pallas_call(
        _dec_compress_kernel,
        grid_spec=pltpu.PrefetchScalarGridSpec(
            num_scalar_prefetch=1,
            grid=(B, steps),
            in_specs=[page_spec(k) for k in range(DEC_PAGES)] + [pl.BlockSpec(w1.shape, lambda b, s, pt: (0, 0, 0, 0))],
            out_specs=pl.BlockSpec((None, DEC_PAGES * seg, ow), lambda b, s, pt: (b, s, 0)),
            scratch_shapes=[pltpu.VMEM((2, DEC_PAGES * seg * nc, HEAD_DIM), jnp.float32)]),
        out_shape=jax.ShapeDtypeStruct((B, n_pages * seg, ow), jnp.float32),
        compiler_params=pltpu.CompilerParams(dimension_semantics=("parallel", "arbitrary"),
                                             vmem_limit_bytes=VMEM_LIMIT),
        name="nsa_dec_compress",
    )(page_table, *([c4] * DEC_PAGES), w1)


def _dec_select_kernel(q_ref, fk_ref, fv_ref, w1_ref, w2_ref, pe_ref, covt_ref, ocmp_ref, idx_ref, *, NSEG, NS, NSP, n_top):
    f32, bf16 = jnp.float32, jnp.bfloat16
    HD, R = HEAD_DIM, NSA_GROUP
    NC = NSEG - 1

    def compressed(kv, f_ref):
        pos = jnp.zeros((SUBLANE, HD), f32)
        for p in range(CMP_STRIDE):
            w = w1_ref[kv, p]
            lo = jnp.broadcast_to(pe_ref[kv, p:p + 1, :], (SUBLANE, HD)).astype(bf16)
            hi = jnp.broadcast_to(pe_ref[kv, CMP_STRIDE + p:CMP_STRIDE + p + 1, :], (SUBLANE, HD)).astype(bf16)
            pos = pos + jnp.dot(lo, w, preferred_element_type=f32)[:, :HD] + jnp.dot(hi, w, preferred_element_type=f32)[:, HD:]
        nxt = pltpu.roll(f_ref[:, HD:], NSEG - 1, axis=0)
        hid = jax.nn.gelu(f_ref[:, :HD] + nxt + pos[0:1])
        return jnp.dot(hid.astype(bf16), w2_ref[kv], preferred_element_type=f32).astype(bf16)

    kc = compressed(0, fk_ref)
    vc = compressed(1, fv_ref)
    q = q_ref[...]
    q4 = jnp.concatenate([q[:, r * HD:(r + 1) * HD] for r in range(R)] + [jnp.zeros((SUBLANE - R, HD), bf16)], axis=0)
    s = _dot_nt(q4, kc)
    valid = lax.broadcasted_iota(jnp.int32, s.shape, 1) < NC
    s = jnp.where(valid, s, NEG_INF)
    p = jnp.where(valid, jnp.exp(s - jnp.max(s, axis=-1, keepdims=True)), 0.0)
    p = (p / jnp.sum(p, axis=-1, keepdims=True)).astype(bf16)
    ocmp_ref[...] = jnp.dot(p, vc, preferred_element_type=f32)
    head = lax.broadcasted_iota(jnp.int32, p.shape, 0) < R
    imp = _dot_nt(covt_ref[...], jnp.where(head, p, jnp.zeros_like(p)))
    imp = jnp.sum(imp, axis=1, keepdims=True)
    j_col = lax.broadcasted_iota(jnp.int32, (NSP, 1), 0)
    qblk = NS - 1
    forced = (j_col < N_INIT_BLOCKS) | ((j_col <= qblk) & (j_col > qblk - N_LOCAL_BLOCKS))
    score_col = jnp.where(forced, FORCE_SCORE, jnp.where(j_col <= qblk, imp, -3e38))
    score_cb = jnp.broadcast_to(score_col, (NSP, LANE))
    score_row = score_cb.T[0:1, :]
    ii = lax.broadcasted_iota(jnp.int32, (NSP, NSP), 0)
    jj = lax.broadcasted_iota(jnp.int32, (NSP, NSP), 1)
    beats = (score_col > score_row) | ((score_col == score_row) & (ii < jj))
    rank = jnp.sum(jnp.where(beats, 1.0, 0.0), axis=0, keepdims=True)
    t_col = lax.broadcasted_iota(jnp.int32, (n_top, NSP), 0).astype(f32)
    j_row = lax.broadcasted_iota(jnp.int32, (n_top, NSP), 1).astype(f32)
    ids = jnp.sum(jnp.where(rank == t_col, j_row, 0.0), axis=1, keepdims=True)
    idx_ref[...] = jnp.broadcast_to(ids, (n_top, LANE)).astype(jnp.int32)


def dec_select(qr, fs, w1, w2, pe, past_len):
    B = qr.shape[0]
    G, R, HD = NSA_KV_HEADS, NSA_GROUP, HEAD_DIM
    NSEG = fs.shape[1]
    NC = NSEG - 1
    NS = -(-(past_len + 1) // SEL_BLOCK)
    NSP = _round_up(NS, LANE)
    n_top = min(SEL_TOP, NS)
    ci = np.arange(NSEG)[:, None] * CMP_STRIDE
    sj = np.arange(NSP)[None, :] * SEL_BLOCK
    cover = np.clip(np.minimum(ci + CMP_BLOCK, sj + SEL_BLOCK) - np.maximum(ci, sj), 0, None) / CMP_BLOCK
    cover[NC:] = 0.0
    cover[:, NS:] = 0.0
    covt = jnp.asarray(cover.T, jnp.bfloat16)
    f_spec = lambda kv: pl.BlockSpec((None, NSEG, 2 * HD), lambda b, g: (b, 0, kv * G + g))
    full = lambda a: pl.BlockSpec(a.shape, lambda b, g: (0,) * a.ndim)
    ocmp, idx = pl.pallas_call(
        functools.partial(_dec_select_kernel, NSEG=NSEG, NS=NS, NSP=NSP, n_top=n_top),
        grid=(B, G),
        in_specs=[pl.BlockSpec((None, 1, R * HD), lambda b, g: (b, 0, g)), f_spec(0), f_spec(1),
                  full(w1), full(w2), full(pe), full(covt)],
        out_specs=[pl.BlockSpec((None, None, SUBLANE, HD), lambda b, g: (b, g, 0, 0)),
                   pl.BlockSpec((None, None, n_top, LANE), lambda b, g: (b, g, 0, 0))],
        out_shape=[jax.ShapeDtypeStruct((B, G, SUBLANE, HD), jnp.float32),
                   jax.ShapeDtypeStruct((B, G, n_top, LANE), jnp.int32)],
        compiler_params=pltpu.CompilerParams(dimension_semantics=("parallel", "parallel"),
                                             vmem_limit_bytes=VMEM_LIMIT),
        name="nsa_dec_select",
    )(qr, fs, fs, w1, w2, pe, covt)
    return ocmp, idx[:, :, :, 0]


def _dec_attend_kernel(pt_ref, idx_ref, q_ref, *refs, NS, n_top):
    f32, bf16 = jnp.float32, jnp.bfloat16
    HD, R, G = HEAD_DIM, NSA_GROUP, NSA_KV_HEADS
    blocks = refs[:n_top]
    nks_ref, nvs_ref, wb_ref, nkw_ref, nvw_ref, ocmp_ref, g_ref, o_ref = refs[n_top:]
    b, g = pl.program_id(0), pl.program_id(1)
    q = q_ref[...]
    q4 = jnp.concatenate([q[:, r * HD:(r + 1) * HD] for r in range(R)] + [jnp.zeros((SUBLANE - R, HD), bf16)], axis=0)

    def head_rows(ref, kv):
        return ref[pl.ds(kv * G + g, ref.shape[0] // (2 * G), stride=2 * G), :].astype(bf16)

    def attend(keys, vals, bias, k_new_ref, v_new_ref):
        k_new = jnp.broadcast_to(k_new_ref[...], (SUBLANE, HD)).astype(bf16).astype(f32)
        v_new = jnp.broadcast_to(v_new_ref[...], (SUBLANE, HD)).astype(bf16).astype(f32)
        s = _dot_nt(q4, keys) + bias
        s_new = jnp.sum(q4.astype(f32) * k_new, axis=-1, keepdims=True)
        m = jnp.maximum(jnp.max(s, axis=-1, keepdims=True), s_new)
        p = jnp.exp(s - m)
        p_new = jnp.exp(s_new - m)
        l = jnp.sum(p, axis=-1, keepdims=True) + p_new
        return (jnp.dot(p.astype(bf16), vals, preferred_element_type=f32) + p_new.astype(bf16).astype(f32) * v_new) / l

    keys = jnp.concatenate([head_rows(r, 0) for r in blocks], axis=0)
    vals = jnp.concatenate([head_rows(r, 1) for r in blocks], axis=0)
    bias = jnp.concatenate(
        [jnp.broadcast_to(jnp.where(idx_ref[b, g, t] != NS - 1, 0.0, NEG_INF), (SUBLANE, SEL_BLOCK)) for t in range(n_top)],
        axis=1)
    o_slc = attend(keys, vals, bias, nks_ref, nvs_ref)
    nwin = wb_ref.shape[0] // (2 * G)
    ok = lax.broadcasted_iota(jnp.int32, (SUBLANE, nwin), 1) > nwin - WINDOW
    o_win = attend(head_rows(wb_ref, 0), head_rows(wb_ref, 1), jnp.where(ok, 0.0, NEG_INF), nkw_ref, nvw_ref)
    gate = pltpu.roll(jax.nn.sigmoid(jnp.broadcast_to(g_ref[...], (SUBLANE, LANE))), (LANE - 3 * R * g) % LANE, axis=1)
    rows = []
    for r in range(R):
        rows.append(gate[r:r + 1, 3 * r:3 * r + 1] * ocmp_ref[r:r + 1, :] + gate[r:r + 1, 3 * r + 1:3 * r + 2] * o_slc[r:r + 1, :]
                    + gate[r:r + 1, 3 * r + 2:3 * r + 3] * o_win[r:r + 1, :])
    o_ref[...] = jnp.concatenate(rows, axis=1).astype(o_ref.dtype)


def dec_attend(qr, slc_cache, win_cache, layer, page_table, idx, new_slc, new_win, ocmp, hn, gate_col):
    B = qr.shape[0]
    G, R, HD = NSA_KV_HEADS, NSA_GROUP, HEAD_DIM
    L, n_phys, page = slc_cache.shape[:3]
    nc = 2 * G
    n_top = idx.shape[2]
    NS = -(-(page_table.shape[1] * page + 1) // SEL_BLOCK)
    per = page // SEL_BLOCK
    blocks = slc_cache.reshape(L, n_phys * per, SEL_BLOCK * nc, HD)
    nwin = win_cache.shape[2]
    wins = win_cache.reshape(L, B, nwin * nc, HD)

    def blk_spec(t):
        def index(b, g, pt, ix):
            j = jnp.minimum(ix[b, g, t], NS - 2)
            return (layer, pt[b, j // per] * per + j % per, 0, 0)
        return pl.BlockSpec((None, None, SEL_BLOCK * nc, HD), index)

    row = lambda col: pl.BlockSpec((None, 1, HD), lambda b, g, pt, ix: (b, 0, col(g)))
    kcol, vcol = (lambda g: g), (lambda g: G + g)
    return pl.pallas_call(
        functools.partial(_dec_attend_kernel, NS=NS, n_top=n_top),
        grid_spec=pltpu.PrefetchScalarGridSpec(
            num_scalar_prefetch=2,
            grid=(B, G),
            in_specs=[pl.BlockSpec((None, 1, R * HD), lambda b, g, pt, ix: (b, 0, g))]
                     + [blk_spec(t) for t in range(n_top)]
                     + [row(kcol), row(vcol),
                        pl.BlockSpec((None, None, nwin * nc, HD), lambda b, g, pt, ix: (layer, b, 0, 0)),
                        row(kcol), row(vcol),
                        pl.BlockSpec((None, None, SUBLANE, HD), lambda b, g, pt, ix: (b, g, 0, 0)),
                        pl.BlockSpec((None, 1, LANE), lambda b, g, pt, ix: (b, 0, gate_col // LANE))],
            out_specs=pl.BlockSpec((None, 1, R * HD), lambda b, g, pt, ix: (b, 0, g))),
        out_shape=jax.ShapeDtypeStruct((B, 1, G * R * HD), jnp.bfloat16),
        compiler_params=pltpu.CompilerParams(dimension_semantics=("parallel", "parallel"),
                                             vmem_limit_bytes=VMEM_LIMIT),
        name="nsa_dec_attend",
    )(page_table, idx, qr, *([blocks] * n_top), new_slc, new_slc, wins, new_win, new_win, ocmp, hn)


def trunk_layer(x, pos0, paged, gla_s0, rwkv_s0, shift0, conv0, lw):
    B, T, _ = x.shape
    dt = x.dtype
    f32 = jnp.float32
    pos = pos0 + jnp.arange(T, dtype=jnp.int32)
    heads = lambda t, n: t.reshape(B, T, n, t.shape[-1] // n)

    bf = jnp.bfloat16
    M = B * T
    x2 = x.reshape(M, D_MODEL)
    xn = rmsnorm_pallas(x2, lw['norm1'])
    hg = mm(xn, lw['w_gla']).reshape(B, T, -1)
    hn = mm(xn, lw['w_nsa']).reshape(B, T, -1)
    hr = mm(xn, lw['w_rwkv']).reshape(B, T, -1)
    mg = mm(xn, lw['w_mg'])

    o_gla, gla_s = gla_pallas(hg, lw['gla_wa2'], lw['gla_ba'], lw['gla_norm'], gla_s0.astype(f32), GLA_OFF)
    o_gla = o_gla.reshape(M, GLA_WIDTH)

    kv5 = lambda t: t.reshape(B, T, 2, NSA_KV_HEADS, HEAD_DIM)
    cw1, cw2, cpe = compress_weights(lw['cmp_w1k'], lw['cmp_w2k'], lw['cmp_pek'],
                                     lw['cmp_w1v'], lw['cmp_w2v'], lw['cmp_pev'])
    if paged is None:
        assert T % NSA_TQ == 0
        qr, cmp2, slc2, win2 = nsa_prep(hn, pos)
        kcmp, vcmp = compress_pallas(cmp2, cw1, cw2, cpe)
        o_nsa = nsa_attention_pallas(qr, kcmp, vcmp, slc2, win2, hn, NSA_GATE_OFF)
        new_cmp, new_slc, win_new = kv5(cmp2), kv5(slc2), kv5(win2)[:, T - min(WINDOW, T):]
    else:
        assert T == 1
        cache_cmp, cache_slc, cache_win, layer, page_table = paged
        qr, cmp2, slc2, win2 = [t.reshape(B, 1, -1) for t in
                                nsa_prep(hn.reshape(1, B, -1), jnp.full((B,), pos0, jnp.int32))]
        fs = dec_compress(cache_cmp, layer, page_table, cw1)
        ocmp, sel_ids = dec_select(qr, fs, cw1, cw2, cpe, pos0)
        o_nsa = dec_attend(qr, cache_slc, cache_win, layer, page_table, sel_ids, slc2, win2, ocmp, hn, NSA_GATE_OFF)
        new_cmp, new_slc = kv5(cmp2), kv5(slc2)
        win_all = jnp.concatenate([cache_win[layer].astype(dt), kv5(win2)], axis=1)
        win_new = win_all[:, win_all.shape[1] - min(WINDOW, win_all.shape[1]):]
    o_nsa = o_nsa.reshape(M, NSA_WIDTH)

    (r_, w_log, k2, v_, kk, a, gate, bonus), shift_new = rwkv_prep(
        hr, shift0.astype(f32), lw['rwkv_mu'], lw['rwkv_w0'], lw['rwkv_w2'], lw['rwkv_a0'], lw['rwkv_a2'],
        lw['rwkv_g2'], lw['rwkv_kk'], lw['rwkv_ka'], lw['rwkv_rk'])
    y, rwkv_s = rwkv_scan_pallas(r_, w_log, k2, v_, kk, a, rwkv_s0.astype(f32))
    flat = lambda t: t.reshape(M, RWKV_WIDTH)
    o_rwkv = rwkv_post(flat(y), flat(bonus), flat(gate), lw['rwkv_ln_w'], lw['rwkv_ln_b'])

    merged = merge_mm(o_gla, o_nsa, o_rwkv, lw['w_o_gla'], lw['w_o_nsa'], lw['w_o_rwkv'], mg)
    x2 = mm(merged, lw['w_out'], res=x2)

    xn2 = rmsnorm_pallas(x2, lw['norm2'])
    act, conv_new = ffn_gate_up(xn2, lw['ffn_gate'], lw['ffn_up'], lw['ffn_conv'], lw['ffn_conv_b'],
                                conv0.astype(f32), B, T)
    x2 = mm(act, lw['ffn_down'], res=x2)
    return x2.reshape(B, T, D_MODEL), (new_cmp, new_slc, win_new, gla_s, rwkv_s, shift_new, conv_new)


def _w_in_group(w, lo, hi):
    seg = w[:, lo:hi].astype(jnp.bfloat16)
    return jnp.pad(seg, ((0, 0), (0, _round_up(hi - lo, W_IN_TILE) - (hi - lo))))


def kernel(x_prompt, x_sample, cache_cmp_kv, cache_slc_kv, cache_win_kv, state_gla, state_rwkv, state_rwkv_shift, state_ffn_conv, page_table, norm1, w_in, gla_wa2, gla_ba, gla_norm, w_o_gla, cmp_w1k, cmp_w2k, cmp_pek, cmp_w1v, cmp_w2v, cmp_pev, w_o_nsa, rwkv_mu, rwkv_w0, rwkv_w2, rwkv_a0, rwkv_a2, rwkv_g2, rwkv_kk, rwkv_ka, rwkv_rk, rwkv_ln_w, rwkv_ln_b, w_o_rwkv, w_out, norm2, ffn_gate, ffn_conv, ffn_conv_b, ffn_up, ffn_down, norm_f):
    G, HD = NSA_KV_HEADS, HEAD_DIM
    n_db, n_pages = page_table.shape
    past_len = n_pages * PAGE_SIZE
    bp = x_prompt.shape[0]
    dt = x_prompt.dtype
    bf = jnp.bfloat16
    xp, xs = x_prompt, x_sample
    st_p, st_s = [], []
    for l in range(DEPTH):
        lw = {'norm1': norm1[l], 'w_gla': _w_in_group(w_in[l], 0, _C_NSA),
              'w_nsa': _w_in_group(w_in[l], _C_NSA, _C_RWKV), 'w_rwkv': _w_in_group(w_in[l], _C_RWKV, _C_MG),
              'w_mg': _w_in_group(w_in[l], _C_MG, _C_MG + IN_SIZES[14]), 'gla_wa2': gla_wa2[l], 'gla_ba': gla_ba[l],
              'gla_norm': gla_norm[l], 'w_o_gla': w_o_gla[l].astype(bf), 'cmp_w1k': cmp_w1k[l], 'cmp_w2k': cmp_w2k[l],
              'cmp_pek': cmp_pek[l], 'cmp_w1v': cmp_w1v[l], 'cmp_w2v': cmp_w2v[l], 'cmp_pev': cmp_pev[l],
              'w_o_nsa': w_o_nsa[l].astype(bf), 'rwkv_mu': rwkv_mu[l], 'rwkv_w0': rwkv_w0[l], 'rwkv_w2': rwkv_w2[l],
              'rwkv_a0': rwkv_a0[l], 'rwkv_a2': rwkv_a2[l], 'rwkv_g2': rwkv_g2[l], 'rwkv_kk': rwkv_kk[l],
              'rwkv_ka': rwkv_ka[l], 'rwkv_rk': rwkv_rk[l], 'rwkv_ln_w': rwkv_ln_w[l], 'rwkv_ln_b': rwkv_ln_b[l],
              'w_o_rwkv': w_o_rwkv[l].astype(bf), 'w_out': w_out[l].astype(bf), 'norm2': norm2[l],
              'ffn_gate': ffn_gate[l], 'ffn_conv': ffn_conv[l], 'ffn_conv_b': ffn_conv_b[l],
              'ffn_up': ffn_up[l], 'ffn_down': ffn_down[l].astype(bf)}
        xp, sp = trunk_layer(xp, 0, None,
                             jnp.zeros((bp, GLA_HEADS, GLA_DK, GLA_DV), jnp.float32),
                             jnp.zeros((bp, RWKV_HEADS, RWKV_N, RWKV_N), jnp.float32),
                             jnp.zeros((bp, RWKV_COLS), dt),
                             jnp.zeros((bp, CONV_W - 1, D_FF), dt), lw)
        paged = (cache_cmp_kv, cache_slc_kv, cache_win_kv, l, page_table)
        xs, ss = trunk_layer(xs, past_len, paged, state_gla[l], state_rwkv[l],
                             state_rwkv_shift[l], state_ffn_conv[l], lw)
        st_p.append(sp)
        st_s.append(ss)
    y_prompt = rmsnorm_pallas(xp.reshape(-1, D_MODEL), norm_f, out_dtype=dt).reshape(xp.shape)
    y_sample = rmsnorm_pallas(xs.reshape(-1, D_MODEL), norm_f, out_dtype=dt).reshape(xs.shape)
    outs = [y_prompt, y_sample]
    for i in range(7):
        outs.append(jnp.stack([s[i] for s in st_p]))
        outs.append(jnp.stack([s[i] for s in st_s]))
    return tuple(outs)
```

```python
import functools

import jax
import jax.numpy as jnp
import numpy as np
from jax import lax
from jax.experimental import pallas as pl
from jax.experimental.pallas import tpu as pltpu

D_MODEL = 4096
DEPTH = 2
PAGE_SIZE = 128
HEAD_DIM = 128
ROPE_DIM = HEAD_DIM // 4
ROPE_THETA = 500000.0
NORM_EPS = 1e-5
NEG_INF = -1e30

GLA_WIDTH = D_MODEL // 4
GLA_HEADS = 4
GLA_DV = GLA_WIDTH // GLA_HEADS
GLA_DK = GLA_DV // 2
GLA_GATE_RANK = 16
GLA_TAU = 16.0
GLA_CHUNK = 64

NSA_HEADS = D_MODEL // 256
NSA_KV_HEADS = 4
NSA_GROUP = NSA_HEADS // NSA_KV_HEADS
NSA_WIDTH = NSA_HEADS * HEAD_DIM
NSA_KV_WIDTH = NSA_KV_HEADS * HEAD_DIM
CMP_STRIDE = 16
CMP_BLOCK = 2 * CMP_STRIDE
SEL_BLOCK = 64
SEL_TOP = 16
N_INIT_BLOCKS = 1
N_LOCAL_BLOCKS = 2
WINDOW = 512
SEL_Q_BLOCK = 32
WIN_Q_BLOCK = 128
FORCE_SCORE = 1e4

RWKV_WIDTH = D_MODEL // 4
RWKV_N = 64
RWKV_HEADS = RWKV_WIDTH // RWKV_N
RWKV_DECAY_RANK = 64
RWKV_AAA_RANK = 64
RWKV_GATE_RANK = 160
RWKV_SIZES = (RWKV_WIDTH, RWKV_WIDTH, RWKV_WIDTH, RWKV_DECAY_RANK, RWKV_AAA_RANK, RWKV_GATE_RANK)
RWKV_COLS = sum(RWKV_SIZES)
RWKV_LN_EPS = 64e-5

N_BRANCH = 3
D_FF = 256 * ((8 * D_MODEL // 3 + 255) // 256)
CONV_W = 3

IN_SIZES = (GLA_HEADS * GLA_DK, GLA_HEADS * GLA_DK, GLA_WIDTH, GLA_WIDTH, GLA_GATE_RANK,
            NSA_WIDTH, NSA_KV_WIDTH, NSA_KV_WIDTH, NSA_KV_WIDTH, NSA_KV_WIDTH, NSA_KV_WIDTH, NSA_KV_WIDTH,
            NSA_HEADS * 3,
            RWKV_COLS,
            N_BRANCH * D_MODEL)

LANE = 128
SUBLANE = 8
NSA_TQ = 512
NSA_TK = 1024
GLA_TC = 256
DEC_PAGES = 16
RWKV_NB = 2
VMEM_LIMIT = 48 * 1024 * 1024


def _round_up(n, m):
    return -(-n // m) * m


W_IN_TILE = 512
_C_NSA = sum(IN_SIZES[:5])
_C_RWKV = sum(IN_SIZES[:13])
_C_MG = sum(IN_SIZES[:14])
GLA_OFF = tuple(int(o) for o in np.concatenate([[0], np.cumsum(IN_SIZES[:4])]))
NSA_GATE_OFF = NSA_WIDTH + 6 * NSA_KV_WIDTH


def _pick(n, cands):
    for c in cands:
        if n % c == 0:
            return c
    return n


def _rmsnorm_kernel(x_ref, g_ref, o_ref):
    x = x_ref[...]
    y = x * lax.rsqrt(jnp.mean(x * x, axis=-1, keepdims=True) + NORM_EPS)
    o_ref[...] = (y * g_ref[...]).astype(o_ref.dtype)


def rmsnorm_pallas(x, g, out_dtype=jnp.bfloat16):
    M, D = x.shape
    tm = _pick(M, (256, 128, 64, 32, 16, 8))
    return pl.pallas_call(
        _rmsnorm_kernel,
        grid=(M // tm,),
        in_specs=[pl.BlockSpec((tm, D), lambda i: (i, 0)), pl.BlockSpec((1, D), lambda i: (0, 0))],
        out_specs=pl.BlockSpec((tm, D), lambda i: (i, 0)),
        out_shape=jax.ShapeDtypeStruct((M, D), out_dtype),
        compiler_params=pltpu.CompilerParams(dimension_semantics=("parallel",), vmem_limit_bytes=VMEM_LIMIT),
        name="rmsnorm",
    )(x, g.reshape(1, D))


def _mm_kernel(*refs, nk, has_res):
    x_ref, w_ref = refs[:2]
    res_ref = refs[2] if has_res else None
    o_ref, acc_ref = refs[-2:]
    k = pl.program_id(2)
    part = jnp.dot(x_ref[...], w_ref[...], preferred_element_type=jnp.float32)

    def finish(v):
        if has_res:
            v = v + res_ref[...]
        o_ref[...] = v.astype(o_ref.dtype)

    if nk == 1:
        finish(part)
    else:
        @pl.when(k == 0)
        def _():
            acc_ref[...] = part

        @pl.when(jnp.logical_and(k > 0, k < nk - 1))
        def _():
            acc_ref[...] += part

        @pl.when(k == nk - 1)
        def _():
            finish(acc_ref[...] + part)


def mm(x, w, res=None, out_dtype=jnp.float32):
    M, K = x.shape
    N = w.shape[1]
    tm = _pick(M, (1024, 512, 256, 128, 64, 32, 16, 8))
    tn = _pick(N, (512, 256, 128))
    tk = K if K <= 4096 else _pick(K, (5504, 4096, 2048, 1024, 512))
    nk = K // tk
    in_specs = [pl.BlockSpec((tm, tk), lambda i, j, k: (i, k)),
                pl.BlockSpec((tk, tn), lambda i, j, k: (k, j))]
    args = [x, w]
    if res is not None:
        in_specs.append(pl.BlockSpec((tm, tn), lambda i, j, k: (i, j)))
        args.append(res)
    return pl.pallas_call(
        functools.partial(_mm_kernel, nk=nk, has_res=res is not None),
        grid=(M // tm, N // tn, nk),
        in_specs=in_specs,
        out_specs=pl.BlockSpec((tm, tn), lambda i, j, k: (i, j)),
        out_shape=jax.ShapeDtypeStruct((M, N), out_dtype),
        scratch_shapes=[pltpu.VMEM((tm, tn) if nk > 1 else (SUBLANE, LANE), jnp.float32)],
        compiler_params=pltpu.CompilerParams(
            dimension_semantics=("parallel", "parallel", "arbitrary"),
            vmem_limit_bytes=VMEM_LIMIT),
        name="dense_mm",
    )(*args)


def _merge_kernel(oa_ref, ob_ref, oc_ref, wa_ref, wb_ref, wc_ref, ga_ref, gb_ref, gc_ref, o_ref):
    f32 = jnp.float32
    acc = jax.nn.sigmoid(ga_ref[...]) * jnp.dot(oa_ref[...], wa_ref[...], preferred_element_type=f32)
    acc += jax.nn.sigmoid(gb_ref[...]) * jnp.dot(ob_ref[...], wb_ref[...], preferred_element_type=f32)
    acc += jax.nn.sigmoid(gc_ref[...]) * jnp.dot(oc_ref[...], wc_ref[...], preferred_element_type=f32)
    o_ref[...] = acc.astype(o_ref.dtype)


def merge_mm(o_a, o_b, o_c, w_a, w_b, w_c, mg, out_dtype=jnp.bfloat16):
    M = o_a.shape[0]
    D = w_a.shape[1]
    tm = _pick(M, (1024, 512, 256, 128, 64, 32, 16, 8))
    tn = _pick(D, (512, 256, 128))
    nj = D // tn
    o_spec = lambda o: pl.BlockSpec((tm, o.shape[1]), lambda i, j: (i, 0))
    w_spec = lambda w: pl.BlockSpec((w.shape[0], tn), lambda i, j: (0, j))
    g_spec = lambda b: pl.BlockSpec((tm, tn), lambda i, j: (i, b * nj + j))
    return pl.pallas_call(
        _merge_kernel,
        grid=(M // tm, nj),
        in_specs=[o_spec(o_a), o_spec(o_b), o_spec(o_c), w_spec(w_a), w_spec(w_b), w_spec(w_c),
                  g_spec(0), g_spec(1), g_spec(2)],
        out_specs=pl.BlockSpec((tm, tn), lambda i, j: (i, j)),
        out_shape=jax.ShapeDtypeStruct((M, D), out_dtype),
        compiler_params=pltpu.CompilerParams(dimension_semantics=("parallel", "parallel"),
                                             vmem_limit_bytes=VMEM_LIMIT),
        name="merge_mm",
    )(o_a, o_b, o_c, w_a, w_b, w_c, mg, mg, mg)


def _ffn_gate_up_kernel(x_ref, wg_ref, wu_ref, cw_ref, cb_ref, st_ref, act_ref, tail_ref, carry_scr, *, tm, tps, T):
    f32 = jnp.float32
    i, j = pl.program_id(0), pl.program_id(1)
    x = x_ref[...]
    h = jnp.dot(x, wg_ref[...].astype(x.dtype), preferred_element_type=f32)
    u = jnp.dot(x, wu_ref[...].astype(x.dtype), preferred_element_type=f32)
    cw = cw_ref[...]
    if T == 1:
        prev2, prev1 = st_ref[0], st_ref[1]
        tail_ref[0] = prev1
        tail_ref[1] = h
    else:
        tail = jnp.where(i % tps == 0, st_ref[...], carry_scr[j])
        row = lax.broadcasted_iota(jnp.int32, h.shape, 0)
        prev1 = jnp.where(row == 0, tail[7:8], pltpu.roll(h, 1, axis=0))
        prev2 = jnp.where(row == 0, tail[6:7], jnp.where(row == 1, tail[7:8], pltpu.roll(h, 2, axis=0)))
        last = h[tm - SUBLANE:tm]
        carry_scr[j] = last
        tail_ref[...] = last
    hc = cb_ref[...] + prev2 * cw[0:1] + prev1 * cw[1:2] + h * cw[2:3]
    act_ref[...] = (jax.nn.silu(hc) * u).astype(act_ref.dtype)


def ffn_gate_up(xn, w_gate, w_up, conv_w, conv_b, conv0, B, T):
    M, D = xn.shape
    F = w_gate.shape[1]
    tn = _pick(F, (512, 256, 128))
    nj = F // tn
    cw = jnp.pad(conv_w, ((0, SUBLANE - CONV_W), (0, 0)))
    cb = conv_b.reshape(1, F)
    if T == 1:
        tm, tps = M, 1
        st = conv0.transpose(1, 0, 2)
        st_spec = pl.BlockSpec((2, B, tn), lambda i, j: (0, 0, j))
        tail_shape, tail_spec = (2, B, F), pl.BlockSpec((2, B, tn), lambda i, j: (0, 0, j))
    else:
        tm = _pick(T, (1024, 512, 256, 128, 64, 32, 16, 8))
        tps = T // tm
        st = jnp.pad(conv0, ((0, 0), (SUBLANE - 2, 0), (0, 0)))
        st_spec = pl.BlockSpec((None, SUBLANE, tn), lambda i, j: (i // tps, 0, j))
        tail_shape, tail_spec = (M // tm, SUBLANE, F), pl.BlockSpec((None, SUBLANE, tn), lambda i, j: (i, 0, j))
    act, tail = pl.pallas_call(
        functools.partial(_ffn_gate_up_kernel, tm=tm, tps=tps, T=T),
        grid=(M // tm, nj),
        in_specs=[pl.BlockSpec((tm, D), lambda i, j: (i, 0)),
                  pl.BlockSpec((D, tn), lambda i, j: (0, j)),
                  pl.BlockSpec((D, tn), lambda i, j: (0, j)),
                  pl.BlockSpec((SUBLANE, tn), lambda i, j: (0, j)),
                  pl.BlockSpec((1, tn), lambda i, j: (0, j)),
                  st_spec],
        out_specs=[pl.BlockSpec((tm, tn), lambda i, j: (i, j)), tail_spec],
        out_shape=[jax.ShapeDtypeStruct((M, F), jnp.bfloat16), jax.ShapeDtypeStruct(tail_shape, jnp.float32)],
        scratch_shapes=[pltpu.VMEM((nj, SUBLANE, tn), jnp.float32)],
        compiler_params=pltpu.CompilerParams(dimension_semantics=("arbitrary", "arbitrary"),
                                             vmem_limit_bytes=VMEM_LIMIT),
        name="ffn_gate_up",
    )(xn, w_gate, w_up, cw, cb, st)
    conv_new = tail.transpose(1, 0, 2) if T == 1 else tail[tps - 1::tps, SUBLANE - 2:]
    return act, conv_new


def _gla_kernel(q_ref, k_ref, v_ref, og_ref, lo_ref, wa_ref, ba_ref, gn_ref, s0_ref, o_ref, sout_ref, s_scr, *, Tc, C, valid):
    f32, bf16 = jnp.float32, jnp.bfloat16
    c = pl.program_id(2)

    @pl.when(c == 0)
    def _():
        s_scr[...] = s0_ref[...]

    row = lax.broadcasted_iota(jnp.int32, (C, GLA_DK), 0)
    tril = lax.broadcasted_iota(jnp.int32, (C, C), 0) >= lax.broadcasted_iota(jnp.int32, (C, C), 1)
    wa = wa_ref[...]
    ba = ba_ref[...]
    gn = gn_ref[...]
    S = s_scr[...]
    for n in range(Tc // C):
        rows = slice(n * C, (n + 1) * C)
        la = jax.nn.log_sigmoid(jnp.dot(lo_ref[rows, :].astype(bf16), wa, preferred_element_type=f32) + ba) / GLA_TAU
        if valid < C:
            la = jnp.where(row < valid, la, 0.0)
        bc = la
        d = 1
        while d < C:
            bc = bc + jnp.where(row >= d, pltpu.roll(bc, d, axis=0), 0.0)
            d *= 2
        b_mid = bc[C // 2:C // 2 + 1]
        b_last = bc[C - 1:C]
        q = q_ref[rows, :] * GLA_DK ** -0.5
        k = k_ref[rows, :]
        v = v_ref[rows, :].astype(bf16)
        att = lax.dot_general((q * jnp.exp(bc - b_mid)).astype(bf16), (k * jnp.exp(b_mid - bc)).astype(bf16),
                              (((1,), (1,)), ((), ())), preferred_element_type=f32)
        att = jnp.where(tril, att, 0.0)
        o = jnp.dot(att.astype(bf16), v, preferred_element_type=f32)
        o = o + jnp.dot((q * jnp.exp(bc)).astype(bf16), S.astype(bf16), preferred_element_type=f32)
        kd = jnp.concatenate([k * jnp.exp(b_last - bc), jnp.broadcast_to(jnp.exp(b_last), (SUBLANE, GLA_DK))], axis=0)
        kdt = kd.T
        S = kdt[:, C:C + 1] * S + jnp.dot(kdt[:, :C].astype(bf16), v, preferred_element_type=f32)
        o = o * lax.rsqrt(jnp.mean(o * o, axis=-1, keepdims=True) + NORM_EPS) * gn
        o_ref[rows, :] = (o * jax.nn.silu(og_ref[rows, :])).astype(o_ref.dtype)
    s_scr[...] = S

    @pl.when(c == pl.num_programs(2) - 1)
    def _():
        sout_ref[...] = S


def gla_pallas(hmix, wa2, ba, gnorm, s0, seg_off):
    B, T, _ = hmix.shape
    H = GLA_HEADS
    n_tok = T
    if T % GLA_CHUNK == 0:
        Tc, C = min(GLA_TC, T), GLA_CHUNK
    else:
        assert T < SUBLANE
        Tc = C = SUBLANE
        hmix = jnp.pad(hmix, ((0, 0), (0, SUBLANE - T), (0, 0)))
        T = SUBLANE
    oq, ok, ov, og, ol = seg_off
    wa = jnp.pad(wa2, ((0, LANE - wa2.shape[0]), (0, 0))).astype(jnp.bfloat16)
    col = lambda off, w: (lambda b, h, c: (b, c, off // w + h))
    o, s = pl.pallas_call(
        functools.partial(_gla_kernel, Tc=Tc, C=C, valid=min(n_tok, C)),
        grid=(B, H, T // Tc),
        in_specs=[pl.BlockSpec((None, Tc, GLA_DK), col(oq, GLA_DK)),
                  pl.BlockSpec((None, Tc, GLA_DK), col(ok, GLA_DK)),
                  pl.BlockSpec((None, Tc, GLA_DV), col(ov, GLA_DV)),
                  pl.BlockSpec((None, Tc, GLA_DV), col(og, GLA_DV)),
                  pl.BlockSpec((None, Tc, LANE), lambda b, h, c: (b, c, ol // LANE)),
                  pl.BlockSpec((LANE, GLA_DK), lambda b, h, c: (0, h)),
                  pl.BlockSpec((1, GLA_DK), lambda b, h, c: (0, h)),
                  pl.BlockSpec((1, GLA_DV), lambda b, h, c: (0, 0)),
                  pl.BlockSpec((None, None, GLA_DK, GLA_DV), lambda b, h, c: (b, h, 0, 0))],
        out_specs=[pl.BlockSpec((None, Tc, GLA_DV), lambda b, h, c: (b, c, h)),
                   pl.BlockSpec((None, None, GLA_DK, GLA_DV), lambda b, h, c: (b, h, 0, 0))],
        out_shape=[jax.ShapeDtypeStruct((B, T, H * GLA_DV), jnp.bfloat16),
                   jax.ShapeDtypeStruct((B, H, GLA_DK, GLA_DV), jnp.float32)],
        scratch_shapes=[pltpu.VMEM((GLA_DK, GLA_DV), jnp.float32)],
        compiler_params=pltpu.CompilerParams(dimension_semantics=("parallel", "parallel", "arbitrary"),
                                             vmem_limit_bytes=VMEM_LIMIT),
        name="gla_chunked",
    )(hmix, hmix, hmix, hmix, hmix, wa, ba.reshape(1, -1), gnorm.reshape(1, -1), s0)
    return o[:, :n_tok], s


def _head_sums(x, bd):
    f32, bf16 = jnp.float32, jnp.bfloat16
    outs = []
    for t in range(x.shape[1] // LANE):
        p = x[:, t * LANE:(t + 1) * LANE]
        hi = p.astype(bf16)
        lo = (p - hi.astype(f32)).astype(bf16)
        outs.append(jnp.dot(hi, bd, preferred_element_type=f32) + jnp.dot(lo, bd, preferred_element_type=f32))
    return jnp.concatenate(outs, axis=1)


def _block_diag_ones():
    rr = lax.broadcasted_iota(jnp.int32, (LANE, LANE), 0) // RWKV_N
    cc = lax.broadcasted_iota(jnp.int32, (LANE, LANE), 1) // RWKV_N
    return jnp.where(rr == cc, 1.0, 0.0).astype(jnp.bfloat16)


def _rwkv_prep_kernel(x_ref, sh_ref, mu_ref, w0_ref, a0_ref, kkw_ref, ka_ref, rk_ref, w2_ref, a2_ref, g2_ref,
                      r_ref, wl_ref, k_ref, v_ref, kk_ref, a_ref, gate_ref, bonus_ref, tail_ref, carry_scr, *, tm, T):
    f32, bf16 = jnp.float32, jnp.bfloat16
    W = RWKV_WIDTH
    i = pl.program_id(1)
    x = x_ref[...]
    if T == 1:
        prev = sh_ref[...]
        tail_ref[...] = x
    else:
        first = jnp.where(i == 0, sh_ref[SUBLANE - 1:SUBLANE], carry_scr[SUBLANE - 1:SUBLANE])
        row = lax.broadcasted_iota(jnp.int32, x.shape, 0)
        prev = jnp.where(row == 0, first, pltpu.roll(x, 1, axis=0))
        last = x[tm - SUBLANE:tm]
        carry_scr[...] = last
        tail_ref[...] = last
    rm = x + (prev - x) * mu_ref[...]
    r, k, v = rm[:, :W], rm[:, W:2 * W], rm[:, 2 * W:3 * W]
    lo = rm[:, 3 * W:3 * W + LANE]
    glo = rm[:, 3 * W + LANE:]
    w_raw = w0_ref[...] + jnp.dot(jnp.tanh(lo).astype(bf16), w2_ref[...], preferred_element_type=f32)
    wl_ref[...] = -jnp.exp(-jax.nn.softplus(-w_raw) - 0.5)
    a = jax.nn.sigmoid(a0_ref[...] + jnp.dot(lo.astype(bf16), a2_ref[...], preferred_element_type=f32))
    gate_ref[...] = jnp.dot(jax.nn.sigmoid(glo).astype(bf16), g2_ref[...], preferred_element_type=f32)
    bd = _block_diag_ones()
    kk = k * kkw_ref[...]
    kk_ref[...] = kk * lax.rsqrt(jnp.maximum(_head_sums(kk * kk, bd), 1e-24))
    k2 = k * (1.0 + (a - 1.0) * ka_ref[...])
    bonus_ref[...] = _head_sums(r * k2 * rk_ref[...], bd) * v
    r_ref[...] = r
    k_ref[...] = k2
    v_ref[...] = v
    a_ref[...] = a


def rwkv_prep(hr, shift0, mu, w0, w2, a0, a2, g2, kkw, ka, rk):
    B, T, WP = hr.shape
    W = RWKV_WIDTH
    bf16 = jnp.bfloat16
    padc = lambda t: jnp.pad(t, ((0, 0), (0, WP - t.shape[1])))
    w2p = jnp.pad(w2, ((0, LANE - RWKV_DECAY_RANK), (0, 0))).astype(bf16)
    a2p = jnp.pad(a2, ((RWKV_DECAY_RANK, 0), (0, 0))).astype(bf16)
    gpad = WP - 3 * W - LANE
    g2p = jnp.pad(g2, ((0, gpad - RWKV_GATE_RANK), (0, 0))).astype(bf16)
    row = lambda t: t.reshape(1, -1)
    if T == 1:
        tm = 1
        sh = padc(shift0).reshape(B, 1, WP)
        sh_spec = pl.BlockSpec((None, 1, WP), lambda b, i: (b, 0, 0))
        tail_rows = 1
    else:
        tm = _pick(T, (256, 128, 64, 32, 16, 8))
        sh = jnp.broadcast_to(padc(shift0)[:, None, :], (B, SUBLANE, WP))
        sh_spec = pl.BlockSpec((None, SUBLANE, WP), lambda b, i: (b, 0, 0))
        tail_rows = SUBLANE
    vec = lambda n: pl.BlockSpec((1, n), lambda b, i: (0, 0))
    mat = lambda m: pl.BlockSpec(m.shape, lambda b, i: (0, 0))
    o_spec = pl.BlockSpec((None, tm, W), lambda b, i: (b, i, 0))
    o_shape = jax.ShapeDtypeStruct((B, T, W), jnp.float32)
    outs = pl.pallas_call(
        functools.partial(_rwkv_prep_kernel, tm=tm, T=T),
        grid=(B, T // tm),
        in_specs=[pl.BlockSpec((None, tm, WP), lambda b, i: (b, i, 0)), sh_spec, vec(WP),
                  vec(W), vec(W), vec(W), vec(W), vec(W), mat(w2p), mat(a2p), mat(g2p)],
        out_specs=[o_spec] * 8 + [pl.BlockSpec((None, tail_rows, WP), lambda b, i: (b, 0, 0))],
        out_shape=[o_shape] * 8 + [jax.ShapeDtypeStruct((B, tail_rows, WP), jnp.float32)],
        scratch_shapes=[pltpu.VMEM((SUBLANE, WP), jnp.float32)],
        compiler_params=pltpu.CompilerParams(dimension_semantics=("parallel", "arbitrary"),
                                             vmem_limit_bytes=VMEM_LIMIT),
        name="rwkv_prep",
    )(hr, sh, row(padc(mu.reshape(1, -1))), row(w0), row(a0), row(kkw), row(ka), row(rk), w2p, a2p, g2p)
    return outs[:8], outs[8][:, tail_rows - 1, :RWKV_COLS]


def _rwkv_post_kernel(y_ref, bonus_ref, gate_ref, lw_ref, lb_ref, o_ref):
    bd = _block_diag_ones()
    y = y_ref[...]
    d = y - _head_sums(y, bd) * (1.0 / RWKV_N)
    var = _head_sums(d * d, bd) * (1.0 / RWKV_N)
    yn = d * lax.rsqrt(var + RWKV_LN_EPS) * lw_ref[...] + lb_ref[...]
    o_ref[...] = ((yn + bonus_ref[...]) * gate_ref[...]).astype(o_ref.dtype)


def rwkv_post(y, bonus, gate, ln_w, ln_b):
    M, W = y.shape
    tm = _pick(M, (256, 128, 64, 32, 16, 8))
    spec = pl.BlockSpec((tm, W), lambda i: (i, 0))
    vec = pl.BlockSpec((1, W), lambda i: (0, 0))
    return pl.pallas_call(
        _rwkv_post_kernel,
        grid=(M // tm,),
        in_specs=[spec, spec, spec, vec, vec],
        out_specs=spec,
        out_shape=jax.ShapeDtypeStruct((M, W), jnp.bfloat16),
        compiler_params=pltpu.CompilerParams(dimension_semantics=("parallel",), vmem_limit_bytes=VMEM_LIMIT),
        name="rwkv_post",
    )(y, bonus, gate, ln_w.reshape(1, W), ln_b.reshape(1, W))


def _rwkv_kernel(r_ref, wl_ref, k_ref, v_ref, kk_ref, a_ref, s0_ref, y_ref, sout_ref, s_scr, *, NB, NP, Tc):
    c = pl.program_id(1)
    f32, bf16 = jnp.float32, jnp.bfloat16
    U = min(SUBLANE, Tc)

    @pl.when(c == 0)
    def _():
        s_scr[...] = s0_ref[...]

    sub = lax.broadcasted_iota(jnp.int32, (RWKV_N, LANE), 0)
    lane = lax.broadcasted_iota(jnp.int32, (RWKV_N, LANE), 1)
    eye2 = (lane % RWKV_N) == sub
    left = lane < RWKV_N
    rr = lax.broadcasted_iota(jnp.int32, (LANE, LANE), 0) // RWKV_N
    cc = lax.broadcasted_iota(jnp.int32, (LANE, LANE), 1) // RWKV_N
    bd = jnp.where(rr == cc, 1.0, 0.0).astype(bf16)
    bd2 = jnp.concatenate([bd, bd], axis=0)

    def ssb(p, two_piece=True):
        hi = p.astype(bf16)
        if not two_piece:
            return jnp.dot(hi, bd, preferred_element_type=f32)
        lo = (p - hi.astype(f32)).astype(bf16)
        return jnp.dot(jnp.concatenate([hi, lo], axis=1), bd2, preferred_element_type=f32)

    eye_all = jnp.concatenate([eye2] * NP, axis=0)

    def bcast(x8, s):
        return jnp.concatenate(
            [jnp.broadcast_to(x8[s:s + 1, p * LANE:(p + 1) * LANE], (RWKV_N, LANE)) for p in range(NP)], axis=0)

    def vcols(vt, v8, s):
        if vt is None:
            return ssb(jnp.where(eye_all, bcast(v8, s), 0.0))
        return jnp.concatenate(
            [jnp.where(left, jnp.broadcast_to(vt[p][:RWKV_N, s:s + 1], (RWKV_N, LANE)),
                       jnp.broadcast_to(vt[p][RWKV_N:, s:s + 1], (RWKV_N, LANE))) for p in range(NP)], axis=0)

    def body(g, carry):
        rows = pl.ds(pl.multiple_of(g * U, U), U)
        tiles = []
        for nb in range(NB):
            kk8 = kk_ref[nb, rows, :]
            v8 = v_ref[nb, rows, :]
            vt = [v8[:, p * LANE:(p + 1) * LANE].T for p in range(NP)] if U == SUBLANE else None
            tiles.append(dict(r=r_ref[nb, rows, :], w=jnp.exp(wl_ref[nb, rows, :]), k=k_ref[nb, rows, :], v=v8, vt=vt,
                              ka=kk8 * a_ref[nb, rows, :], nk=-kk8))
        S = [s_scr[nb] for nb in range(NB)]
        ys = [[] for _ in range(NB)]
        for s in range(U):
            sa = [ssb(S[nb] * bcast(t['nk'], s)) for nb, t in enumerate(tiles)]
            for nb, t in enumerate(tiles):
                S[nb] = S[nb] * bcast(t['w'], s) + sa[nb] * bcast(t['ka'], s) + vcols(t['vt'], t['v'], s) * bcast(t['k'], s)
            yb = [jnp.where(eye_all, ssb(S[nb] * bcast(t['r'], s), two_piece=False), 0.0) for nb, t in enumerate(tiles)]
            for nb in range(NB):
                ys[nb].append(jnp.concatenate(
                    [jnp.sum(yb[nb][p * RWKV_N:(p + 1) * RWKV_N], axis=0, keepdims=True) for p in range(NP)], axis=1))
        for nb in range(NB):
            s_scr[nb] = S[nb]
            y_ref[nb, rows, :] = ys[nb][0] if U == 1 else jnp.concatenate(ys[nb], axis=0)
        return carry

    lax.fori_loop(0, Tc // U, body, 0)

    @pl.when(c == pl.num_programs(1) - 1)
    def _():
        sout_ref[...] = s_scr[...]


def rwkv_scan_pallas(r, w_log, k, v, kk, a, s0):
    B, T, W = r.shape
    H = W // RWKV_N
    NP = H // 2
    NB = RWKV_NB if B % RWKV_NB == 0 else 1
    Tc = 128 if T % 128 == 0 else T
    s0p = s0.reshape(B, NP, 2, RWKV_N, RWKV_N).transpose(0, 1, 3, 2, 4).reshape(B, NP * RWKV_N, LANE)
    blk = pl.BlockSpec((NB, Tc, W), lambda b, c: (b, c, 0))
    sblk = pl.BlockSpec((NB, NP * RWKV_N, LANE), lambda b, c: (b, 0, 0))
    y, sp = pl.pallas_call(
        functools.partial(_rwkv_kernel, NB=NB, NP=NP, Tc=Tc),
        grid=(B // NB, T // Tc),
        in_specs=[blk] * 6 + [sblk],
        out_specs=[blk, sblk],
        out_shape=[jax.ShapeDtypeStruct((B, T, W), jnp.float32),
                   jax.ShapeDtypeStruct((B, NP * RWKV_N, LANE), jnp.float32)],
        scratch_shapes=[pltpu.VMEM((NB, NP * RWKV_N, LANE), jnp.float32)],
        compiler_params=pltpu.CompilerParams(dimension_semantics=("parallel", "arbitrary"),
                                             vmem_limit_bytes=VMEM_LIMIT),
        name="rwkv7_scan",
    )(r, w_log, k, v, kk, a, s0p)
    s_fin = sp.reshape(B, NP, RWKV_N, 2, RWKV_N).transpose(0, 1, 3, 2, 4).reshape(B, H, RWKV_N, RWKV_N)
    return y, s_fin


def rope_tables(pos):
    half = ROPE_DIM // 2
    inv = ROPE_THETA ** (-jnp.arange(half, dtype=jnp.float32) / half)
    ang = pos.astype(jnp.float32)[:, None] * inv[None, :]
    cos, sin = jnp.cos(ang), jnp.sin(ang)
    T = pos.shape[0]
    z = jnp.zeros((T, HEAD_DIM - ROPE_DIM), jnp.float32)
    zh = jnp.zeros((T, half), jnp.float32)
    c = jnp.concatenate([cos, cos, jnp.ones_like(z)], axis=1)
    s_up = jnp.concatenate([-sin, zh, z], axis=1)
    s_dn = jnp.concatenate([zh, sin, z], axis=1)
    return c, s_up, s_dn


def _nsa_prep_kernel(q_ref, c_ref, s_ref, w_ref, tc_ref, tu_ref, td_ref, qo_ref, co_ref, so_ref, wo_ref):
    c, su, sd = tc_ref[...], tu_ref[...], td_ref[...]
    half = ROPE_DIM // 2

    def rot(x):
        return x * c + pltpu.roll(x, HEAD_DIM - half, axis=1) * su + pltpu.roll(x, half, axis=1) * sd

    for h in range(NSA_HEADS):
        cols = slice(h * HEAD_DIM, (h + 1) * HEAD_DIM)
        qo_ref[:, cols] = (rot(q_ref[:, cols]) * HEAD_DIM ** -0.5).astype(qo_ref.dtype)
    for src, dst in ((c_ref, co_ref), (s_ref, so_ref), (w_ref, wo_ref)):
        for g in range(NSA_KV_HEADS):
            cols = slice(g * HEAD_DIM, (g + 1) * HEAD_DIM)
            dst[:, cols] = rot(src[:, cols])
        dst[:, NSA_KV_WIDTH:] = src[:, NSA_KV_WIDTH:]


def nsa_prep(hn, pos):
    B, T, _ = hn.shape
    tm = _pick(T, (256, 128, 64, 32, 16, 8))
    tabs = rope_tables(pos)
    kvw = 2 * NSA_KV_WIDTH
    q_spec = pl.BlockSpec((None, tm, NSA_WIDTH), lambda b, i: (b, i, 0))
    kv_spec = lambda n: pl.BlockSpec((None, tm, kvw), lambda b, i: (b, i, NSA_WIDTH // kvw + n))
    t_spec = pl.BlockSpec((tm, HEAD_DIM), lambda b, i: (i, 0))
    o_spec = pl.BlockSpec((None, tm, kvw), lambda b, i: (b, i, 0))
    kv_shape = jax.ShapeDtypeStruct((B, T, kvw), jnp.float32)
    return pl.pallas_call(
        _nsa_prep_kernel,
        grid=(B, T // tm),
        in_specs=[q_spec, kv_spec(0), kv_spec(1), kv_spec(2), t_spec, t_spec, t_spec],
        out_specs=[q_spec, o_spec, o_spec, o_spec],
        out_shape=[jax.ShapeDtypeStruct((B, T, NSA_WIDTH), jnp.bfloat16), kv_shape, kv_shape, kv_shape],
        compiler_params=pltpu.CompilerParams(dimension_semantics=("parallel", "parallel"),
                                             vmem_limit_bytes=VMEM_LIMIT),
        name="nsa_prep",
    )(hn, hn, hn, hn, *tabs)


def _compress_kernel(x_ref, w1_ref, w2_ref, pe_ref, ko_ref, vo_ref, *, ns):
    f32, bf16 = jnp.float32, jnp.bfloat16
    G, HD = NSA_KV_HEADS, HEAD_DIM
    row_w = 2 * NSA_KV_WIDTH
    for kv, o_ref in ((0, ko_ref), (1, vo_ref)):
        pos = jnp.zeros((SUBLANE, HD), f32)
        for p in range(CMP_STRIDE):
            w = w1_ref[kv, p]
            lo = jnp.broadcast_to(pe_ref[kv, p:p + 1, :], (SUBLANE, HD)).astype(bf16)
            hi = jnp.broadcast_to(pe_ref[kv, CMP_STRIDE + p:CMP_STRIDE + p + 1, :], (SUBLANE, HD)).astype(bf16)
            pos = pos + jnp.dot(lo, w, preferred_element_type=f32)[:, :HD] + jnp.dot(hi, w, preferred_element_type=f32)[:, HD:]
        pos = pos[0:1]
        for g in range(G):
            acc = jnp.zeros((ns, 2 * HD), f32)
            for p in range(CMP_STRIDE):
                c0 = p * row_w + kv * NSA_KV_WIDTH + g * HD
                acc = acc + jnp.dot(x_ref[:, c0:c0 + HD].astype(bf16), w1_ref[kv, p], preferred_element_type=f32)
            nxt = pltpu.roll(acc[:, HD:], ns - 1, axis=0)
            hid = jax.nn.gelu(acc[:, :HD] + nxt + pos)
            o_ref[g] = jnp.dot(hid.astype(bf16), w2_ref[kv], preferred_element_type=f32).astype(o_ref.dtype)


def compress_weights(w1k, w2k, pek, w1v, w2v, pev):
    bf16 = jnp.bfloat16
    cat = lambda w1: jnp.concatenate([w1[:CMP_STRIDE], w1[CMP_STRIDE:]], axis=-1)
    return (jnp.stack([cat(w1k), cat(w1v)]).astype(bf16), jnp.stack([w2k, w2v]).astype(bf16), jnp.stack([pek, pev]))


def compress_pallas(kv_rows, w1, w2, pe):
    B, T, W = kv_rows.shape
    ns = T // CMP_STRIDE
    bf16 = jnp.bfloat16
    x = kv_rows.reshape(B, ns, CMP_STRIDE * W)
    out = jax.ShapeDtypeStruct((B, NSA_KV_HEADS, ns, HEAD_DIM), bf16)
    o_spec = pl.BlockSpec((None, NSA_KV_HEADS, ns, HEAD_DIM), lambda b: (b, 0, 0, 0))
    return pl.pallas_call(
        functools.partial(_compress_kernel, ns=ns),
        grid=(B,),
        in_specs=[pl.BlockSpec((None, ns, CMP_STRIDE * W), lambda b: (b, 0, 0)),
                  pl.BlockSpec(w1.shape, lambda b: (0, 0, 0, 0)),
                  pl.BlockSpec(w2.shape, lambda b: (0, 0, 0)),
                  pl.BlockSpec(pe.shape, lambda b: (0, 0, 0))],
        out_specs=[o_spec, o_spec],
        out_shape=[out, out],
        compiler_params=pltpu.CompilerParams(dimension_semantics=("parallel",), vmem_limit_bytes=VMEM_LIMIT),
        name="nsa_compress",
    )(x, w1, w2, pe)


def _dot_nt(a, b):
    return lax.dot_general(a, b, (((1,), (1,)), ((), ())), preferred_element_type=jnp.float32)


def _nsa_kernel(q_ref, kc_ref, vc_ref, ks_ref, vs_ref, kw_ref, vw_ref, g_ref, covt_ref, e_ref, o_ref,
                bias_scr, p4_scr, ocmp_scr, m_scr, acc_scr, *, TQ, TK, T, NS, NCP, n_top):
    f32, bf16 = jnp.float32, jnp.bfloat16
    R = NSA_GROUP
    i = pl.program_id(2)
    nchunk = T // TK
    qpos_col = i * TQ + lax.broadcasted_iota(jnp.int32, (TQ, 1), 0)

    kc = kc_ref[...]
    vc = vc_ref[...]
    cend = lax.broadcasted_iota(jnp.int32, (1, NCP), 1) * CMP_STRIDE + (CMP_BLOCK - 1)
    valid = cend <= qpos_col
    for r in range(R):
        s = _dot_nt(q_ref[:, r * HEAD_DIM:(r + 1) * HEAD_DIM], kc)
        s = jnp.where(valid, s, NEG_INF)
        m = jnp.max(s, axis=-1, keepdims=True)
        p = jnp.where(valid, jnp.exp(s - m), 0.0)
        l = jnp.sum(p, axis=-1, keepdims=True)
        p = (p / jnp.where(l > 0.0, l, 1.0)).astype(bf16)
        p4_scr[:, r * NCP:(r + 1) * NCP] = p
        ocmp_scr[r] = jnp.dot(p, vc, preferred_element_type=f32)

    imp_t = _dot_nt(covt_ref[...], p4_scr[...])
    j = lax.broadcasted_iota(jnp.int32, (NS, TQ), 0)
    qblk = (i * TQ + lax.broadcasted_iota(jnp.int32, (NS, TQ), 1)) // SEL_BLOCK
    forced = (j < N_INIT_BLOCKS) | ((j <= qblk) & (j > qblk - N_LOCAL_BLOCKS))
    score = jnp.where(forced, FORCE_SCORE, jnp.where(j <= qblk, imp_t, NEG_INF))
    rank = jnp.zeros((NS, TQ), f32)
    for a in range(NS):
        row = score[a:a + 1, :]
        beats = (row > score) | ((row == score) & (a < j))
        rank = rank + jnp.where(beats, 1.0, 0.0)
    sel_t = jnp.where(rank < n_top, 1.0, 0.0)
    if NS < LANE:
        sel_t = jnp.concatenate([sel_t, jnp.zeros((LANE - NS, TQ), f32)], axis=0)
    sel = sel_t.T.astype(bf16)
    for c in range(nchunk):
        selexp = jnp.dot(sel, e_ref[:, c * TK:(c + 1) * TK], preferred_element_type=f32)
        kpos = c * TK + lax.broadcasted_iota(jnp.int32, (TQ, TK), 1)
        bias_scr[c] = jnp.where((selexp > 0.5) & (kpos <= qpos_col), 0.0, NEG_INF)

    hi = (i * TQ + TQ - 1) // TK + 1

    def attend(k_ref, v_ref, lo, masker):
        m_scr[...] = jnp.full(m_scr.shape, NEG_INF, f32)
        acc_scr[...] = jnp.zeros(acc_scr.shape, f32)

        def chunk(c, carry):
            rows = pl.ds(pl.multiple_of(c * TK, TK), TK)
            k = k_ref[rows, :].astype(bf16)
            v = jnp.concatenate([v_ref[rows, :].astype(bf16), jnp.ones((TK, HEAD_DIM), bf16)], axis=1)
            mk = masker(c)
            heads = range(R)
            sk = [mk(_dot_nt(q_ref[:, r * HEAD_DIM:(r + 1) * HEAD_DIM], k)) for r in heads]
            m_prev = [m_scr[r] for r in heads]
            m_new = [jnp.maximum(m_prev[r], jnp.max(sk[r][0], axis=-1, keepdims=True)) for r in heads]
            alpha = [jnp.exp(m_prev[r] - m_new[r]) for r in heads]
            ps = [jnp.exp(sk[r][0] - m_new[r]) for r in heads]
            ps = [p if sk[r][1] is None else jnp.where(sk[r][1], p, 0.0) for r, p in enumerate(ps)]
            pv = [jnp.dot(ps[r].astype(bf16), v, preferred_element_type=f32) for r in heads]
            for r in heads:
                acc_scr[r] = alpha[r] * acc_scr[r] + pv[r]
                m_scr[r] = m_new[r]
            return carry

        lax.fori_loop(lo, hi, chunk, 0)

    def normalised(r):
        return acc_scr[r, :, :HEAD_DIM] / acc_scr[r, :, HEAD_DIM:]

    def sel_masker(c):
        b = bias_scr[c]
        return lambda s: (s + b, None)

    attend(ks_ref, vs_ref, 0, sel_masker)
    g = pltpu.roll(jax.nn.sigmoid(g_ref[...]), (LANE - 3 * R * pl.program_id(1)) % LANE, axis=1)
    for r in range(R):
        ocmp_scr[r] = (g[:, 3 * r:3 * r + 1] * ocmp_scr[r]
                       + g[:, 3 * r + 1:3 * r + 2] * normalised(r))

    def win_masker(c):
        rel = (i * TQ - c * TK + lax.broadcasted_iota(jnp.int32, (TQ, TK), 0)
               - lax.broadcasted_iota(jnp.int32, (TQ, TK), 1))
        ok = (rel >= 0) & (rel < WINDOW)
        return lambda s: (jnp.where(ok, s, NEG_INF), ok)

    attend(kw_ref, vw_ref, jnp.maximum(i * TQ - (WINDOW - 1), 0) // TK, win_masker)
    for r in range(R):
        o = ocmp_scr[r] + g[:, 3 * r + 2:3 * r + 3] * normalised(r)
        o_ref[:, r * HEAD_DIM:(r + 1) * HEAD_DIM] = o.astype(o_ref.dtype)


def nsa_attention_pallas(qr, kcmp, vcmp, slc, win, hn, gate_col):
    B, T, _ = qr.shape
    G, R = NSA_KV_HEADS, NSA_GROUP
    TQ = min(NSA_TQ, T)
    TK = min(NSA_TK, T)
    NS = T // SEL_BLOCK
    NC = T // CMP_STRIDE - 1
    NCP = kcmp.shape[2]
    n_top = min(SEL_TOP, NS)
    ci = np.arange(NCP)[:, None] * CMP_STRIDE
    sj = np.arange(NS)[None, :] * SEL_BLOCK
    cover = np.clip(np.minimum(ci + CMP_BLOCK, sj + SEL_BLOCK) - np.maximum(ci, sj), 0, None) / CMP_BLOCK
    cover[NC:] = 0.0
    covt = jnp.asarray(np.tile(cover.T, (1, R)), jnp.bfloat16)
    e = jnp.asarray((np.arange(T)[None, :] // SEL_BLOCK) == np.arange(LANE)[:, None], jnp.bfloat16)
    k_spec = pl.BlockSpec((None, T, HEAD_DIM), lambda b, g, i: (b, 0, g))
    v_spec = pl.BlockSpec((None, T, HEAD_DIM), lambda b, g, i: (b, 0, G + g))
    cmp_spec = pl.BlockSpec((None, None, NCP, HEAD_DIM), lambda b, g, i: (b, g, 0, 0))
    return pl.pallas_call(
        functools.partial(_nsa_kernel, TQ=TQ, TK=TK, T=T, NS=NS, NCP=NCP, n_top=n_top),
        grid=(B, G, T // TQ),
        in_specs=[pl.BlockSpec((None, TQ, R * HEAD_DIM), lambda b, g, i: (b, i, g)),
                  cmp_spec, cmp_spec, k_spec, v_spec, k_spec, v_spec,
                  pl.BlockSpec((None, TQ, LANE), lambda b, g, i: (b, i, gate_col // LANE)),
                  pl.BlockSpec((NS, R * NCP), lambda b, g, i: (0, 0)),
                  pl.BlockSpec((LANE, T), lambda b, g, i: (0, 0))],
        out_specs=pl.BlockSpec((None, TQ, R * HEAD_DIM), lambda b, g, i: (b, i, g)),
        out_shape=jax.ShapeDtypeStruct((B, T, G * R * HEAD_DIM), jnp.bfloat16),
        scratch_shapes=[pltpu.VMEM((T // TK, TQ, TK), jnp.float32),
                        pltpu.VMEM((TQ, R * NCP), jnp.bfloat16),
                        pltpu.VMEM((R, TQ, HEAD_DIM), jnp.float32),
                        pltpu.VMEM((R, TQ, 1), jnp.float32),
                        pltpu.VMEM((R, TQ, 2 * HEAD_DIM), jnp.float32)],
        compiler_params=pltpu.CompilerParams(dimension_semantics=("parallel", "parallel", "arbitrary"),
                                             vmem_limit_bytes=VMEM_LIMIT),
        name="nsa_attention",
    )(qr, kcmp, vcmp, slc, slc, win, win, hn, covt, e)


def _dec_compress_kernel(pt_ref, *refs):
    f32, bf16 = jnp.float32, jnp.bfloat16
    pages, w1_ref, o_ref, reg_scr = refs[:DEC_PAGES], refs[DEC_PAGES], refs[DEC_PAGES + 1], refs[DEC_PAGES + 2]
    HD = HEAD_DIM
    nc = 2 * NSA_KV_HEADS
    seg = pages[0].shape[0] // (nc * CMP_STRIDE)
    acc = None
    for p in range(CMP_STRIDE):
        lhs = jnp.concatenate([pg[(CMP_STRIDE * n + p) * nc:(CMP_STRIDE * n + p + 1) * nc, :]
                               for pg in pages for n in range(seg)], axis=0).astype(bf16)
        w = jnp.concatenate([w1_ref[0, p], w1_ref[1, p]], axis=1)
        part = jnp.dot(lhs, w, preferred_element_type=f32)
        acc = part if acc is None else acc + part
    is_k = (lax.broadcasted_iota(jnp.int32, (acc.shape[0], 2 * HD), 0) % nc) < NSA_KV_HEADS
    fs = jnp.where(is_k, acc[:, :2 * HD], acc[:, 2 * HD:])
    reg_scr[0] = fs[:, :HD]
    reg_scr[1] = fs[:, HD:]
    nseg = DEC_PAGES * seg
    for c in range(nc):
        o_ref[:, c * 2 * HD:c * 2 * HD + HD] = reg_scr[0, pl.ds(c, nseg, stride=nc), :]
        o_ref[:, c * 2 * HD + HD:(c + 1) * 2 * HD] = reg_scr[1, pl.ds(c, nseg, stride=nc), :]


def dec_compress(cache, layer, page_table, w1):
    L, n_phys, page = cache.shape[:3]
    nc = 2 * NSA_KV_HEADS
    B, n_pages = page_table.shape
    seg = page // CMP_STRIDE
    steps = n_pages // DEC_PAGES
    c4 = cache.reshape(L, n_phys, page * nc, HEAD_DIM)
    page_spec = lambda k: pl.BlockSpec((None, None, page * nc, HEAD_DIM),
                                       lambda b, s, pt: (layer, pt[b, s * DEC_PAGES + k], 0, 0))
    ow = nc * 2 * HEAD_DIM
    return pl.pallas_call(
        _dec_compress_kernel,
        grid_spec=pltpu.PrefetchScalarGridSpec(
            num_scalar_prefetch=1,
            grid=(B, steps),
            in_specs=[page_spec(k) for k in range(DEC_PAGES)] + [pl.BlockSpec(w1.shape, lambda b, s, pt: (0, 0, 0, 0))],
            out_specs=pl.BlockSpec((None, DEC_PAGES * seg, ow), lambda b, s, pt: (b, s, 0)),
            scratch_shapes=[pltpu.VMEM((2, DEC_PAGES * seg * nc, HEAD_DIM), jnp.float32)]),
        out_shape=jax.ShapeDtypeStruct((B, n_pages * seg, ow), jnp.float32),
        compiler_params=pltpu.CompilerParams(dimension_semantics=("parallel", "arbitrary"),
                                             vmem_limit_bytes=VMEM_LIMIT),
        name="nsa_dec_compress",
    )(page_table, *([c4] * DEC_PAGES), w1)


def _dec_select_kernel(q_ref, fk_ref, fv_ref, w1_ref, w2_ref, pe_ref, covt_ref, ocmp_ref, idx_ref, *, NSEG, NS, NSP, n_top):
    f32, bf16 = jnp.float32, jnp.bfloat16
    HD, R = HEAD_DIM, NSA_GROUP
    NC = NSEG - 1

    def compressed(kv, f_ref):
        pos = jnp.zeros((SUBLANE, HD), f32)
        for p in range(CMP_STRIDE):
            w = w1_ref[kv, p]
            lo = jnp.broadcast_to(pe_ref[kv, p:p + 1, :], (SUBLANE, HD)).astype(bf16)
            hi = jnp.broadcast_to(pe_ref[kv, CMP_STRIDE + p:CMP_STRIDE + p + 1, :], (SUBLANE, HD)).astype(bf16)
            pos = pos + jnp.dot(lo, w, preferred_element_type=f32)[:, :HD] + jnp.dot(hi, w, preferred_element_type=f32)[:, HD:]
        nxt = pltpu.roll(f_ref[:, HD:], NSEG - 1, axis=0)
        hid = jax.nn.gelu(f_ref[:, :HD] + nxt + pos[0:1])
        return jnp.dot(hid.astype(bf16), w2_ref[kv], preferred_element_type=f32).astype(bf16)

    kc = compressed(0, fk_ref)
    vc = compressed(1, fv_ref)
    q = q_ref[...]
    q4 = jnp.concatenate([q[:, r * HD:(r + 1) * HD] for r in range(R)] + [jnp.zeros((SUBLANE - R, HD), bf16)], axis=0)
    s = _dot_nt(q4, kc)
    valid = lax.broadcasted_iota(jnp.int32, s.shape, 1) < NC
    s = jnp.where(valid, s, NEG_INF)
    p = jnp.where(valid, jnp.exp(s - jnp.max(s, axis=-1, keepdims=True)), 0.0)
    p = (p / jnp.sum(p, axis=-1, keepdims=True)).astype(bf16)
    ocmp_ref[...] = jnp.dot(p, vc, preferred_element_type=f32)
    head = lax.broadcasted_iota(jnp.int32, p.shape, 0) < R
    imp = _dot_nt(covt_ref[...], jnp.where(head, p, jnp.zeros_like(p)))
    imp = jnp.sum(imp, axis=1, keepdims=True)
    j_col = lax.broadcasted_iota(jnp.int32, (NSP, 1), 0)
    qblk = NS - 1
    forced = (j_col < N_INIT_BLOCKS) | ((j_col <= qblk) & (j_col > qblk - N_LOCAL_BLOCKS))
    score_col = jnp.where(forced, FORCE_SCORE, jnp.where(j_col <= qblk, imp, -3e38))
    score_cb = jnp.broadcast_to(score_col, (NSP, LANE))
    score_row = score_cb.T[0:1, :]
    ii = lax.broadcasted_iota(jnp.int32, (NSP, NSP), 0)
    jj = lax.broadcasted_iota(jnp.int32, (NSP, NSP), 1)
    beats = (score_col > score_row) | ((score_col == score_row) & (ii < jj))
    rank = jnp.sum(jnp.where(beats, 1.0, 0.0), axis=0, keepdims=True)
    t_col = lax.broadcasted_iota(jnp.int32, (n_top, NSP), 0).astype(f32)
    j_row = lax.broadcasted_iota(jnp.int32, (n_top, NSP), 1).astype(f32)
    ids = jnp.sum(jnp.where(rank == t_col, j_row, 0.0), axis=1, keepdims=True)
    idx_ref[...] = jnp.broadcast_to(ids, (n_top, LANE)).astype(jnp.int32)


def dec_select(qr, fs, w1, w2, pe, past_len):
    B = qr.shape[0]
    G, R, HD = NSA_KV_HEADS, NSA_GROUP, HEAD_DIM
    NSEG = fs.shape[1]
    NC = NSEG - 1
    NS = -(-(past_len + 1) // SEL_BLOCK)
    NSP = _round_up(NS, LANE)
    n_top = min(SEL_TOP, NS)
    ci = np.arange(NSEG)[:, None] * CMP_STRIDE
    sj = np.arange(NSP)[None, :] * SEL_BLOCK
    cover = np.clip(np.minimum(ci + CMP_BLOCK, sj + SEL_BLOCK) - np.maximum(ci, sj), 0, None) / CMP_BLOCK
    cover[NC:] = 0.0
    cover[:, NS:] = 0.0
    covt = jnp.asarray(cover.T, jnp.bfloat16)
    f_spec = lambda kv: pl.BlockSpec((None, NSEG, 2 * HD), lambda b, g: (b, 0, kv * G + g))
    full = lambda a: pl.BlockSpec(a.shape, lambda b, g: (0,) * a.ndim)
    ocmp, idx = pl.pallas_call(
        functools.partial(_dec_select_kernel, NSEG=NSEG, NS=NS, NSP=NSP, n_top=n_top),
        grid=(B, G),
        in_specs=[pl.BlockSpec((None, 1, R * HD), lambda b, g: (b, 0, g)), f_spec(0), f_spec(1),
                  full(w1), full(w2), full(pe), full(covt)],
        out_specs=[pl.BlockSpec((None, None, SUBLANE, HD), lambda b, g: (b, g, 0, 0)),
                   pl.BlockSpec((None, None, n_top, LANE), lambda b, g: (b, g, 0, 0))],
        out_shape=[jax.ShapeDtypeStruct((B, G, SUBLANE, HD), jnp.float32),
                   jax.ShapeDtypeStruct((B, G, n_top, LANE), jnp.int32)],
        compiler_params=pltpu.CompilerParams(dimension_semantics=("parallel", "parallel"),
                                             vmem_limit_bytes=VMEM_LIMIT),
        name="nsa_dec_select",
    )(qr, fs, fs, w1, w2, pe, covt)
    return ocmp, idx[:, :, :, 0]


def _dec_attend_kernel(pt_ref, idx_ref, q_ref, *refs, NS, n_top):
    f32, bf16 = jnp.float32, jnp.bfloat16
    HD, R, G = HEAD_DIM, NSA_GROUP, NSA_KV_HEADS
    blocks = refs[:n_top]
    nks_ref, nvs_ref, wb_ref, nkw_ref, nvw_ref, ocmp_ref, g_ref, o_ref = refs[n_top:]
    b, g = pl.program_id(0), pl.program_id(1)
    q = q_ref[...]
    q4 = jnp.concatenate([q[:, r * HD:(r + 1) * HD] for r in range(R)] + [jnp.zeros((SUBLANE - R, HD), bf16)], axis=0)

    def head_rows(ref, kv):
        return ref[pl.ds(kv * G + g, ref.shape[0] // (2 * G), stride=2 * G), :].astype(bf16)

    def attend(keys, vals, bias, k_new_ref, v_new_ref):
        k_new = jnp.broadcast_to(k_new_ref[...], (SUBLANE, HD)).astype(bf16).astype(f32)
        v_new = jnp.broadcast_to(v_new_ref[...], (SUBLANE, HD)).astype(bf16).astype(f32)
        s = _dot_nt(q4, keys) + bias
        s_new = jnp.sum(q4.astype(f32) * k_new, axis=-1, keepdims=True)
        m = jnp.maximum(jnp.max(s, axis=-1, keepdims=True), s_new)
        p = jnp.exp(s - m)
        p_new = jnp.exp(s_new - m)
        l = jnp.sum(p, axis=-1, keepdims=True) + p_new
        return (jnp.dot(p.astype(bf16), vals, preferred_element_type=f32) + p_new.astype(bf16).astype(f32) * v_new) / l

    keys = jnp.concatenate([head_rows(r, 0) for r in blocks], axis=0)
    vals = jnp.concatenate([head_rows(r, 1) for r in blocks], axis=0)
    bias = jnp.concatenate(
        [jnp.broadcast_to(jnp.where(idx_ref[b, g, t] != NS - 1, 0.0, NEG_INF), (SUBLANE, SEL_BLOCK)) for t in range(n_top)],
        axis=1)
    o_slc = attend(keys, vals, bias, nks_ref, nvs_ref)
    nwin = wb_ref.shape[0] // (2 * G)
    ok = lax.broadcasted_iota(jnp.int32, (SUBLANE, nwin), 1) > nwin - WINDOW
    o_win = attend(head_rows(wb_ref, 0), head_rows(wb_ref, 1), jnp.where(ok, 0.0, NEG_INF), nkw_ref, nvw_ref)
    gate = pltpu.roll(jax.nn.sigmoid(jnp.broadcast_to(g_ref[...], (SUBLANE, LANE))), (LANE - 3 * R * g) % LANE, axis=1)
    rows = []
    for r in range(R):
        rows.append(gate[r:r + 1, 3 * r:3 * r + 1] * ocmp_ref[r:r + 1, :] + gate[r:r + 1, 3 * r + 1:3 * r + 2] * o_slc[r:r + 1, :]
                    + gate[r:r + 1, 3 * r + 2:3 * r + 3] * o_win[r:r + 1, :])
    o_ref[...] = jnp.concatenate(rows, axis=1).astype(o_ref.dtype)


def dec_attend(qr, slc_cache, win_cache, layer, page_table, idx, new_slc, new_win, ocmp, hn, gate_col):
    B = qr.shape[0]
    G, R, HD = NSA_KV_HEADS, NSA_GROUP, HEAD_DIM
    L, n_phys, page = slc_cache.shape[:3]
    nc = 2 * G
    n_top = idx.shape[2]
    NS = -(-(page_table.shape[1] * page + 1) // SEL_BLOCK)
    per = page // SEL_BLOCK
    blocks = slc_cache.reshape(L, n_phys * per, SEL_BLOCK * nc, HD)
    nwin = win_cache.shape[2]
    wins = win_cache.reshape(L, B, nwin * nc, HD)

    def blk_spec(t):
        def index(b, g, pt, ix):
            j = jnp.minimum(ix[b, g, t], NS - 2)
            return (layer, pt[b, j // per] * per + j % per, 0, 0)
        return pl.BlockSpec((None, None, SEL_BLOCK * nc, HD), index)

    row = lambda col: pl.BlockSpec((None, 1, HD), lambda b, g, pt, ix: (b, 0, col(g)))
    kcol, vcol = (lambda g: g), (lambda g: G + g)
    return pl.pallas_call(
        functools.partial(_dec_attend_kernel, NS=NS, n_top=n_top),
        grid_spec=pltpu.PrefetchScalarGridSpec(
            num_scalar_prefetch=2,
            grid=(B, G),
            in_specs=[pl.BlockSpec((None, 1, R * HD), lambda b, g, pt, ix: (b, 0, g))]
                     + [blk_spec(t) for t in range(n_top)]
                     + [row(kcol), row(vcol),
                        pl.BlockSpec((None, None, nwin * nc, HD), lambda b, g, pt, ix: (layer, b, 0, 0)),
                        row(kcol), row(vcol),
                        pl.BlockSpec((None, None, SUBLANE, HD), lambda b, g, pt, ix: (b, g, 0, 0)),
                        pl.BlockSpec((None, 1, LANE), lambda b, g, pt, ix: (b, 0, gate_col // LANE))],
            out_specs=pl.BlockSpec((None, 1, R * HD), lambda b, g, pt, ix: (b, 0, g))),
        out_shape=jax.ShapeDtypeStruct((B, 1, G * R * HD), jnp.bfloat16),
        compiler_params=pltpu.CompilerParams(dimension_semantics=("parallel", "parallel"),
                                             vmem_limit_bytes=VMEM_LIMIT),
        name="nsa_dec_attend",
    )(page_table, idx, qr, *([blocks] * n_top), new_slc, new_slc, wins, new_win, new_win, ocmp, hn)


def trunk_layer(x, pos0, paged, gla_s0, rwkv_s0, shift0, conv0, lw):
    B, T, _ = x.shape
    dt = x.dtype
    f32 = jnp.float32
    pos = pos0 + jnp.arange(T, dtype=jnp.int32)
    heads = lambda t, n: t.reshape(B, T, n, t.shape[-1] // n)

    bf = jnp.bfloat16
    M = B * T
    x2 = x.reshape(M, D_MODEL)
    xn = rmsnorm_pallas(x2, lw['norm1'])
    hg = mm(xn, lw['w_gla']).reshape(B, T, -1)
    hn = mm(xn, lw['w_nsa']).reshape(B, T, -1)
    hr = mm(xn, lw['w_rwkv']).reshape(B, T, -1)
    mg = mm(xn, lw['w_mg'])

    o_gla, gla_s = gla_pallas(hg, lw['gla_wa2'], lw['gla_ba'], lw['gla_norm'], gla_s0.astype(f32), GLA_OFF)
    o_gla = o_gla.reshape(M, GLA_WIDTH)

    kv5 = lambda t: t.reshape(B, T, 2, NSA_KV_HEADS, HEAD_DIM)
    cw1, cw2, cpe = compress_weights(lw['cmp_w1k'], lw['cmp_w2k'], lw['cmp_pek'],
                                     lw['cmp_w1v'], lw['cmp_w2v'], lw['cmp_pev'])
    if paged is None:
        assert T % NSA_TQ == 0
        qr, cmp2, slc2, win2 = nsa_prep(hn, pos)
        kcmp, vcmp = compress_pallas(cmp2, cw1, cw2, cpe)
        o_nsa = nsa_attention_pallas(qr, kcmp, vcmp, slc2, win2, hn, NSA_GATE_OFF)
        new_cmp, new_slc, win_new = kv5(cmp2), kv5(slc2), kv5(win2)[:, T - min(WINDOW, T):]
    else:
        assert T == 1
        cache_cmp, cache_slc, cache_win, layer, page_table = paged
        qr, cmp2, slc2, win2 = [t.reshape(B, 1, -1) for t in
                                nsa_prep(hn.reshape(1, B, -1), jnp.full((B,), pos0, jnp.int32))]
        fs = dec_compress(cache_cmp, layer, page_table, cw1)
        ocmp, sel_ids = dec_select(qr, fs, cw1, cw2, cpe, pos0)
        o_nsa = dec_attend(qr, cache_slc, cache_win, layer, page_table, sel_ids, slc2, win2, ocmp, hn, NSA_GATE_OFF)
        new_cmp, new_slc = kv5(cmp2), kv5(slc2)
        win_all = jnp.concatenate([cache_win[layer].astype(dt), kv5(win2)], axis=1)
        win_new = win_all[:, win_all.shape[1] - min(WINDOW, win_all.shape[1]):]
    o_nsa = o_nsa.reshape(M, NSA_WIDTH)

    (r_, w_log, k2, v_, kk, a, gate, bonus), shift_new = rwkv_prep(
        hr, shift0.astype(f32), lw['rwkv_mu'], lw['rwkv_w0'], lw['rwkv_w2'], lw['rwkv_a0'], lw['rwkv_a2'],
        lw['rwkv_g2'], lw['rwkv_kk'], lw['rwkv_ka'], lw['rwkv_rk'])
    y, rwkv_s = rwkv_scan_pallas(r_, w_log, k2, v_, kk, a, rwkv_s0.astype(f32))
    flat = lambda t: t.reshape(M, RWKV_WIDTH)
    o_rwkv = rwkv_post(flat(y), flat(bonus), flat(gate), lw['rwkv_ln_w'], lw['rwkv_ln_b'])

    merged = merge_mm(o_gla, o_nsa, o_rwkv, lw['w_o_gla'], lw['w_o_nsa'], lw['w_o_rwkv'], mg)
    x2 = mm(merged, lw['w_out'], res=x2)

    xn2 = rmsnorm_pallas(x2, lw['norm2'])
    act, conv_new = ffn_gate_up(xn2, lw['ffn_gate'], lw['ffn_up'], lw['ffn_conv'], lw['ffn_conv_b'],
                                conv0.astype(f32), B, T)
    x2 = mm(act, lw['ffn_down'], res=x2)
    return x2.reshape(B, T, D_MODEL), (new_cmp, new_slc, win_new, gla_s, rwkv_s, shift_new, conv_new)


def _w_in_group(w, lo, hi):
    seg = w[:, lo:hi].astype(jnp.bfloat16)
    return jnp.pad(seg, ((0, 0), (0, _round_up(hi - lo, W_IN_TILE) - (hi - lo))))


def kernel(x_prompt, x_sample, cache_cmp_kv, cache_slc_kv, cache_win_kv, state_gla, state_rwkv, state_rwkv_shift, state_ffn_conv, page_table, norm1, w_in, gla_wa2, gla_ba, gla_norm, w_o_gla, cmp_w1k, cmp_w2k, cmp_pek, cmp_w1v, cmp_w2v, cmp_pev, w_o_nsa, rwkv_mu, rwkv_w0, rwkv_w2, rwkv_a0, rwkv_a2, rwkv_g2, rwkv_kk, rwkv_ka, rwkv_rk, rwkv_ln_w, rwkv_ln_b, w_o_rwkv, w_out, norm2, ffn_gate, ffn_conv, ffn_conv_b, ffn_up, ffn_down, norm_f):
    G, HD = NSA_KV_HEADS, HEAD_DIM
    n_db, n_pages = page_table.shape
    past_len = n_pages * PAGE_SIZE
    bp = x_prompt.shape[0]
    dt = x_prompt.dtype
    bf = jnp.bfloat16
    xp, xs = x_prompt, x_sample
    st_p, st_s = [], []
    for l in range(DEPTH):
        lw = {'norm1': norm1[l], 'w_gla': _w_in_group(w_in[l], 0, _C_NSA),
              'w_nsa': _w_in_group(w_in[l], _C_NSA, _C_RWKV), 'w_rwkv': _w_in_group(w_in[l], _C_RWKV, _C_MG),
              'w_mg': _w_in_group(w_in[l], _C_MG, _C_MG + IN_SIZES[14]), 'gla_wa2': gla_wa2[l], 'gla_ba': gla_ba[l],
              'gla_norm': gla_norm[l], 'w_o_gla': w_o_gla[l].astype(bf), 'cmp_w1k': cmp_w1k[l], 'cmp_w2k': cmp_w2k[l],
              'cmp_pek': cmp_pek[l], 'cmp_w1v': cmp_w1v[l], 'cmp_w2v': cmp_w2v[l], 'cmp_pev': cmp_pev[l],
              'w_o_nsa': w_o_nsa[l].astype(bf), 'rwkv_mu': rwkv_mu[l], 'rwkv_w0': rwkv_w0[l], 'rwkv_w2': rwkv_w2[l],
              'rwkv_a0': rwkv_a0[l], 'rwkv_a2': rwkv_a2[l], 'rwkv_g2': rwkv_g2[l], 'rwkv_kk': rwkv_kk[l],
              'rwkv_ka': rwkv_ka[l], 'rwkv_rk': rwkv_rk[l], 'rwkv_ln_w': rwkv_ln_w[l], 'rwkv_ln_b': rwkv_ln_b[l],
              'w_o_rwkv': w_o_rwkv[l].astype(bf), 'w_out': w_out[l].astype(bf), 'norm2': norm2[l],
              'ffn_gate': ffn_gate[l], 'ffn_conv': ffn_conv[l], 'ffn_conv_b': ffn_conv_b[l],
              'ffn_up': ffn_up[l], 'ffn_down': ffn_down[l].astype(bf)}
        xp, sp = trunk_layer(xp, 0, None,
                             jnp.zeros((bp, GLA_HEADS, GLA_DK, GLA_DV), jnp.float32),
                             jnp.zeros((bp, RWKV_HEADS, RWKV_N, RWKV_N), jnp.float32),
                             jnp.zeros((bp, RWKV_COLS), dt),
                             jnp.zeros((bp, CONV_W - 1, D_FF), dt), lw)
        paged = (cache_cmp_kv, cache_slc_kv, cache_win_kv, l, page_table)
        xs, ss = trunk_layer(xs, past_len, paged, state_gla[l], state_rwkv[l],
                             state_rwkv_shift[l], state_ffn_conv[l], lw)
        st_p.append(sp)
        st_s.append(ss)
    y_prompt = rmsnorm_pallas(xp.reshape(-1, D_MODEL), norm_f, out_dtype=dt).reshape(xp.shape)
    y_sample = rmsnorm_pallas(xs.reshape(-1, D_MODEL), norm_f, out_dtype=dt).reshape(xs.shape)
    outs = [y_prompt, y_sample]
    for i in range(7):
        outs.append(jnp.stack([s[i] for s in st_p]))
        outs.append(jnp.stack([s[i] for s in st_s]))
    return tuple(outs)
```

```python
import functools

import jax
import jax.numpy as jnp
import numpy as np
from jax import lax
from jax.experimental import pallas as pl
from jax.experimental.pallas import tpu as pltpu

D_MODEL = 4096
DEPTH = 2
PAGE_SIZE = 128
HEAD_DIM = 128
ROPE_DIM = HEAD_DIM // 4
ROPE_THETA = 500000.0
NORM_EPS = 1e-5
NEG_INF = -1e30

GLA_WIDTH = D_MODEL // 4
GLA_HEADS = 4
GLA_DV = GLA_WIDTH // GLA_HEADS
GLA_DK = GLA_DV // 2
GLA_GATE_RANK = 16
GLA_TAU = 16.0
GLA_CHUNK = 64

NSA_HEADS = D_MODEL // 256
NSA_KV_HEADS = 4
NSA_GROUP = NSA_HEADS // NSA_KV_HEADS
NSA_WIDTH = NSA_HEADS * HEAD_DIM
NSA_KV_WIDTH = NSA_KV_HEADS * HEAD_DIM
CMP_STRIDE = 16
CMP_BLOCK = 2 * CMP_STRIDE
SEL_BLOCK = 64
SEL_TOP = 16
N_INIT_BLOCKS = 1
N_LOCAL_BLOCKS = 2
WINDOW = 512
FORCE_SCORE = 1e4

RWKV_WIDTH = D_MODEL // 4
RWKV_N = 64
RWKV_HEADS = RWKV_WIDTH // RWKV_N
RWKV_DECAY_RANK = 64
RWKV_AAA_RANK = 64
RWKV_GATE_RANK = 160
RWKV_SIZES = (RWKV_WIDTH, RWKV_WIDTH, RWKV_WIDTH, RWKV_DECAY_RANK, RWKV_AAA_RANK, RWKV_GATE_RANK)
RWKV_COLS = sum(RWKV_SIZES)
RWKV_LN_EPS = 64e-5

N_BRANCH = 3
D_FF = 256 * ((8 * D_MODEL // 3 + 255) // 256)
CONV_W = 3

IN_SIZES = (GLA_HEADS * GLA_DK, GLA_HEADS * GLA_DK, GLA_WIDTH, GLA_WIDTH, GLA_GATE_RANK,
            NSA_WIDTH, NSA_KV_WIDTH, NSA_KV_WIDTH, NSA_KV_WIDTH, NSA_KV_WIDTH, NSA_KV_WIDTH, NSA_KV_WIDTH,
            NSA_HEADS * 3,
            RWKV_COLS,
            N_BRANCH * D_MODEL)

LANE = 128
SUBLANE = 8
NSA_TQ = 512
NSA_TK = 1024
GLA_TC = 512
DEC_PAGES = 16
RWKV_NB = 2
VMEM_LIMIT = 48 * 1024 * 1024


def _round_up(n, m):
    return -(-n // m) * m


W_IN_TILE = 512
_C_NSA = sum(IN_SIZES[:5])
_C_RWKV = sum(IN_SIZES[:13])
_C_MG = sum(IN_SIZES[:14])
GLA_OFF = tuple(int(o) for o in np.concatenate([[0], np.cumsum(IN_SIZES[:4])]))
NSA_GATE_OFF = NSA_WIDTH + 6 * NSA_KV_WIDTH


def _pick(n, cands):
    for c in cands:
        if n % c == 0:
            return c
    return n


def _rmsnorm_kernel(x_ref, g_ref, o_ref):
    x = x_ref[...]
    y = x * lax.rsqrt(jnp.mean(x * x, axis=-1, keepdims=True) + NORM_EPS)
    o_ref[...] = (y * g_ref[...]).astype(o_ref.dtype)


def rmsnorm_pallas(x, g, out_dtype=jnp.bfloat16):
    M, D = x.shape
    tm = _pick(M, (512, 256, 128, 64, 32, 16, 8))
    return pl.pallas_call(
        _rmsnorm_kernel,
        grid=(M // tm,),
        in_specs=[pl.BlockSpec((tm, D), lambda i: (i, 0)), pl.BlockSpec((1, D), lambda i: (0, 0))],
        out_specs=pl.BlockSpec((tm, D), lambda i: (i, 0)),
        out_shape=jax.ShapeDtypeStruct((M, D), out_dtype),
        compiler_params=pltpu.CompilerParams(dimension_semantics=("parallel",), vmem_limit_bytes=VMEM_LIMIT),
        name="rmsnorm",
    )(x, g.reshape(1, D))


def _mm_kernel(*refs, nk, has_res):
    x_ref, w_ref = refs[:2]
    res_ref = refs[2] if has_res else None
    o_ref, acc_ref = refs[-2:]
    k = pl.program_id(2)
    part = jnp.dot(x_ref[...], w_ref[...], preferred_element_type=jnp.float32)

    def finish(v):
        if has_res:
            v = v + res_ref[...]
        o_ref[...] = v.astype(o_ref.dtype)

    if nk == 1:
        finish(part)
    else:
        @pl.when(k == 0)
        def _():
            acc_ref[...] = part

        @pl.when(jnp.logical_and(k > 0, k < nk - 1))
        def _():
            acc_ref[...] += part

        @pl.when(k == nk - 1)
        def _():
            finish(acc_ref[...] + part)


def mm(x, w, res=None, out_dtype=jnp.float32):
    M, K = x.shape
    N = w.shape[1]
    tm = _pick(M, (1024, 512, 256, 128, 64, 32, 16, 8))
    tn = _pick(N, (512, 256, 128))
    tk = K if K <= 4096 else _pick(K, (5504, 4096, 2048, 1024, 512))
    nk = K // tk
    in_specs = [pl.BlockSpec((tm, tk), lambda i, j, k: (i, k)),
                pl.BlockSpec((tk, tn), lambda i, j, k: (k, j))]
    args = [x, w]
    if res is not None:
        in_specs.append(pl.BlockSpec((tm, tn), lambda i, j, k: (i, j)))
        args.append(res)
    return pl.pallas_call(
        functools.partial(_mm_kernel, nk=nk, has_res=res is not None),
        grid=(M // tm, N // tn, nk),
        in_specs=in_specs,
        out_specs=pl.BlockSpec((tm, tn), lambda i, j, k: (i, j)),
        out_shape=jax.ShapeDtypeStruct((M, N), out_dtype),
        scratch_shapes=[pltpu.VMEM((tm, tn) if nk > 1 else (SUBLANE, LANE), jnp.float32)],
        compiler_params=pltpu.CompilerParams(
            dimension_semantics=("parallel", "parallel", "arbitrary"),
            vmem_limit_bytes=VMEM_LIMIT),
        name="dense_mm",
    )(*args)


def _merge_kernel(oa_ref, ob_ref, oc_ref, wa_ref, wb_ref, wc_ref, ga_ref, gb_ref, gc_ref, o_ref):
    f32 = jnp.float32
    acc = jax.nn.sigmoid(ga_ref[...]) * jnp.dot(oa_ref[...], wa_ref[...], preferred_element_type=f32)
    acc += jax.nn.sigmoid(gb_ref[...]) * jnp.dot(ob_ref[...], wb_ref[...], preferred_element_type=f32)
    acc += jax.nn.sigmoid(gc_ref[...]) * jnp.dot(oc_ref[...], wc_ref[...], preferred_element_type=f32)
    o_ref[...] = acc.astype(o_ref.dtype)


def merge_mm(o_a, o_b, o_c, w_a, w_b, w_c, mg, out_dtype=jnp.bfloat16):
    M = o_a.shape[0]
    D = w_a.shape[1]
    tm = _pick(M, (1024, 512, 256, 128, 64, 32, 16, 8))
    tn = _pick(D, (512, 256, 128))
    nj = D // tn
    o_spec = lambda o: pl.BlockSpec((tm, o.shape[1]), lambda i, j: (i, 0))
    w_spec = lambda w: pl.BlockSpec((w.shape[0], tn), lambda i, j: (0, j))
    g_spec = lambda b: pl.BlockSpec((tm, tn), lambda i, j: (i, b * nj + j))
    return pl.pallas_call(
        _merge_kernel,
        grid=(M // tm, nj),
        in_specs=[o_spec(o_a), o_spec(o_b), o_spec(o_c), w_spec(w_a), w_spec(w_b), w_spec(w_c),
                  g_spec(0), g_spec(1), g_spec(2)],
        out_specs=pl.BlockSpec((tm, tn), lambda i, j: (i, j)),
        out_shape=jax.ShapeDtypeStruct((M, D), out_dtype),
        compiler_params=pltpu.CompilerParams(dimension_semantics=("parallel", "parallel"),
                                             vmem_limit_bytes=VMEM_LIMIT),
        name="merge_mm",
    )(o_a, o_b, o_c, w_a, w_b, w_c, mg, mg, mg)


def _ffn_gate_up_kernel(x_ref, wg_ref, wu_ref, cw_ref, cb_ref, st_ref, act_ref, tail_ref, carry_scr, *, tm, tps, T):
    f32 = jnp.float32
    i, j = pl.program_id(0), pl.program_id(1)
    x = x_ref[...]
    h = jnp.dot(x, wg_ref[...].astype(x.dtype), preferred_element_type=f32)
    u = jnp.dot(x, wu_ref[...].astype(x.dtype), preferred_element_type=f32)
    cw = cw_ref[...]
    if T == 1:
        prev2, prev1 = st_ref[0], st_ref[1]
        tail_ref[0] = prev1
        tail_ref[1] = h
    else:
        tail = jnp.where(i % tps == 0, st_ref[...], carry_scr[j])
        row = lax.broadcasted_iota(jnp.int32, h.shape, 0)
        prev1 = jnp.where(row == 0, tail[7:8], pltpu.roll(h, 1, axis=0))
        prev2 = jnp.where(row == 0, tail[6:7], jnp.where(row == 1, tail[7:8], pltpu.roll(h, 2, axis=0)))
        last = h[tm - SUBLANE:tm]
        carry_scr[j] = last
        tail_ref[...] = last
    hc = cb_ref[...] + prev2 * cw[0:1] + prev1 * cw[1:2] + h * cw[2:3]
    act_ref[...] = (jax.nn.silu(hc) * u).astype(act_ref.dtype)


def ffn_gate_up(xn, w_gate, w_up, conv_w, conv_b, conv0, B, T):
    M, D = xn.shape
    F = w_gate.shape[1]
    tn = _pick(F, (512, 256, 128))
    nj = F // tn
    cw = jnp.pad(conv_w, ((0, SUBLANE - CONV_W), (0, 0)))
    cb = conv_b.reshape(1, F)
    if T == 1:
        tm, tps = M, 1
        st = conv0.transpose(1, 0, 2)
        st_spec = pl.BlockSpec((2, B, tn), lambda i, j: (0, 0, j))
        tail_shape, tail_spec = (2, B, F), pl.BlockSpec((2, B, tn), lambda i, j: (0, 0, j))
    else:
        tm = _pick(T, (1024, 512, 256, 128, 64, 32, 16, 8))
        tps = T // tm
        st = jnp.pad(conv0, ((0, 0), (SUBLANE - 2, 0), (0, 0)))
        st_spec = pl.BlockSpec((None, SUBLANE, tn), lambda i, j: (i // tps, 0, j))
        tail_shape, tail_spec = (M // tm, SUBLANE, F), pl.BlockSpec((None, SUBLANE, tn), lambda i, j: (i, 0, j))
    act, tail = pl.pallas_call(
        functools.partial(_ffn_gate_up_kernel, tm=tm, tps=tps, T=T),
        grid=(M // tm, nj),
        in_specs=[pl.BlockSpec((tm, D), lambda i, j: (i, 0)),
                  pl.BlockSpec((D, tn), lambda i, j: (0, j)),
                  pl.BlockSpec((D, tn), lambda i, j: (0, j)),
                  pl.BlockSpec((SUBLANE, tn), lambda i, j: (0, j)),
                  pl.BlockSpec((1, tn), lambda i, j: (0, j)),
                  st_spec],
        out_specs=[pl.BlockSpec((tm, tn), lambda i, j: (i, j)), tail_spec],
        out_shape=[jax.ShapeDtypeStruct((M, F), jnp.bfloat16), jax.ShapeDtypeStruct(tail_shape, jnp.float32)],
        scratch_shapes=[pltpu.VMEM((nj, SUBLANE, tn), jnp.float32)],
        compiler_params=pltpu.CompilerParams(dimension_semantics=("arbitrary", "arbitrary"),
                                             vmem_limit_bytes=VMEM_LIMIT),
        name="ffn_gate_up",
    )(xn, w_gate, w_up, cw, cb, st)
    conv_new = tail.transpose(1, 0, 2) if T == 1 else tail[tps - 1::tps, SUBLANE - 2:]
    return act, conv_new


def _gla_kernel(q_ref, k_ref, v_ref, og_ref, lo_ref, wa_ref, ba_ref, gn_ref, s0_ref, o_ref, sout_ref, s_scr, *, Tc, C, valid):
    f32, bf16 = jnp.float32, jnp.bfloat16
    c = pl.program_id(2)

    @pl.when(c == 0)
    def _():
        s_scr[...] = s0_ref[...]

    row = lax.broadcasted_iota(jnp.int32, (C, GLA_DK), 0)
    tril = lax.broadcasted_iota(jnp.int32, (C, C), 0) >= lax.broadcasted_iota(jnp.int32, (C, C), 1)
    wa = wa_ref[...]
    ba = ba_ref[...]
    gn = gn_ref[...]
    S = s_scr[...]
    for n in range(Tc // C):
        rows = slice(n * C, (n + 1) * C)
        la = jax.nn.log_sigmoid(jnp.dot(lo_ref[rows, :].astype(bf16), wa, preferred_element_type=f32) + ba) / GLA_TAU
        if valid < C:
            la = jnp.where(row < valid, la, 0.0)
        bc = la
        d = 1
        while d < C:
            bc = bc + jnp.where(row >= d, pltpu.roll(bc, d, axis=0), 0.0)
            d *= 2
        b_mid = bc[C // 2:C // 2 + 1]
        b_last = bc[C - 1:C]
        q = q_ref[rows, :] * GLA_DK ** -0.5
        k = k_ref[rows, :]
        v = v_ref[rows, :].astype(bf16)
        att = lax.dot_general((q * jnp.exp(bc - b_mid)).astype(bf16), (k * jnp.exp(b_mid - bc)).astype(bf16),
                              (((1,), (1,)), ((), ())), preferred_element_type=f32)
        att = jnp.where(tril, att, 0.0)
        o = jnp.dot(att.astype(bf16), v, preferred_element_type=f32)
        o = o + jnp.dot((q * jnp.exp(bc)).astype(bf16), S.astype(bf16), preferred_element_type=f32)
        kd = jnp.concatenate([k * jnp.exp(b_last - bc), jnp.broadcast_to(jnp.exp(b_last), (SUBLANE, GLA_DK))], axis=0)
        kdt = kd.T
        S = kdt[:, C:C + 1] * S + jnp.dot(kdt[:, :C].astype(bf16), v, preferred_element_type=f32)
        o = o * lax.rsqrt(jnp.mean(o * o, axis=-1, keepdims=True) + NORM_EPS) * gn
        o_ref[rows, :] = (o * jax.nn.silu(og_ref[rows, :])).astype(o_ref.dtype)
    s_scr[...] = S

    @pl.when(c == pl.num_programs(2) - 1)
    def _():
        sout_ref[...] = S


def gla_pallas(hmix, wa2, ba, gnorm, s0, seg_off):
    B, T, _ = hmix.shape
    H = GLA_HEADS
    n_tok = T
    if T % GLA_CHUNK == 0:
        Tc, C = min(GLA_TC, T), GLA_CHUNK
    else:
        assert T < SUBLANE
        Tc = C = SUBLANE
        hmix = jnp.pad(hmix, ((0, 0), (0, SUBLANE - T), (0, 0)))
        T = SUBLANE
    oq, ok, ov, og, ol = seg_off
    wa = jnp.pad(wa2, ((0, LANE - wa2.shape[0]), (0, 0))).astype(jnp.bfloat16)
    col = lambda off, w: (lambda b, h, c: (b, c, off // w + h))
    o, s = pl.pallas_call(
        functools.partial(_gla_kernel, Tc=Tc, C=C, valid=min(n_tok, C)),
        grid=(B, H, T // Tc),
        in_specs=[pl.BlockSpec((None, Tc, GLA_DK), col(oq, GLA_DK)),
                  pl.BlockSpec((None, Tc, GLA_DK), col(ok, GLA_DK)),
                  pl.BlockSpec((None, Tc, GLA_DV), col(ov, GLA_DV)),
                  pl.BlockSpec((None, Tc, GLA_DV), col(og, GLA_DV)),
                  pl.BlockSpec((None, Tc, LANE), lambda b, h, c: (b, c, ol // LANE)),
                  pl.BlockSpec((LANE, GLA_DK), lambda b, h, c: (0, h)),
                  pl.BlockSpec((1, GLA_DK), lambda b, h, c: (0, h)),
                  pl.BlockSpec((1, GLA_DV), lambda b, h, c: (0, 0)),
                  pl.BlockSpec((None, None, GLA_DK, GLA_DV), lambda b, h, c: (b, h, 0, 0))],
        out_specs=[pl.BlockSpec((None, Tc, GLA_DV), lambda b, h, c: (b, c, h)),
                   pl.BlockSpec((None, None, GLA_DK, GLA_DV), lambda b, h, c: (b, h, 0, 0))],
        out_shape=[jax.ShapeDtypeStruct((B, T, H * GLA_DV), jnp.bfloat16),
                   jax.ShapeDtypeStruct((B, H, GLA_DK, GLA_DV), jnp.float32)],
        scratch_shapes=[pltpu.VMEM((GLA_DK, GLA_DV), jnp.float32)],
        compiler_params=pltpu.CompilerParams(dimension_semantics=("parallel", "parallel", "arbitrary"),
                                             vmem_limit_bytes=VMEM_LIMIT),
        name="gla_chunked",
    )(hmix, hmix, hmix, hmix, hmix, wa, ba.reshape(1, -1), gnorm.reshape(1, -1), s0)
    return o[:, :n_tok], s


def _head_sums(x, bd):
    f32, bf16 = jnp.float32, jnp.bfloat16
    outs = []
    for t in range(x.shape[1] // LANE):
        p = x[:, t * LANE:(t + 1) * LANE]
        hi = p.astype(bf16)
        lo = (p - hi.astype(f32)).astype(bf16)
        outs.append(jnp.dot(hi, bd, preferred_element_type=f32) + jnp.dot(lo, bd, preferred_element_type=f32))
    return jnp.concatenate(outs, axis=1)


def _block_diag_ones():
    rr = lax.broadcasted_iota(jnp.int32, (LANE, LANE), 0) // RWKV_N
    cc = lax.broadcasted_iota(jnp.int32, (LANE, LANE), 1) // RWKV_N
    return jnp.where(rr == cc, 1.0, 0.0).astype(jnp.bfloat16)


def _rwkv_prep_kernel(x_ref, sh_ref, mu_ref, w0_ref, a0_ref, kkw_ref, ka_ref, rk_ref, w2_ref, a2_ref, g2_ref,
                      r_ref, wl_ref, k_ref, v_ref, kk_ref, a_ref, gate_ref, bonus_ref, tail_ref, carry_scr, *, tm, T):
    f32, bf16 = jnp.float32, jnp.bfloat16
    W = RWKV_WIDTH
    i = pl.program_id(1)
    x = x_ref[...]
    if T == 1:
        prev = sh_ref[...]
        tail_ref[...] = x
    else:
        first = jnp.where(i == 0, sh_ref[SUBLANE - 1:SUBLANE], carry_scr[SUBLANE - 1:SUBLANE])
        row = lax.broadcasted_iota(jnp.int32, x.shape, 0)
        prev = jnp.where(row == 0, first, pltpu.roll(x, 1, axis=0))
        last = x[tm - SUBLANE:tm]
        carry_scr[...] = last
        tail_ref[...] = last
    rm = x + (prev - x) * mu_ref[...]
    r, k, v = rm[:, :W], rm[:, W:2 * W], rm[:, 2 * W:3 * W]
    lo = rm[:, 3 * W:3 * W + LANE]
    glo = rm[:, 3 * W + LANE:]
    w_raw = w0_ref[...] + jnp.dot(jnp.tanh(lo).astype(bf16), w2_ref[...], preferred_element_type=f32)
    wl_ref[...] = -jnp.exp(-jax.nn.softplus(-w_raw) - 0.5)
    a = jax.nn.sigmoid(a0_ref[...] + jnp.dot(lo.astype(bf16), a2_ref[...], preferred_element_type=f32))
    gate_ref[...] = jnp.dot(jax.nn.sigmoid(glo).astype(bf16), g2_ref[...], preferred_element_type=f32)
    bd = _block_diag_ones()
    kk = k * kkw_ref[...]
    kk_ref[...] = kk * lax.rsqrt(jnp.maximum(_head_sums(kk * kk, bd), 1e-24))
    k2 = k * (1.0 + (a - 1.0) * ka_ref[...])
    bonus_ref[...] = _head_sums(r * k2 * rk_ref[...], bd) * v
    r_ref[...] = r
    k_ref[...] = k2
    v_ref[...] = v
    a_ref[...] = a


def rwkv_prep(hr, shift0, mu, w0, w2, a0, a2, g2, kkw, ka, rk):
    B, T, WP = hr.shape
    W = RWKV_WIDTH
    bf16 = jnp.bfloat16
    padc = lambda t: jnp.pad(t, ((0, 0), (0, WP - t.shape[1])))
    w2p = jnp.pad(w2, ((0, LANE - RWKV_DECAY_RANK), (0, 0))).astype(bf16)
    a2p = jnp.pad(a2, ((RWKV_DECAY_RANK, 0), (0, 0))).astype(bf16)
    gpad = WP - 3 * W - LANE
    g2p = jnp.pad(g2, ((0, gpad - RWKV_GATE_RANK), (0, 0))).astype(bf16)
    row = lambda t: t.reshape(1, -1)
    if T == 1:
        tm = 1
        sh = padc(shift0).reshape(B, 1, WP)
        sh_spec = pl.BlockSpec((None, 1, WP), lambda b, i: (b, 0, 0))
        tail_rows = 1
    else:
        tm = _pick(T, (256, 128, 64, 32, 16, 8))
        sh = jnp.broadcast_to(padc(shift0)[:, None, :], (B, SUBLANE, WP))
        sh_spec = pl.BlockSpec((None, SUBLANE, WP), lambda b, i: (b, 0, 0))
        tail_rows = SUBLANE
    vec = lambda n: pl.BlockSpec((1, n), lambda b, i: (0, 0))
    mat = lambda m: pl.BlockSpec(m.shape, lambda b, i: (0, 0))
    o_spec = pl.BlockSpec((None, tm, W), lambda b, i: (b, i, 0))
    o_shape = jax.ShapeDtypeStruct((B, T, W), jnp.float32)
    outs = pl.pallas_call(
        functools.partial(_rwkv_prep_kernel, tm=tm, T=T),
        grid=(B, T // tm),
        in_specs=[pl.BlockSpec((None, tm, WP), lambda b, i: (b, i, 0)), sh_spec, vec(WP),
                  vec(W), vec(W), vec(W), vec(W), vec(W), mat(w2p), mat(a2p), mat(g2p)],
        out_specs=[o_spec] * 8 + [pl.BlockSpec((None, tail_rows, WP), lambda b, i: (b, 0, 0))],
        out_shape=[o_shape] * 8 + [jax.ShapeDtypeStruct((B, tail_rows, WP), jnp.float32)],
        scratch_shapes=[pltpu.VMEM((SUBLANE, WP), jnp.float32)],
        compiler_params=pltpu.CompilerParams(dimension_semantics=("parallel", "arbitrary"),
                                             vmem_limit_bytes=VMEM_LIMIT),
        name="rwkv_prep",
    )(hr, sh, row(padc(mu.reshape(1, -1))), row(w0), row(a0), row(kkw), row(ka), row(rk), w2p, a2p, g2p)
    return outs[:8], outs[8][:, tail_rows - 1, :RWKV_COLS]


def _rwkv_post_kernel(y_ref, bonus_ref, gate_ref, lw_ref, lb_ref, o_ref):
    bd = _block_diag_ones()
    y = y_ref[...]
    d = y - _head_sums(y, bd) * (1.0 / RWKV_N)
    var = _head_sums(d * d, bd) * (1.0 / RWKV_N)
    yn = d * lax.rsqrt(var + RWKV_LN_EPS) * lw_ref[...] + lb_ref[...]
    o_ref[...] = ((yn + bonus_ref[...]) * gate_ref[...]).astype(o_ref.dtype)


def rwkv_post(y, bonus, gate, ln_w, ln_b):
    M, W = y.shape
    tm = _pick(M, (256, 128, 64, 32, 16, 8))
    spec = pl.BlockSpec((tm, W), lambda i: (i, 0))
    vec = pl.BlockSpec((1, W), lambda i: (0, 0))
    return pl.pallas_call(
        _rwkv_post_kernel,
        grid=(M // tm,),
        in_specs=[spec, spec, spec, vec, vec],
        out_specs=spec,
        out_shape=jax.ShapeDtypeStruct((M, W), jnp.bfloat16),
        compiler_params=pltpu.CompilerParams(dimension_semantics=("parallel",), vmem_limit_bytes=VMEM_LIMIT),
        name="rwkv_post",
    )(y, bonus, gate, ln_w.reshape(1, W), ln_b.reshape(1, W))


def _rwkv_kernel(r_ref, wl_ref, k_ref, v_ref, kk_ref, a_ref, s0_ref, y_ref, sout_ref, s_scr, *, NB, NP, Tc):
    c = pl.program_id(1)
    f32, bf16 = jnp.float32, jnp.bfloat16
    U = min(SUBLANE, Tc)

    @pl.when(c == 0)
    def _():
        s_scr[...] = s0_ref[...]

    sub = lax.broadcasted_iota(jnp.int32, (RWKV_N, LANE), 0)
    lane = lax.broadcasted_iota(jnp.int32, (RWKV_N, LANE), 1)
    eye2 = (lane % RWKV_N) == sub
    left = lane < RWKV_N
    rr = lax.broadcasted_iota(jnp.int32, (LANE, LANE), 0) // RWKV_N
    cc = lax.broadcasted_iota(jnp.int32, (LANE, LANE), 1) // RWKV_N
    bd = jnp.where(rr == cc, 1.0, 0.0).astype(bf16)
    bd2 = jnp.concatenate([bd, bd], axis=0)

    def ssb(p, two_piece=True):
        hi = p.astype(bf16)
        if not two_piece:
            return jnp.dot(hi, bd, preferred_element_type=f32)
        lo = (p - hi.astype(f32)).astype(bf16)
        return jnp.dot(jnp.concatenate([hi, lo], axis=1), bd2, preferred_element_type=f32)

    eye_all = jnp.concatenate([eye2] * NP, axis=0)

    def bcast(x8, s):
        return jnp.concatenate(
            [jnp.broadcast_to(x8[s:s + 1, p * LANE:(p + 1) * LANE], (RWKV_N, LANE)) for p in range(NP)], axis=0)

    def vcols(vt, v8, s):
        if vt is None:
            return ssb(jnp.where(eye_all, bcast(v8, s), 0.0))
        return jnp.concatenate(
            [jnp.where(left, jnp.broadcast_to(vt[p][:RWKV_N, s:s + 1], (RWKV_N, LANE)),
                       jnp.broadcast_to(vt[p][RWKV_N:, s:s + 1], (RWKV_N, LANE))) for p in range(NP)], axis=0)

    def body(g, carry):
        rows = pl.ds(pl.multiple_of(g * U, U), U)
        tiles = []
        for nb in range(NB):
            kk8 = kk_ref[nb, rows, :]
            v8 = v_ref[nb, rows, :]
            vt = [v8[:, p * LANE:(p + 1) * LANE].T for p in range(NP)] if U == SUBLANE else None
            tiles.append(dict(r=r_ref[nb, rows, :], w=jnp.exp(wl_ref[nb, rows, :]), k=k_ref[nb, rows, :], v=v8, vt=vt,
                              ka=kk8 * a_ref[nb, rows, :], nk=-kk8))
        S = [s_scr[nb] for nb in range(NB)]
        ys = [[] for _ in range(NB)]
        for s in range(U):
            sa = [ssb(S[nb] * bcast(t['nk'], s)) for nb, t in enumerate(tiles)]
            for nb, t in enumerate(tiles):
                S[nb] = S[nb] * bcast(t['w'], s) + sa[nb] * bcast(t['ka'], s) + vcols(t['vt'], t['v'], s) * bcast(t['k'], s)
            yb = [jnp.where(eye_all, ssb(S[nb] * bcast(t['r'], s), two_piece=False), 0.0) for nb, t in enumerate(tiles)]
            for nb in range(NB):
                ys[nb].append(jnp.concatenate(
                    [jnp.sum(yb[nb][p * RWKV_N:(p + 1) * RWKV_N], axis=0, keepdims=True) for p in range(NP)], axis=1))
        for nb in range(NB):
            s_scr[nb] = S[nb]
            y_ref[nb, rows, :] = ys[nb][0] if U == 1 else jnp.concatenate(ys[nb], axis=0)
        return carry

    lax.fori_loop(0, Tc // U, body, 0)

    @pl.when(c == pl.num_programs(1) - 1)
    def _():
        sout_ref[...] = s_scr[...]


def rwkv_scan_pallas(r, w_log, k, v, kk, a, s0):
    B, T, W = r.shape
    H = W // RWKV_N
    NP = H // 2
    NB = RWKV_NB if B % RWKV_NB == 0 else 1
    Tc = 128 if T % 128 == 0 else T
    s0p = s0.reshape(B, NP, 2, RWKV_N, RWKV_N).transpose(0, 1, 3, 2, 4).reshape(B, NP * RWKV_N, LANE)
    blk = pl.BlockSpec((NB, Tc, W), lambda b, c: (b, c, 0))
    sblk = pl.BlockSpec((NB, NP * RWKV_N, LANE), lambda b, c: (b, 0, 0))
    y, sp = pl.pallas_call(
        functools.partial(_rwkv_kernel, NB=NB, NP=NP, Tc=Tc),
        grid=(B // NB, T // Tc),
        in_specs=[blk] * 6 + [sblk],
        out_specs=[blk, sblk],
        out_shape=[jax.ShapeDtypeStruct((B, T, W), jnp.float32),
                   jax.ShapeDtypeStruct((B, NP * RWKV_N, LANE), jnp.float32)],
        scratch_shapes=[pltpu.VMEM((NB, NP * RWKV_N, LANE), jnp.float32)],
        compiler_params=pltpu.CompilerParams(dimension_semantics=("parallel", "arbitrary"),
                                             vmem_limit_bytes=VMEM_LIMIT),
        name="rwkv7_scan",
    )(r, w_log, k, v, kk, a, s0p)
    s_fin = sp.reshape(B, NP, RWKV_N, 2, RWKV_N).transpose(0, 1, 3, 2, 4).reshape(B, H, RWKV_N, RWKV_N)
    return y, s_fin


def rope_tables(pos):
    half = ROPE_DIM // 2
    inv = ROPE_THETA ** (-jnp.arange(half, dtype=jnp.float32) / half)
    ang = pos.astype(jnp.float32)[:, None] * inv[None, :]
    cos, sin = jnp.cos(ang), jnp.sin(ang)
    T = pos.shape[0]
    z = jnp.zeros((T, HEAD_DIM - ROPE_DIM), jnp.float32)
    zh = jnp.zeros((T, half), jnp.float32)
    c = jnp.concatenate([cos, cos, jnp.ones_like(z)], axis=1)
    s_up = jnp.concatenate([-sin, zh, z], axis=1)
    s_dn = jnp.concatenate([zh, sin, z], axis=1)
    return c, s_up, s_dn


def _nsa_prep_kernel(q_ref, c_ref, s_ref, w_ref, tc_ref, tu_ref, td_ref, qo_ref, co_ref, so_ref, wo_ref):
    c, su, sd = tc_ref[...], tu_ref[...], td_ref[...]
    half = ROPE_DIM // 2

    def rot(x):
        return x * c + pltpu.roll(x, HEAD_DIM - half, axis=1) * su + pltpu.roll(x, half, axis=1) * sd

    for h in range(NSA_HEADS):
        cols = slice(h * HEAD_DIM, (h + 1) * HEAD_DIM)
        qo_ref[:, cols] = (rot(q_ref[:, cols]) * HEAD_DIM ** -0.5).astype(qo_ref.dtype)
    for src, dst in ((c_ref, co_ref), (s_ref, so_ref), (w_ref, wo_ref)):
        for g in range(NSA_KV_HEADS):
            cols = slice(g * HEAD_DIM, (g + 1) * HEAD_DIM)
            dst[:, cols] = rot(src[:, cols])
        dst[:, NSA_KV_WIDTH:] = src[:, NSA_KV_WIDTH:]


def nsa_prep(hn, pos):
    B, T, _ = hn.shape
    tm = _pick(T, (256, 128, 64, 32, 16, 8))
    tabs = rope_tables(pos)
    kvw = 2 * NSA_KV_WIDTH
    q_spec = pl.BlockSpec((None, tm, NSA_WIDTH), lambda b, i: (b, i, 0))
    kv_spec = lambda n: pl.BlockSpec((None, tm, kvw), lambda b, i: (b, i, NSA_WIDTH // kvw + n))
    t_spec = pl.BlockSpec((tm, HEAD_DIM), lambda b, i: (i, 0))
    o_spec = pl.BlockSpec((None, tm, kvw), lambda b, i: (b, i, 0))
    kv_shape = jax.ShapeDtypeStruct((B, T, kvw), jnp.float32)
    return pl.pallas_call(
        _nsa_prep_kernel,
        grid=(B, T // tm),
        in_specs=[q_spec, kv_spec(0), kv_spec(1), kv_spec(2), t_spec, t_spec, t_spec],
        out_specs=[q_spec, o_spec, o_spec, o_spec],
        out_shape=[jax.ShapeDtypeStruct((B, T, NSA_WIDTH), jnp.bfloat16), kv_shape, kv_shape, kv_shape],
        compiler_params=pltpu.CompilerParams(dimension_semantics=("parallel", "parallel"),
                                             vmem_limit_bytes=VMEM_LIMIT),
        name="nsa_prep",
    )(hn, hn, hn, hn, *tabs)


def _compress_kernel(x_ref, w1_ref, w2_ref, pe_ref, ko_ref, vo_ref, *, ns):
    f32, bf16 = jnp.float32, jnp.bfloat16
    G, HD = NSA_KV_HEADS, HEAD_DIM
    row_w = 2 * NSA_KV_WIDTH
    for kv, o_ref in ((0, ko_ref), (1, vo_ref)):
        pos = jnp.zeros((SUBLANE, HD), f32)
        for p in range(CMP_STRIDE):
            w = w1_ref[kv, p]
            lo = jnp.broadcast_to(pe_ref[kv, p:p + 1, :], (SUBLANE, HD)).astype(bf16)
            hi = jnp.broadcast_to(pe_ref[kv, CMP_STRIDE + p:CMP_STRIDE + p + 1, :], (SUBLANE, HD)).astype(bf16)
            pos = pos + jnp.dot(lo, w, preferred_element_type=f32)[:, :HD] + jnp.dot(hi, w, preferred_element_type=f32)[:, HD:]
        pos = pos[0:1]
        for g in range(G):
            acc = jnp.zeros((ns, 2 * HD), f32)
            for p in range(CMP_STRIDE):
                c0 = p * row_w + kv * NSA_KV_WIDTH + g * HD
                acc = acc + jnp.dot(x_ref[:, c0:c0 + HD].astype(bf16), w1_ref[kv, p], preferred_element_type=f32)
            nxt = pltpu.roll(acc[:, HD:], ns - 1, axis=0)
            hid = jax.nn.gelu(acc[:, :HD] + nxt + pos)
            o_ref[g] = jnp.dot(hid.astype(bf16), w2_ref[kv], preferred_element_type=f32).astype(o_ref.dtype)


def compress_weights(w1k, w2k, pek, w1v, w2v, pev):
    bf16 = jnp.bfloat16
    cat = lambda w1: jnp.concatenate([w1[:CMP_STRIDE], w1[CMP_STRIDE:]], axis=-1)
    return (jnp.stack([cat(w1k), cat(w1v)]).astype(bf16), jnp.stack([w2k, w2v]).astype(bf16), jnp.stack([pek, pev]))


def compress_pallas(kv_rows, w1, w2, pe):
    B, T, W = kv_rows.shape
    ns = T // CMP_STRIDE
    bf16 = jnp.bfloat16
    x = kv_rows.reshape(B, ns, CMP_STRIDE * W)
    out = jax.ShapeDtypeStruct((B, NSA_KV_HEADS, ns, HEAD_DIM), bf16)
    o_spec = pl.BlockSpec((None, NSA_KV_HEADS, ns, HEAD_DIM), lambda b: (b, 0, 0, 0))
    return pl.pallas_call(
        functools.partial(_compress_kernel, ns=ns),
        grid=(B,),
        in_specs=[pl.BlockSpec((None, ns, CMP_STRIDE * W), lambda b: (b, 0, 0)),
                  pl.BlockSpec(w1.shape, lambda b: (0, 0, 0, 0)),
                  pl.BlockSpec(w2.shape, lambda b: (0, 0, 0)),
                  pl.BlockSpec(pe.shape, lambda b: (0, 0, 0))],
        out_specs=[o_spec, o_spec],
        out_shape=[out, out],
        compiler_params=pltpu.CompilerParams(dimension_semantics=("parallel",), vmem_limit_bytes=VMEM_LIMIT),
        name="nsa_compress",
    )(x, w1, w2, pe)


def _dot_nt(a, b):
    return lax.dot_general(a, b, (((1,), (1,)), ((), ())), preferred_element_type=jnp.float32)


def _nsa_kernel(q_ref, kc_ref, vc_ref, ks_ref, vs_ref, kw_ref, vw_ref, g_ref, covt_ref, e_ref, o_ref,
                bias_scr, p4_scr, ocmp_scr, m_scr, acc_scr, *, TQ, TK, T, NS, NCP, n_top):
    f32, bf16 = jnp.float32, jnp.bfloat16
    R = NSA_GROUP
    i = pl.program_id(2)
    nchunk = T // TK
    qpos_col = i * TQ + lax.broadcasted_iota(jnp.int32, (TQ, 1), 0)

    kc = kc_ref[...]
    vc = vc_ref[...]
    cend = lax.broadcasted_iota(jnp.int32, (1, NCP), 1) * CMP_STRIDE + (CMP_BLOCK - 1)
    valid = cend <= qpos_col
    for r in range(R):
        s = _dot_nt(q_ref[:, r * HEAD_DIM:(r + 1) * HEAD_DIM], kc)
        s = jnp.where(valid, s, NEG_INF)
        m = jnp.max(s, axis=-1, keepdims=True)
        p = jnp.where(valid, jnp.exp(s - m), 0.0)
        l = jnp.sum(p, axis=-1, keepdims=True)
        p = (p / jnp.where(l > 0.0, l, 1.0)).astype(bf16)
        p4_scr[:, r * NCP:(r + 1) * NCP] = p
        ocmp_scr[r] = jnp.dot(p, vc, preferred_element_type=f32)

    imp_t = _dot_nt(covt_ref[...], p4_scr[...])
    j = lax.broadcasted_iota(jnp.int32, (NS, TQ), 0)
    qblk = (i * TQ + lax.broadcasted_iota(jnp.int32, (NS, TQ), 1)) // SEL_BLOCK
    forced = (j < N_INIT_BLOCKS) | ((j <= qblk) & (j > qblk - N_LOCAL_BLOCKS))
    score = jnp.where(forced, FORCE_SCORE, jnp.where(j <= qblk, imp_t, NEG_INF))
    rank = jnp.zeros((NS, TQ), f32)
    for a in range(NS):
        row = score[a:a + 1, :]
        beats = (row > score) | ((row == score) & (a < j))
        rank = rank + jnp.where(beats, 1.0, 0.0)
    sel_t = jnp.where(rank < n_top, 1.0, 0.0)
    if NS < LANE:
        sel_t = jnp.concatenate([sel_t, jnp.zeros((LANE - NS, TQ), f32)], axis=0)
    sel = sel_t.T.astype(bf16)
    for c in range(nchunk):
        selexp = jnp.dot(sel, e_ref[:, c * TK:(c + 1) * TK], preferred_element_type=f32)
        kpos = c * TK + lax.broadcasted_iota(jnp.int32, (TQ, TK), 1)
        bias_scr[c] = jnp.where((selexp > 0.5) & (kpos <= qpos_col), 0.0, NEG_INF)

    hi = (i * TQ + TQ - 1) // TK + 1

    def attend(k_ref, v_ref, lo, masker):
        m_scr[...] = jnp.full(m_scr.shape, NEG_INF, f32)
        acc_scr[...] = jnp.zeros(acc_scr.shape, f32)

        def chunk(c, carry):
            rows = pl.ds(pl.multiple_of(c * TK, TK), TK)
            k = k_ref[rows, :].astype(bf16)
            v = jnp.concatenate([v_ref[rows, :].astype(bf16), jnp.ones((TK, HEAD_DIM), bf16)], axis=1)
            mk = masker(c)
            heads = range(R)
            sk = [mk(_dot_nt(q_ref[:, r * HEAD_DIM:(r + 1) * HEAD_DIM], k)) for r in heads]
            m_prev = [m_scr[r] for r in heads]
            m_new = [jnp.maximum(m_prev[r], jnp.max(sk[r][0], axis=-1, keepdims=True)) for r in heads]
            alpha = [jnp.exp(m_prev[r] - m_new[r]) for r in heads]
            ps = [jnp.exp(sk[r][0] - m_new[r]) for r in heads]
            ps = [p if sk[r][1] is None else jnp.where(sk[r][1], p, 0.0) for r, p in enumerate(ps)]
            pv = [jnp.dot(ps[r].astype(bf16), v, preferred_element_type=f32) for r in heads]
            for r in heads:
                acc_scr[r] = alpha[r] * acc_scr[r] + pv[r]
                m_scr[r] = m_new[r]
            return carry

        lax.fori_loop(lo, hi, chunk, 0)

    def normalised(r):
        return acc_scr[r, :, :HEAD_DIM] / acc_scr[r, :, HEAD_DIM:]

    def sel_masker(c):
        b = bias_scr[c]
        return lambda s: (s + b, None)

    attend(ks_ref, vs_ref, 0, sel_masker)
    g = pltpu.roll(jax.nn.sigmoid(g_ref[...]), (LANE - 3 * R * pl.program_id(1)) % LANE, axis=1)
    for r in range(R):
        ocmp_scr[r] = (g[:, 3 * r:3 * r + 1] * ocmp_scr[r]
                       + g[:, 3 * r + 1:3 * r + 2] * normalised(r))

    def win_masker(c):
        rel = (i * TQ - c * TK + lax.broadcasted_iota(jnp.int32, (TQ, TK), 0)
               - lax.broadcasted_iota(jnp.int32, (TQ, TK), 1))
        ok = (rel >= 0) & (rel < WINDOW)
        return lambda s: (jnp.where(ok, s, NEG_INF), ok)

    attend(kw_ref, vw_ref, jnp.maximum(i * TQ - (WINDOW - 1), 0) // TK, win_masker)
    for r in range(R):
        o = ocmp_scr[r] + g[:, 3 * r + 2:3 * r + 3] * normalised(r)
        o_ref[:, r * HEAD_DIM:(r + 1) * HEAD_DIM] = o.astype(o_ref.dtype)


def nsa_attention_pallas(qr, kcmp, vcmp, slc, win, hn, gate_col):
    B, T, _ = qr.shape
    G, R = NSA_KV_HEADS, NSA_GROUP
    TQ = min(NSA_TQ, T)
    TK = min(NSA_TK, T)
    NS = T // SEL_BLOCK
    NC = T // CMP_STRIDE - 1
    NCP = kcmp.shape[2]
    n_top = min(SEL_TOP, NS)
    ci = np.arange(NCP)[:, None] * CMP_STRIDE
    sj = np.arange(NS)[None, :] * SEL_BLOCK
    cover = np.clip(np.minimum(ci + CMP_BLOCK, sj + SEL_BLOCK) - np.maximum(ci, sj), 0, None) / CMP_BLOCK
    cover[NC:] = 0.0
    covt = jnp.asarray(np.tile(cover.T, (1, R)), jnp.bfloat16)
    e = jnp.asarray((np.arange(T)[None, :] // SEL_BLOCK) == np.arange(LANE)[:, None], jnp.bfloat16)
    k_spec = pl.BlockSpec((None, T, HEAD_DIM), lambda b, g, i: (b, 0, g))
    v_spec = pl.BlockSpec((None, T, HEAD_DIM), lambda b, g, i: (b, 0, G + g))
    cmp_spec = pl.BlockSpec((None, None, NCP, HEAD_DIM), lambda b, g, i: (b, g, 0, 0))
    return pl.pallas_call(
        functools.partial(_nsa_kernel, TQ=TQ, TK=TK, T=T, NS=NS, NCP=NCP, n_top=n_top),
        grid=(B, G, T // TQ),
        in_specs=[pl.BlockSpec((None, TQ, R * HEAD_DIM), lambda b, g, i: (b, i, g)),
                  cmp_spec, cmp_spec, k_spec, v_spec, k_spec, v_spec,
                  pl.BlockSpec((None, TQ, LANE), lambda b, g, i: (b, i, gate_col // LANE)),
                  pl.BlockSpec((NS, R * NCP), lambda b, g, i: (0, 0)),
                  pl.BlockSpec((LANE, T), lambda b, g, i: (0, 0))],
        out_specs=pl.BlockSpec((None, TQ, R * HEAD_DIM), lambda b, g, i: (b, i, g)),
        out_shape=jax.ShapeDtypeStruct((B, T, G * R * HEAD_DIM), jnp.bfloat16),
        scratch_shapes=[pltpu.VMEM((T // TK, TQ, TK), jnp.float32),
                        pltpu.VMEM((TQ, R * NCP), jnp.bfloat16),
                        pltpu.VMEM((R, TQ, HEAD_DIM), jnp.float32),
                        pltpu.VMEM((R, TQ, 1), jnp.float32),
                        pltpu.VMEM((R, TQ, 2 * HEAD_DIM), jnp.float32)],
        compiler_params=pltpu.CompilerParams(dimension_semantics=("parallel", "parallel", "arbitrary"),
                                             vmem_limit_bytes=VMEM_LIMIT),
        name="nsa_attention",
    )(qr, kcmp, vcmp, slc, slc, win, win, hn, covt, e)


def _dec_compress_kernel(pt_ref, *refs):
    f32, bf16 = jnp.float32, jnp.bfloat16
    pages, w1_ref, o_ref, reg_scr = refs[:DEC_PAGES], refs[DEC_PAGES], refs[DEC_PAGES + 1], refs[DEC_PAGES + 2]
    HD = HEAD_DIM
    nc = 2 * NSA_KV_HEADS
    seg = pages[0].shape[0] // (nc * CMP_STRIDE)
    acc = None
    for p in range(CMP_STRIDE):
        lhs = jnp.concatenate([pg[(CMP_STRIDE * n + p) * nc:(CMP_STRIDE * n + p + 1) * nc, :]
                               for pg in pages for n in range(seg)], axis=0).astype(bf16)
        w = jnp.concatenate([w1_ref[0, p], w1_ref[1, p]], axis=1)
        part = jnp.dot(lhs, w, preferred_element_type=f32)
        acc = part if acc is None else acc + part
    is_k = (lax.broadcasted_iota(jnp.int32, (acc.shape[0], 2 * HD), 0) % nc) < NSA_KV_HEADS
    fs = jnp.where(is_k, acc[:, :2 * HD], acc[:, 2 * HD:])
    reg_scr[0] = fs[:, :HD]
    reg_scr[1] = fs[:, HD:]
    nseg = DEC_PAGES * seg
    for c in range(nc):
        o_ref[:, c * 2 * HD:c * 2 * HD + HD] = reg_scr[0, pl.ds(c, nseg, stride=nc), :]
        o_ref[:, c * 2 * HD + HD:(c + 1) * 2 * HD] = reg_scr[1, pl.ds(c, nseg, stride=nc), :]


def dec_compress(cache, layer, page_table, w1):
    L, n_phys, page = cache.shape[:3]
    nc = 2 * NSA_KV_HEADS
    B, n_pages = page_table.shape
    seg = page // CMP_STRIDE
    steps = n_pages // DEC_PAGES
    c4 = cache.reshape(L, n_phys, page * nc, HEAD_DIM)
    page_spec = lambda k: pl.BlockSpec((None, None, page * nc, HEAD_DIM),
                                       lambda b, s, pt: (layer, pt[b, s * DEC_PAGES + k], 0, 0))
    ow = nc * 2 * HEAD_DIM
    return pl.pallas_call(
        _dec_compress_kernel,
        grid_spec=pltpu.PrefetchScalarGridSpec(
            num_scalar_prefetch=1,
            grid=(B, steps),
            in_specs=[page_spec(k) for k in range(DEC_PAGES)] + [pl.BlockSpec(w1.shape, lambda b, s, pt: (0, 0, 0, 0))],
            out_specs=pl.BlockSpec((None, DEC_PAGES * seg, ow), lambda b, s, pt: (b, s, 0)),
            scratch_shapes=[pltpu.VMEM((2, DEC_PAGES * seg * nc, HEAD_DIM), jnp.float32)]),
        out_shape=jax.ShapeDtypeStruct((B, n_pages * seg, ow), jnp.float32),
        compiler_params=pltpu.CompilerParams(dimension_semantics=("parallel", "arbitrary"),
                                             vmem_limit_bytes=VMEM_LIMIT),
        name="nsa_dec_compress",
    )(page_table, *([c4] * DEC_PAGES), w1)


def _dec_select_kernel(q_ref, fk_ref, fv_ref, w1_ref, w2_ref, pe_ref, covt_ref, ocmp_ref, idx_ref, *, NSEG, NS, NSP, n_top):
    f32, bf16 = jnp.float32, jnp.bfloat16
    HD, R = HEAD_DIM, NSA_GROUP
    NC = NSEG - 1

    def compressed(kv, f_ref):
        pos = jnp.zeros((SUBLANE, HD), f32)
        for p in range(CMP_STRIDE):
            w = w1_ref[kv, p]
            lo = jnp.broadcast_to(pe_ref[kv, p:p + 1, :], (SUBLANE, HD)).astype(bf16)
            hi = jnp.broadcast_to(pe_ref[kv, CMP_STRIDE + p:CMP_STRIDE + p + 1, :], (SUBLANE, HD)).astype(bf16)
            pos = pos + jnp.dot(lo, w, preferred_element_type=f32)[:, :HD] + jnp.dot(hi, w, preferred_element_type=f32)[:, HD:]
        nxt = pltpu.roll(f_ref[:, HD:], NSEG - 1, axis=0)
        hid = jax.nn.gelu(f_ref[:, :HD] + nxt + pos[0:1])
        return jnp.dot(hid.astype(bf16), w2_ref[kv], preferred_element_type=f32).astype(bf16)

    kc = compressed(0, fk_ref)
    vc = compressed(1, fv_ref)
    q = q_ref[...]
    q4 = jnp.concatenate([q[:, r * HD:(r + 1) * HD] for r in range(R)] + [jnp.zeros((SUBLANE - R, HD), bf16)], axis=0)
    s = _dot_nt(q4, kc)
    valid = lax.broadcasted_iota(jnp.int32, s.shape, 1) < NC
    s = jnp.where(valid, s, NEG_INF)
    p = jnp.where(valid, jnp.exp(s - jnp.max(s, axis=-1, keepdims=True)), 0.0)
    p = (p / jnp.sum(p, axis=-1, keepdims=True)).astype(bf16)
    ocmp_ref[...] = jnp.dot(p, vc, preferred_element_type=f32)
    head = lax.broadcasted_iota(jnp.int32, p.shape, 0) < R
    imp = _dot_nt(covt_ref[...], jnp.where(head, p, jnp.zeros_like(p)))
    imp = jnp.sum(imp, axis=1, keepdims=True)
    j_col = lax.broadcasted_iota(jnp.int32, (NSP, 1), 0)
    qblk = NS - 1
    forced = (j_col < N_INIT_BLOCKS) | ((j_col <= qblk) & (j_col > qblk - N_LOCAL_BLOCKS))
    score_col = jnp.where(forced, FORCE_SCORE, jnp.where(j_col <= qblk, imp, -3e38))
    score_cb = jnp.broadcast_to(score_col, (NSP, LANE))
    score_row = score_cb.T[0:1, :]
    ii = lax.broadcasted_iota(jnp.int32, (NSP, NSP), 0)
    jj = lax.broadcasted_iota(jnp.int32, (NSP, NSP), 1)
    beats = (score_col > score_row) | ((score_col == score_row) & (ii < jj))
    rank = jnp.sum(jnp.where(beats, 1.0, 0.0), axis=0, keepdims=True)
    t_col = lax.broadcasted_iota(jnp.int32, (n_top, NSP), 0).astype(f32)
    j_row = lax.broadcasted_iota(jnp.int32, (n_top, NSP), 1).astype(f32)
    ids = jnp.sum(jnp.where(rank == t_col, j_row, 0.0), axis=1, keepdims=True)
    idx_ref[...] = jnp.broadcast_to(ids, (n_top, LANE)).astype(jnp.int32)


def dec_select(qr, fs, w1, w2, pe, past_len):
    B = qr.shape[0]
    G, R, HD = NSA_KV_HEADS, NSA_GROUP, HEAD_DIM
    NSEG = fs.shape[1]
    NC = NSEG - 1
    NS = -(-(past_len + 1) // SEL_BLOCK)
    NSP = _round_up(NS, LANE)
    n_top = min(SEL_TOP, NS)
    ci = np.arange(NSEG)[:, None] * CMP_STRIDE
    sj = np.arange(NSP)[None, :] * SEL_BLOCK
    cover = np.clip(np.minimum(ci + CMP_BLOCK, sj + SEL_BLOCK) - np.maximum(ci, sj), 0, None) / CMP_BLOCK
    cover[NC:] = 0.0
    cover[:, NS:] = 0.0
    covt = jnp.asarray(cover.T, jnp.bfloat16)
    f_spec = lambda kv: pl.BlockSpec((None, NSEG, 2 * HD), lambda b, g: (b, 0, kv * G + g))
    full = lambda a: pl.BlockSpec(a.shape, lambda b, g: (0,) * a.ndim)
    ocmp, idx = pl.pallas_call(
        functools.partial(_dec_select_kernel, NSEG=NSEG, NS=NS, NSP=NSP, n_top=n_top),
        grid=(B, G),
        in_specs=[pl.BlockSpec((None, 1, R * HD), lambda b, g: (b, 0, g)), f_spec(0), f_spec(1),
                  full(w1), full(w2), full(pe), full(covt)],
        out_specs=[pl.BlockSpec((None, None, SUBLANE, HD), lambda b, g: (b, g, 0, 0)),
                   pl.BlockSpec((None, None, n_top, LANE), lambda b, g: (b, g, 0, 0))],
        out_shape=[jax.ShapeDtypeStruct((B, G, SUBLANE, HD), jnp.float32),
                   jax.ShapeDtypeStruct((B, G, n_top, LANE), jnp.int32)],
        compiler_params=pltpu.CompilerParams(dimension_semantics=("parallel", "parallel"),
                                             vmem_limit_bytes=VMEM_LIMIT),
        name="nsa_dec_select",
    )(qr, fs, fs, w1, w2, pe, covt)
    return ocmp, idx[:, :, :, 0]


def _dec_attend_kernel(pt_ref, idx_ref, q_ref, *refs, NS, n_top):
    f32, bf16 = jnp.float32, jnp.bfloat16
    HD, R, G = HEAD_DIM, NSA_GROUP, NSA_KV_HEADS
    blocks = refs[:n_top]
    nks_ref, nvs_ref, wb_ref, nkw_ref, nvw_ref, ocmp_ref, g_ref, o_ref = refs[n_top:]
    b, g = pl.program_id(0), pl.program_id(1)
    q = q_ref[...]
    q4 = jnp.concatenate([q[:, r * HD:(r + 1) * HD] for r in range(R)] + [jnp.zeros((SUBLANE - R, HD), bf16)], axis=0)

    def head_rows(ref, kv):
        return ref[pl.ds(kv * G + g, ref.shape[0] // (2 * G), stride=2 * G), :].astype(bf16)

    def attend(keys, vals, bias, k_new_ref, v_new_ref):
        k_new = jnp.broadcast_to(k_new_ref[...], (SUBLANE, HD)).astype(bf16).astype(f32)
        v_new = jnp.broadcast_to(v_new_ref[...], (SUBLANE, HD)).astype(bf16).astype(f32)
        s = _dot_nt(q4, keys) + bias
        s_new = jnp.sum(q4.astype(f32) * k_new, axis=-1, keepdims=True)
        m = jnp.maximum(jnp.max(s, axis=-1, keepdims=True), s_new)
        p = jnp.exp(s - m)
        p_new = jnp.exp(s_new - m)
        l = jnp.sum(p, axis=-1, keepdims=True) + p_new
        return (jnp.dot(p.astype(bf16), vals, preferred_element_type=f32) + p_new.astype(bf16).astype(f32) * v_new) / l

    keys = jnp.concatenate([head_rows(r, 0) for r in blocks], axis=0)
    vals = jnp.concatenate([head_rows(r, 1) for r in blocks], axis=0)
    bias = jnp.concatenate(
        [jnp.broadcast_to(jnp.where(idx_ref[b, g, t] != NS - 1, 0.0, NEG_INF), (SUBLANE, SEL_BLOCK)) for t in range(n_top)],
        axis=1)
    o_slc = attend(keys, vals, bias, nks_ref, nvs_ref)
    nwin = wb_ref.shape[0] // (2 * G)
    ok = lax.broadcasted_iota(jnp.int32, (SUBLANE, nwin), 1) > nwin - WINDOW
    o_win = attend(head_rows(wb_ref, 0), head_rows(wb_ref, 1), jnp.where(ok, 0.0, NEG_INF), nkw_ref, nvw_ref)
    gate = pltpu.roll(jax.nn.sigmoid(jnp.broadcast_to(g_ref[...], (SUBLANE, LANE))), (LANE - 3 * R * g) % LANE, axis=1)
    rows = []
    for r in range(R):
        rows.append(gate[r:r + 1, 3 * r:3 * r + 1] * ocmp_ref[r:r + 1, :] + gate[r:r + 1, 3 * r + 1:3 * r + 2] * o_slc[r:r + 1, :]
                    + gate[r:r + 1, 3 * r + 2:3 * r + 3] * o_win[r:r + 1, :])
    o_ref[...] = jnp.concatenate(rows, axis=1).astype(o_ref.dtype)


def dec_attend(qr, slc_cache, win_cache, layer, page_table, idx, new_slc, new_win, ocmp, hn, gate_col):
    B = qr.shape[0]
    G, R, HD = NSA_KV_HEADS, NSA_GROUP, HEAD_DIM
    L, n_phys, page = slc_cache.shape[:3]
    nc = 2 * G
    n_top = idx.shape[2]
    NS = -(-(page_table.shape[1] * page + 1) // SEL_BLOCK)
    per = page // SEL_BLOCK
    blocks = slc_cache.reshape(L, n_phys * per, SEL_BLOCK * nc, HD)
    nwin = win_cache.shape[2]
    wins = win_cache.reshape(L, B, nwin * nc, HD)

    def blk_spec(t):
        def index(b, g, pt, ix):
            j = jnp.minimum(ix[b, g, t], NS - 2)
            return (layer, pt[b, j // per] * per + j % per, 0, 0)
        return pl.BlockSpec((None, None, SEL_BLOCK * nc, HD), index)

    row = lambda col: pl.BlockSpec((None, 1, HD), lambda b, g, pt, ix: (b, 0, col(g)))
    kcol, vcol = (lambda g: g), (lambda g: G + g)
    return pl.pallas_call(
        functools.partial(_dec_attend_kernel, NS=NS, n_top=n_top),
        grid_spec=pltpu.PrefetchScalarGridSpec(
            num_scalar_prefetch=2,
            grid=(B, G),
            in_specs=[pl.BlockSpec((None, 1, R * HD), lambda b, g, pt, ix: (b, 0, g))]
                     + [blk_spec(t) for t in range(n_top)]
                     + [row(kcol), row(vcol),
                        pl.BlockSpec((None, None, nwin * nc, HD), lambda b, g, pt, ix: (layer, b, 0, 0)),
                        row(kcol), row(vcol),
                        pl.BlockSpec((None, None, SUBLANE, HD), lambda b, g, pt, ix: (b, g, 0, 0)),
                        pl.BlockSpec((None, 1, LANE), lambda b, g, pt, ix: (b, 0, gate_col // LANE))],
            out_specs=pl.BlockSpec((None, 1, R * HD), lambda b, g, pt, ix: (b, 0, g))),
        out_shape=jax.ShapeDtypeStruct((B, 1, G * R * HD), jnp.bfloat16),
        compiler_params=pltpu.CompilerParams(dimension_semantics=("parallel", "parallel"),
                                             vmem_limit_bytes=VMEM_LIMIT),
        name="nsa_dec_attend",
    )(page_table, idx, qr, *([blocks] * n_top), new_slc, new_slc, wins, new_win, new_win, ocmp, hn)


def trunk_layer(x, pos0, paged, gla_s0, rwkv_s0, shift0, conv0, lw):
    B, T, _ = x.shape
    dt = x.dtype
    f32 = jnp.float32
    pos = pos0 + jnp.arange(T, dtype=jnp.int32)
    M = B * T
    x2 = x.reshape(M, D_MODEL)
    xn = rmsnorm_pallas(x2, lw['norm1'])
    hg = mm(xn, lw['w_gla']).reshape(B, T, -1)
    hn = mm(xn, lw['w_nsa']).reshape(B, T, -1)
    hr = mm(xn, lw['w_rwkv']).reshape(B, T, -1)
    mg = mm(xn, lw['w_mg'])

    o_gla, gla_s = gla_pallas(hg, lw['gla_wa2'], lw['gla_ba'], lw['gla_norm'], gla_s0.astype(f32), GLA_OFF)
    o_gla = o_gla.reshape(M, GLA_WIDTH)

    kv5 = lambda t: t.reshape(B, T, 2, NSA_KV_HEADS, HEAD_DIM)
    cw1, cw2, cpe = compress_weights(lw['cmp_w1k'], lw['cmp_w2k'], lw['cmp_pek'],
                                     lw['cmp_w1v'], lw['cmp_w2v'], lw['cmp_pev'])
    if paged is None:
        assert T % NSA_TQ == 0
        qr, cmp2, slc2, win2 = nsa_prep(hn, pos)
        kcmp, vcmp = compress_pallas(cmp2, cw1, cw2, cpe)
        o_nsa = nsa_attention_pallas(qr, kcmp, vcmp, slc2, win2, hn, NSA_GATE_OFF)
        new_cmp, new_slc, win_new = kv5(cmp2), kv5(slc2), kv5(win2)[:, T - min(WINDOW, T):]
    else:
        assert T == 1
        cache_cmp, cache_slc, cache_win, layer, page_table = paged
        qr, cmp2, slc2, win2 = [t.reshape(B, 1, -1) for t in
                                nsa_prep(hn.reshape(1, B, -1), jnp.full((B,), pos0, jnp.int32))]
        fs = dec_compress(cache_cmp, layer, page_table, cw1)
        ocmp, sel_ids = dec_select(qr, fs, cw1, cw2, cpe, pos0)
        o_nsa = dec_attend(qr, cache_slc, cache_win, layer, page_table, sel_ids, slc2, win2, ocmp, hn, NSA_GATE_OFF)
        new_cmp, new_slc = kv5(cmp2), kv5(slc2)
        win_all = jnp.concatenate([cache_win[layer].astype(dt), kv5(win2)], axis=1)
        win_new = win_all[:, win_all.shape[1] - min(WINDOW, win_all.shape[1]):]
    o_nsa = o_nsa.reshape(M, NSA_WIDTH)

    (r_, w_log, k2, v_, kk, a, gate, bonus), shift_new = rwkv_prep(
        hr, shift0.astype(f32), lw['rwkv_mu'], lw['rwkv_w0'], lw['rwkv_w2'], lw['rwkv_a0'], lw['rwkv_a2'],
        lw['rwkv_g2'], lw['rwkv_kk'], lw['rwkv_ka'], lw['rwkv_rk'])
    y, rwkv_s = rwkv_scan_pallas(r_, w_log, k2, v_, kk, a, rwkv_s0.astype(f32))
    flat = lambda t: t.reshape(M, RWKV_WIDTH)
    o_rwkv = rwkv_post(flat(y), flat(bonus), flat(gate), lw['rwkv_ln_w'], lw['rwkv_ln_b'])

    merged = merge_mm(o_gla, o_nsa, o_rwkv, lw['w_o_gla'], lw['w_o_nsa'], lw['w_o_rwkv'], mg)
    x2 = mm(merged, lw['w_out'], res=x2)

    xn2 = rmsnorm_pallas(x2, lw['norm2'])
    act, conv_new = ffn_gate_up(xn2, lw['ffn_gate'], lw['ffn_up'], lw['ffn_conv'], lw['ffn_conv_b'],
                                conv0.astype(f32), B, T)
    x2 = mm(act, lw['ffn_down'], res=x2)
    return x2.reshape(B, T, D_MODEL), (new_cmp, new_slc, win_new, gla_s, rwkv_s, shift_new, conv_new)


def _w_in_group(w, lo, hi):
    seg = w[:, lo:hi].astype(jnp.bfloat16)
    return jnp.pad(seg, ((0, 0), (0, _round_up(hi - lo, W_IN_TILE) - (hi - lo))))


def kernel(x_prompt, x_sample, cache_cmp_kv, cache_slc_kv, cache_win_kv, state_gla, state_rwkv, state_rwkv_shift, state_ffn_conv, page_table, norm1, w_in, gla_wa2, gla_ba, gla_norm, w_o_gla, cmp_w1k, cmp_w2k, cmp_pek, cmp_w1v, cmp_w2v, cmp_pev, w_o_nsa, rwkv_mu, rwkv_w0, rwkv_w2, rwkv_a0, rwkv_a2, rwkv_g2, rwkv_kk, rwkv_ka, rwkv_rk, rwkv_ln_w, rwkv_ln_b, w_o_rwkv, w_out, norm2, ffn_gate, ffn_conv, ffn_conv_b, ffn_up, ffn_down, norm_f):
    past_len = page_table.shape[1] * PAGE_SIZE
    bp = x_prompt.shape[0]
    dt = x_prompt.dtype
    bf = jnp.bfloat16
    xp, xs = x_prompt, x_sample
    st_p, st_s = [], []
    for l in range(DEPTH):
        lw = {'norm1': norm1[l], 'w_gla': _w_in_group(w_in[l], 0, _C_NSA),
              'w_nsa': _w_in_group(w_in[l], _C_NSA, _C_RWKV), 'w_rwkv': _w_in_group(w_in[l], _C_RWKV, _C_MG),
              'w_mg': _w_in_group(w_in[l], _C_MG, _C_MG + IN_SIZES[14]), 'gla_wa2': gla_wa2[l], 'gla_ba': gla_ba[l],
              'gla_norm': gla_norm[l], 'w_o_gla': w_o_gla[l].astype(bf), 'cmp_w1k': cmp_w1k[l], 'cmp_w2k': cmp_w2k[l],
              'cmp_pek': cmp_pek[l], 'cmp_w1v': cmp_w1v[l], 'cmp_w2v': cmp_w2v[l], 'cmp_pev': cmp_pev[l],
              'w_o_nsa': w_o_nsa[l].astype(bf), 'rwkv_mu': rwkv_mu[l], 'rwkv_w0': rwkv_w0[l], 'rwkv_w2': rwkv_w2[l],
              'rwkv_a0': rwkv_a0[l], 'rwkv_a2': rwkv_a2[l], 'rwkv_g2': rwkv_g2[l], 'rwkv_kk': rwkv_kk[l],
              'rwkv_ka': rwkv_ka[l], 'rwkv_rk': rwkv_rk[l], 'rwkv_ln_w': rwkv_ln_w[l], 'rwkv_ln_b': rwkv_ln_b[l],
              'w_o_rwkv': w_o_rwkv[l].astype(bf), 'w_out': w_out[l].astype(bf), 'norm2': norm2[l],
              'ffn_gate': ffn_gate[l], 'ffn_conv': ffn_conv[l], 'ffn_conv_b': ffn_conv_b[l],
              'ffn_up': ffn_up[l], 'ffn_down': ffn_down[l].astype(bf)}
        xp, sp = trunk_layer(xp, 0, None,
                             jnp.zeros((bp, GLA_HEADS, GLA_DK, GLA_DV), jnp.float32),
                             jnp.zeros((bp, RWKV_HEADS, RWKV_N, RWKV_N), jnp.float32),
                             jnp.zeros((bp, RWKV_COLS), dt),
                             jnp.zeros((bp, CONV_W - 1, D_FF), dt), lw)
        paged = (cache_cmp_kv, cache_slc_kv, cache_win_kv, l, page_table)
        xs, ss = trunk_layer(xs, past_len, paged, state_gla[l], state_rwkv[l],
                             state_rwkv_shift[l], state_ffn_conv[l], lw)
        st_p.append(sp)
        st_s.append(ss)
    y_prompt = rmsnorm_pallas(xp.reshape(-1, D_MODEL), norm_f, out_dtype=dt).reshape(xp.shape)
    y_sample = rmsnorm_pallas(xs.reshape(-1, D_MODEL), norm_f, out_dtype=dt).reshape(xs.shape)
    outs = [y_prompt, y_sample]
    for i in range(7):
        outs.append(jnp.stack([s[i] for s in st_p]))
        outs.append(jnp.stack([s[i] for s in st_s]))
    return tuple(outs)
```

```python
import functools

import jax
import jax.numpy as jnp
import numpy as np
from jax import lax
from jax.experimental import pallas as pl
from jax.experimental.pallas import tpu as pltpu

D_MODEL = 4096
DEPTH = 2
PAGE_SIZE = 128
HEAD_DIM = 128
ROPE_DIM = HEAD_DIM // 4
ROPE_THETA = 500000.0
NORM_EPS = 1e-5
NEG_INF = -1e30

GLA_WIDTH = D_MODEL // 4
GLA_HEADS = 4
GLA_DV = GLA_WIDTH // GLA_HEADS
GLA_DK = GLA_DV // 2
GLA_GATE_RANK = 16
GLA_TAU = 16.0
GLA_CHUNK = 64

NSA_HEADS = D_MODEL // 256
NSA_KV_HEADS = 4
NSA_GROUP = NSA_HEADS // NSA_KV_HEADS
NSA_WIDTH = NSA_HEADS * HEAD_DIM
NSA_KV_WIDTH = NSA_KV_HEADS * HEAD_DIM
CMP_STRIDE = 16
CMP_BLOCK = 2 * CMP_STRIDE
SEL_BLOCK = 64
SEL_TOP = 16
N_INIT_BLOCKS = 1
N_LOCAL_BLOCKS = 2
WINDOW = 512
FORCE_SCORE = 1e4

RWKV_WIDTH = D_MODEL // 4
RWKV_N = 64
RWKV_HEADS = RWKV_WIDTH // RWKV_N
RWKV_DECAY_RANK = 64
RWKV_AAA_RANK = 64
RWKV_GATE_RANK = 160
RWKV_SIZES = (RWKV_WIDTH, RWKV_WIDTH, RWKV_WIDTH, RWKV_DECAY_RANK, RWKV_AAA_RANK, RWKV_GATE_RANK)
RWKV_COLS = sum(RWKV_SIZES)
RWKV_LN_EPS = 64e-5

N_BRANCH = 3
D_FF = 256 * ((8 * D_MODEL // 3 + 255) // 256)
CONV_W = 3

IN_SIZES = (GLA_HEADS * GLA_DK, GLA_HEADS * GLA_DK, GLA_WIDTH, GLA_WIDTH, GLA_GATE_RANK,
            NSA_WIDTH, NSA_KV_WIDTH, NSA_KV_WIDTH, NSA_KV_WIDTH, NSA_KV_WIDTH, NSA_KV_WIDTH, NSA_KV_WIDTH,
            NSA_HEADS * 3,
            RWKV_COLS,
            N_BRANCH * D_MODEL)

LANE = 128
SUBLANE = 8
NSA_TQ = 512
NSA_TK = 1024
GLA_TC = 512
DEC_PAGES = 16
RWKV_NB = 4
VMEM_LIMIT = 48 * 1024 * 1024


def _round_up(n, m):
    return -(-n // m) * m


W_IN_TILE = 512
_C_NSA = sum(IN_SIZES[:5])
_C_RWKV = sum(IN_SIZES[:13])
_C_MG = sum(IN_SIZES[:14])
GLA_OFF = tuple(int(o) for o in np.concatenate([[0], np.cumsum(IN_SIZES[:4])]))
NSA_GATE_OFF = NSA_WIDTH + 6 * NSA_KV_WIDTH


def _pick(n, cands):
    for c in cands:
        if n % c == 0:
            return c
    return n


def _rmsnorm_kernel(x_ref, g_ref, o_ref):
    x = x_ref[...]
    y = x * lax.rsqrt(jnp.mean(x * x, axis=-1, keepdims=True) + NORM_EPS)
    o_ref[...] = (y * g_ref[...]).astype(o_ref.dtype)


def rmsnorm_pallas(x, g, out_dtype=jnp.bfloat16):
    M, D = x.shape
    tm = _pick(M, (512, 256, 128, 64, 32, 16, 8))
    return pl.pallas_call(
        _rmsnorm_kernel,
        grid=(M // tm,),
        in_specs=[pl.BlockSpec((tm, D), lambda i: (i, 0)), pl.BlockSpec((1, D), lambda i: (0, 0))],
        out_specs=pl.BlockSpec((tm, D), lambda i: (i, 0)),
        out_shape=jax.ShapeDtypeStruct((M, D), out_dtype),
        compiler_params=pltpu.CompilerParams(dimension_semantics=("parallel",), vmem_limit_bytes=VMEM_LIMIT),
        name="rmsnorm",
    )(x, g.reshape(1, D))


def _mm_kernel(*refs, nk, has_res):
    x_ref, w_ref = refs[:2]
    res_ref = refs[2] if has_res else None
    o_ref, acc_ref = refs[-2:]
    k = pl.program_id(2)
    part = jnp.dot(x_ref[...], w_ref[...], preferred_element_type=jnp.float32)

    def finish(v):
        if has_res:
            v = v + res_ref[...]
        o_ref[...] = v.astype(o_ref.dtype)

    if nk == 1:
        finish(part)
    else:
        @pl.when(k == 0)
        def _():
            acc_ref[...] = part

        @pl.when(jnp.logical_and(k > 0, k < nk - 1))
        def _():
            acc_ref[...] += part

        @pl.when(k == nk - 1)
        def _():
            finish(acc_ref[...] + part)


def mm(x, w, res=None, out_dtype=jnp.float32):
    M, K = x.shape
    N = w.shape[1]
    tm = _pick(M, (1024, 512, 256, 128, 64, 32, 16, 8))
    tk = K if K <= 4096 else _pick(K, (5504, 4096, 2048, 1024, 512))
    nk = K // tk
    tn = _pick(N, (1024, 512, 256, 128) if (nk == 1 and res is None) else (512, 256, 128))
    in_specs = [pl.BlockSpec((tm, tk), lambda i, j, k: (i, k)),
                pl.BlockSpec((tk, tn), lambda i, j, k: (k, j))]
    args = [x, w]
    if res is not None:
        in_specs.append(pl.BlockSpec((tm, tn), lambda i, j, k: (i, j)))
        args.append(res)
    return pl.pallas_call(
        functools.partial(_mm_kernel, nk=nk, has_res=res is not None),
        grid=(M // tm, N // tn, nk),
        in_specs=in_specs,
        out_specs=pl.BlockSpec((tm, tn), lambda i, j, k: (i, j)),
        out_shape=jax.ShapeDtypeStruct((M, N), out_dtype),
        scratch_shapes=[pltpu.VMEM((tm, tn) if nk > 1 else (SUBLANE, LANE), jnp.float32)],
        compiler_params=pltpu.CompilerParams(
            dimension_semantics=("parallel", "parallel", "arbitrary"),
            vmem_limit_bytes=VMEM_LIMIT),
        name="dense_mm",
    )(*args)


def _merge_kernel(oa_ref, ob_ref, oc_ref, wa_ref, wb_ref, wc_ref, ga_ref, gb_ref, gc_ref, o_ref):
    f32 = jnp.float32
    acc = jax.nn.sigmoid(ga_ref[...]) * jnp.dot(oa_ref[...], wa_ref[...], preferred_element_type=f32)
    acc += jax.nn.sigmoid(gb_ref[...]) * jnp.dot(ob_ref[...], wb_ref[...], preferred_element_type=f32)
    acc += jax.nn.sigmoid(gc_ref[...]) * jnp.dot(oc_ref[...], wc_ref[...], preferred_element_type=f32)
    o_ref[...] = acc.astype(o_ref.dtype)


def merge_mm(o_a, o_b, o_c, w_a, w_b, w_c, mg, out_dtype=jnp.bfloat16):
    M = o_a.shape[0]
    D = w_a.shape[1]
    tm = _pick(M, (1024, 512, 256, 128, 64, 32, 16, 8))
    tn = _pick(D, (512, 256, 128))
    nj = D // tn
    o_spec = lambda o: pl.BlockSpec((tm, o.shape[1]), lambda i, j: (i, 0))
    w_spec = lambda w: pl.BlockSpec((w.shape[0], tn), lambda i, j: (0, j))
    g_spec = lambda b: pl.BlockSpec((tm, tn), lambda i, j: (i, b * nj + j))
    return pl.pallas_call(
        _merge_kernel,
        grid=(M // tm, nj),
        in_specs=[o_spec(o_a), o_spec(o_b), o_spec(o_c), w_spec(w_a), w_spec(w_b), w_spec(w_c),
                  g_spec(0), g_spec(1), g_spec(2)],
        out_specs=pl.BlockSpec((tm, tn), lambda i, j: (i, j)),
        out_shape=jax.ShapeDtypeStruct((M, D), out_dtype),
        compiler_params=pltpu.CompilerParams(dimension_semantics=("parallel", "parallel"),
                                             vmem_limit_bytes=VMEM_LIMIT),
        name="merge_mm",
    )(o_a, o_b, o_c, w_a, w_b, w_c, mg, mg, mg)


def _ffn_gate_up_kernel(x_ref, wg_ref, wu_ref, cw_ref, cb_ref, st_ref, act_ref, tail_ref, carry_scr, *, tm, tps, T):
    f32 = jnp.float32
    i, j = pl.program_id(0), pl.program_id(1)
    x = x_ref[...]
    h = jnp.dot(x, wg_ref[...].astype(x.dtype), preferred_element_type=f32)
    u = jnp.dot(x, wu_ref[...].astype(x.dtype), preferred_element_type=f32)
    cw = cw_ref[...]
    if T == 1:
        prev2, prev1 = st_ref[0], st_ref[1]
        tail_ref[0] = prev1
        tail_ref[1] = h
    else:
        tail = jnp.where(i % tps == 0, st_ref[...], carry_scr[j])
        row = lax.broadcasted_iota(jnp.int32, h.shape, 0)
        prev1 = jnp.where(row == 0, tail[7:8], pltpu.roll(h, 1, axis=0))
        prev2 = jnp.where(row == 0, tail[6:7], jnp.where(row == 1, tail[7:8], pltpu.roll(h, 2, axis=0)))
        last = h[tm - SUBLANE:tm]
        carry_scr[j] = last
        tail_ref[...] = last
    hc = cb_ref[...] + prev2 * cw[0:1] + prev1 * cw[1:2] + h * cw[2:3]
    act_ref[...] = (jax.nn.silu(hc) * u).astype(act_ref.dtype)


def ffn_gate_up(xn, w_gate, w_up, conv_w, conv_b, conv0, B, T):
    M, D = xn.shape
    F = w_gate.shape[1]
    tn = _pick(F, (512, 256, 128))
    nj = F // tn
    cw = jnp.pad(conv_w, ((0, SUBLANE - CONV_W), (0, 0)))
    cb = conv_b.reshape(1, F)
    if T == 1:
        tm, tps = M, 1
        st = conv0.transpose(1, 0, 2)
        st_spec = pl.BlockSpec((2, B, tn), lambda i, j: (0, 0, j))
        tail_shape, tail_spec = (2, B, F), pl.BlockSpec((2, B, tn), lambda i, j: (0, 0, j))
    else:
        tm = _pick(T, (1024, 512, 256, 128, 64, 32, 16, 8))
        tps = T // tm
        st = jnp.pad(conv0, ((0, 0), (SUBLANE - 2, 0), (0, 0)))
        st_spec = pl.BlockSpec((None, SUBLANE, tn), lambda i, j: (i // tps, 0, j))
        tail_shape, tail_spec = (M // tm, SUBLANE, F), pl.BlockSpec((None, SUBLANE, tn), lambda i, j: (i, 0, j))
    act, tail = pl.pallas_call(
        functools.partial(_ffn_gate_up_kernel, tm=tm, tps=tps, T=T),
        grid=(M // tm, nj),
        in_specs=[pl.BlockSpec((tm, D), lambda i, j: (i, 0)),
                  pl.BlockSpec((D, tn), lambda i, j: (0, j)),
                  pl.BlockSpec((D, tn), lambda i, j: (0, j)),
                  pl.BlockSpec((SUBLANE, tn), lambda i, j: (0, j)),
                  pl.BlockSpec((1, tn), lambda i, j: (0, j)),
                  st_spec],
        out_specs=[pl.BlockSpec((tm, tn), lambda i, j: (i, j)), tail_spec],
        out_shape=[jax.ShapeDtypeStruct((M, F), jnp.bfloat16), jax.ShapeDtypeStruct(tail_shape, jnp.float32)],
        scratch_shapes=[pltpu.VMEM((nj, SUBLANE, tn), jnp.float32)],
        compiler_params=pltpu.CompilerParams(dimension_semantics=("arbitrary", "arbitrary"),
                                             vmem_limit_bytes=VMEM_LIMIT),
        name="ffn_gate_up",
    )(xn, w_gate, w_up, cw, cb, st)
    conv_new = tail.transpose(1, 0, 2) if T == 1 else tail[tps - 1::tps, SUBLANE - 2:]
    return act, conv_new


def _gla_kernel(q_ref, k_ref, v_ref, og_ref, lo_ref, wa_ref, ba_ref, gn_ref, s0_ref, o_ref, sout_ref, s_scr, *, Tc, C, valid):
    f32, bf16 = jnp.float32, jnp.bfloat16
    c = pl.program_id(2)

    @pl.when(c == 0)
    def _():
        s_scr[...] = s0_ref[...]

    row = lax.broadcasted_iota(jnp.int32, (C, GLA_DK), 0)
    tril = lax.broadcasted_iota(jnp.int32, (C, C), 0) >= lax.broadcasted_iota(jnp.int32, (C, C), 1)
    wa = wa_ref[...]
    ba = ba_ref[...]
    gn = gn_ref[...]
    S = s_scr[...]
    for n in range(Tc // C):
        rows = slice(n * C, (n + 1) * C)
        la = jax.nn.log_sigmoid(jnp.dot(lo_ref[rows, :].astype(bf16), wa, preferred_element_type=f32) + ba) / GLA_TAU
        if valid < C:
            la = jnp.where(row < valid, la, 0.0)
        bc = la
        d = 1
        while d < C:
            bc = bc + jnp.where(row >= d, pltpu.roll(bc, d, axis=0), 0.0)
            d *= 2
        b_mid = bc[C // 2:C // 2 + 1]
        b_last = bc[C - 1:C]
        q = q_ref[rows, :] * GLA_DK ** -0.5
        k = k_ref[rows, :]
        v = v_ref[rows, :].astype(bf16)
        att = lax.dot_general((q * jnp.exp(bc - b_mid)).astype(bf16), (k * jnp.exp(b_mid - bc)).astype(bf16),
                              (((1,), (1,)), ((), ())), preferred_element_type=f32)
        att = jnp.where(tril, att, 0.0)
        o = jnp.dot(att.astype(bf16), v, preferred_element_type=f32)
        o = o + jnp.dot((q * jnp.exp(bc)).astype(bf16), S.astype(bf16), preferred_element_type=f32)
        kd = jnp.concatenate([k * jnp.exp(b_last - bc), jnp.broadcast_to(jnp.exp(b_last), (SUBLANE, GLA_DK))], axis=0)
        kdt = kd.T
        S = kdt[:, C:C + 1] * S + jnp.dot(kdt[:, :C].astype(bf16), v, preferred_element_type=f32)
        o = o * lax.rsqrt(jnp.mean(o * o, axis=-1, keepdims=True) + NORM_EPS) * gn
        o_ref[rows, :] = (o * jax.nn.silu(og_ref[rows, :])).astype(o_ref.dtype)
    s_scr[...] = S

    @pl.when(c == pl.num_programs(2) - 1)
    def _():
        sout_ref[...] = S


def gla_pallas(hmix, wa2, ba, gnorm, s0, seg_off):
    B, T, _ = hmix.shape
    H = GLA_HEADS
    n_tok = T
    if T % GLA_CHUNK == 0:
        Tc, C = min(GLA_TC, T), GLA_CHUNK
    else:
        assert T < SUBLANE
        Tc = C = SUBLANE
        hmix = jnp.pad(hmix, ((0, 0), (0, SUBLANE - T), (0, 0)))
        T = SUBLANE
    oq, ok, ov, og, ol = seg_off
    wa = jnp.pad(wa2, ((0, LANE - wa2.shape[0]), (0, 0))).astype(jnp.bfloat16)
    col = lambda off, w: (lambda b, h, c: (b, c, off // w + h))
    o, s = pl.pallas_call(
        functools.partial(_gla_kernel, Tc=Tc, C=C, valid=min(n_tok, C)),
        grid=(B, H, T // Tc),
        in_specs=[pl.BlockSpec((None, Tc, GLA_DK), col(oq, GLA_DK)),
                  pl.BlockSpec((None, Tc, GLA_DK), col(ok, GLA_DK)),
                  pl.BlockSpec((None, Tc, GLA_DV), col(ov, GLA_DV)),
                  pl.BlockSpec((None, Tc, GLA_DV), col(og, GLA_DV)),
                  pl.BlockSpec((None, Tc, LANE), lambda b, h, c: (b, c, ol // LANE)),
                  pl.BlockSpec((LANE, GLA_DK), lambda b, h, c: (0, h)),
                  pl.BlockSpec((1, GLA_DK), lambda b, h, c: (0, h)),
                  pl.BlockSpec((1, GLA_DV), lambda b, h, c: (0, 0)),
                  pl.BlockSpec((None, None, GLA_DK, GLA_DV), lambda b, h, c: (b, h, 0, 0))],
        out_specs=[pl.BlockSpec((None, Tc, GLA_DV), lambda b, h, c: (b, c, h)),
                   pl.BlockSpec((None, None, GLA_DK, GLA_DV), lambda b, h, c: (b, h, 0, 0))],
        out_shape=[jax.ShapeDtypeStruct((B, T, H * GLA_DV), jnp.bfloat16),
                   jax.ShapeDtypeStruct((B, H, GLA_DK, GLA_DV), jnp.float32)],
        scratch_shapes=[pltpu.VMEM((GLA_DK, GLA_DV), jnp.float32)],
        compiler_params=pltpu.CompilerParams(dimension_semantics=("parallel", "parallel", "arbitrary"),
                                             vmem_limit_bytes=VMEM_LIMIT),
        name="gla_chunked",
    )(hmix, hmix, hmix, hmix, hmix, wa, ba.reshape(1, -1), gnorm.reshape(1, -1), s0)
    return o[:, :n_tok], s


def _head_sums(x, bd):
    f32, bf16 = jnp.float32, jnp.bfloat16
    outs = []
    for t in range(x.shape[1] // LANE):
        p = x[:, t * LANE:(t + 1) * LANE]
        hi = p.astype(bf16)
        lo = (p - hi.astype(f32)).astype(bf16)
        outs.append(jnp.dot(hi, bd, preferred_element_type=f32) + jnp.dot(lo, bd, preferred_element_type=f32))
    return jnp.concatenate(outs, axis=1)


def _block_diag_ones():
    rr = lax.broadcasted_iota(jnp.int32, (LANE, LANE), 0) // RWKV_N
    cc = lax.broadcasted_iota(jnp.int32, (LANE, LANE), 1) // RWKV_N
    return jnp.where(rr == cc, 1.0, 0.0).astype(jnp.bfloat16)


def _rwkv_prep_kernel(x_ref, sh_ref, mu_ref, w0_ref, a0_ref, kkw_ref, ka_ref, rk_ref, w2_ref, a2_ref, g2_ref,
                      r_ref, wl_ref, k_ref, v_ref, kk_ref, a_ref, gate_ref, bonus_ref, tail_ref, carry_scr, *, tm, T):
    f32, bf16 = jnp.float32, jnp.bfloat16
    W = RWKV_WIDTH
    i = pl.program_id(1)
    x = x_ref[...]
    if T == 1:
        prev = sh_ref[...]
        tail_ref[...] = x
    else:
        first = jnp.where(i == 0, sh_ref[SUBLANE - 1:SUBLANE], carry_scr[SUBLANE - 1:SUBLANE])
        row = lax.broadcasted_iota(jnp.int32, x.shape, 0)
        prev = jnp.where(row == 0, first, pltpu.roll(x, 1, axis=0))
        last = x[tm - SUBLANE:tm]
        carry_scr[...] = last
        tail_ref[...] = last
    rm = x + (prev - x) * mu_ref[...]
    r, k, v = rm[:, :W], rm[:, W:2 * W], rm[:, 2 * W:3 * W]
    lo = rm[:, 3 * W:3 * W + LANE]
    glo = rm[:, 3 * W + LANE:]
    w_raw = w0_ref[...] + jnp.dot(jnp.tanh(lo).astype(bf16), w2_ref[...], preferred_element_type=f32)
    wl_ref[...] = -jnp.exp(-jax.nn.softplus(-w_raw) - 0.5)
    a = jax.nn.sigmoid(a0_ref[...] + jnp.dot(lo.astype(bf16), a2_ref[...], preferred_element_type=f32))
    gate_ref[...] = jnp.dot(jax.nn.sigmoid(glo).astype(bf16), g2_ref[...], preferred_element_type=f32)
    bd = _block_diag_ones()
    kk = k * kkw_ref[...]
    kk_ref[...] = kk * lax.rsqrt(jnp.maximum(_head_sums(kk * kk, bd), 1e-24))
    k2 = k * (1.0 + (a - 1.0) * ka_ref[...])
    bonus_ref[...] = _head_sums(r * k2 * rk_ref[...], bd) * v
    r_ref[...] = r
    k_ref[...] = k2
    v_ref[...] = v
    a_ref[...] = a


def rwkv_prep(hr, shift0, mu, w0, w2, a0, a2, g2, kkw, ka, rk):
    B, T, WP = hr.shape
    W = RWKV_WIDTH
    bf16 = jnp.bfloat16
    padc = lambda t: jnp.pad(t, ((0, 0), (0, WP - t.shape[1])))
    w2p = jnp.pad(w2, ((0, LANE - RWKV_DECAY_RANK), (0, 0))).astype(bf16)
    a2p = jnp.pad(a2, ((RWKV_DECAY_RANK, 0), (0, 0))).astype(bf16)
    gpad = WP - 3 * W - LANE
    g2p = jnp.pad(g2, ((0, gpad - RWKV_GATE_RANK), (0, 0))).astype(bf16)
    row = lambda t: t.reshape(1, -1)
    if T == 1:
        tm = 1
        sh = padc(shift0).reshape(B, 1, WP)
        sh_spec = pl.BlockSpec((None, 1, WP), lambda b, i: (b, 0, 0))
        tail_rows = 1
    else:
        tm = _pick(T, (256, 128, 64, 32, 16, 8))
        sh = jnp.broadcast_to(padc(shift0)[:, None, :], (B, SUBLANE, WP))
        sh_spec = pl.BlockSpec((None, SUBLANE, WP), lambda b, i: (b, 0, 0))
        tail_rows = SUBLANE
    vec = lambda n: pl.BlockSpec((1, n), lambda b, i: (0, 0))
    mat = lambda m: pl.BlockSpec(m.shape, lambda b, i: (0, 0))
    o_spec = pl.BlockSpec((None, tm, W), lambda b, i: (b, i, 0))
    o_shape = jax.ShapeDtypeStruct((B, T, W), jnp.float32)
    outs = pl.pallas_call(
        functools.partial(_rwkv_prep_kernel, tm=tm, T=T),
        grid=(B, T // tm),
        in_specs=[pl.BlockSpec((None, tm, WP), lambda b, i: (b, i, 0)), sh_spec, vec(WP),
                  vec(W), vec(W), vec(W), vec(W), vec(W), mat(w2p), mat(a2p), mat(g2p)],
        out_specs=[o_spec] * 8 + [pl.BlockSpec((None, tail_rows, WP), lambda b, i: (b, 0, 0))],
        out_shape=[o_shape] * 8 + [jax.ShapeDtypeStruct((B, tail_rows, WP), jnp.float32)],
        scratch_shapes=[pltpu.VMEM((SUBLANE, WP), jnp.float32)],
        compiler_params=pltpu.CompilerParams(dimension_semantics=("parallel", "arbitrary"),
                                             vmem_limit_bytes=VMEM_LIMIT),
        name="rwkv_prep",
    )(hr, sh, row(padc(mu.reshape(1, -1))), row(w0), row(a0), row(kkw), row(ka), row(rk), w2p, a2p, g2p)
    return outs[:8], outs[8][:, tail_rows - 1, :RWKV_COLS]


def _rwkv_post_kernel(y_ref, bonus_ref, gate_ref, lw_ref, lb_ref, o_ref):
    bd = _block_diag_ones()
    y = y_ref[...]
    d = y - _head_sums(y, bd) * (1.0 / RWKV_N)
    var = _head_sums(d * d, bd) * (1.0 / RWKV_N)
    yn = d * lax.rsqrt(var + RWKV_LN_EPS) * lw_ref[...] + lb_ref[...]
    o_ref[...] = ((yn + bonus_ref[...]) * gate_ref[...]).astype(o_ref.dtype)


def rwkv_post(y, bonus, gate, ln_w, ln_b):
    M, W = y.shape
    tm = _pick(M, (256, 128, 64, 32, 16, 8))
    spec = pl.BlockSpec((tm, W), lambda i: (i, 0))
    vec = pl.BlockSpec((1, W), lambda i: (0, 0))
    return pl.pallas_call(
        _rwkv_post_kernel,
        grid=(M // tm,),
        in_specs=[spec, spec, spec, vec, vec],
        out_specs=spec,
        out_shape=jax.ShapeDtypeStruct((M, W), jnp.bfloat16),
        compiler_params=pltpu.CompilerParams(dimension_semantics=("parallel",), vmem_limit_bytes=VMEM_LIMIT),
        name="rwkv_post",
    )(y, bonus, gate, ln_w.reshape(1, W), ln_b.reshape(1, W))


def _rwkv_kernel(r_ref, wl_ref, k_ref, v_ref, kk_ref, a_ref, s0_ref, y_ref, sout_ref, s_scr, *, NB, NP, Tc):
    c = pl.program_id(1)
    f32, bf16 = jnp.float32, jnp.bfloat16
    U = min(SUBLANE, Tc)

    @pl.when(c == 0)
    def _():
        s_scr[...] = s0_ref[...]

    sub = lax.broadcasted_iota(jnp.int32, (RWKV_N, LANE), 0)
    lane = lax.broadcasted_iota(jnp.int32, (RWKV_N, LANE), 1)
    eye2 = (lane % RWKV_N) == sub
    left = lane < RWKV_N
    rr = lax.broadcasted_iota(jnp.int32, (LANE, LANE), 0) // RWKV_N
    cc = lax.broadcasted_iota(jnp.int32, (LANE, LANE), 1) // RWKV_N
    bd = jnp.where(rr == cc, 1.0, 0.0).astype(bf16)
    bd2 = jnp.concatenate([bd, bd], axis=0)

    def ssb(p, two_piece=True):
        hi = p.astype(bf16)
        if not two_piece:
            return jnp.dot(hi, bd, preferred_element_type=f32)
        lo = (p - hi.astype(f32)).astype(bf16)
        return jnp.dot(jnp.concatenate([hi, lo], axis=1), bd2, preferred_element_type=f32)

    eye_all = jnp.concatenate([eye2] * NP, axis=0)

    def bcast(x8, s):
        return jnp.concatenate(
            [jnp.broadcast_to(x8[s:s + 1, p * LANE:(p + 1) * LANE], (RWKV_N, LANE)) for p in range(NP)], axis=0)

    def vcols(vt, v8, s):
        if vt is None:
            return ssb(jnp.where(eye_all, bcast(v8, s), 0.0))
        return jnp.concatenate(
            [jnp.where(left, jnp.broadcast_to(vt[p][:RWKV_N, s:s + 1], (RWKV_N, LANE)),
                       jnp.broadcast_to(vt[p][RWKV_N:, s:s + 1], (RWKV_N, LANE))) for p in range(NP)], axis=0)

    def body(g, carry):
        rows = pl.ds(pl.multiple_of(g * U, U), U)
        tiles = []
        for nb in range(NB):
            kk8 = kk_ref[nb, rows, :]
            v8 = v_ref[nb, rows, :]
            vt = [v8[:, p * LANE:(p + 1) * LANE].T for p in range(NP)] if U == SUBLANE else None
            tiles.append(dict(r=r_ref[nb, rows, :], w=jnp.exp(wl_ref[nb, rows, :]), k=k_ref[nb, rows, :], v=v8, vt=vt,
                              ka=kk8 * a_ref[nb, rows, :], nk=-kk8))
        S = [s_scr[nb] for nb in range(NB)]
        ys = [[] for _ in range(NB)]
        for s in range(U):
            sa = [ssb(S[nb] * bcast(t['nk'], s)) for nb, t in enumerate(tiles)]
            for nb, t in enumerate(tiles):
                S[nb] = S[nb] * bcast(t['w'], s) + sa[nb] * bcast(t['ka'], s) + vcols(t['vt'], t['v'], s) * bcast(t['k'], s)
            yb = [jnp.where(eye_all, ssb(S[nb] * bcast(t['r'], s), two_piece=False), 0.0) for nb, t in enumerate(tiles)]
            for nb in range(NB):
                ys[nb].append(jnp.concatenate(
                    [jnp.sum(yb[nb][p * RWKV_N:(p + 1) * RWKV_N], axis=0, keepdims=True) for p in range(NP)], axis=1))
        for nb in range(NB):
            s_scr[nb] = S[nb]
            y_ref[nb, rows, :] = ys[nb][0] if U == 1 else jnp.concatenate(ys[nb], axis=0)
        return carry

    lax.fori_loop(0, Tc // U, body, 0)

    @pl.when(c == pl.num_programs(1) - 1)
    def _():
        sout_ref[...] = s_scr[...]


def rwkv_scan_pallas(r, w_log, k, v, kk, a, s0):
    B, T, W = r.shape
    H = W // RWKV_N
    NP = H // 2
    NB = RWKV_NB if B % RWKV_NB == 0 else 1
    Tc = 128 if T % 128 == 0 else T
    s0p = s0.reshape(B, NP, 2, RWKV_N, RWKV_N).transpose(0, 1, 3, 2, 4).reshape(B, NP * RWKV_N, LANE)
    blk = pl.BlockSpec((NB, Tc, W), lambda b, c: (b, c, 0))
    sblk = pl.BlockSpec((NB, NP * RWKV_N, LANE), lambda b, c: (b, 0, 0))
    y, sp = pl.pallas_call(
        functools.partial(_rwkv_kernel, NB=NB, NP=NP, Tc=Tc),
        grid=(B // NB, T // Tc),
        in_specs=[blk] * 6 + [sblk],
        out_specs=[blk, sblk],
        out_shape=[jax.ShapeDtypeStruct((B, T, W), jnp.float32),
                   jax.ShapeDtypeStruct((B, NP * RWKV_N, LANE), jnp.float32)],
        scratch_shapes=[pltpu.VMEM((NB, NP * RWKV_N, LANE), jnp.float32)],
        compiler_params=pltpu.CompilerParams(dimension_semantics=("parallel", "arbitrary"),
                                             vmem_limit_bytes=VMEM_LIMIT),
        name="rwkv7_scan",
    )(r, w_log, k, v, kk, a, s0p)
    s_fin = sp.reshape(B, NP, RWKV_N, 2, RWKV_N).transpose(0, 1, 3, 2, 4).reshape(B, H, RWKV_N, RWKV_N)
    return y, s_fin


def rope_tables(pos):
    half = ROPE_DIM // 2
    inv = ROPE_THETA ** (-jnp.arange(half, dtype=jnp.float32) / half)
    ang = pos.astype(jnp.float32)[:, None] * inv[None, :]
    cos, sin = jnp.cos(ang), jnp.sin(ang)
    T = pos.shape[0]
    z = jnp.zeros((T, HEAD_DIM - ROPE_DIM), jnp.float32)
    zh = jnp.zeros((T, half), jnp.float32)
    c = jnp.concatenate([cos, cos, jnp.ones_like(z)], axis=1)
    s_up = jnp.concatenate([-sin, zh, z], axis=1)
    s_dn = jnp.concatenate([zh, sin, z], axis=1)
    return c, s_up, s_dn


def _nsa_prep_kernel(q_ref, c_ref, s_ref, w_ref, tc_ref, tu_ref, td_ref, qo_ref, co_ref, so_ref, wo_ref):
    c, su, sd = tc_ref[...], tu_ref[...], td_ref[...]
    half = ROPE_DIM // 2

    def rot(x):
        return x * c + pltpu.roll(x, HEAD_DIM - half, axis=1) * su + pltpu.roll(x, half, axis=1) * sd

    for h in range(NSA_HEADS):
        cols = slice(h * HEAD_DIM, (h + 1) * HEAD_DIM)
        qo_ref[:, cols] = (rot(q_ref[:, cols]) * HEAD_DIM ** -0.5).astype(qo_ref.dtype)
    for src, dst in ((c_ref, co_ref), (s_ref, so_ref), (w_ref, wo_ref)):
        for g in range(NSA_KV_HEADS):
            cols = slice(g * HEAD_DIM, (g + 1) * HEAD_DIM)
            dst[:, cols] = rot(src[:, cols])
        dst[:, NSA_KV_WIDTH:] = src[:, NSA_KV_WIDTH:]


def nsa_prep(hn, pos):
    B, T, _ = hn.shape
    tm = _pick(T, (256, 128, 64, 32, 16, 8))
    tabs = rope_tables(pos)
    kvw = 2 * NSA_KV_WIDTH
    q_spec = pl.BlockSpec((None, tm, NSA_WIDTH), lambda b, i: (b, i, 0))
    kv_spec = lambda n: pl.BlockSpec((None, tm, kvw), lambda b, i: (b, i, NSA_WIDTH // kvw + n))
    t_spec = pl.BlockSpec((tm, HEAD_DIM), lambda b, i: (i, 0))
    o_spec = pl.BlockSpec((None, tm, kvw), lambda b, i: (b, i, 0))
    kv_shape = jax.ShapeDtypeStruct((B, T, kvw), jnp.float32)
    return pl.pallas_call(
        _nsa_prep_kernel,
        grid=(B, T // tm),
        in_specs=[q_spec, kv_spec(0), kv_spec(1), kv_spec(2), t_spec, t_spec, t_spec],
        out_specs=[q_spec, o_spec, o_spec, o_spec],
        out_shape=[jax.ShapeDtypeStruct((B, T, NSA_WIDTH), jnp.bfloat16), kv_shape, kv_shape, kv_shape],
        compiler_params=pltpu.CompilerParams(dimension_semantics=("parallel", "parallel"),
                                             vmem_limit_bytes=VMEM_LIMIT),
        name="nsa_prep",
    )(hn, hn, hn, hn, *tabs)


def _compress_kernel(x_ref, w1_ref, w2_ref, pe_ref, ko_ref, vo_ref, *, ns):
    f32, bf16 = jnp.float32, jnp.bfloat16
    G, HD = NSA_KV_HEADS, HEAD_DIM
    row_w = 2 * NSA_KV_WIDTH
    for kv, o_ref in ((0, ko_ref), (1, vo_ref)):
        pos = jnp.zeros((SUBLANE, HD), f32)
        for p in range(CMP_STRIDE):
            w = w1_ref[kv, p]
            lo = jnp.broadcast_to(pe_ref[kv, p:p + 1, :], (SUBLANE, HD)).astype(bf16)
            hi = jnp.broadcast_to(pe_ref[kv, CMP_STRIDE + p:CMP_STRIDE + p + 1, :], (SUBLANE, HD)).astype(bf16)
            pos = pos + jnp.dot(lo, w, preferred_element_type=f32)[:, :HD] + jnp.dot(hi, w, preferred_element_type=f32)[:, HD:]
        pos = pos[0:1]
        for g in range(G):
            acc = jnp.zeros((ns, 2 * HD), f32)
            for p in range(CMP_STRIDE):
                c0 = p * row_w + kv * NSA_KV_WIDTH + g * HD
                acc = acc + jnp.dot(x_ref[:, c0:c0 + HD].astype(bf16), w1_ref[kv, p], preferred_element_type=f32)
            nxt = pltpu.roll(acc[:, HD:], ns - 1, axis=0)
            hid = jax.nn.gelu(acc[:, :HD] + nxt + pos)
            o_ref[g] = jnp.dot(hid.astype(bf16), w2_ref[kv], preferred_element_type=f32).astype(o_ref.dtype)


def compress_weights(w1k, w2k, pek, w1v, w2v, pev):
    bf16 = jnp.bfloat16
    cat = lambda w1: jnp.concatenate([w1[:CMP_STRIDE], w1[CMP_STRIDE:]], axis=-1)
    return (jnp.stack([cat(w1k), cat(w1v)]).astype(bf16), jnp.stack([w2k, w2v]).astype(bf16), jnp.stack([pek, pev]))


def compress_pallas(kv_rows, w1, w2, pe):
    B, T, W = kv_rows.shape
    ns = T // CMP_STRIDE
    bf16 = jnp.bfloat16
    x = kv_rows.reshape(B, ns, CMP_STRIDE * W)
    out = jax.ShapeDtypeStruct((B, NSA_KV_HEADS, ns, HEAD_DIM), bf16)
    o_spec = pl.BlockSpec((None, NSA_KV_HEADS, ns, HEAD_DIM), lambda b: (b, 0, 0, 0))
    return pl.pallas_call(
        functools.partial(_compress_kernel, ns=ns),
        grid=(B,),
        in_specs=[pl.BlockSpec((None, ns, CMP_STRIDE * W), lambda b: (b, 0, 0)),
                  pl.BlockSpec(w1.shape, lambda b: (0, 0, 0, 0)),
                  pl.BlockSpec(w2.shape, lambda b: (0, 0, 0)),
                  pl.BlockSpec(pe.shape, lambda b: (0, 0, 0))],
        out_specs=[o_spec, o_spec],
        out_shape=[out, out],
        compiler_params=pltpu.CompilerParams(dimension_semantics=("parallel",), vmem_limit_bytes=VMEM_LIMIT),
        name="nsa_compress",
    )(x, w1, w2, pe)


def _dot_nt(a, b):
    return lax.dot_general(a, b, (((1,), (1,)), ((), ())), preferred_element_type=jnp.float32)


def _nsa_kernel(q_ref, kc_ref, vc_ref, ks_ref, vs_ref, kw_ref, vw_ref, g_ref, covt_ref, e_ref, o_ref,
                bias_scr, p4_scr, ocmp_scr, m_scr, acc_scr, *, TQ, TK, T, NS, NCP, n_top):
    f32, bf16 = jnp.float32, jnp.bfloat16
    R = NSA_GROUP
    i = pl.program_id(2)
    nchunk = T // TK
    qpos_col = i * TQ + lax.broadcasted_iota(jnp.int32, (TQ, 1), 0)

    kc = kc_ref[...]
    vc = vc_ref[...]
    cend = lax.broadcasted_iota(jnp.int32, (1, NCP), 1) * CMP_STRIDE + (CMP_BLOCK - 1)
    valid = cend <= qpos_col
    for r in range(R):
        s = _dot_nt(q_ref[:, r * HEAD_DIM:(r + 1) * HEAD_DIM], kc)
        s = jnp.where(valid, s, NEG_INF)
        m = jnp.max(s, axis=-1, keepdims=True)
        p = jnp.where(valid, jnp.exp(s - m), 0.0)
        l = jnp.sum(p, axis=-1, keepdims=True)
        p = (p / jnp.where(l > 0.0, l, 1.0)).astype(bf16)
        p4_scr[:, r * NCP:(r + 1) * NCP] = p
        ocmp_scr[r] = jnp.dot(p, vc, preferred_element_type=f32)

    imp_t = _dot_nt(covt_ref[...], p4_scr[...])
    j = lax.broadcasted_iota(jnp.int32, (NS, TQ), 0)
    qblk = (i * TQ + lax.broadcasted_iota(jnp.int32, (NS, TQ), 1)) // SEL_BLOCK
    forced = (j < N_INIT_BLOCKS) | ((j <= qblk) & (j > qblk - N_LOCAL_BLOCKS))
    score = jnp.where(forced, FORCE_SCORE, jnp.where(j <= qblk, imp_t, NEG_INF))
    rank = jnp.zeros((NS, TQ), f32)
    for a in range(NS):
        row = score[a:a + 1, :]
        beats = (row > score) | ((row == score) & (a < j))
        rank = rank + jnp.where(beats, 1.0, 0.0)
    sel_t = jnp.where(rank < n_top, 1.0, 0.0)
    if NS < LANE:
        sel_t = jnp.concatenate([sel_t, jnp.zeros((LANE - NS, TQ), f32)], axis=0)
    sel = sel_t.T.astype(bf16)
    for c in range(nchunk):
        selexp = jnp.dot(sel, e_ref[:, c * TK:(c + 1) * TK], preferred_element_type=f32)
        kpos = c * TK + lax.broadcasted_iota(jnp.int32, (TQ, TK), 1)
        bias_scr[c] = jnp.where((selexp > 0.5) & (kpos <= qpos_col), 0.0, NEG_INF)

    hi = (i * TQ + TQ - 1) // TK + 1

    def attend(k_ref, v_ref, lo, masker):
        m_scr[...] = jnp.full(m_scr.shape, NEG_INF, f32)
        acc_scr[...] = jnp.zeros(acc_scr.shape, f32)

        def chunk(c, carry):
            rows = pl.ds(pl.multiple_of(c * TK, TK), TK)
            k = k_ref[rows, :].astype(bf16)
            v = jnp.concatenate([v_ref[rows, :].astype(bf16), jnp.ones((TK, HEAD_DIM), bf16)], axis=1)
            mk = masker(c)
            heads = range(R)
            sk = [mk(_dot_nt(q_ref[:, r * HEAD_DIM:(r + 1) * HEAD_DIM], k)) for r in heads]
            m_prev = [m_scr[r] for r in heads]
            m_new = [jnp.maximum(m_prev[r], jnp.max(sk[r][0], axis=-1, keepdims=True)) for r in heads]
            alpha = [jnp.exp(m_prev[r] - m_new[r]) for r in heads]
            ps = [jnp.exp(sk[r][0] - m_new[r]) for r in heads]
            ps = [p if sk[r][1] is None else jnp.where(sk[r][1], p, 0.0) for r, p in enumerate(ps)]
            pv = [jnp.dot(ps[r].astype(bf16), v, preferred_element_type=f32) for r in heads]
            for r in heads:
                acc_scr[r] = alpha[r] * acc_scr[r] + pv[r]
                m_scr[r] = m_new[r]
            return carry

        lax.fori_loop(lo, hi, chunk, 0)

    def normalised(r):
        return acc_scr[r, :, :HEAD_DIM] / acc_scr[r, :, HEAD_DIM:]

    def sel_masker(c):
        b = bias_scr[c]
        return lambda s: (s + b, None)

    attend(ks_ref, vs_ref, 0, sel_masker)
    g = pltpu.roll(jax.nn.sigmoid(g_ref[...]), (LANE - 3 * R * pl.program_id(1)) % LANE, axis=1)
    for r in range(R):
        ocmp_scr[r] = (g[:, 3 * r:3 * r + 1] * ocmp_scr[r]
                       + g[:, 3 * r + 1:3 * r + 2] * normalised(r))

    def win_masker(c):
        rel = (i * TQ - c * TK + lax.broadcasted_iota(jnp.int32, (TQ, TK), 0)
               - lax.broadcasted_iota(jnp.int32, (TQ, TK), 1))
        ok = (rel >= 0) & (rel < WINDOW)
        return lambda s: (jnp.where(ok, s, NEG_INF), ok)

    attend(kw_ref, vw_ref, jnp.maximum(i * TQ - (WINDOW - 1), 0) // TK, win_masker)
    for r in range(R):
        o = ocmp_scr[r] + g[:, 3 * r + 2:3 * r + 3] * normalised(r)
        o_ref[:, r * HEAD_DIM:(r + 1) * HEAD_DIM] = o.astype(o_ref.dtype)


def nsa_attention_pallas(qr, kcmp, vcmp, slc, win, hn, gate_col):
    B, T, _ = qr.shape
    G, R = NSA_KV_HEADS, NSA_GROUP
    TQ = min(NSA_TQ, T)
    TK = min(NSA_TK, T)
    NS = T // SEL_BLOCK
    NC = T // CMP_STRIDE - 1
    NCP = kcmp.shape[2]
    n_top = min(SEL_TOP, NS)
    ci = np.arange(NCP)[:, None] * CMP_STRIDE
    sj = np.arange(NS)[None, :] * SEL_BLOCK
    cover = np.clip(np.minimum(ci + CMP_BLOCK, sj + SEL_BLOCK) - np.maximum(ci, sj), 0, None) / CMP_BLOCK
    cover[NC:] = 0.0
    covt = jnp.asarray(np.tile(cover.T, (1, R)), jnp.bfloat16)
    e = jnp.asarray((np.arange(T)[None, :] // SEL_BLOCK) == np.arange(LANE)[:, None], jnp.bfloat16)
    k_spec = pl.BlockSpec((None, T, HEAD_DIM), lambda b, g, i: (b, 0, g))
    v_spec = pl.BlockSpec((None, T, HEAD_DIM), lambda b, g, i: (b, 0, G + g))
    cmp_spec = pl.BlockSpec((None, None, NCP, HEAD_DIM), lambda b, g, i: (b, g, 0, 0))
    return pl.pallas_call(
        functools.partial(_nsa_kernel, TQ=TQ, TK=TK, T=T, NS=NS, NCP=NCP, n_top=n_top),
        grid=(B, G, T // TQ),
        in_specs=[pl.BlockSpec((None, TQ, R * HEAD_DIM), lambda b, g, i: (b, i, g)),
                  cmp_spec, cmp_spec, k_spec, v_spec, k_spec, v_spec,
                  pl.BlockSpec((None, TQ, LANE), lambda b, g, i: (b, i, gate_col // LANE)),
                  pl.BlockSpec((NS, R * NCP), lambda b, g, i: (0, 0)),
                  pl.BlockSpec((LANE, T), lambda b, g, i: (0, 0))],
        out_specs=pl.BlockSpec((None, TQ, R * HEAD_DIM), lambda b, g, i: (b, i, g)),
        out_shape=jax.ShapeDtypeStruct((B, T, G * R * HEAD_DIM), jnp.bfloat16),
        scratch_shapes=[pltpu.VMEM((T // TK, TQ, TK), jnp.float32),
                        pltpu.VMEM((TQ, R * NCP), jnp.bfloat16),
                        pltpu.VMEM((R, TQ, HEAD_DIM), jnp.float32),
                        pltpu.VMEM((R, TQ, 1), jnp.float32),
                        pltpu.VMEM((R, TQ, 2 * HEAD_DIM), jnp.float32)],
        compiler_params=pltpu.CompilerParams(dimension_semantics=("parallel", "parallel", "arbitrary"),
                                             vmem_limit_bytes=VMEM_LIMIT),
        name="nsa_attention",
    )(qr, kcmp, vcmp, slc, slc, win, win, hn, covt, e)


def _dec_compress_kernel(pt_ref, *refs):
    f32, bf16 = jnp.float32, jnp.bfloat16
    pages, w1_ref, o_ref, reg_scr = refs[:DEC_PAGES], refs[DEC_PAGES], refs[DEC_PAGES + 1], refs[DEC_PAGES + 2]
    HD = HEAD_DIM
    nc = 2 * NSA_KV_HEADS
    seg = pages[0].shape[0] // (nc * CMP_STRIDE)
    acc = None
    for p in range(CMP_STRIDE):
        lhs = jnp.concatenate([pg[(CMP_STRIDE * n + p) * nc:(CMP_STRIDE * n + p + 1) * nc, :]
                               for pg in pages for n in range(seg)], axis=0).astype(bf16)
        w = jnp.concatenate([w1_ref[0, p], w1_ref[1, p]], axis=1)
        part = jnp.dot(lhs, w, preferred_element_type=f32)
        acc = part if acc is None else acc + part
    is_k = (lax.broadcasted_iota(jnp.int32, (acc.shape[0], 2 * HD), 0) % nc) < NSA_KV_HEADS
    fs = jnp.where(is_k, acc[:, :2 * HD], acc[:, 2 * HD:])
    reg_scr[0] = fs[:, :HD]
    reg_scr[1] = fs[:, HD:]
    nseg = DEC_PAGES * seg
    for c in range(nc):
        o_ref[:, c * 2 * HD:c * 2 * HD + HD] = reg_scr[0, pl.ds(c, nseg, stride=nc), :]
        o_ref[:, c * 2 * HD + HD:(c + 1) * 2 * HD] = reg_scr[1, pl.ds(c, nseg, stride=nc), :]


def dec_compress(cache, layer, page_table, w1):
    L, n_phys, page = cache.shape[:3]
    nc = 2 * NSA_KV_HEADS
    B, n_pages = page_table.shape
    seg = page // CMP_STRIDE
    steps = n_pages // DEC_PAGES
    c4 = cache.reshape(L, n_phys, page * nc, HEAD_DIM)
    page_spec = lambda k: pl.BlockSpec((None, None, page * nc, HEAD_DIM),
                                       lambda b, s, pt: (layer, pt[b, s * DEC_PAGES + k], 0, 0))
    ow = nc * 2 * HEAD_DIM
    return pl.pallas_call(
        _dec_compress_kernel,
        grid_spec=pltpu.PrefetchScalarGridSpec(
            num_scalar_prefetch=1,
            grid=(B, steps),
            in_specs=[page_spec(k) for k in range(DEC_PAGES)] + [pl.BlockSpec(w1.shape, lambda b, s, pt: (0, 0, 0, 0))],
            out_specs=pl.BlockSpec((None, DEC_PAGES * seg, ow), lambda b, s, pt: (b, s, 0)),
            scratch_shapes=[pltpu.VMEM((2, DEC_PAGES * seg * nc, HEAD_DIM), jnp.float32)]),
        out_shape=jax.ShapeDtypeStruct((B, n_pages * seg, ow), jnp.float32),
        compiler_params=pltpu.CompilerParams(dimension_semantics=("parallel", "arbitrary"),
                                             vmem_limit_bytes=VMEM_LIMIT),
        name="nsa_dec_compress",
    )(page_table, *([c4] * DEC_PAGES), w1)


def _dec_select_kernel(q_ref, fk_ref, fv_ref, w1_ref, w2_ref, pe_ref, covt_ref, ocmp_ref, idx_ref, *, NSEG, NS, NSP, n_top):
    f32, bf16 = jnp.float32, jnp.bfloat16
    HD, R = HEAD_DIM, NSA_GROUP
    NC = NSEG - 1

    def compressed(kv, f_ref):
        pos = jnp.zeros((SUBLANE, HD), f32)
        for p in range(CMP_STRIDE):
            w = w1_ref[kv, p]
            lo = jnp.broadcast_to(pe_ref[kv, p:p + 1, :], (SUBLANE, HD)).astype(bf16)
            hi = jnp.broadcast_to(pe_ref[kv, CMP_STRIDE + p:CMP_STRIDE + p + 1, :], (SUBLANE, HD)).astype(bf16)
            pos = pos + jnp.dot(lo, w, preferred_element_type=f32)[:, :HD] + jnp.dot(hi, w, preferred_element_type=f32)[:, HD:]
        nxt = pltpu.roll(f_ref[:, HD:], NSEG - 1, axis=0)
        hid = jax.nn.gelu(f_ref[:, :HD] + nxt + pos[0:1])
        return jnp.dot(hid.astype(bf16), w2_ref[kv], preferred_element_type=f32).astype(bf16)

    kc = compressed(0, fk_ref)
    vc = compressed(1, fv_ref)
    q = q_ref[...]
    q4 = jnp.concatenate([q[:, r * HD:(r + 1) * HD] for r in range(R)] + [jnp.zeros((SUBLANE - R, HD), bf16)], axis=0)
    s = _dot_nt(q4, kc)
    valid = lax.broadcasted_iota(jnp.int32, s.shape, 1) < NC
    s = jnp.where(valid, s, NEG_INF)
    p = jnp.where(valid, jnp.exp(s - jnp.max(s, axis=-1, keepdims=True)), 0.0)
    p = (p / jnp.sum(p, axis=-1, keepdims=True)).astype(bf16)
    ocmp_ref[...] = jnp.dot(p, vc, preferred_element_type=f32)
    head = lax.broadcasted_iota(jnp.int32, p.shape, 0) < R
    imp = _dot_nt(covt_ref[...], jnp.where(head, p, jnp.zeros_like(p)))
    imp = jnp.sum(imp, axis=1, keepdims=True)
    j_col = lax.broadcasted_iota(jnp.int32, (NSP, 1), 0)
    qblk = NS - 1
    forced = (j_col < N_INIT_BLOCKS) | ((j_col <= qblk) & (j_col > qblk - N_LOCAL_BLOCKS))
    score_col = jnp.where(forced, FORCE_SCORE, jnp.where(j_col <= qblk, imp, -3e38))
    score_cb = jnp.broadcast_to(score_col, (NSP, LANE))
    score_row = score_cb.T[0:1, :]
    ii = lax.broadcasted_iota(jnp.int32, (NSP, NSP), 0)
    jj = lax.broadcasted_iota(jnp.int32, (NSP, NSP), 1)
    beats = (score_col > score_row) | ((score_col == score_row) & (ii < jj))
    rank = jnp.sum(jnp.where(beats, 1.0, 0.0), axis=0, keepdims=True)
    t_col = lax.broadcasted_iota(jnp.int32, (n_top, NSP), 0).astype(f32)
    j_row = lax.broadcasted_iota(jnp.int32, (n_top, NSP), 1).astype(f32)
    ids = jnp.sum(jnp.where(rank == t_col, j_row, 0.0), axis=1, keepdims=True)
    idx_ref[...] = jnp.broadcast_to(ids, (n_top, LANE)).astype(jnp.int32)


def dec_select(qr, fs, w1, w2, pe, past_len):
    B = qr.shape[0]
    G, R, HD = NSA_KV_HEADS, NSA_GROUP, HEAD_DIM
    NSEG = fs.shape[1]
    NC = NSEG - 1
    NS = -(-(past_len + 1) // SEL_BLOCK)
    NSP = _round_up(NS, LANE)
    n_top = min(SEL_TOP, NS)
    ci = np.arange(NSEG)[:, None] * CMP_STRIDE
    sj = np.arange(NSP)[None, :] * SEL_BLOCK
    cover = np.clip(np.minimum(ci + CMP_BLOCK, sj + SEL_BLOCK) - np.maximum(ci, sj), 0, None) / CMP_BLOCK
    cover[NC:] = 0.0
    cover[:, NS:] = 0.0
    covt = jnp.asarray(cover.T, jnp.bfloat16)
    f_spec = lambda kv: pl.BlockSpec((None, NSEG, 2 * HD), lambda b, g: (b, 0, kv * G + g))
    full = lambda a: pl.BlockSpec(a.shape, lambda b, g: (0,) * a.ndim)
    ocmp, idx = pl.pallas_call(
        functools.partial(_dec_select_kernel, NSEG=NSEG, NS=NS, NSP=NSP, n_top=n_top),
        grid=(B, G),
        in_specs=[pl.BlockSpec((None, 1, R * HD), lambda b, g: (b, 0, g)), f_spec(0), f_spec(1),
                  full(w1), full(w2), full(pe), full(covt)],
        out_specs=[pl.BlockSpec((None, None, SUBLANE, HD), lambda b, g: (b, g, 0, 0)),
                   pl.BlockSpec((None, None, n_top, LANE), lambda b, g: (b, g, 0, 0))],
        out_shape=[jax.ShapeDtypeStruct((B, G, SUBLANE, HD), jnp.float32),
                   jax.ShapeDtypeStruct((B, G, n_top, LANE), jnp.int32)],
        compiler_params=pltpu.CompilerParams(dimension_semantics=("parallel", "parallel"),
                                             vmem_limit_bytes=VMEM_LIMIT),
        name="nsa_dec_select",
    )(qr, fs, fs, w1, w2, pe, covt)
    return ocmp, idx[:, :, :, 0]


def _dec_attend_kernel(pt_ref, idx_ref, q_ref, *refs, NS, n_top):
    f32, bf16 = jnp.float32, jnp.bfloat16
    HD, R, G = HEAD_DIM, NSA_GROUP, NSA_KV_HEADS
    blocks = refs[:n_top]
    nks_ref, nvs_ref, wb_ref, nkw_ref, nvw_ref, ocmp_ref, g_ref, o_ref = refs[n_top:]
    b, g = pl.program_id(0), pl.program_id(1)
    q = q_ref[...]
    q4 = jnp.concatenate([q[:, r * HD:(r + 1) * HD] for r in range(R)] + [jnp.zeros((SUBLANE - R, HD), bf16)], axis=0)

    def head_rows(ref, kv):
        return ref[pl.ds(kv * G + g, ref.shape[0] // (2 * G), stride=2 * G), :].astype(bf16)

    def attend(keys, vals, bias, k_new_ref, v_new_ref):
        k_new = jnp.broadcast_to(k_new_ref[...], (SUBLANE, HD)).astype(bf16).astype(f32)
        v_new = jnp.broadcast_to(v_new_ref[...], (SUBLANE, HD)).astype(bf16).astype(f32)
        s = _dot_nt(q4, keys) + bias
        s_new = jnp.sum(q4.astype(f32) * k_new, axis=-1, keepdims=True)
        m = jnp.maximum(jnp.max(s, axis=-1, keepdims=True), s_new)
        p = jnp.exp(s - m)
        p_new = jnp.exp(s_new - m)
        l = jnp.sum(p, axis=-1, keepdims=True) + p_new
        return (jnp.dot(p.astype(bf16), vals, preferred_element_type=f32) + p_new.astype(bf16).astype(f32) * v_new) / l

    keys = jnp.concatenate([head_rows(r, 0) for r in blocks], axis=0)
    vals = jnp.concatenate([head_rows(r, 1) for r in blocks], axis=0)
    bias = jnp.concatenate(
        [jnp.broadcast_to(jnp.where(idx_ref[b, g, t] != NS - 1, 0.0, NEG_INF), (SUBLANE, SEL_BLOCK)) for t in range(n_top)],
        axis=1)
    o_slc = attend(keys, vals, bias, nks_ref, nvs_ref)
    nwin = wb_ref.shape[0] // (2 * G)
    ok = lax.broadcasted_iota(jnp.int32, (SUBLANE, nwin), 1) > nwin - WINDOW
    o_win = attend(head_rows(wb_ref, 0), head_rows(wb_ref, 1), jnp.where(ok, 0.0, NEG_INF), nkw_ref, nvw_ref)
    gate = pltpu.roll(jax.nn.sigmoid(jnp.broadcast_to(g_ref[...], (SUBLANE, LANE))), (LANE - 3 * R * g) % LANE, axis=1)
    rows = []
    for r in range(R):
        rows.append(gate[r:r + 1, 3 * r:3 * r + 1] * ocmp_ref[r:r + 1, :] + gate[r:r + 1, 3 * r + 1:3 * r + 2] * o_slc[r:r + 1, :]
                    + gate[r:r + 1, 3 * r + 2:3 * r + 3] * o_win[r:r + 1, :])
    o_ref[...] = jnp.concatenate(rows, axis=1).astype(o_ref.dtype)


def dec_attend(qr, slc_cache, win_cache, layer, page_table, idx, new_slc, new_win, ocmp, hn, gate_col):
    B = qr.shape[0]
    G, R, HD = NSA_KV_HEADS, NSA_GROUP, HEAD_DIM
    L, n_phys, page = slc_cache.shape[:3]
    nc = 2 * G
    n_top = idx.shape[2]
    NS = -(-(page_table.shape[1] * page + 1) // SEL_BLOCK)
    per = page // SEL_BLOCK
    blocks = slc_cache.reshape(L, n_phys * per, SEL_BLOCK * nc, HD)
    nwin = win_cache.shape[2]
    wins = win_cache.reshape(L, B, nwin * nc, HD)

    def blk_spec(t):
        def index(b, g, pt, ix):
            j = jnp.minimum(ix[b, g, t], NS - 2)
            return (layer, pt[b, j // per] * per + j % per, 0, 0)
        return pl.BlockSpec((None, None, SEL_BLOCK * nc, HD), index)

    row = lambda col: pl.BlockSpec((None, 1, HD), lambda b, g, pt, ix: (b, 0, col(g)))
    kcol, vcol = (lambda g: g), (lambda g: G + g)
    return pl.pallas_call(
        functools.partial(_dec_attend_kernel, NS=NS, n_top=n_top),
        grid_spec=pltpu.PrefetchScalarGridSpec(
            num_scalar_prefetch=2,
            grid=(B, G),
            in_specs=[pl.BlockSpec((None, 1, R * HD), lambda b, g, pt, ix: (b, 0, g))]
                     + [blk_spec(t) for t in range(n_top)]
                     + [row(kcol), row(vcol),
                        pl.BlockSpec((None, None, nwin * nc, HD), lambda b, g, pt, ix: (layer, b, 0, 0)),
                        row(kcol), row(vcol),
                        pl.BlockSpec((None, None, SUBLANE, HD), lambda b, g, pt, ix: (b, g, 0, 0)),
                        pl.BlockSpec((None, 1, LANE), lambda b, g, pt, ix: (b, 0, gate_col // LANE))],
            out_specs=pl.BlockSpec((None, 1, R * HD), lambda b, g, pt, ix: (b, 0, g))),
        out_shape=jax.ShapeDtypeStruct((B, 1, G * R * HD), jnp.bfloat16),
        compiler_params=pltpu.CompilerParams(dimension_semantics=("parallel", "parallel"),
                                             vmem_limit_bytes=VMEM_LIMIT),
        name="nsa_dec_attend",
    )(page_table, idx, qr, *([blocks] * n_top), new_slc, new_slc, wins, new_win, new_win, ocmp, hn)


def trunk_layer(x, pos0, paged, gla_s0, rwkv_s0, shift0, conv0, lw):
    B, T, _ = x.shape
    dt = x.dtype
    f32 = jnp.float32
    pos = pos0 + jnp.arange(T, dtype=jnp.int32)
    M = B * T
    x2 = x.reshape(M, D_MODEL)
    xn = rmsnorm_pallas(x2, lw['norm1'])
    hg = mm(xn, lw['w_gla']).reshape(B, T, -1)
    hn = mm(xn, lw['w_nsa']).reshape(B, T, -1)
    hr = mm(xn, lw['w_rwkv']).reshape(B, T, -1)
    mg = mm(xn, lw['w_mg'])

    o_gla, gla_s = gla_pallas(hg, lw['gla_wa2'], lw['gla_ba'], lw['gla_norm'], gla_s0.astype(f32), GLA_OFF)
    o_gla = o_gla.reshape(M, GLA_WIDTH)

    kv5 = lambda t: t.reshape(B, T, 2, NSA_KV_HEADS, HEAD_DIM)
    cw1, cw2, cpe = compress_weights(lw['cmp_w1k'], lw['cmp_w2k'], lw['cmp_pek'],
                                     lw['cmp_w1v'], lw['cmp_w2v'], lw['cmp_pev'])
    if paged is None:
        assert T % NSA_TQ == 0
        qr, cmp2, slc2, win2 = nsa_prep(hn, pos)
        kcmp, vcmp = compress_pallas(cmp2, cw1, cw2, cpe)
        o_nsa = nsa_attention_pallas(qr, kcmp, vcmp, slc2, win2, hn, NSA_GATE_OFF)
        new_cmp, new_slc, win_new = kv5(cmp2), kv5(slc2), kv5(win2)[:, T - min(WINDOW, T):]
    else:
        assert T == 1
        cache_cmp, cache_slc, cache_win, layer, page_table = paged
        qr, cmp2, slc2, win2 = [t.reshape(B, 1, -1) for t in
                                nsa_prep(hn.reshape(1, B, -1), jnp.full((B,), pos0, jnp.int32))]
        fs = dec_compress(cache_cmp, layer, page_table, cw1)
        ocmp, sel_ids = dec_select(qr, fs, cw1, cw2, cpe, pos0)
        o_nsa = dec_attend(qr, cache_slc, cache_win, layer, page_table, sel_ids, slc2, win2, ocmp, hn, NSA_GATE_OFF)
        new_cmp, new_slc = kv5(cmp2), kv5(slc2)
        win_all = jnp.concatenate([cache_win[layer].astype(dt), kv5(win2)], axis=1)
        win_new = win_all[:, win_all.shape[1] - min(WINDOW, win_all.shape[1]):]
    o_nsa = o_nsa.reshape(M, NSA_WIDTH)

    (r_, w_log, k2, v_, kk, a, gate, bonus), shift_new = rwkv_prep(
        hr, shift0.astype(f32), lw['rwkv_mu'], lw['rwkv_w0'], lw['rwkv_w2'], lw['rwkv_a0'], lw['rwkv_a2'],
        lw['rwkv_g2'], lw['rwkv_kk'], lw['rwkv_ka'], lw['rwkv_rk'])
    y, rwkv_s = rwkv_scan_pallas(r_, w_log, k2, v_, kk, a, rwkv_s0.astype(f32))
    flat = lambda t: t.reshape(M, RWKV_WIDTH)
    o_rwkv = rwkv_post(flat(y), flat(bonus), flat(gate), lw['rwkv_ln_w'], lw['rwkv_ln_b'])

    merged = merge_mm(o_gla, o_nsa, o_rwkv, lw['w_o_gla'], lw['w_o_nsa'], lw['w_o_rwkv'], mg)
    x2 = mm(merged, lw['w_out'], res=x2)

    xn2 = rmsnorm_pallas(x2, lw['norm2'])
    act, conv_new = ffn_gate_up(xn2, lw['ffn_gate'], lw['ffn_up'], lw['ffn_conv'], lw['ffn_conv_b'],
                                conv0.astype(f32), B, T)
    x2 = mm(act, lw['ffn_down'], res=x2)
    return x2.reshape(B, T, D_MODEL), (new_cmp, new_slc, win_new, gla_s, rwkv_s, shift_new, conv_new)


def _w_in_group(w, lo, hi):
    seg = w[:, lo:hi].astype(jnp.bfloat16)
    return jnp.pad(seg, ((0, 0), (0, _round_up(hi - lo, W_IN_TILE) - (hi - lo))))


def kernel(x_prompt, x_sample, cache_cmp_kv, cache_slc_kv, cache_win_kv, state_gla, state_rwkv, state_rwkv_shift, state_ffn_conv, page_table, norm1, w_in, gla_wa2, gla_ba, gla_norm, w_o_gla, cmp_w1k, cmp_w2k, cmp_pek, cmp_w1v, cmp_w2v, cmp_pev, w_o_nsa, rwkv_mu, rwkv_w0, rwkv_w2, rwkv_a0, rwkv_a2, rwkv_g2, rwkv_kk, rwkv_ka, rwkv_rk, rwkv_ln_w, rwkv_ln_b, w_o_rwkv, w_out, norm2, ffn_gate, ffn_conv, ffn_conv_b, ffn_up, ffn_down, norm_f):
    past_len = page_table.shape[1] * PAGE_SIZE
    bp = x_prompt.shape[0]
    dt = x_prompt.dtype
    bf = jnp.bfloat16
    xp, xs = x_prompt, x_sample
    st_p, st_s = [], []
    for l in range(DEPTH):
        lw = {'norm1': norm1[l], 'w_gla': _w_in_group(w_in[l], 0, _C_NSA),
              'w_nsa': _w_in_group(w_in[l], _C_NSA, _C_RWKV), 'w_rwkv': _w_in_group(w_in[l], _C_RWKV, _C_MG),
              'w_mg': _w_in_group(w_in[l], _C_MG, _C_MG + IN_SIZES[14]), 'gla_wa2': gla_wa2[l], 'gla_ba': gla_ba[l],
              'gla_norm': gla_norm[l], 'w_o_gla': w_o_gla[l].astype(bf), 'cmp_w1k': cmp_w1k[l], 'cmp_w2k': cmp_w2k[l],
              'cmp_pek': cmp_pek[l], 'cmp_w1v': cmp_w1v[l], 'cmp_w2v': cmp_w2v[l], 'cmp_pev': cmp_pev[l],
              'w_o_nsa': w_o_nsa[l].astype(bf), 'rwkv_mu': rwkv_mu[l], 'rwkv_w0': rwkv_w0[l], 'rwkv_w2': rwkv_w2[l],
              'rwkv_a0': rwkv_a0[l], 'rwkv_a2': rwkv_a2[l], 'rwkv_g2': rwkv_g2[l], 'rwkv_kk': rwkv_kk[l],
              'rwkv_ka': rwkv_ka[l], 'rwkv_rk': rwkv_rk[l], 'rwkv_ln_w': rwkv_ln_w[l], 'rwkv_ln_b': rwkv_ln_b[l],
              'w_o_rwkv': w_o_rwkv[l].astype(bf), 'w_out': w_out[l].astype(bf), 'norm2': norm2[l],
              'ffn_gate': ffn_gate[l], 'ffn_conv': ffn_conv[l], 'ffn_conv_b': ffn_conv_b[l],
              'ffn_up': ffn_up[l], 'ffn_down': ffn_down[l].astype(bf)}
        xp, sp = trunk_layer(xp, 0, None,
                             jnp.zeros((bp, GLA_HEADS, GLA_DK, GLA_DV), jnp.float32),
                             jnp.zeros((bp, RWKV_HEADS, RWKV_N, RWKV_N), jnp.float32),
                             jnp.zeros((bp, RWKV_COLS), dt),
                             jnp.zeros((bp, CONV_W - 1, D_FF), dt), lw)
        paged = (cache_cmp_kv, cache_slc_kv, cache_win_kv, l, page_table)
        xs, ss = trunk_layer(xs, past_len, paged, state_gla[l], state_rwkv[l],
                             state_rwkv_shift[l], state_ffn_conv[l], lw)
        st_p.append(sp)
        st_s.append(ss)
    y_prompt = rmsnorm_pallas(xp.reshape(-1, D_MODEL), norm_f, out_dtype=dt).reshape(xp.shape)
    y_sample = rmsnorm_pallas(xs.reshape(-1, D_MODEL), norm_f, out_dtype=dt).reshape(xs.shape)
    outs = [y_prompt, y_sample]
    for i in range(7):
        outs.append(jnp.stack([s[i] for s in st_p]))
        outs.append(jnp.stack([s[i] for s in st_s]))
    return tuple(outs)
```
